```python
import math
import jax, jax.numpy as jnp
from jax import lax
import numpy as np

D_MODEL = 2048
BATCH = 8
SEQ = 4096
DEPTH = 4

D_MIX = D_MODEL
D_POOL = D_MIX // 2
POOL_WINDOWS = (2, 4, 8, 16)
N_POOL_GROUPS = len(POOL_WINDOWS)
POOL_C = D_POOL // N_POOL_GROUPS
N_HEADS = 8
NOPE_DIM = 128
ROPE_DIM = 64
V_DIM = 128
D_ATT = N_HEADS * V_DIM
Q_LORA = D_MODEL // 4
KV_LORA = D_MODEL // 4
ROPE_THETA = 10000.0
Q_BLOCK = 128
D_IN = D_POOL + Q_LORA + KV_LORA + ROPE_DIM
D_FF = 5632
CONV_W = 3
PLE_DIM = 256
EPS = 1e-6

kernel_name = 'hybrid_pool_mla_convffn_ple'


def rms_norm(x, g):
    xf = x.astype(jnp.float32)
    y = xf * lax.rsqrt(jnp.mean(xf * xf, axis=-1, keepdims=True) + EPS)
    return (y * g.astype(jnp.float32)).astype(x.dtype)


def rope_tables(positions):
    inv_freq = 1.0 / (ROPE_THETA ** (jnp.arange(0, ROPE_DIM, 2, dtype=jnp.float32) / ROPE_DIM))
    ang = positions.astype(jnp.float32)[..., None] * inv_freq
    return jnp.cos(ang), jnp.sin(ang)


def apply_rope(x, cos, sin):
    half = x.shape[-1] // 2
    xf = x.astype(jnp.float32)
    x1, x2 = xf[..., :half], xf[..., half:]
    return jnp.concatenate([x1 * cos - x2 * sin, x2 * cos + x1 * sin], axis=-1).astype(x.dtype)


def multiscale_pool(u, pool_w, pool_scale):
    B, S, _ = u.shape
    uf = u.astype(jnp.float32)
    cs = jnp.cumsum(uf, axis=1)
    t = jnp.arange(S)
    outs = []
    for g, w in enumerate(POOL_WINDOWS):
        c = cs[..., g * POOL_C:(g + 1) * POOL_C]
        lag = jnp.pad(c, ((0, 0), (w, 0), (0, 0)))[:, :S]
        cnt = jnp.minimum(t + 1, w).astype(jnp.float32)[None, :, None]
        outs.append((c - lag) / cnt)
    pooled = jnp.stack(outs, axis=2)
    diff = (pooled - uf.reshape(B, S, N_POOL_GROUPS, POOL_C)).astype(u.dtype)
    y = jnp.einsum('bsgc,gcd->bsgd', diff, pool_w).reshape(B, S, D_POOL)
    return y * pool_scale


def latent_attention(c_q, c_kv, k_rope_raw, cos, sin, q_norm_g, w_uq, kv_norm_g, w_ukv):
    B, S, _ = c_q.shape
    q = (rms_norm(c_q, q_norm_g) @ w_uq).reshape(B, S, N_HEADS, NOPE_DIM + ROPE_DIM)
    q_nope = q[..., :NOPE_DIM]
    q_rope = apply_rope(q[..., NOPE_DIM:], cos[:, :, None], sin[:, :, None])
    kv = (rms_norm(c_kv, kv_norm_g) @ w_ukv).reshape(B, S, N_HEADS, NOPE_DIM + V_DIM)
    k_nope, v = kv[..., :NOPE_DIM], kv[..., NOPE_DIM:]
    k_rope = apply_rope(k_rope_raw, cos, sin)
    scale = 1.0 / math.sqrt(NOPE_DIM + ROPE_DIM)
    nb = S // Q_BLOCK
    qn = q_nope.reshape(B, nb, Q_BLOCK, N_HEADS, NOPE_DIM).transpose(1, 0, 2, 3, 4)
    qr = q_rope.reshape(B, nb, Q_BLOCK, N_HEADS, ROPE_DIM).transpose(1, 0, 2, 3, 4)
    k_pos = jnp.arange(S)

    def block(args):
        qn_b, qr_b, bi = args
        s = (jnp.einsum('bqhd,bkhd->bhqk', qn_b, k_nope).astype(jnp.float32)
             + jnp.einsum('bqhr,bkr->bhqk', qr_b, k_rope).astype(jnp.float32)) * scale
        q_pos = bi * Q_BLOCK + jnp.arange(Q_BLOCK)
        s = jnp.where(q_pos[:, None] >= k_pos[None, :], s, -jnp.inf)
        pr = jax.nn.softmax(s, axis=-1).astype(v.dtype)
        return jnp.einsum('bhqk,bkhd->bqhd', pr, v)

    out = lax.map(block, (qn, qr, jnp.arange(nb)))
    return out.transpose(1, 0, 2, 3, 4).reshape(B, S, D_ATT)


def causal_depthwise_conv(x, w, b):
    C = x.shape[-1]
    y = lax.conv_general_dilated(x, w[:, None, :].astype(x.dtype), window_strides=(1,),
                                 padding=[(CONV_W - 1, 0)],
                                 dimension_numbers=('NWC', 'WIO', 'NWC'),
                                 feature_group_count=C)
    return y + b


def conv_gated_ffn(x, w_up, conv_w, conv_b, w_down):
    gu = x @ w_up
    gate, up = gu[..., :D_FF], gu[..., D_FF:]
    gate = causal_depthwise_conv(gate, conv_w, conv_b)
    return (jax.nn.silu(gate) * up) @ w_down


def _fwd_setup_inputs(seed: int = 0) -> dict:
    key = jax.random.key(seed)
    ks = jax.random.split(key, 24)
    f32 = jnp.float32

    def nrm(k, shape, fan_in):
        return jax.random.normal(k, shape, f32) * (fan_in ** -0.5)

    def gain(k, shape):
        return 1.0 + 0.02 * jax.random.normal(k, shape, f32)

    return {
        'x': jax.random.normal(ks[0], (BATCH, SEQ, D_MODEL), f32),
        'p': jax.random.normal(ks[1], (DEPTH, BATCH, SEQ, PLE_DIM), f32),
        'positions': jnp.broadcast_to(jnp.arange(SEQ, dtype=jnp.int32), (BATCH, SEQ)),
        'norm_mix_g': gain(ks[2], (DEPTH, D_MODEL)),
        'w_in': nrm(ks[3], (DEPTH, D_MODEL, D_IN), D_MODEL),
        'pool_w': nrm(ks[4], (DEPTH, N_POOL_GROUPS, POOL_C, POOL_C), POOL_C),
        'pool_scale': gain(ks[5], (DEPTH, D_POOL)),
        'q_norm_g': gain(ks[6], (DEPTH, Q_LORA)),
        'w_uq': nrm(ks[7], (DEPTH, Q_LORA, N_HEADS * (NOPE_DIM + ROPE_DIM)), Q_LORA),
        'kv_norm_g': gain(ks[8], (DEPTH, KV_LORA)),
        'w_ukv': nrm(ks[9], (DEPTH, KV_LORA, N_HEADS * (NOPE_DIM + V_DIM)), KV_LORA),
        'w_out': nrm(ks[10], (DEPTH, D_MIX, D_MODEL), D_MIX),
        'norm_ffn_g': gain(ks[11], (DEPTH, D_MODEL)),
        'w_up': nrm(ks[12], (DEPTH, D_MODEL, 2 * D_FF), D_MODEL),
        'conv_w': nrm(ks[13], (DEPTH, CONV_W, D_FF), CONV_W),
        'conv_b': 0.01 * jax.random.normal(ks[14], (DEPTH, D_FF), f32),
        'w_down': nrm(ks[15], (DEPTH, D_FF, D_MODEL), D_FF),
        'norm_ple_g': gain(ks[16], (DEPTH, D_MODEL)),
        'w_ple': nrm(ks[17], (DEPTH, PLE_DIM, D_MODEL), PLE_DIM),
        'w_ple_gate': nrm(ks[18], (DEPTH, D_MODEL, D_MODEL), D_MODEL),
        'final_norm_g': gain(ks[19], (D_MODEL,)),
    }


def _fwd_reference(x, p, positions, norm_mix_g, w_in, pool_w, pool_scale, q_norm_g, w_uq,
              kv_norm_g, w_ukv, w_out, norm_ffn_g, w_up, conv_w, conv_b, w_down,
              norm_ple_g, w_ple, w_ple_gate, final_norm_g):
    cos, sin = rope_tables(positions)
    o1 = D_POOL
    o2 = o1 + Q_LORA
    o3 = o2 + KV_LORA
    h = x
    for i in range(DEPTH):
        u = rms_norm(h, norm_mix_g[i]) @ w_in[i]
        y_pool = multiscale_pool(u[..., :o1], pool_w[i], pool_scale[i])
        y_att = latent_attention(u[..., o1:o2], u[..., o2:o3], u[..., o3:], cos, sin,
                                 q_norm_g[i], w_uq[i], kv_norm_g[i], w_ukv[i])
        h = h + jnp.concatenate([y_pool, y_att], axis=-1) @ w_out[i]
        h = h + conv_gated_ffn(rms_norm(h, norm_ffn_g[i]), w_up[i], conv_w[i], conv_b[i], w_down[i])
        gate = jax.nn.sigmoid(rms_norm(h, norm_ple_g[i]) @ w_ple_gate[i])
        h = h + (p[i] @ w_ple[i]) * gate
    return rms_norm(h, final_norm_g)


import jax as _jax
import jax.numpy as _jnp

TWIN_FORMAT = 'train_step'
FWD_PARAMS = ['x', 'p', 'positions', 'norm_mix_g', 'w_in', 'pool_w', 'pool_scale', 'q_norm_g', 'w_uq', 'kv_norm_g', 'w_ukv', 'w_out', 'norm_ffn_g', 'w_up', 'conv_w', 'conv_b', 'w_down', 'norm_ple_g', 'w_ple', 'w_ple_gate', 'final_norm_g']
TWIN_WEIGHTS = ['norm_mix_g', 'w_in', 'pool_w', 'pool_scale', 'q_norm_g', 'w_uq', 'kv_norm_g', 'w_ukv', 'w_out', 'norm_ffn_g', 'w_up', 'conv_w', 'conv_b', 'w_down', 'norm_ple_g', 'w_ple', 'w_ple_gate', 'final_norm_g']
TWIN_DIFF_INPUT = 'x'
TWIN_INPUTS = ['x', 'p', 'positions', 'norm_mix_g', 'w_in', 'pool_w', 'pool_scale', 'q_norm_g', 'w_uq', 'kv_norm_g', 'w_ukv', 'w_out', 'norm_ffn_g', 'w_up', 'conv_w', 'conv_b', 'w_down', 'norm_ple_g', 'w_ple', 'w_ple_gate', 'final_norm_g', 'loss_target', 'm_norm_mix_g', 'm_w_in', 'm_pool_w', 'm_pool_scale', 'm_q_norm_g', 'm_w_uq', 'm_kv_norm_g', 'm_w_ukv', 'm_w_out', 'm_norm_ffn_g', 'm_w_up', 'm_conv_w', 'm_conv_b', 'm_w_down', 'm_norm_ple_g', 'm_w_ple', 'm_w_ple_gate', 'm_final_norm_g', 'v_norm_mix_g', 'v_w_in', 'v_pool_w', 'v_pool_scale', 'v_q_norm_g', 'v_w_uq', 'v_kv_norm_g', 'v_w_ukv', 'v_w_out', 'v_norm_ffn_g', 'v_w_up', 'v_conv_w', 'v_conv_b', 'v_w_down', 'v_norm_ple_g', 'v_w_ple', 'v_w_ple_gate', 'v_final_norm_g']
TWIN_OUTPUTS = ['loss', 'grad_x', 'grad_norm_mix_g', 'grad_w_in', 'grad_pool_w', 'grad_pool_scale', 'grad_q_norm_g', 'grad_w_uq', 'grad_kv_norm_g', 'grad_w_ukv', 'grad_w_out', 'grad_norm_ffn_g', 'grad_w_up', 'grad_conv_w', 'grad_conv_b', 'grad_w_down', 'grad_norm_ple_g', 'grad_w_ple', 'grad_w_ple_gate', 'grad_final_norm_g', 'delta_norm_mix_g', 'delta_w_in', 'delta_pool_w', 'delta_pool_scale', 'delta_q_norm_g', 'delta_w_uq', 'delta_kv_norm_g', 'delta_w_ukv', 'delta_w_out', 'delta_norm_ffn_g', 'delta_w_up', 'delta_conv_w', 'delta_conv_b', 'delta_w_down', 'delta_norm_ple_g', 'delta_w_ple', 'delta_w_ple_gate', 'delta_final_norm_g', 'new_m_norm_mix_g', 'new_m_w_in', 'new_m_pool_w', 'new_m_pool_scale', 'new_m_q_norm_g', 'new_m_w_uq', 'new_m_kv_norm_g', 'new_m_w_ukv', 'new_m_w_out', 'new_m_norm_ffn_g', 'new_m_w_up', 'new_m_conv_w', 'new_m_conv_b', 'new_m_w_down', 'new_m_norm_ple_g', 'new_m_w_ple', 'new_m_w_ple_gate', 'new_m_final_norm_g', 'new_v_norm_mix_g', 'new_v_w_in', 'new_v_pool_w', 'new_v_pool_scale', 'new_v_q_norm_g', 'new_v_w_uq', 'new_v_kv_norm_g', 'new_v_w_ukv', 'new_v_w_out', 'new_v_norm_ffn_g', 'new_v_w_up', 'new_v_conv_w', 'new_v_conv_b', 'new_v_w_down', 'new_v_norm_ple_g', 'new_v_w_ple', 'new_v_w_ple_gate', 'new_v_final_norm_g']
TWIN_LEAF_KINDS = {'loss': 'loss', 'grad_x': 'grad_x', 'grad_norm_mix_g': 'grad_w', 'grad_w_in': 'grad_w', 'grad_pool_w': 'grad_w', 'grad_pool_scale': 'grad_w', 'grad_q_norm_g': 'grad_w', 'grad_w_uq': 'grad_w', 'grad_kv_norm_g': 'grad_w', 'grad_w_ukv': 'grad_w', 'grad_w_out': 'grad_w', 'grad_norm_ffn_g': 'grad_w', 'grad_w_up': 'grad_w', 'grad_conv_w': 'grad_w', 'grad_conv_b': 'grad_w', 'grad_w_down': 'grad_w', 'grad_norm_ple_g': 'grad_w', 'grad_w_ple': 'grad_w', 'grad_w_ple_gate': 'grad_w', 'grad_final_norm_g': 'grad_w', 'delta_norm_mix_g': 'delta_w', 'delta_w_in': 'delta_w', 'delta_pool_w': 'delta_w', 'delta_pool_scale': 'delta_w', 'delta_q_norm_g': 'delta_w', 'delta_w_uq': 'delta_w', 'delta_kv_norm_g': 'delta_w', 'delta_w_ukv': 'delta_w', 'delta_w_out': 'delta_w', 'delta_norm_ffn_g': 'delta_w', 'delta_w_up': 'delta_w', 'delta_conv_w': 'delta_w', 'delta_conv_b': 'delta_w', 'delta_w_down': 'delta_w', 'delta_norm_ple_g': 'delta_w', 'delta_w_ple': 'delta_w', 'delta_w_ple_gate': 'delta_w', 'delta_final_norm_g': 'delta_w', 'new_m_norm_mix_g': 'new_m', 'new_m_w_in': 'new_m', 'new_m_pool_w': 'new_m', 'new_m_pool_scale': 'new_m', 'new_m_q_norm_g': 'new_m', 'new_m_w_uq': 'new_m', 'new_m_kv_norm_g': 'new_m', 'new_m_w_ukv': 'new_m', 'new_m_w_out': 'new_m', 'new_m_norm_ffn_g': 'new_m', 'new_m_w_up': 'new_m', 'new_m_conv_w': 'new_m', 'new_m_conv_b': 'new_m', 'new_m_w_down': 'new_m', 'new_m_norm_ple_g': 'new_m', 'new_m_w_ple': 'new_m', 'new_m_w_ple_gate': 'new_m', 'new_m_final_norm_g': 'new_m', 'new_v_norm_mix_g': 'new_v', 'new_v_w_in': 'new_v', 'new_v_pool_w': 'new_v', 'new_v_pool_scale': 'new_v', 'new_v_q_norm_g': 'new_v', 'new_v_w_uq': 'new_v', 'new_v_kv_norm_g': 'new_v', 'new_v_w_ukv': 'new_v', 'new_v_w_out': 'new_v', 'new_v_norm_ffn_g': 'new_v', 'new_v_w_up': 'new_v', 'new_v_conv_w': 'new_v', 'new_v_conv_b': 'new_v', 'new_v_w_down': 'new_v', 'new_v_norm_ple_g': 'new_v', 'new_v_w_ple': 'new_v', 'new_v_w_ple_gate': 'new_v', 'new_v_final_norm_g': 'new_v'}


def _forward(args):
    return _fwd_reference(*[args[k] for k in FWD_PARAMS])


def _output_shape():
    def fwd():
        inp = _fwd_setup_inputs(0)
        return _fwd_reference(*[inp[k] for k in FWD_PARAMS])
    out = _jax.eval_shape(fwd)
    return out.shape, out.dtype

N_MICROBATCH = 1
ADAM_LR = 0.001
ADAM_B1 = 0.9
ADAM_B2 = 0.999
ADAM_EPS = 1e-08
ADAM_WD = 0.01
ADAM_STEP = 10
PER_EXAMPLE_BATCH_AXIS = {'x': 0, 'p': 1, 'positions': 0, 'loss_target': 0}
SHARED_INPUTS = []
_WEIGHT_DTYPES = {'norm_mix_g': _jnp.float32, 'w_in': _jnp.float32, 'pool_w': _jnp.float32, 'pool_scale': _jnp.float32, 'q_norm_g': _jnp.float32, 'w_uq': _jnp.float32, 'kv_norm_g': _jnp.float32, 'w_ukv': _jnp.float32, 'w_out': _jnp.float32, 'norm_ffn_g': _jnp.float32, 'w_up': _jnp.float32, 'conv_w': _jnp.float32, 'conv_b': _jnp.float32, 'w_down': _jnp.float32, 'norm_ple_g': _jnp.float32, 'w_ple': _jnp.float32, 'w_ple_gate': _jnp.float32, 'final_norm_g': _jnp.float32}
MOMENT_SCALE = {'norm_mix_g': 4.045348e-02, 'w_in': 3.955604e-02, 'pool_w': 5.367519e-02, 'pool_scale': 5.325971e-02, 'q_norm_g': 1.546876e-02, 'w_uq': 8.776711e-03, 'kv_norm_g': 2.317024e-02, 'w_ukv': 1.186860e-02, 'w_out': 3.871782e-02, 'norm_ffn_g': 4.579324e-02, 'w_up': 1.952764e-02, 'conv_w': 1.965540e-02, 'conv_b': 1.862798e-02, 'w_down': 3.191901e-02, 'norm_ple_g': 1.106672e-02, 'w_ple': 2.765561e-02, 'w_ple_gate': 1.079105e-02, 'final_norm_g': 1.600755e+01}


def _to_microbatches(a, axis):
    t = _jnp.moveaxis(a, axis, 0)
    t = t.reshape((N_MICROBATCH, t.shape[0] // N_MICROBATCH) + t.shape[1:])
    return _jnp.moveaxis(t, 1, axis + 1)


def setup_inputs(seed: int = 0) -> dict:
    inp = _fwd_setup_inputs(seed)
    key = _jax.random.fold_in(_jax.random.key(seed), 7919)
    shape, _ = _output_shape()
    out = dict(inp)
    out["loss_target"] = _jax.random.normal(_jax.random.fold_in(key, 0), shape, _jnp.float32)
    for i, name in enumerate(TWIN_WEIGHTS):
        w = inp[name].astype(_jnp.float32)
        if MOMENT_SCALE is None:
            s = _jnp.sqrt(_jnp.mean(_jnp.square(w)) + 1e-30)
        else:
            s = MOMENT_SCALE[name]
        km, kv = _jax.random.split(_jax.random.fold_in(key, i + 1))
        out[name] = w
        out["m_" + name] = s * _jax.random.normal(km, w.shape, _jnp.float32)
        out["v_" + name] = (s * s) * _jax.random.uniform(kv, w.shape, _jnp.float32, 0.5, 1.5)
    if N_MICROBATCH > 1:
        for name, axis in PER_EXAMPLE_BATCH_AXIS.items():
            out[name] = _to_microbatches(out[name], axis)
    return {'x': out['x'], 'p': out['p'], 'positions': out['positions'], 'norm_mix_g': out['norm_mix_g'], 'w_in': out['w_in'], 'pool_w': out['pool_w'], 'pool_scale': out['pool_scale'], 'q_norm_g': out['q_norm_g'], 'w_uq': out['w_uq'], 'kv_norm_g': out['kv_norm_g'], 'w_ukv': out['w_ukv'], 'w_out': out['w_out'], 'norm_ffn_g': out['norm_ffn_g'], 'w_up': out['w_up'], 'conv_w': out['conv_w'], 'conv_b': out['conv_b'], 'w_down': out['w_down'], 'norm_ple_g': out['norm_ple_g'], 'w_ple': out['w_ple'], 'w_ple_gate': out['w_ple_gate'], 'final_norm_g': out['final_norm_g'], 'loss_target': out['loss_target'], 'm_norm_mix_g': out['m_norm_mix_g'], 'm_w_in': out['m_w_in'], 'm_pool_w': out['m_pool_w'], 'm_pool_scale': out['m_pool_scale'], 'm_q_norm_g': out['m_q_norm_g'], 'm_w_uq': out['m_w_uq'], 'm_kv_norm_g': out['m_kv_norm_g'], 'm_w_ukv': out['m_w_ukv'], 'm_w_out': out['m_w_out'], 'm_norm_ffn_g': out['m_norm_ffn_g'], 'm_w_up': out['m_w_up'], 'm_conv_w': out['m_conv_w'], 'm_conv_b': out['m_conv_b'], 'm_w_down': out['m_w_down'], 'm_norm_ple_g': out['m_norm_ple_g'], 'm_w_ple': out['m_w_ple'], 'm_w_ple_gate': out['m_w_ple_gate'], 'm_final_norm_g': out['m_final_norm_g'], 'v_norm_mix_g': out['v_norm_mix_g'], 'v_w_in': out['v_w_in'], 'v_pool_w': out['v_pool_w'], 'v_pool_scale': out['v_pool_scale'], 'v_q_norm_g': out['v_q_norm_g'], 'v_w_uq': out['v_w_uq'], 'v_kv_norm_g': out['v_kv_norm_g'], 'v_w_ukv': out['v_w_ukv'], 'v_w_out': out['v_w_out'], 'v_norm_ffn_g': out['v_norm_ffn_g'], 'v_w_up': out['v_w_up'], 'v_conv_w': out['v_conv_w'], 'v_conv_b': out['v_conv_b'], 'v_w_down': out['v_w_down'], 'v_norm_ple_g': out['v_norm_ple_g'], 'v_w_ple': out['v_w_ple'], 'v_w_ple_gate': out['v_w_ple_gate'], 'v_final_norm_g': out['v_final_norm_g']}


def _loss(weights, diff, rest, loss_target):
    with _jax.named_scope("forward"):
        args = {**rest, TWIN_DIFF_INPUT: diff, **{k: w.astype(_WEIGHT_DTYPES[k]) for k, w in weights.items()}}
        y = _forward(args)
    with _jax.named_scope("loss_head"):
        err = _jnp.square(y.astype(_jnp.float32) - loss_target)
        return 0.5 * _jnp.sum(_jnp.mean(err, axis=-1)) if err.ndim else 0.5 * err


def _adamw(w, g, m, v):
    m = ADAM_B1 * m + (1.0 - ADAM_B1) * g
    v = ADAM_B2 * v + (1.0 - ADAM_B2) * _jnp.square(g)
    m_hat = m / (1.0 - ADAM_B1 ** ADAM_STEP)
    v_hat = v / (1.0 - ADAM_B2 ** ADAM_STEP)
    delta = -ADAM_LR * (m_hat / (_jnp.sqrt(v_hat) + ADAM_EPS) + ADAM_WD * w)
    return delta, m, v


def reference(x, p, positions, norm_mix_g, w_in, pool_w, pool_scale, q_norm_g, w_uq, kv_norm_g, w_ukv, w_out, norm_ffn_g, w_up, conv_w, conv_b, w_down, norm_ple_g, w_ple, w_ple_gate, final_norm_g, loss_target, m_norm_mix_g, m_w_in, m_pool_w, m_pool_scale, m_q_norm_g, m_w_uq, m_kv_norm_g, m_w_ukv, m_w_out, m_norm_ffn_g, m_w_up, m_conv_w, m_conv_b, m_w_down, m_norm_ple_g, m_w_ple, m_w_ple_gate, m_final_norm_g, v_norm_mix_g, v_w_in, v_pool_w, v_pool_scale, v_q_norm_g, v_w_uq, v_kv_norm_g, v_w_ukv, v_w_out, v_norm_ffn_g, v_w_up, v_conv_w, v_conv_b, v_w_down, v_norm_ple_g, v_w_ple, v_w_ple_gate, v_final_norm_g):
    given = dict(x=x, p=p, positions=positions, norm_mix_g=norm_mix_g, w_in=w_in, pool_w=pool_w, pool_scale=pool_scale, q_norm_g=q_norm_g, w_uq=w_uq, kv_norm_g=kv_norm_g, w_ukv=w_ukv, w_out=w_out, norm_ffn_g=norm_ffn_g, w_up=w_up, conv_w=conv_w, conv_b=conv_b, w_down=w_down, norm_ple_g=norm_ple_g, w_ple=w_ple, w_ple_gate=w_ple_gate, final_norm_g=final_norm_g, loss_target=loss_target, m_norm_mix_g=m_norm_mix_g, m_w_in=m_w_in, m_pool_w=m_pool_w, m_pool_scale=m_pool_scale, m_q_norm_g=m_q_norm_g, m_w_uq=m_w_uq, m_kv_norm_g=m_kv_norm_g, m_w_ukv=m_w_ukv, m_w_out=m_w_out, m_norm_ffn_g=m_norm_ffn_g, m_w_up=m_w_up, m_conv_w=m_conv_w, m_conv_b=m_conv_b, m_w_down=m_w_down, m_norm_ple_g=m_norm_ple_g, m_w_ple=m_w_ple, m_w_ple_gate=m_w_ple_gate, m_final_norm_g=m_final_norm_g, v_norm_mix_g=v_norm_mix_g, v_w_in=v_w_in, v_pool_w=v_pool_w, v_pool_scale=v_pool_scale, v_q_norm_g=v_q_norm_g, v_w_uq=v_w_uq, v_kv_norm_g=v_kv_norm_g, v_w_ukv=v_w_ukv, v_w_out=v_w_out, v_norm_ffn_g=v_norm_ffn_g, v_w_up=v_w_up, v_conv_w=v_conv_w, v_conv_b=v_conv_b, v_w_down=v_w_down, v_norm_ple_g=v_norm_ple_g, v_w_ple=v_w_ple, v_w_ple_gate=v_w_ple_gate, v_final_norm_g=v_final_norm_g)
    weights = {n: given[n] for n in TWIN_WEIGHTS}
    shared = {n: given[n] for n in SHARED_INPUTS}
    per_example = {n: given[n] for n in ['x', 'p', 'positions']}
    grad_fn = _jax.value_and_grad(_loss, argnums=(0, 1))

    def one_microbatch(ex, loss_target):
        ex = dict(ex)
        diff = ex.pop(TWIN_DIFF_INPUT)
        return grad_fn(weights, diff, {**shared, **ex}, loss_target)

    if N_MICROBATCH == 1:
        loss, (grad_w, grad_x) = one_microbatch(per_example, given["loss_target"])
    else:
        def body(carry, xs):
            loss_sum, grad_sum = carry
            l_k, (gw_k, gx_k) = one_microbatch(xs[0], xs[1])
            with _jax.named_scope("update"):
                return (loss_sum + l_k, _jax.tree.map(_jnp.add, grad_sum, gw_k)), gx_k

        init = (_jnp.zeros((), _jnp.float32), _jax.tree.map(_jnp.zeros_like, weights))
        (loss, grad_w), grad_x = _jax.lax.scan(body, init, (per_example, given["loss_target"]))
    with _jax.named_scope("update"):
        delta_w, new_m, new_v = {}, {}, {}
        for n in TWIN_WEIGHTS:
            delta_w[n], new_m[n], new_v[n] = _adamw(weights[n], grad_w[n], given["m_" + n], given["v_" + n])
    return (loss, grad_x, *[grad_w[n] for n in TWIN_WEIGHTS], *[delta_w[n] for n in TWIN_WEIGHTS],
            *[new_m[n] for n in TWIN_WEIGHTS], *[new_v[n] for n in TWIN_WEIGHTS])
```

```python
import functools
import math

import jax
import jax.numpy as jnp
from jax import lax
from jax.experimental import pallas as pl
from jax.experimental.pallas import tpu as pltpu

F32 = jnp.float32
BF16 = jnp.bfloat16

N_DEV = 8
MESH_AXES = ("x", "y", "c")
NOPE_DIM = 128
ROPE_DIM = 64
V_DIM = 128
POOL_GROUPS = 4
CONV_TAPS = 3
ROPE_THETA = 10000.0
NORM_EPS = 1e-6
ADAM_LR = 0.001
ADAM_B1 = 0.9
ADAM_B2 = 0.999
ADAM_EPS = 1e-08
ADAM_WD = 0.01
ADAM_STEP = 10
LANES = 128
VMEM_LIMIT_BYTES = 56 * 1024 * 1024

SHARDED = ("w_in", "pool_w", "w_uq", "w_ukv", "w_out", "w_up", "conv_w", "w_down", "w_ple", "w_ple_gate")
SHARD_KIND = {"w_in": "col", "pool_w": "pool", "w_uq": "col", "w_ukv": "col", "w_out": "row", "w_up": "col",
              "conv_w": "col", "w_down": "row", "w_ple": "col", "w_ple_gate": "row"}
REPLICATED = ("norm_mix_g", "pool_scale", "q_norm_g", "kv_norm_g", "norm_ffn_g", "conv_b", "norm_ple_g")
WEIGHT_ORDER = ("norm_mix_g", "w_in", "pool_w", "pool_scale", "q_norm_g", "w_uq", "kv_norm_g", "w_ukv", "w_out",
                "norm_ffn_g", "w_up", "conv_w", "conv_b", "w_down", "norm_ple_g", "w_ple", "w_ple_gate",
                "final_norm_g")

_pcall = pl.pallas_call


def _params(*sem):
    return pltpu.CompilerParams(dimension_semantics=sem or None, vmem_limit_bytes=VMEM_LIMIT_BYTES)


def _tile(n, pref, mult=LANES):
    if n <= pref:
        return n
    t = (pref // mult) * mult
    while t >= mult:
        if n % t == 0:
            return t
        t -= mult
    return n


def _mm(a, b, *, name, ta=False, tb=False, add=None, out_dtype=F32, a_col=(0, None), b_row=(0, None),
        b_col=(0, None), tm=1024, tn=1024, tk=512):
    if ta:
        k_a, m = a.shape
    else:
        m, k_a = a.shape
    k_off_a, k_sz = a_col if not ta else (0, None)
    if k_sz is None:
        k_sz = k_a - k_off_a
    kdim = k_sz
    if tb:
        n_b, k_b = b.shape
    else:
        k_b, n_b = b.shape
    k_off_b, kb_sz = b_row
    if kb_sz is None:
        kb_sz = k_b - k_off_b
    assert kb_sz == kdim, (name, kb_sz, kdim)
    n_off, n = b_col
    if n is None:
        n = n_b - n_off
    tm = _tile(m, tm)
    tn = _tile(n, tn)
    tk = _tile(kdim, tk)
    assert k_off_a % tk == 0 and k_off_b % tk == 0 and n_off % tn == 0, name
    ka0, kb0, n0 = k_off_a // tk, k_off_b // tk, n_off // tn
    nk = kdim // tk
    dims = (((0 if ta else 1,), (1 if tb else 0,)), ((), ()))

    def body(*refs):
        if add is None:
            a_ref, b_ref, o_ref, acc = refs
            add_ref = None
        else:
            a_ref, b_ref, add_ref, o_ref, acc = refs
        k = pl.program_id(2)

        @pl.when(k == 0)
        def _():
            acc[...] = jnp.zeros_like(acc)

        acc[...] += lax.dot_general(a_ref[...].astype(BF16), b_ref[...].astype(BF16), dims,
                                    preferred_element_type=F32)

        @pl.when(k == nk - 1)
        def _():
            r = acc[...]
            if add_ref is not None:
                r = r + add_ref[...].astype(F32)
            o_ref[...] = r.astype(out_dtype)

    if ta:
        a_spec = pl.BlockSpec((tk, tm), lambda i, j, k: (k, i))
    else:
        a_spec = pl.BlockSpec((tm, tk), lambda i, j, k: (i, k + ka0))
    if tb:
        b_spec = pl.BlockSpec((tn, tk), lambda i, j, k: (j + n0, k + kb0))
    else:
        b_spec = pl.BlockSpec((tk, tn), lambda i, j, k: (k + kb0, j + n0))
    in_specs = [a_spec, b_spec]
    args = [a, b]
    if add is not None:
        in_specs.append(pl.BlockSpec((tm, tn), lambda i, j, k: (i, j)))
        args.append(add)
    return _pcall(
        body, name=name, grid=(m // tm, n // tn, nk), in_specs=in_specs,
        out_specs=pl.BlockSpec((tm, tn), lambda i, j, k: (i, j)),
        out_shape=jax.ShapeDtypeStruct((m, n), out_dtype),
        scratch_shapes=[pltpu.VMEM((tm, tn), F32)],
        compiler_params=_params("parallel", "parallel", "arbitrary"),
    )(*args)


def _rms_fwd(h, g, *, name, col_block=0):
    s = h.shape[0]
    d = g.shape[-1]
    ts = _tile(s, 512, 8)

    def body(h_ref, g_ref, n_ref):
        x = h_ref[...]
        r = lax.rsqrt(jnp.mean(x * x, axis=-1, keepdims=True) + NORM_EPS)
        n_ref[...] = (x * r * g_ref[...]).astype(BF16)

    return _pcall(
        body, name=name, grid=(s // ts,),
        in_specs=[pl.BlockSpec((ts, d), lambda i: (i, col_block)), pl.BlockSpec((1, d), lambda i: (0, 0))],
        out_specs=pl.BlockSpec((ts, d), lambda i: (i, 0)),
        out_shape=jax.ShapeDtypeStruct((s, d), BF16),
        compiler_params=_params("parallel"),
    )(h, g.reshape(1, d))


def _rms_bwd(dn, h, g, *, name, res=None, col_block=0, out_dtype=F32):
    s = dn.shape[0]
    d = g.shape[-1]
    ts = _tile(s, 512, 8)

    def body(*refs):
        if res is None:
            dn_ref, h_ref, g_ref, dh_ref, dg_ref = refs
            res_ref = None
        else:
            dn_ref, h_ref, g_ref, res_ref, dh_ref, dg_ref = refs
        i = pl.program_id(0)
        x = h_ref[...]
        r = lax.rsqrt(jnp.mean(x * x, axis=-1, keepdims=True) + NORM_EPS)
        nh = x * r
        dnv = dn_ref[...]
        gd = dnv * g_ref[...]
        dh = (gd - nh * jnp.mean(gd * nh, axis=-1, keepdims=True)) * r
        if res_ref is not None:
            dh = dh + res_ref[...]
        dh_ref[...] = dh.astype(out_dtype)
        part = jnp.sum(dnv * nh, axis=0, keepdims=True)

        @pl.when(i == 0)
        def _():
            dg_ref[...] = part

        @pl.when(i > 0)
        def _():
            dg_ref[...] += part

    row = pl.BlockSpec((ts, d), lambda i: (i, 0))
    in_specs = [row, pl.BlockSpec((ts, d), lambda i: (i, col_block)), pl.BlockSpec((1, d), lambda i: (0, 0))]
    args = [dn, h, g.reshape(1, d)]
    if res is not None:
        in_specs.append(row)
        args.append(res)
    return _pcall(
        body, name=name, grid=(s // ts,), in_specs=in_specs,
        out_specs=(row, pl.BlockSpec((1, d), lambda i: (0, 0))),
        out_shape=(jax.ShapeDtypeStruct((s, d), out_dtype), jax.ShapeDtypeStruct((1, d), F32)),
        compiler_params=_params("arbitrary"),
    )(*args)


ROW_CHUNK = 512


def _rows_with_halo(ref, r, t_rows, n_chunks, before, after):
    r0 = r * t_rows
    parts = []
    if before:
        hb = ref[pl.ds(pl.multiple_of(jnp.maximum(r0 - before, 0), before), before), :]
        parts.append(jnp.where(r > 0, hb, jnp.zeros_like(hb)))
    parts.append(ref[pl.ds(pl.multiple_of(r0, t_rows), t_rows), :])
    if after:
        ha = ref[pl.ds(pl.multiple_of(jnp.minimum(r0 + t_rows, n_chunks * t_rows - after), after), after), :]
        parts.append(jnp.where(r < n_chunks - 1, ha, jnp.zeros_like(ha)))
    return jnp.concatenate(parts, axis=0)


POOL_HALO = 16


def _pool_fwd(u, pool_w, pool_scale, *, c):
    s = u.shape[0]
    g_n = POOL_GROUPS
    tr = _tile(s, ROW_CHUNK, POOL_HALO)
    n_chunks = s // tr

    def body(u_ref, pw_ref, sc_ref, y_ref, d_ref):
        r = pl.program_id(1)
        w = jnp.left_shift(2, pl.program_id(0))
        xe = _rows_with_halo(u_ref, r, tr, n_chunks, POOL_HALO, 0)
        acc = xe
        for k in (1, 2, 4, 8):
            acc = jnp.where(k < w, acc + pltpu.roll(acc, k, 0), acc)
        t = r * tr + lax.broadcasted_iota(jnp.int32, (tr, 1), 0)
        cnt = jnp.minimum(t + 1, w).astype(F32)
        diff = (acc[POOL_HALO:] / cnt - xe[POOL_HALO:]).astype(BF16)
        d_ref[...] = diff
        y = jnp.dot(diff, pw_ref[...], preferred_element_type=F32) * sc_ref[...]
        y_ref[...] = y.astype(BF16)

    out = pl.BlockSpec((tr, c), lambda g, r: (r, g))
    return _pcall(
        body, name="pool_fwd", grid=(g_n, n_chunks),
        in_specs=[pl.BlockSpec((s, c), lambda g, r: (0, g)), pl.BlockSpec((None, c, c), lambda g, r: (g, 0, 0)),
                  pl.BlockSpec((1, c), lambda g, r: (0, g))],
        out_specs=(out, out),
        out_shape=(jax.ShapeDtypeStruct((s, g_n * c), BF16), jax.ShapeDtypeStruct((s, g_n * c), BF16)),
        compiler_params=_params("parallel", "arbitrary"),
    )(u, pool_w, pool_scale.reshape(1, g_n * c))


def _pool_bwd(dcat, diff, pool_w, pool_scale, *, c):
    s = dcat.shape[0]
    g_n = POOL_GROUPS
    tr = _tile(s, ROW_CHUNK, POOL_HALO)
    n_chunks = s // tr

    def body(dy_ref, d_ref, pw_ref, sc_ref, du_ref, dpw_ref, dsc_ref):
        r = pl.program_id(1)
        w = jnp.left_shift(2, pl.program_id(0))
        dye = _rows_with_halo(dy_ref, r, tr, n_chunks, 0, POOL_HALO)
        diff = d_ref[pl.ds(pl.multiple_of(r * tr, tr), tr), :]
        pw = pw_ref[...]
        yp = jnp.dot(diff, pw, preferred_element_type=F32)
        dsc = jnp.sum(dye[:tr] * yp, axis=0, keepdims=True)
        dyp = (dye * sc_ref[...]).astype(BF16)
        ddiff = lax.dot_general(dyp, pw, (((1,), (1,)), ((), ())), preferred_element_type=F32)
        dpw = lax.dot_general(diff, dyp[:tr], (((0,), (0,)), ((), ())), preferred_element_type=F32)
        t = r * tr + lax.broadcasted_iota(jnp.int32, (tr + POOL_HALO, 1), 0)
        cnt = jnp.minimum(t + 1, w).astype(F32)
        acc = ddiff / cnt
        rows = tr + POOL_HALO
        for k in (1, 2, 4, 8):
            acc = jnp.where(k < w, acc + pltpu.roll(acc, rows - k, 0), acc)
        du_ref[...] = (acc[:tr] - ddiff[:tr]).astype(BF16)

        @pl.when(r == 0)
        def _():
            dpw_ref[...] = dpw
            dsc_ref[...] = dsc

        @pl.when(r > 0)
        def _():
            dpw_ref[...] += dpw
            dsc_ref[...] += dsc

    col = lambda g, r: (0, g)
    wspec = pl.BlockSpec((None, c, c), lambda g, r: (g, 0, 0))
    vec = pl.BlockSpec((1, c), col)
    return _pcall(
        body, name="pool_bwd", grid=(g_n, n_chunks),
        in_specs=[pl.BlockSpec((s, c), col), pl.BlockSpec((s, c), col), wspec, vec],
        out_specs=(pl.BlockSpec((tr, c), lambda g, r: (r, g)), wspec, vec),
        out_shape=(jax.ShapeDtypeStruct((s, g_n * c), BF16), jax.ShapeDtypeStruct((g_n, c, c), F32),
                   jax.ShapeDtypeStruct((1, g_n * c), F32)),
        compiler_params=_params("parallel", "arbitrary"),
    )(dcat, diff, pool_w, pool_scale.reshape(1, g_n * c))


def _swap_halves(x, lane):
    return jnp.where((lane % ROPE_DIM) < ROPE_DIM // 2, pltpu.roll(x, LANES - ROPE_DIM // 2, 1),
                     pltpu.roll(x, ROPE_DIM // 2, 1))


def _qkv_prep(q, kv, kr2, cos_t, sin_t, *, heads):
    s = q.shape[0]
    ts = _tile(s, 512, 8)

    def body(qn_ref, qr_ref, kn_ref, v_ref, kr_ref, cos_ref, sin_ref, qo_ref, ko_ref, vo_ref):
        half = pl.program_id(1) % 2
        lane = lax.broadcasted_iota(jnp.int32, (ts, LANES), 1)
        cos_v = cos_ref[...]
        sin_v = sin_ref[...]

        def rope(x):
            return x * cos_v + _swap_halves(x, lane) * sin_v

        qo_ref[:, :LANES] = qn_ref[...].astype(BF16)
        qo_ref[:, LANES:] = jnp.where(lane // ROPE_DIM == half, rope(qr_ref[...]), 0.0).astype(BF16)
        ko_ref[:, :LANES] = kn_ref[...].astype(BF16)
        ko_ref[:, LANES:] = rope(kr_ref[...]).astype(BF16)
        vo_ref[...] = v_ref[...].astype(BF16)

    blk = lambda f: pl.BlockSpec((ts, LANES), f)
    return _pcall(
        body, name="qkv_prep", grid=(s // ts, heads),
        in_specs=[blk(lambda i, h: (i, h)), blk(lambda i, h: (i, heads + h // 2)),
                  blk(lambda i, h: (i, h)), blk(lambda i, h: (i, heads + h)),
                  blk(lambda i, h: (i, 0)), blk(lambda i, h: (i, 0)), blk(lambda i, h: (i, 0))],
        out_specs=(pl.BlockSpec((None, ts, 2 * LANES), lambda i, h: (h, i, 0)),
                   pl.BlockSpec((None, ts, 2 * LANES), lambda i, h: (h, i, 0)),
                   pl.BlockSpec((None, ts, LANES), lambda i, h: (h, i, 0))),
        out_shape=(jax.ShapeDtypeStruct((heads, s, 2 * LANES), BF16),
                   jax.ShapeDtypeStruct((heads, s, 2 * LANES), BF16),
                   jax.ShapeDtypeStruct((heads, s, LANES), BF16)),
        compiler_params=_params("parallel", "parallel"),
    )(q, q, kv, kv, kr2, cos_t, sin_t)


def _masked_scores(q, k, i, j, tq, tk, scale):
    sc = lax.dot_general(q, k, (((1,), (1,)), ((), ())), preferred_element_type=F32) * scale
    rows = i * tq + lax.broadcasted_iota(jnp.int32, (tq, tk), 0)
    cols = j * tk + lax.broadcasted_iota(jnp.int32, (tq, tk), 1)
    return jnp.where(rows >= cols, sc, -jnp.inf)


def _flash_fwd(qp, kp, v, *, scale, tq=512):
    heads, s, dk = qp.shape
    tq = _tile(s, tq, 8)
    tk = tq
    nq = s // tq

    def body(q_ref, k_ref, v_ref, o_ref, lse_ref, m_sc, l_sc, acc_sc):
        i = pl.program_id(1)
        j = pl.program_id(2)

        @pl.when(j == 0)
        def _():
            m_sc[...] = jnp.full_like(m_sc, -jnp.inf)
            l_sc[...] = jnp.zeros_like(l_sc)
            acc_sc[...] = jnp.zeros_like(acc_sc)

        @pl.when(j <= i)
        def _():
            sc = _masked_scores(q_ref[...], k_ref[...], i, j, tq, tk, scale)
            m_old = m_sc[...]
            m_new = jnp.maximum(m_old, jnp.max(sc, axis=-1, keepdims=True))
            alpha = jnp.exp(m_old - m_new)
            p = jnp.exp(sc - m_new)
            l_sc[...] = alpha * l_sc[...] + jnp.sum(p, axis=-1, keepdims=True)
            acc_sc[...] = alpha * acc_sc[...] + jnp.dot(p.astype(BF16), v_ref[...], preferred_element_type=F32)
            m_sc[...] = m_new

        @pl.when(j == nq - 1)
        def _():
            o_ref[...] = acc_sc[...] / l_sc[...]
            lse_ref[...] = m_sc[...] + jnp.log(l_sc[...])

    return _pcall(
        body, name="flash_fwd", grid=(heads, nq, nq),
        in_specs=[pl.BlockSpec((None, tq, dk), lambda h, i, j: (h, i, 0)),
                  pl.BlockSpec((None, tk, dk), lambda h, i, j: (h, jnp.minimum(i, j), 0)),
                  pl.BlockSpec((None, tk, V_DIM), lambda h, i, j: (h, jnp.minimum(i, j), 0))],
        out_specs=(pl.BlockSpec((tq, V_DIM), lambda h, i, j: (i, h)),
                   pl.BlockSpec((None, tq, 1), lambda h, i, j: (h, i, 0))),
        out_shape=(jax.ShapeDtypeStruct((s, heads * V_DIM), F32), jax.ShapeDtypeStruct((heads, s, 1), F32)),
        scratch_shapes=[pltpu.VMEM((tq, 1), F32), pltpu.VMEM((tq, 1), F32), pltpu.VMEM((tq, V_DIM), F32)],
        compiler_params=_params("parallel", "parallel", "arbitrary"),
    )(qp, kp, v)


def _flash_probs(q_ref, k_ref, v_ref, do_ref, o_ref, lse_ref, i, j, tq, tk, scale):
    sc = _masked_scores(q_ref[...], k_ref[...], i, j, tq, tk, scale)
    p = jnp.exp(sc - lse_ref[...])
    do = do_ref[...]
    dp = lax.dot_general(do.astype(BF16), v_ref[...], (((1,), (1,)), ((), ())), preferred_element_type=F32)
    delta = jnp.sum(do * o_ref[...], axis=-1, keepdims=True)
    ds = p * (dp - delta) * scale
    return p, ds, do


def _flash_bwd_kv(qp, kp, v, dcat, o, lse, *, scale, do_col0, tq=512):
    heads, s, dk = qp.shape
    tq = _tile(s, tq, 8)
    tk = tq
    nq = s // tq

    def body(q_ref, k_ref, v_ref, do_ref, o_ref, lse_ref, dk_ref, dv_ref, dk_sc, dv_sc):
        j = pl.program_id(1)
        i = pl.program_id(2)

        @pl.when(i == 0)
        def _():
            dk_sc[...] = jnp.zeros_like(dk_sc)
            dv_sc[...] = jnp.zeros_like(dv_sc)

        @pl.when(i >= j)
        def _():
            p, ds, do = _flash_probs(q_ref, k_ref, v_ref, do_ref, o_ref, lse_ref, i, j, tq, tk, scale)
            dv_sc[...] += lax.dot_general(p.astype(BF16), do.astype(BF16), (((0,), (0,)), ((), ())),
                                          preferred_element_type=F32)
            dk_sc[...] += lax.dot_general(ds.astype(BF16), q_ref[...], (((0,), (0,)), ((), ())),
                                          preferred_element_type=F32)

        @pl.when(i == nq - 1)
        def _():
            dk_ref[...] = dk_sc[...]
            dv_ref[...] = dv_sc[...]

    qrow = lambda h, j, i: (h, jnp.maximum(i, j), 0)
    return _pcall(
        body, name="flash_bwd_kv", grid=(heads, nq, nq),
        in_specs=[pl.BlockSpec((None, tq, dk), qrow),
                  pl.BlockSpec((None, tk, dk), lambda h, j, i: (h, j, 0)),
                  pl.BlockSpec((None, tk, V_DIM), lambda h, j, i: (h, j, 0)),
                  pl.BlockSpec((tq, V_DIM), lambda h, j, i: (jnp.maximum(i, j), do_col0 + h)),
                  pl.BlockSpec((tq, V_DIM), lambda h, j, i: (jnp.maximum(i, j), h)),
                  pl.BlockSpec((None, tq, 1), qrow)],
        out_specs=(pl.BlockSpec((None, tk, dk), lambda h, j, i: (h, j, 0)),
                   pl.BlockSpec((None, tk, V_DIM), lambda h, j, i: (h, j, 0))),
        out_shape=(jax.ShapeDtypeStruct((heads, s, dk), F32), jax.ShapeDtypeStruct((heads, s, V_DIM), F32)),
        scratch_shapes=[pltpu.VMEM((tk, dk), F32), pltpu.VMEM((tk, V_DIM), F32)],
        compiler_params=_params("parallel", "parallel", "arbitrary"),
    )(qp, kp, v, dcat, o, lse)


def _flash_bwd_q(qp, kp, v, dcat, o, lse, *, scale, do_col0, tq=512):
    heads, s, dk = qp.shape
    tq = _tile(s, tq, 8)
    tk = tq
    nq = s // tq

    def body(q_ref, k_ref, v_ref, do_ref, o_ref, lse_ref, dq_ref, dq_sc):
        i = pl.program_id(1)
        j = pl.program_id(2)

        @pl.when(j == 0)
        def _():
            dq_sc[...] = jnp.zeros_like(dq_sc)

        @pl.when(j <= i)
        def _():
            _, ds, _ = _flash_probs(q_ref, k_ref, v_ref, do_ref, o_ref, lse_ref, i, j, tq, tk, scale)
            dq_sc[...] += jnp.dot(ds.astype(BF16), k_ref[...], preferred_element_type=F32)

        @pl.when(j == nq - 1)
        def _():
            dq_ref[...] = dq_sc[...]

    kcol = lambda h, i, j: (h, jnp.minimum(i, j), 0)
    return _pcall(
        body, name="flash_bwd_q", grid=(heads, nq, nq),
        in_specs=[pl.BlockSpec((None, tq, dk), lambda h, i, j: (h, i, 0)),
                  pl.BlockSpec((None, tk, dk), kcol),
                  pl.BlockSpec((None, tk, V_DIM), kcol),
                  pl.BlockSpec((tq, V_DIM), lambda h, i, j: (i, do_col0 + h)),
                  pl.BlockSpec((tq, V_DIM), lambda h, i, j: (i, h)),
                  pl.BlockSpec((None, tq, 1), lambda h, i, j: (h, i, 0))],
        out_specs=pl.BlockSpec((None, tq, dk), lambda h, i, j: (h, i, 0)),
        out_shape=jax.ShapeDtypeStruct((heads, s, dk), F32),
        scratch_shapes=[pltpu.VMEM((tq, dk), F32)],
        compiler_params=_params("parallel", "parallel", "arbitrary"),
    )(qp, kp, v, dcat, o, lse)


def _attn_bwd_post(dqp, dkp, dv, cos_t, sin_t):
    heads, s, _ = dqp.shape
    ts = _tile(s, 256, 8)

    def body(dq_ref, dk_ref, dv_ref, cos_ref, sin_ref, q_out, kv_out, kr_out):
        lane = lax.broadcasted_iota(jnp.int32, (ts, LANES), 1)
        cos_v = cos_ref[...]
        sin_v = sin_ref[...]

        def rope_t(dy):
            return dy * cos_v + _swap_halves(dy * sin_v, lane)

        kr_sum = jnp.zeros((ts, LANES), F32)
        for h in range(heads):
            q_out[:, h * LANES:(h + 1) * LANES] = dq_ref[h, :, :LANES].astype(BF16)
            kv_out[:, h * LANES:(h + 1) * LANES] = dk_ref[h, :, :LANES].astype(BF16)
            kv_out[:, (heads + h) * LANES:(heads + h + 1) * LANES] = dv_ref[h].astype(BF16)
            kr_sum = kr_sum + dk_ref[h, :, LANES:]
        for pair in range(heads // 2):
            r = jnp.where(lane < ROPE_DIM, dq_ref[2 * pair, :, LANES:], dq_ref[2 * pair + 1, :, LANES:])
            q_out[:, (heads + pair) * LANES:(heads + pair + 1) * LANES] = rope_t(r).astype(BF16)
        kr = rope_t(kr_sum)
        kr_out[...] = (kr + pltpu.roll(kr, ROPE_DIM, 1)).astype(BF16)

    wq = heads * (NOPE_DIM + ROPE_DIM)
    wkv = heads * (NOPE_DIM + V_DIM)
    tab = pl.BlockSpec((ts, LANES), lambda i: (i, 0))
    return _pcall(
        body, name="attn_bwd_post", grid=(s // ts,),
        in_specs=[pl.BlockSpec((heads, ts, 2 * LANES), lambda i: (0, i, 0)),
                  pl.BlockSpec((heads, ts, 2 * LANES), lambda i: (0, i, 0)),
                  pl.BlockSpec((heads, ts, LANES), lambda i: (0, i, 0)), tab, tab],
        out_specs=(pl.BlockSpec((ts, wq), lambda i: (i, 0)), pl.BlockSpec((ts, wkv), lambda i: (i, 0)), tab),
        out_shape=(jax.ShapeDtypeStruct((s, wq), BF16), jax.ShapeDtypeStruct((s, wkv), BF16),
                   jax.ShapeDtypeStruct((s, LANES), BF16)),
        compiler_params=_params("parallel"),
    )(dqp, dkp, dv, cos_t, sin_t)


CONV_HALO = 8


def _conv_gate(xe, cw_ref, cb_ref):
    return (cw_ref[2:3, :] * xe + cw_ref[1:2, :] * pltpu.roll(xe, 1, 0) + cw_ref[0:1, :] * pltpu.roll(xe, 2, 0)
            + cb_ref[...])


def _ffn_act_fwd(gu, conv_w, conv_b, *, d_ff, tc=256):
    s = gu.shape[0]
    tc = _tile(d_ff, tc)
    nf = d_ff // tc
    tr = _tile(s, ROW_CHUNK, CONV_HALO)
    n_chunks = s // tr

    def body(g_ref, u_ref, cw_ref, cb_ref, a_ref):
        r = pl.program_id(1)
        xe = _rows_with_halo(g_ref, r, tr, n_chunks, CONV_HALO, 0)
        gc = _conv_gate(xe, cw_ref, cb_ref)[CONV_HALO:]
        a_ref[...] = (gc * jax.nn.sigmoid(gc) * u_ref[...]).astype(BF16)

    return _pcall(
        body, name="ffn_act_fwd", grid=(nf, n_chunks),
        in_specs=[pl.BlockSpec((s, tc), lambda j, r: (0, j)), pl.BlockSpec((tr, tc), lambda j, r: (r, nf + j)),
                  pl.BlockSpec((CONV_TAPS, tc), lambda j, r: (0, j)), pl.BlockSpec((1, tc), lambda j, r: (0, j))],
        out_specs=pl.BlockSpec((tr, tc), lambda j, r: (r, j)),
        out_shape=jax.ShapeDtypeStruct((s, d_ff), BF16),
        compiler_params=_params("parallel", "arbitrary"),
    )(gu, gu, conv_w, conv_b.reshape(1, d_ff))


def _ffn_act_bwd(da, gu, conv_w, conv_b, *, d_ff, tc=256):
    s = gu.shape[0]
    tc = _tile(d_ff, tc)
    nf = d_ff // tc
    tr = _tile(s, ROW_CHUNK, CONV_HALO)
    n_chunks = s // tr
    rows = tr + 2 * CONV_HALO
    main = slice(CONV_HALO, CONV_HALO + tr)

    def body(da_ref, g_ref, u_ref, cw_ref, cb_ref, dg_ref, du_ref, dcw_ref, dcb_ref):
        r = pl.program_id(1)
        xe = _rows_with_halo(g_ref, r, tr, n_chunks, CONV_HALO, CONV_HALO)
        dae = _rows_with_halo(da_ref, r, tr, n_chunks, CONV_HALO, CONV_HALO)
        ue = _rows_with_halo(u_ref, r, tr, n_chunks, CONV_HALO, CONV_HALO)
        gc = _conv_gate(xe, cw_ref, cb_ref)
        sg = jax.nn.sigmoid(gc)
        du_ref[...] = (dae * gc * sg)[main].astype(BF16)
        dgc = dae * ue * sg * (1.0 + gc * (1.0 - sg))
        dg = (cw_ref[2:3, :] * dgc + cw_ref[1:2, :] * pltpu.roll(dgc, rows - 1, 0)
              + cw_ref[0:1, :] * pltpu.roll(dgc, rows - 2, 0))
        dg_ref[...] = dg[main].astype(BF16)
        dgc_m = dgc[main]
        dcb = jnp.sum(dgc_m, axis=0, keepdims=True)
        dcw = jnp.concatenate([jnp.sum(dgc_m * pltpu.roll(xe, 2, 0)[main], axis=0, keepdims=True),
                               jnp.sum(dgc_m * pltpu.roll(xe, 1, 0)[main], axis=0, keepdims=True),
                               jnp.sum(dgc_m * xe[main], axis=0, keepdims=True)], axis=0)

        @pl.when(r == 0)
        def _():
            dcb_ref[...] = dcb
            dcw_ref[...] = dcw

        @pl.when(r > 0)
        def _():
            dcb_ref[...] += dcb
            dcw_ref[...] += dcw

    col = pl.BlockSpec((s, tc), lambda j, r: (0, j))
    out = pl.BlockSpec((tr, tc), lambda j, r: (r, j))
    return _pcall(
        body, name="ffn_act_bwd", grid=(nf, n_chunks),
        in_specs=[col, col, pl.BlockSpec((s, tc), lambda j, r: (0, nf + j)),
                  pl.BlockSpec((CONV_TAPS, tc), lambda j, r: (0, j)), pl.BlockSpec((1, tc), lambda j, r: (0, j))],
        out_specs=(out, out, pl.BlockSpec((CONV_TAPS, tc), lambda j, r: (0, j)),
                   pl.BlockSpec((1, tc), lambda j, r: (0, j))),
        out_shape=(jax.ShapeDtypeStruct((s, d_ff), BF16), jax.ShapeDtypeStruct((s, d_ff), BF16),
                   jax.ShapeDtypeStruct((CONV_TAPS, d_ff), F32), jax.ShapeDtypeStruct((1, d_ff), F32)),
        compiler_params=_params("parallel", "arbitrary"),
    )(da, gu, gu, conv_w, conv_b.reshape(1, d_ff))


def _ple_fwd(h, z, e):
    s, d = h.shape
    ts = _tile(s, 512, 8)

    def body(h_ref, z_ref, e_ref, o_ref):
        o_ref[...] = h_ref[...] + e_ref[...] * jax.nn.sigmoid(z_ref[...])

    row = pl.BlockSpec((ts, d), lambda i: (i, 0))
    return _pcall(body, name="ple_fwd", grid=(s // ts,), in_specs=[row, row, row], out_specs=row,
                  out_shape=jax.ShapeDtypeStruct((s, d), F32), compiler_params=_params("parallel"))(h, z, e)


def _ple_bwd(dh, z, e):
    s, d = dh.shape
    ts = _tile(s, 512, 8)

    def body(dh_ref, z_ref, e_ref, de_ref, dz_ref):
        gt = jax.nn.sigmoid(z_ref[...])
        dhv = dh_ref[...]
        de_ref[...] = (dhv * gt).astype(BF16)
        dz_ref[...] = (dhv * e_ref[...] * gt * (1.0 - gt)).astype(BF16)

    row = pl.BlockSpec((ts, d), lambda i: (i, 0))
    return _pcall(body, name="ple_bwd", grid=(s // ts,), in_specs=[row, row, row], out_specs=(row, row),
                  out_shape=(jax.ShapeDtypeStruct((s, d), BF16), jax.ShapeDtypeStruct((s, d), BF16)),
                  compiler_params=_params("parallel"))(dh, z, e)


def _loss_head(h, g, target):
    s, d = h.shape
    ts = _tile(s, 512, 8)

    def body(h_ref, g_ref, t_ref, loss_ref, dh_ref, dg_ref):
        i = pl.program_id(0)
        x = h_ref[...]
        gv = g_ref[...]
        r = lax.rsqrt(jnp.mean(x * x, axis=-1, keepdims=True) + NORM_EPS)
        nh = x * r
        err = nh * gv - t_ref[...]
        part_loss = 0.5 * jnp.sum(jnp.mean(err * err, axis=-1, keepdims=True), axis=0, keepdims=True)
        dy = err * (1.0 / d)
        gd = dy * gv
        dh_ref[...] = (gd - nh * jnp.mean(gd * nh, axis=-1, keepdims=True)) * r
        part_g = jnp.sum(dy * nh, axis=0, keepdims=True)
        part_l = jnp.broadcast_to(part_loss, (1, LANES))

        @pl.when(i == 0)
        def _():
            dg_ref[...] = part_g
            loss_ref[...] = part_l

        @pl.when(i > 0)
        def _():
            dg_ref[...] += part_g
            loss_ref[...] += part_l

    row = pl.BlockSpec((ts, d), lambda i: (i, 0))
    vec = pl.BlockSpec((1, d), lambda i: (0, 0))
    return _pcall(
        body, name="loss_head", grid=(s // ts,), in_specs=[row, vec, row],
        out_specs=(pl.BlockSpec((1, LANES), lambda i: (0, 0)), row, vec),
        out_shape=(jax.ShapeDtypeStruct((1, LANES), F32), jax.ShapeDtypeStruct((s, d), F32),
                   jax.ShapeDtypeStruct((1, d), F32)),
        compiler_params=_params("arbitrary"),
    )(h, g.reshape(1, d), target)


HBM_SPEC = pl.BlockSpec(memory_space=pl.ANY)


def _flat_index(px, py, pc):
    return 4 * px + 2 * py + pc


def _all_gather(shards, *, name):
    n = len(shards)

    def body(*refs):
        ins, outs = refs[:n], refs[n:2 * n]
        send_sems, recv_sems, local_sems = refs[2 * n:]
        x, y, c = lax.axis_index("x"), lax.axis_index("y"), lax.axis_index("c")
        me, sibling = (x, y, c), (x, y, 1 - c)
        chips = [(1 - x, y), (x, 1 - y), (1 - x, 1 - y)]

        def copy(a, k, block, to, src=None):
            slot = outs[a].at[_flat_index(*block)]
            return pltpu.make_async_remote_copy(
                src_ref=slot if src is None else src, dst_ref=slot,
                send_sem=send_sems.at[a, k], recv_sem=recv_sems.at[a, k],
                device_id=to, device_id_type=pl.DeviceIdType.MESH)

        mine, first, passed = [], [], []
        for a in range(n):
            cp = pltpu.make_async_copy(ins[a], outs[a].at[_flat_index(*me)], local_sems.at[a])
            cp.start()
            mine.append(cp)
            first.append(copy(a, 0, me, sibling, src=ins[a]))
            first += [copy(a, 1 + j, me, (*chip, c), src=ins[a]) for j, chip in enumerate(chips)]
        for cp in first:
            cp.start()
        for j, chip in enumerate(chips):
            for a in range(n):
                copy(a, 1 + j, (*chip, c), me).wait_recv()
                fwd = copy(a, 4 + j, (*chip, c), sibling)
                fwd.start()
                passed.append(fwd)
        for a in range(n):
            copy(a, 0, sibling, me).wait_recv()
            for j, chip in enumerate(chips):
                copy(a, 4 + j, (*chip, 1 - c), me).wait_recv()
        for cp in first + passed:
            cp.wait_send()
        for cp in mine:
            cp.wait()

    return _pcall(
        body, name=name,
        in_specs=[HBM_SPEC] * n, out_specs=[HBM_SPEC] * n,
        out_shape=[jax.ShapeDtypeStruct((N_DEV,) + a.shape, a.dtype) for a in shards],
        scratch_shapes=[pltpu.SemaphoreType.DMA((n, 7)), pltpu.SemaphoreType.DMA((n, 7)),
                        pltpu.SemaphoreType.DMA((n,))],
    )(*shards)


def _reduce_scatter_exchange(blocks, *, name):
    n = len(blocks)

    def body(*refs):
        ins, outs = refs[:n], refs[n:2 * n]
        send_sems, recv_sems, local_sems = refs[2 * n:]
        x, y, c = lax.axis_index("x"), lax.axis_index("y"), lax.axis_index("c")
        me = _flat_index(x, y, c)
        peers = []
        for fx, fy, fc in ((0, 0, 1), (1, 0, 0), (0, 1, 0), (1, 1, 0), (1, 0, 1), (0, 1, 1), (1, 1, 1)):
            peers.append((x ^ fx, y ^ fy, c ^ fc))

        def copy(a, k):
            peer = peers[k]
            return pltpu.make_async_remote_copy(
                src_ref=ins[a].at[_flat_index(*peer)], dst_ref=outs[a].at[me],
                send_sem=send_sems.at[a, k], recv_sem=recv_sems.at[a, k],
                device_id=peer, device_id_type=pl.DeviceIdType.MESH)

        local, remote = [], []
        for a in range(n):
            cp = pltpu.make_async_copy(ins[a].at[me], outs[a].at[me], local_sems.at[a])
            cp.start()
            local.append(cp)
        for k in range(7):
            for a in range(n):
                cp = copy(a, k)
                cp.start()
                remote.append((a, k, cp))
        for a, k, cp in remote:
            peer = peers[k]
            pltpu.make_async_remote_copy(
                src_ref=ins[a].at[me], dst_ref=outs[a].at[_flat_index(*peer)],
                send_sem=send_sems.at[a, k], recv_sem=recv_sems.at[a, k],
                device_id=peer, device_id_type=pl.DeviceIdType.MESH).wait_recv()
        for a, k, cp in remote:
            cp.wait_send()
        for cp in local:
            cp.wait()

    return _pcall(
        body, name=name,
        in_specs=[HBM_SPEC] * n, out_specs=[HBM_SPEC] * n,
        out_shape=[jax.ShapeDtypeStruct(a.shape, a.dtype) for a in blocks],
        scratch_shapes=[pltpu.SemaphoreType.DMA((n, 7)), pltpu.SemaphoreType.DMA((n, 7)),
                        pltpu.SemaphoreType.DMA((n,))],
    )(*blocks)


def _adam_math(g, w, m, v):
    m = ADAM_B1 * m + (1.0 - ADAM_B1) * g
    v = ADAM_B2 * v + (1.0 - ADAM_B2) * jnp.square(g)
    m_hat = m / (1.0 - ADAM_B1 ** ADAM_STEP)
    v_hat = v / (1.0 - ADAM_B2 ** ADAM_STEP)
    delta = -ADAM_LR * (m_hat / (jnp.sqrt(v_hat) + ADAM_EPS) + ADAM_WD * w)
    return delta, m, v


def _adamw(contrib, w, m, v, *, name):
    _, layers, r, c = contrib.shape
    tr = _tile(r, max(8, (256 * 1024 // c) // 8 * 8), 8)

    def body(g_ref, w_ref, m_ref, v_ref, go_ref, d_ref, mo_ref, vo_ref):
        g = g_ref[0].astype(F32)
        for k in range(1, N_DEV):
            g = g + g_ref[k].astype(F32)
        delta, m_new, v_new = _adam_math(g, w_ref[...], m_ref[...], v_ref[...])
        go_ref[...] = g
        d_ref[...] = delta
        mo_ref[...] = m_new
        vo_ref[...] = v_new

    blk = pl.BlockSpec((None, tr, c), lambda l, i: (l, i, 0))
    out = jax.ShapeDtypeStruct((layers, r, c), F32)
    return _pcall(
        body, name=name, grid=(layers, r // tr),
        in_specs=[pl.BlockSpec((N_DEV, None, tr, c), lambda l, i: (0, l, i, 0)), blk, blk, blk],
        out_specs=(blk, blk, blk, blk), out_shape=(out, out, out, out),
        compiler_params=_params("parallel", "parallel"),
    )(contrib, w, m, v)


def _heads_split(w, heads, first, second):
    k = w.shape[0]
    w3 = w.reshape(k, heads, first + second)
    return jnp.concatenate([w3[:, :, :first].reshape(k, heads * first),
                            w3[:, :, first:].reshape(k, heads * second)], axis=1)


def _heads_join(w, heads, first, second):
    k = w.shape[0]
    a = w[:, :heads * first].reshape(k, heads, first)
    b = w[:, heads * first:].reshape(k, heads, second)
    return jnp.concatenate([a, b], axis=2).reshape(k, heads * (first + second))


def _full_from_gathered(kind, g):
    if kind == "col":
        return jnp.transpose(g, (1, 0, 2)).reshape(g.shape[1], N_DEV * g.shape[2])
    if kind == "row":
        return g.reshape(N_DEV * g.shape[1], g.shape[2])
    return jnp.transpose(g, (1, 0, 2, 3)).reshape(g.shape[1], N_DEV * g.shape[2], g.shape[3])


def _blocks_from_full(kind, f):
    if kind == "col":
        k, n = f.shape
        return jnp.transpose(f.reshape(k, N_DEV, n // N_DEV), (1, 0, 2))
    if kind == "row":
        k, n = f.shape
        return f.reshape(N_DEV, k // N_DEV, n)
    g, c_in, c = f.shape
    return jnp.transpose(f.reshape(g, N_DEV, c_in // N_DEV, c), (1, 0, 2, 3))


def _rope_tables(positions):
    inv_freq = 1.0 / (ROPE_THETA ** (jnp.arange(0, ROPE_DIM, 2, dtype=F32) / ROPE_DIM))
    ang = positions.astype(F32)[:, None] * inv_freq
    cos, sin = jnp.cos(ang), jnp.sin(ang)
    return jnp.concatenate([cos, cos, cos, cos], axis=-1), jnp.concatenate([-sin, sin, -sin, sin], axis=-1)


def _layer_fwd(h0, p_i, w, rep, tabs, dims):
    heads, d_pool, q_lora, d_ff = dims["heads"], dims["d_pool"], dims["q_lora"], dims["d_ff"]
    c = d_pool // POOL_GROUPS
    cos_t, sin_t = tabs
    scale = 1.0 / math.sqrt(NOPE_DIM + ROPE_DIM)
    n1 = _rms_fwd(h0, rep["norm_mix_g"], name="rms_mix_fwd")
    u = _mm(n1, w["w_in"], name="mm_in_fwd")
    y_pool, diff = _pool_fwd(u, w["pool_w"], rep["pool_scale"], c=c)
    nq = _rms_fwd(u, rep["q_norm_g"], name="rms_q_fwd", col_block=d_pool // q_lora)
    nkv = _rms_fwd(u, rep["kv_norm_g"], name="rms_kv_fwd", col_block=d_pool // q_lora + 1)
    q = _mm(nq, w["w_uq"], name="mm_uq_fwd")
    kv = _mm(nkv, w["w_ukv"], name="mm_ukv_fwd")
    kr = u[:, d_pool + 2 * q_lora:]
    kr2 = jnp.concatenate([kr, kr], axis=-1)
    qp, kp, v = _qkv_prep(q, kv, kr2, cos_t, sin_t, heads=heads)
    o, lse = _flash_fwd(qp, kp, v, scale=scale)
    t = _mm(y_pool, w["w_out"], name="mm_out_pool_fwd", add=h0, b_row=(0, d_pool))
    h1 = _mm(o, w["w_out"], name="mm_out_att_fwd", add=t, b_row=(d_pool, None))
    n2 = _rms_fwd(h1, rep["norm_ffn_g"], name="rms_ffn_fwd")
    gu = _mm(n2, w["w_up"], name="mm_up_fwd")
    a = _ffn_act_fwd(gu, w["conv_w"], rep["conv_b"], d_ff=d_ff)
    h2 = _mm(a, w["w_down"], name="mm_down_fwd", add=h1)
    n3 = _rms_fwd(h2, rep["norm_ple_g"], name="rms_ple_fwd")
    z = _mm(n3, w["w_ple_gate"], name="mm_pgate_fwd")
    e = _mm(p_i, w["w_ple"], name="mm_ple_fwd")
    h3 = _ple_fwd(h2, z, e)
    saved = dict(h0=h0, n1=n1, u=u, y_pool=y_pool, diff=diff, nq=nq, nkv=nkv, qp=qp, kp=kp, v=v, o=o, lse=lse,
                 h1=h1, n2=n2, gu=gu, a=a, h2=h2, n3=n3, z=z, e=e)
    return h3, saved


def _layer_bwd(dh3, p_i, w, rep, tabs, dims, sv):
    heads, d_pool, q_lora, d_ff = dims["heads"], dims["d_pool"], dims["q_lora"], dims["d_ff"]
    c = d_pool // POOL_GROUPS
    cos_t, sin_t = tabs
    scale = 1.0 / math.sqrt(NOPE_DIM + ROPE_DIM)
    gr = {}
    de, dz = _ple_bwd(dh3, sv["z"], sv["e"])
    gr["w_ple"] = _mm(p_i, de, name="mm_ple_dw", ta=True, out_dtype=BF16)
    gr["w_ple_gate"] = _mm(sv["n3"], dz, name="mm_pgate_dw", ta=True, out_dtype=BF16)
    dn3 = _mm(dz, w["w_ple_gate"], name="mm_pgate_dx", tb=True)
    dh2, gr["norm_ple_g"] = _rms_bwd(dn3, sv["h2"], rep["norm_ple_g"], name="rms_ple_bwd", res=dh3)
    gr["w_down"] = _mm(sv["a"], dh2, name="mm_down_dw", ta=True, out_dtype=BF16)
    da = _mm(dh2, w["w_down"], name="mm_down_dx", tb=True)
    dgate, dup, gr["conv_w"], gr["conv_b"] = _ffn_act_bwd(da, sv["gu"], w["conv_w"], rep["conv_b"], d_ff=d_ff)
    gr["w_up"] = jnp.concatenate([_mm(sv["n2"], dgate, name="mm_up_gate_dw", ta=True, out_dtype=BF16),
                                  _mm(sv["n2"], dup, name="mm_up_up_dw", ta=True, out_dtype=BF16)], axis=1)
    dn2 = _mm(dgate, w["w_up"], name="mm_up_gate_dx", tb=True, b_row=(0, d_ff))
    dn2 = _mm(dup, w["w_up"], name="mm_up_up_dx", tb=True, b_row=(d_ff, d_ff), add=dn2)
    dh1, gr["norm_ffn_g"] = _rms_bwd(dn2, sv["h1"], rep["norm_ffn_g"], name="rms_ffn_bwd", res=dh2)
    dw_out_pool = _mm(sv["y_pool"], dh1, name="mm_out_pool_dw", ta=True, out_dtype=BF16)
    dw_out_att = _mm(sv["o"], dh1, name="mm_out_att_dw", ta=True, out_dtype=BF16)
    gr["w_out"] = jnp.concatenate([dw_out_pool, dw_out_att], axis=0)
    dcat = _mm(dh1, w["w_out"], name="mm_out_dx", tb=True)
    do_col0 = d_pool // V_DIM
    dkp, dv = _flash_bwd_kv(sv["qp"], sv["kp"], sv["v"], dcat, sv["o"], sv["lse"], scale=scale, do_col0=do_col0)
    dqp = _flash_bwd_q(sv["qp"], sv["kp"], sv["v"], dcat, sv["o"], sv["lse"], scale=scale, do_col0=do_col0)
    dq, dkv, dkr2 = _attn_bwd_post(dqp, dkp, dv, cos_t, sin_t)
    gr["w_uq"] = _mm(sv["nq"], dq, name="mm_uq_dw", ta=True, out_dtype=BF16)
    gr["w_ukv"] = _mm(sv["nkv"], dkv, name="mm_ukv_dw", ta=True, out_dtype=BF16)
    dnq = _mm(dq, w["w_uq"], name="mm_uq_dx", tb=True)
    dnkv = _mm(dkv, w["w_ukv"], name="mm_ukv_dx", tb=True)
    dcq, gr["q_norm_g"] = _rms_bwd(dnq, sv["u"], rep["q_norm_g"], name="rms_q_bwd",
                                   col_block=d_pool // q_lora, out_dtype=BF16)
    dckv, gr["kv_norm_g"] = _rms_bwd(dnkv, sv["u"], rep["kv_norm_g"], name="rms_kv_bwd",
                                     col_block=d_pool // q_lora + 1, out_dtype=BF16)
    du_pool, gr["pool_w"], gr["pool_scale"] = _pool_bwd(dcat, sv["diff"], w["pool_w"], rep["pool_scale"], c=c)
    du = jnp.concatenate([du_pool, dcq, dckv, dkr2[:, :ROPE_DIM]], axis=-1)
    gr["w_in"] = _mm(sv["n1"], du, name="mm_in_dw", ta=True, out_dtype=BF16)
    dn1 = _mm(du, w["w_in"], name="mm_in_dx", tb=True)
    dh0, gr["norm_mix_g"] = _rms_bwd(dn1, sv["h0"], rep["norm_mix_g"], name="rms_mix_bwd", res=dh1)
    return dh0, gr


def _as2d(a):
    return a.reshape(a.shape[0], -1, a.shape[-1])


def kernel(x, p, positions, norm_mix_g, w_in, pool_w, pool_scale, q_norm_g, w_uq, kv_norm_g, w_ukv, w_out, norm_ffn_g, w_up, conv_w, conv_b, w_down, norm_ple_g, w_ple, w_ple_gate, final_norm_g, loss_target, m_norm_mix_g, m_w_in, m_pool_w, m_pool_scale, m_q_norm_g, m_w_uq, m_kv_norm_g, m_w_ukv, m_w_out, m_norm_ffn_g, m_w_up, m_conv_w, m_conv_b, m_w_down, m_norm_ple_g, m_w_ple, m_w_ple_gate, m_final_norm_g, v_norm_mix_g, v_w_in, v_pool_w, v_pool_scale, v_q_norm_g, v_w_uq, v_kv_norm_g, v_w_ukv, v_w_out, v_norm_ffn_g, v_w_up, v_conv_w, v_conv_b, v_w_down, v_norm_ple_g, v_w_ple, v_w_ple_gate, v_final_norm_g):
    weights = dict(norm_mix_g=norm_mix_g, w_in=w_in, pool_w=pool_w, pool_scale=pool_scale, q_norm_g=q_norm_g,
                   w_uq=w_uq, kv_norm_g=kv_norm_g, w_ukv=w_ukv, w_out=w_out, norm_ffn_g=norm_ffn_g, w_up=w_up,
                   conv_w=conv_w, conv_b=conv_b, w_down=w_down, norm_ple_g=norm_ple_g, w_ple=w_ple,
                   w_ple_gate=w_ple_gate, final_norm_g=final_norm_g)
    m_in = dict(norm_mix_g=m_norm_mix_g, w_in=m_w_in, pool_w=m_pool_w, pool_scale=m_pool_scale, q_norm_g=m_q_norm_g,
                w_uq=m_w_uq, kv_norm_g=m_kv_norm_g, w_ukv=m_w_ukv, w_out=m_w_out, norm_ffn_g=m_norm_ffn_g,
                w_up=m_w_up, conv_w=m_conv_w, conv_b=m_conv_b, w_down=m_w_down, norm_ple_g=m_norm_ple_g,
                w_ple=m_w_ple, w_ple_gate=m_w_ple_gate, final_norm_g=m_final_norm_g)
    v_in = dict(norm_mix_g=v_norm_mix_g, w_in=v_w_in, pool_w=v_pool_w, pool_scale=v_pool_scale, q_norm_g=v_q_norm_g,
                w_uq=v_w_uq, kv_norm_g=v_kv_norm_g, w_ukv=v_w_ukv, w_out=v_w_out, norm_ffn_g=v_norm_ffn_g,
                w_up=v_w_up, conv_w=v_conv_w, conv_b=v_conv_b, w_down=v_w_down, norm_ple_g=v_norm_ple_g,
                w_ple=v_w_ple, w_ple_gate=v_w_ple_gate, final_norm_g=v_final_norm_g)

    depth = w_in.shape[0]
    s, d_model = x.shape[1], x.shape[2]
    d_pool = pool_scale.shape[-1]
    q_lora = q_norm_g.shape[-1]
    d_ff = conv_b.shape[-1]
    heads = (w_uq.shape[-1] * N_DEV) // (NOPE_DIM + ROPE_DIM)
    dims = dict(heads=heads, d_pool=d_pool, q_lora=q_lora, d_ff=d_ff)

    gathered = _all_gather([weights[n] if n == "conv_w" else weights[n].astype(BF16) for n in SHARDED],
                           name="weights_all_gather")
    gathered = dict(zip(SHARDED, gathered))

    def layer_weights(i):
        w = {n: _full_from_gathered(SHARD_KIND[n], gathered[n][:, i]) for n in SHARDED}
        w["w_uq"] = _heads_split(w["w_uq"], heads, NOPE_DIM, ROPE_DIM)
        w["w_ukv"] = _heads_split(w["w_ukv"], heads, NOPE_DIM, V_DIM)
        rep = {n: weights[n][i] for n in REPLICATED}
        return w, rep

    tabs = _rope_tables(positions[0])

    h = x[0]
    saved = []
    for i in range(depth):
        w, rep = layer_weights(i)
        h, sv = _layer_fwd(h, p[i, 0], w, rep, tabs, dims)
        saved.append(sv)
    loss_row, dh, g_final = _loss_head(h, final_norm_g, loss_target[0])
    loss = lax.psum(loss_row[0, 0], MESH_AXES)

    layer_grads = [None] * depth
    for i in reversed(range(depth)):
        w, rep = layer_weights(i)
        dh, gr = _layer_bwd(dh, p[i, 0], w, rep, tabs, dims, saved[i])
        gr["w_uq"] = _heads_join(gr["w_uq"], heads, NOPE_DIM, ROPE_DIM)
        gr["w_ukv"] = _heads_join(gr["w_ukv"], heads, NOPE_DIM, V_DIM)
        layer_grads[i] = gr
    grad_x = dh[None]

    blocks = []
    for n in SHARDED:
        per_layer = [_blocks_from_full(SHARD_KIND[n], layer_grads[i][n]).astype(BF16) for i in range(depth)]
        blocks.append(jnp.stack(per_layer, axis=1))
    received = _reduce_scatter_exchange(blocks, name="grads_reduce_scatter")
    out = {}
    for n, rec in zip(SHARDED, received):
        shape = weights[n].shape
        rec2 = rec.reshape((N_DEV, depth, -1, shape[-1]))
        res = _adamw(rec2, _as2d(weights[n]), _as2d(m_in[n]), _as2d(v_in[n]), name="adamw_" + n)
        out[n] = tuple(r.reshape(shape) for r in res)

    small_names = REPLICATED + ("final_norm_g",)

    def pack(get):
        rows = []
        for n in REPLICATED:
            for i in range(depth):
                rows.append(get(n, i).reshape(-1))
        rows.append(get("final_norm_g", None).reshape(-1))
        flat = jnp.concatenate(rows)
        return flat.reshape(1, -1, LANES)

    g_small = pack(lambda n, i: g_final if i is None else layer_grads[i][n])
    w_small = pack(lambda n, i: weights[n] if i is None else weights[n][i])
    m_small = pack(lambda n, i: m_in[n] if i is None else m_in[n][i])
    v_small = pack(lambda n, i: v_in[n] if i is None else v_in[n][i])
    (g_all,) = _all_gather([g_small], name="small_grads_all_gather")
    res_small = _adamw(g_all, w_small, m_small, v_small, name="adamw_small")

    def unpack(flat3):
        flat = flat3.reshape(-1)
        res, off = {}, 0
        for n in REPLICATED:
            width = weights[n].shape[-1]
            res[n] = flat[off:off + depth * width].reshape(depth, width)
            off += depth * width
        res["final_norm_g"] = flat[off:off + d_model]
        return res

    small = [unpack(r) for r in res_small]
    for n in small_names:
        out[n] = tuple(small[k][n] for k in range(4))

    outs = [loss, grad_x]
    for k in range(4):
        outs += [out[n][k] for n in WEIGHT_ORDER]
    return tuple(outs)
```

```python
import functools
import math

import jax
import jax.numpy as jnp
from jax import lax
from jax.experimental import pallas as pl
from jax.experimental.pallas import tpu as pltpu

F32 = jnp.float32
BF16 = jnp.bfloat16

N_DEV = 8
MESH_AXES = ("x", "y", "c")
NOPE_DIM = 128
ROPE_DIM = 64
V_DIM = 128
POOL_GROUPS = 4
CONV_TAPS = 3
ROPE_THETA = 10000.0
NORM_EPS = 1e-6
ADAM_LR = 0.001
ADAM_B1 = 0.9
ADAM_B2 = 0.999
ADAM_EPS = 1e-08
ADAM_WD = 0.01
ADAM_STEP = 10
LANES = 128
VMEM_LIMIT_BYTES = 56 * 1024 * 1024

SHARDED = ("w_in", "pool_w", "w_uq", "w_ukv", "w_out", "w_up", "conv_w", "w_down", "w_ple", "w_ple_gate")
SHARD_KIND = {"w_in": "col", "pool_w": "pool", "w_uq": "col", "w_ukv": "col", "w_out": "row", "w_up": "col",
              "conv_w": "col", "w_down": "row", "w_ple": "col", "w_ple_gate": "row"}
REPLICATED = ("norm_mix_g", "pool_scale", "q_norm_g", "kv_norm_g", "norm_ffn_g", "conv_b", "norm_ple_g")
WEIGHT_ORDER = ("norm_mix_g", "w_in", "pool_w", "pool_scale", "q_norm_g", "w_uq", "kv_norm_g", "w_ukv", "w_out",
                "norm_ffn_g", "w_up", "conv_w", "conv_b", "w_down", "norm_ple_g", "w_ple", "w_ple_gate",
                "final_norm_g")

_pcall = pl.pallas_call


def _params(*sem):
    return pltpu.CompilerParams(dimension_semantics=sem or None, vmem_limit_bytes=VMEM_LIMIT_BYTES)


def _tile(n, pref, mult=LANES):
    if n <= pref:
        return n
    t = (pref // mult) * mult
    while t >= mult:
        if n % t == 0:
            return t
        t -= mult
    return n


def _mm(a, b, *, name, ta=False, tb=False, add=None, out_dtype=F32, a_col=(0, None), b_row=(0, None),
        b_col=(0, None), tm=1024, tn=1024, tk=512):
    if ta:
        k_a, m = a.shape
    else:
        m, k_a = a.shape
    k_off_a, k_sz = a_col if not ta else (0, None)
    if k_sz is None:
        k_sz = k_a - k_off_a
    kdim = k_sz
    if tb:
        n_b, k_b = b.shape
    else:
        k_b, n_b = b.shape
    k_off_b, kb_sz = b_row
    if kb_sz is None:
        kb_sz = k_b - k_off_b
    assert kb_sz == kdim, (name, kb_sz, kdim)
    n_off, n = b_col
    if n is None:
        n = n_b - n_off
    tm = _tile(m, tm)
    tn = _tile(n, tn)
    tk = _tile(kdim, tk)
    assert k_off_a % tk == 0 and k_off_b % tk == 0 and n_off % tn == 0, name
    ka0, kb0, n0 = k_off_a // tk, k_off_b // tk, n_off // tn
    nk = kdim // tk
    dims = (((0 if ta else 1,), (1 if tb else 0,)), ((), ()))

    def body(*refs):
        if add is None:
            a_ref, b_ref, o_ref, acc = refs
            add_ref = None
        else:
            a_ref, b_ref, add_ref, o_ref, acc = refs
        k = pl.program_id(2)

        @pl.when(k == 0)
        def _():
            acc[...] = jnp.zeros_like(acc)

        acc[...] += lax.dot_general(a_ref[...].astype(BF16), b_ref[...].astype(BF16), dims,
                                    preferred_element_type=F32)

        @pl.when(k == nk - 1)
        def _():
            r = acc[...]
            if add_ref is not None:
                r = r + add_ref[...].astype(F32)
            o_ref[...] = r.astype(out_dtype)

    if ta:
        a_spec = pl.BlockSpec((tk, tm), lambda i, j, k: (k, i))
    else:
        a_spec = pl.BlockSpec((tm, tk), lambda i, j, k: (i, k + ka0))
    if tb:
        b_spec = pl.BlockSpec((tn, tk), lambda i, j, k: (j + n0, k + kb0))
    else:
        b_spec = pl.BlockSpec((tk, tn), lambda i, j, k: (k + kb0, j + n0))
    in_specs = [a_spec, b_spec]
    args = [a, b]
    if add is not None:
        in_specs.append(pl.BlockSpec((tm, tn), lambda i, j, k: (i, j)))
        args.append(add)
    return _pcall(
        body, name=name, grid=(m // tm, n // tn, nk), in_specs=in_specs,
        out_specs=pl.BlockSpec((tm, tn), lambda i, j, k: (i, j)),
        out_shape=jax.ShapeDtypeStruct((m, n), out_dtype),
        scratch_shapes=[pltpu.VMEM((tm, tn), F32)],
        compiler_params=_params("parallel", "parallel", "arbitrary"),
    )(*args)


def _rms_fwd(h, g, *, name, col_block=0, dep=None):
    s = h.shape[0]
    d = g.shape[-1]
    ts = _tile(s, 512, 8)

    def body(h_ref, g_ref, *rest):
        n_ref = rest[-1]
        x = h_ref[...]
        r = lax.rsqrt(jnp.mean(x * x, axis=-1, keepdims=True) + NORM_EPS)
        n_ref[...] = (x * r * g_ref[...]).astype(BF16)

    deps = [] if dep is None else [dep]
    return _pcall(
        body, name=name, grid=(s // ts,),
        in_specs=[pl.BlockSpec((ts, d), lambda i: (i, col_block)), pl.BlockSpec((1, d), lambda i: (0, 0))]
        + [HBM_SPEC] * len(deps),
        out_specs=pl.BlockSpec((ts, d), lambda i: (i, 0)),
        out_shape=jax.ShapeDtypeStruct((s, d), BF16),
        compiler_params=_params("parallel"),
    )(h, g.reshape(1, d), *deps)


def _rms_bwd(dn, h, g, *, name, res=None, col_block=0, out_dtype=F32):
    s = dn.shape[0]
    d = g.shape[-1]
    ts = _tile(s, 512, 8)

    def body(*refs):
        if res is None:
            dn_ref, h_ref, g_ref, dh_ref, dg_ref = refs
            res_ref = None
        else:
            dn_ref, h_ref, g_ref, res_ref, dh_ref, dg_ref = refs
        i = pl.program_id(0)
        x = h_ref[...]
        r = lax.rsqrt(jnp.mean(x * x, axis=-1, keepdims=True) + NORM_EPS)
        nh = x * r
        dnv = dn_ref[...]
        gd = dnv * g_ref[...]
        dh = (gd - nh * jnp.mean(gd * nh, axis=-1, keepdims=True)) * r
        if res_ref is not None:
            dh = dh + res_ref[...]
        dh_ref[...] = dh.astype(out_dtype)
        part = jnp.sum(dnv * nh, axis=0, keepdims=True)

        @pl.when(i == 0)
        def _():
            dg_ref[...] = part

        @pl.when(i > 0)
        def _():
            dg_ref[...] += part

    row = pl.BlockSpec((ts, d), lambda i: (i, 0))
    in_specs = [row, pl.BlockSpec((ts, d), lambda i: (i, col_block)), pl.BlockSpec((1, d), lambda i: (0, 0))]
    args = [dn, h, g.reshape(1, d)]
    if res is not None:
        in_specs.append(row)
        args.append(res)
    return _pcall(
        body, name=name, grid=(s // ts,), in_specs=in_specs,
        out_specs=(row, pl.BlockSpec((1, d), lambda i: (0, 0))),
        out_shape=(jax.ShapeDtypeStruct((s, d), out_dtype), jax.ShapeDtypeStruct((1, d), F32)),
        compiler_params=_params("arbitrary"),
    )(*args)


ROW_CHUNK = 512


def _rows_with_halo(ref, r, t_rows, n_chunks, before, after):
    r0 = r * t_rows
    parts = []
    if before:
        hb = ref[pl.ds(pl.multiple_of(jnp.maximum(r0 - before, 0), before), before), :]
        parts.append(jnp.where(r > 0, hb, jnp.zeros_like(hb)))
    parts.append(ref[pl.ds(pl.multiple_of(r0, t_rows), t_rows), :])
    if after:
        ha = ref[pl.ds(pl.multiple_of(jnp.minimum(r0 + t_rows, n_chunks * t_rows - after), after), after), :]
        parts.append(jnp.where(r < n_chunks - 1, ha, jnp.zeros_like(ha)))
    return jnp.concatenate(parts, axis=0)


POOL_HALO = 16


def _pool_fwd(u, pool_w, pool_scale, *, c):
    s = u.shape[0]
    g_n = POOL_GROUPS
    tr = _tile(s, ROW_CHUNK, POOL_HALO)
    n_chunks = s // tr

    def body(u_ref, pw_ref, sc_ref, y_ref, d_ref):
        r = pl.program_id(1)
        w = jnp.left_shift(2, pl.program_id(0))
        xe = _rows_with_halo(u_ref, r, tr, n_chunks, POOL_HALO, 0)
        acc = xe
        for k in (1, 2, 4, 8):
            acc = jnp.where(k < w, acc + pltpu.roll(acc, k, 0), acc)
        t = r * tr + lax.broadcasted_iota(jnp.int32, (tr, 1), 0)
        cnt = jnp.minimum(t + 1, w).astype(F32)
        diff = (acc[POOL_HALO:] / cnt - xe[POOL_HALO:]).astype(BF16)
        d_ref[...] = diff
        y = jnp.dot(diff, pw_ref[...], preferred_element_type=F32) * sc_ref[...]
        y_ref[...] = y.astype(BF16)

    out = pl.BlockSpec((tr, c), lambda g, r: (r, g))
    return _pcall(
        body, name="pool_fwd", grid=(g_n, n_chunks),
        in_specs=[pl.BlockSpec((s, c), lambda g, r: (0, g)), pl.BlockSpec((None, c, c), lambda g, r: (g, 0, 0)),
                  pl.BlockSpec((1, c), lambda g, r: (0, g))],
        out_specs=(out, out),
        out_shape=(jax.ShapeDtypeStruct((s, g_n * c), BF16), jax.ShapeDtypeStruct((s, g_n * c), BF16)),
        compiler_params=_params("parallel", "arbitrary"),
    )(u, pool_w, pool_scale.reshape(1, g_n * c))


def _pool_bwd(dcat, diff, pool_w, pool_scale, *, c):
    s = dcat.shape[0]
    g_n = POOL_GROUPS
    tr = _tile(s, ROW_CHUNK, POOL_HALO)
    n_chunks = s // tr

    def body(dy_ref, d_ref, pw_ref, sc_ref, du_ref, dpw_ref, dsc_ref):
        r = pl.program_id(1)
        w = jnp.left_shift(2, pl.program_id(0))
        dye = _rows_with_halo(dy_ref, r, tr, n_chunks, 0, POOL_HALO)
        diff = d_ref[pl.ds(pl.multiple_of(r * tr, tr), tr), :]
        pw = pw_ref[...]
        yp = jnp.dot(diff, pw, preferred_element_type=F32)
        dsc = jnp.sum(dye[:tr] * yp, axis=0, keepdims=True)
        dyp = (dye * sc_ref[...]).astype(BF16)
        ddiff = lax.dot_general(dyp, pw, (((1,), (1,)), ((), ())), preferred_element_type=F32)
        dpw = lax.dot_general(diff, dyp[:tr], (((0,), (0,)), ((), ())), preferred_element_type=F32)
        t = r * tr + lax.broadcasted_iota(jnp.int32, (tr + POOL_HALO, 1), 0)
        cnt = jnp.minimum(t + 1, w).astype(F32)
        acc = ddiff / cnt
        rows = tr + POOL_HALO
        for k in (1, 2, 4, 8):
            acc = jnp.where(k < w, acc + pltpu.roll(acc, rows - k, 0), acc)
        du_ref[...] = (acc[:tr] - ddiff[:tr]).astype(BF16)

        @pl.when(r == 0)
        def _():
            dpw_ref[...] = dpw
            dsc_ref[...] = dsc

        @pl.when(r > 0)
        def _():
            dpw_ref[...] += dpw
            dsc_ref[...] += dsc

    col = lambda g, r: (0, g)
    wspec = pl.BlockSpec((None, c, c), lambda g, r: (g, 0, 0))
    vec = pl.BlockSpec((1, c), col)
    return _pcall(
        body, name="pool_bwd", grid=(g_n, n_chunks),
        in_specs=[pl.BlockSpec((s, c), col), pl.BlockSpec((s, c), col), wspec, vec],
        out_specs=(pl.BlockSpec((tr, c), lambda g, r: (r, g)), wspec, vec),
        out_shape=(jax.ShapeDtypeStruct((s, g_n * c), BF16), jax.ShapeDtypeStruct((g_n, c, c), F32),
                   jax.ShapeDtypeStruct((1, g_n * c), F32)),
        compiler_params=_params("parallel", "arbitrary"),
    )(dcat, diff, pool_w, pool_scale.reshape(1, g_n * c))


def _swap_halves(x, lane):
    return jnp.where((lane % ROPE_DIM) < ROPE_DIM // 2, pltpu.roll(x, LANES - ROPE_DIM // 2, 1),
                     pltpu.roll(x, ROPE_DIM // 2, 1))


def _qkv_prep(q, kv, kr2, cos_t, sin_t, *, heads):
    s = q.shape[0]
    ts = _tile(s, 512, 8)

    def body(qn_ref, qr_ref, kn_ref, v_ref, kr_ref, cos_ref, sin_ref, qo_ref, ko_ref, vo_ref):
        half = pl.program_id(1) % 2
        lane = lax.broadcasted_iota(jnp.int32, (ts, LANES), 1)
        cos_v = cos_ref[...]
        sin_v = sin_ref[...]

        def rope(x):
            return x * cos_v + _swap_halves(x, lane) * sin_v

        qo_ref[:, :LANES] = qn_ref[...].astype(BF16)
        qo_ref[:, LANES:] = jnp.where(lane // ROPE_DIM == half, rope(qr_ref[...]), 0.0).astype(BF16)
        ko_ref[:, :LANES] = kn_ref[...].astype(BF16)
        ko_ref[:, LANES:] = rope(kr_ref[...]).astype(BF16)
        vo_ref[...] = v_ref[...].astype(BF16)

    blk = lambda f: pl.BlockSpec((ts, LANES), f)
    return _pcall(
        body, name="qkv_prep", grid=(s // ts, heads),
        in_specs=[blk(lambda i, h: (i, h)), blk(lambda i, h: (i, heads + h // 2)),
                  blk(lambda i, h: (i, h)), blk(lambda i, h: (i, heads + h)),
                  blk(lambda i, h: (i, 0)), blk(lambda i, h: (i, 0)), blk(lambda i, h: (i, 0))],
        out_specs=(pl.BlockSpec((None, ts, 2 * LANES), lambda i, h: (h, i, 0)),
                   pl.BlockSpec((None, ts, 2 * LANES), lambda i, h: (h, i, 0)),
                   pl.BlockSpec((None, ts, LANES), lambda i, h: (h, i, 0))),
        out_shape=(jax.ShapeDtypeStruct((heads, s, 2 * LANES), BF16),
                   jax.ShapeDtypeStruct((heads, s, 2 * LANES), BF16),
                   jax.ShapeDtypeStruct((heads, s, LANES), BF16)),
        compiler_params=_params("parallel", "parallel"),
    )(q, q, kv, kv, kr2, cos_t, sin_t)


def _masked_scores(q, k, i, j, tq, tk, scale):
    sc = lax.dot_general(q, k, (((1,), (1,)), ((), ())), preferred_element_type=F32) * scale
    rows = i * tq + lax.broadcasted_iota(jnp.int32, (tq, tk), 0)
    cols = j * tk + lax.broadcasted_iota(jnp.int32, (tq, tk), 1)
    return jnp.where(rows >= cols, sc, -jnp.inf)


def _flash_fwd(qp, kp, v, *, scale, tq=512):
    heads, s, dk = qp.shape
    tq = _tile(s, tq, 8)
    tk = tq
    nq = s // tq

    def body(q_ref, k_ref, v_ref, o_ref, lse_ref, m_sc, l_sc, acc_sc):
        i = pl.program_id(1)
        j = pl.program_id(2)

        @pl.when(j == 0)
        def _():
            m_sc[...] = jnp.full_like(m_sc, -jnp.inf)
            l_sc[...] = jnp.zeros_like(l_sc)
            acc_sc[...] = jnp.zeros_like(acc_sc)

        @pl.when(j <= i)
        def _():
            sc = _masked_scores(q_ref[...], k_ref[...], i, j, tq, tk, scale)
            m_old = m_sc[...]
            m_new = jnp.maximum(m_old, jnp.max(sc, axis=-1, keepdims=True))
            alpha = jnp.exp(m_old - m_new)
            p = jnp.exp(sc - m_new)
            l_sc[...] = alpha * l_sc[...] + jnp.sum(p, axis=-1, keepdims=True)
            acc_sc[...] = alpha * acc_sc[...] + jnp.dot(p.astype(BF16), v_ref[...], preferred_element_type=F32)
            m_sc[...] = m_new

        @pl.when(j == nq - 1)
        def _():
            o_ref[...] = acc_sc[...] / l_sc[...]
            lse_ref[...] = m_sc[...] + jnp.log(l_sc[...])

    return _pcall(
        body, name="flash_fwd", grid=(heads, nq, nq),
        in_specs=[pl.BlockSpec((None, tq, dk), lambda h, i, j: (h, i, 0)),
                  pl.BlockSpec((None, tk, dk), lambda h, i, j: (h, jnp.minimum(i, j), 0)),
                  pl.BlockSpec((None, tk, V_DIM), lambda h, i, j: (h, jnp.minimum(i, j), 0))],
        out_specs=(pl.BlockSpec((tq, V_DIM), lambda h, i, j: (i, h)),
                   pl.BlockSpec((None, tq, 1), lambda h, i, j: (h, i, 0))),
        out_shape=(jax.ShapeDtypeStruct((s, heads * V_DIM), F32), jax.ShapeDtypeStruct((heads, s, 1), F32)),
        scratch_shapes=[pltpu.VMEM((tq, 1), F32), pltpu.VMEM((tq, 1), F32), pltpu.VMEM((tq, V_DIM), F32)],
        compiler_params=_params("parallel", "parallel", "arbitrary"),
    )(qp, kp, v)


def _flash_probs(q_ref, k_ref, v_ref, do_ref, o_ref, lse_ref, i, j, tq, tk, scale):
    sc = _masked_scores(q_ref[...], k_ref[...], i, j, tq, tk, scale)
    p = jnp.exp(sc - lse_ref[...])
    do = do_ref[...]
    dp = lax.dot_general(do.astype(BF16), v_ref[...], (((1,), (1,)), ((), ())), preferred_element_type=F32)
    delta = jnp.sum(do * o_ref[...], axis=-1, keepdims=True)
    ds = p * (dp - delta) * scale
    return p, ds, do


def _flash_bwd_kv(qp, kp, v, dcat, o, lse, *, scale, do_col0, tq=512):
    heads, s, dk = qp.shape
    tq = _tile(s, tq, 8)
    tk = tq
    nq = s // tq

    def body(q_ref, k_ref, v_ref, do_ref, o_ref, lse_ref, dk_ref, dv_ref, dk_sc, dv_sc):
        j = pl.program_id(1)
        i = pl.program_id(2)

        @pl.when(i == 0)
        def _():
            dk_sc[...] = jnp.zeros_like(dk_sc)
            dv_sc[...] = jnp.zeros_like(dv_sc)

        @pl.when(i >= j)
        def _():
            p, ds, do = _flash_probs(q_ref, k_ref, v_ref, do_ref, o_ref, lse_ref, i, j, tq, tk, scale)
            dv_sc[...] += lax.dot_general(p.astype(BF16), do.astype(BF16), (((0,), (0,)), ((), ())),
                                          preferred_element_type=F32)
            dk_sc[...] += lax.dot_general(ds.astype(BF16), q_ref[...], (((0,), (0,)), ((), ())),
                                          preferred_element_type=F32)

        @pl.when(i == nq - 1)
        def _():
            dk_ref[...] = dk_sc[...]
            dv_ref[...] = dv_sc[...]

    qrow = lambda h, j, i: (h, jnp.maximum(i, j), 0)
    return _pcall(
        body, name="flash_bwd_kv", grid=(heads, nq, nq),
        in_specs=[pl.BlockSpec((None, tq, dk), qrow),
                  pl.BlockSpec((None, tk, dk), lambda h, j, i: (h, j, 0)),
                  pl.BlockSpec((None, tk, V_DIM), lambda h, j, i: (h, j, 0)),
                  pl.BlockSpec((tq, V_DIM), lambda h, j, i: (jnp.maximum(i, j), do_col0 + h)),
                  pl.BlockSpec((tq, V_DIM), lambda h, j, i: (jnp.maximum(i, j), h)),
                  pl.BlockSpec((None, tq, 1), qrow)],
        out_specs=(pl.BlockSpec((None, tk, dk), lambda h, j, i: (h, j, 0)),
                   pl.BlockSpec((None, tk, V_DIM), lambda h, j, i: (h, j, 0))),
        out_shape=(jax.ShapeDtypeStruct((heads, s, dk), F32), jax.ShapeDtypeStruct((heads, s, V_DIM), F32)),
        scratch_shapes=[pltpu.VMEM((tk, dk), F32), pltpu.VMEM((tk, V_DIM), F32)],
        compiler_params=_params("parallel", "parallel", "arbitrary"),
    )(qp, kp, v, dcat, o, lse)


def _flash_bwd_q(qp, kp, v, dcat, o, lse, *, scale, do_col0, tq=512):
    heads, s, dk = qp.shape
    tq = _tile(s, tq, 8)
    tk = tq
    nq = s // tq

    def body(q_ref, k_ref, v_ref, do_ref, o_ref, lse_ref, dq_ref, dq_sc):
        i = pl.program_id(1)
        j = pl.program_id(2)

        @pl.when(j == 0)
        def _():
            dq_sc[...] = jnp.zeros_like(dq_sc)

        @pl.when(j <= i)
        def _():
            _, ds, _ = _flash_probs(q_ref, k_ref, v_ref, do_ref, o_ref, lse_ref, i, j, tq, tk, scale)
            dq_sc[...] += jnp.dot(ds.astype(BF16), k_ref[...], preferred_element_type=F32)

        @pl.when(j == nq - 1)
        def _():
            dq_ref[...] = dq_sc[...]

    kcol = lambda h, i, j: (h, jnp.minimum(i, j), 0)
    return _pcall(
        body, name="flash_bwd_q", grid=(heads, nq, nq),
        in_specs=[pl.BlockSpec((None, tq, dk), lambda h, i, j: (h, i, 0)),
                  pl.BlockSpec((None, tk, dk), kcol),
                  pl.BlockSpec((None, tk, V_DIM), kcol),
                  pl.BlockSpec((tq, V_DIM), lambda h, i, j: (i, do_col0 + h)),
                  pl.BlockSpec((tq, V_DIM), lambda h, i, j: (i, h)),
                  pl.BlockSpec((None, tq, 1), lambda h, i, j: (h, i, 0))],
        out_specs=pl.BlockSpec((None, tq, dk), lambda h, i, j: (h, i, 0)),
        out_shape=jax.ShapeDtypeStruct((heads, s, dk), F32),
        scratch_shapes=[pltpu.VMEM((tq, dk), F32)],
        compiler_params=_params("parallel", "parallel", "arbitrary"),
    )(qp, kp, v, dcat, o, lse)


def _attn_bwd_post(dqp, dkp, dv, cos_t, sin_t):
    heads, s, _ = dqp.shape
    ts = _tile(s, 256, 8)

    def body(dq_ref, dk_ref, dv_ref, cos_ref, sin_ref, q_out, kv_out, kr_out):
        lane = lax.broadcasted_iota(jnp.int32, (ts, LANES), 1)
        cos_v = cos_ref[...]
        sin_v = sin_ref[...]

        def rope_t(dy):
            return dy * cos_v + _swap_halves(dy * sin_v, lane)

        kr_sum = jnp.zeros((ts, LANES), F32)
        for h in range(heads):
            q_out[:, h * LANES:(h + 1) * LANES] = dq_ref[h, :, :LANES].astype(BF16)
            kv_out[:, h * LANES:(h + 1) * LANES] = dk_ref[h, :, :LANES].astype(BF16)
            kv_out[:, (heads + h) * LANES:(heads + h + 1) * LANES] = dv_ref[h].astype(BF16)
            kr_sum = kr_sum + dk_ref[h, :, LANES:]
        for pair in range(heads // 2):
            r = jnp.where(lane < ROPE_DIM, dq_ref[2 * pair, :, LANES:], dq_ref[2 * pair + 1, :, LANES:])
            q_out[:, (heads + pair) * LANES:(heads + pair + 1) * LANES] = rope_t(r).astype(BF16)
        kr = rope_t(kr_sum)
        kr_out[...] = (kr + pltpu.roll(kr, ROPE_DIM, 1)).astype(BF16)

    wq = heads * (NOPE_DIM + ROPE_DIM)
    wkv = heads * (NOPE_DIM + V_DIM)
    tab = pl.BlockSpec((ts, LANES), lambda i: (i, 0))
    return _pcall(
        body, name="attn_bwd_post", grid=(s // ts,),
        in_specs=[pl.BlockSpec((heads, ts, 2 * LANES), lambda i: (0, i, 0)),
                  pl.BlockSpec((heads, ts, 2 * LANES), lambda i: (0, i, 0)),
                  pl.BlockSpec((heads, ts, LANES), lambda i: (0, i, 0)), tab, tab],
        out_specs=(pl.BlockSpec((ts, wq), lambda i: (i, 0)), pl.BlockSpec((ts, wkv), lambda i: (i, 0)), tab),
        out_shape=(jax.ShapeDtypeStruct((s, wq), BF16), jax.ShapeDtypeStruct((s, wkv), BF16),
                   jax.ShapeDtypeStruct((s, LANES), BF16)),
        compiler_params=_params("parallel"),
    )(dqp, dkp, dv, cos_t, sin_t)


CONV_HALO = 8


def _conv_gate(xe, cw_ref, cb_ref):
    return (cw_ref[2:3, :] * xe + cw_ref[1:2, :] * pltpu.roll(xe, 1, 0) + cw_ref[0:1, :] * pltpu.roll(xe, 2, 0)
            + cb_ref[...])


def _ffn_act_fwd(gu, conv_w, conv_b, *, d_ff, tc=256):
    s = gu.shape[0]
    tc = _tile(d_ff, tc)
    nf = d_ff // tc
    tr = _tile(s, ROW_CHUNK, CONV_HALO)
    n_chunks = s // tr

    def body(g_ref, u_ref, cw_ref, cb_ref, a_ref):
        r = pl.program_id(1)
        xe = _rows_with_halo(g_ref, r, tr, n_chunks, CONV_HALO, 0)
        gc = _conv_gate(xe, cw_ref, cb_ref)[CONV_HALO:]
        a_ref[...] = (gc * jax.nn.sigmoid(gc) * u_ref[...]).astype(BF16)

    return _pcall(
        body, name="ffn_act_fwd", grid=(nf, n_chunks),
        in_specs=[pl.BlockSpec((s, tc), lambda j, r: (0, j)), pl.BlockSpec((tr, tc), lambda j, r: (r, nf + j)),
                  pl.BlockSpec((CONV_TAPS, tc), lambda j, r: (0, j)), pl.BlockSpec((1, tc), lambda j, r: (0, j))],
        out_specs=pl.BlockSpec((tr, tc), lambda j, r: (r, j)),
        out_shape=jax.ShapeDtypeStruct((s, d_ff), BF16),
        compiler_params=_params("parallel", "arbitrary"),
    )(gu, gu, conv_w, conv_b.reshape(1, d_ff))


def _ffn_act_bwd(da, gu, conv_w, conv_b, *, d_ff, tc=256):
    s = gu.shape[0]
    tc = _tile(d_ff, tc)
    nf = d_ff // tc
    tr = _tile(s, ROW_CHUNK, CONV_HALO)
    n_chunks = s // tr
    rows = tr + 2 * CONV_HALO
    main = slice(CONV_HALO, CONV_HALO + tr)

    def body(da_ref, g_ref, u_ref, cw_ref, cb_ref, dg_ref, du_ref, dcw_ref, dcb_ref):
        r = pl.program_id(1)
        xe = _rows_with_halo(g_ref, r, tr, n_chunks, CONV_HALO, CONV_HALO)
        dae = _rows_with_halo(da_ref, r, tr, n_chunks, CONV_HALO, CONV_HALO)
        ue = _rows_with_halo(u_ref, r, tr, n_chunks, CONV_HALO, CONV_HALO)
        gc = _conv_gate(xe, cw_ref, cb_ref)
        sg = jax.nn.sigmoid(gc)
        du_ref[...] = (dae * gc * sg)[main].astype(BF16)
        dgc = dae * ue * sg * (1.0 + gc * (1.0 - sg))
        dg = (cw_ref[2:3, :] * dgc + cw_ref[1:2, :] * pltpu.roll(dgc, rows - 1, 0)
              + cw_ref[0:1, :] * pltpu.roll(dgc, rows - 2, 0))
        dg_ref[...] = dg[main].astype(BF16)
        dgc_m = dgc[main]
        dcb = jnp.sum(dgc_m, axis=0, keepdims=True)
        dcw = jnp.concatenate([jnp.sum(dgc_m * pltpu.roll(xe, 2, 0)[main], axis=0, keepdims=True),
                               jnp.sum(dgc_m * pltpu.roll(xe, 1, 0)[main], axis=0, keepdims=True),
                               jnp.sum(dgc_m * xe[main], axis=0, keepdims=True)], axis=0)

        @pl.when(r == 0)
        def _():
            dcb_ref[...] = dcb
            dcw_ref[...] = dcw

        @pl.when(r > 0)
        def _():
            dcb_ref[...] += dcb
            dcw_ref[...] += dcw

    col = pl.BlockSpec((s, tc), lambda j, r: (0, j))
    out = pl.BlockSpec((tr, tc), lambda j, r: (r, j))
    return _pcall(
        body, name="ffn_act_bwd", grid=(nf, n_chunks),
        in_specs=[col, col, pl.BlockSpec((s, tc), lambda j, r: (0, nf + j)),
                  pl.BlockSpec((CONV_TAPS, tc), lambda j, r: (0, j)), pl.BlockSpec((1, tc), lambda j, r: (0, j))],
        out_specs=(out, out, pl.BlockSpec((CONV_TAPS, tc), lambda j, r: (0, j)),
                   pl.BlockSpec((1, tc), lambda j, r: (0, j))),
        out_shape=(jax.ShapeDtypeStruct((s, d_ff), BF16), jax.ShapeDtypeStruct((s, d_ff), BF16),
                   jax.ShapeDtypeStruct((CONV_TAPS, d_ff), F32), jax.ShapeDtypeStruct((1, d_ff), F32)),
        compiler_params=_params("parallel", "arbitrary"),
    )(da, gu, gu, conv_w, conv_b.reshape(1, d_ff))


def _ple_fwd(h, z, e):
    s, d = h.shape
    ts = _tile(s, 512, 8)

    def body(h_ref, z_ref, e_ref, o_ref):
        o_ref[...] = h_ref[...] + e_ref[...] * jax.nn.sigmoid(z_ref[...])

    row = pl.BlockSpec((ts, d), lambda i: (i, 0))
    return _pcall(body, name="ple_fwd", grid=(s // ts,), in_specs=[row, row, row], out_specs=row,
                  out_shape=jax.ShapeDtypeStruct((s, d), F32), compiler_params=_params("parallel"))(h, z, e)


def _ple_bwd(dh, z, e, dep=None):
    s, d = dh.shape
    ts = _tile(s, 512, 8)

    def body(dh_ref, z_ref, e_ref, *rest):
        de_ref, dz_ref = rest[-2:]
        gt = jax.nn.sigmoid(z_ref[...])
        dhv = dh_ref[...]
        de_ref[...] = (dhv * gt).astype(BF16)
        dz_ref[...] = (dhv * e_ref[...] * gt * (1.0 - gt)).astype(BF16)

    row = pl.BlockSpec((ts, d), lambda i: (i, 0))
    deps = [] if dep is None else [dep]
    return _pcall(body, name="ple_bwd", grid=(s // ts,), in_specs=[row, row, row] + [HBM_SPEC] * len(deps),
                  out_specs=(row, row),
                  out_shape=(jax.ShapeDtypeStruct((s, d), BF16), jax.ShapeDtypeStruct((s, d), BF16)),
                  compiler_params=_params("parallel"))(dh, z, e, *deps)


def _loss_head(h, g, target):
    s, d = h.shape
    ts = _tile(s, 512, 8)

    def body(h_ref, g_ref, t_ref, loss_ref, dh_ref, dg_ref):
        i = pl.program_id(0)
        x = h_ref[...]
        gv = g_ref[...]
        r = lax.rsqrt(jnp.mean(x * x, axis=-1, keepdims=True) + NORM_EPS)
        nh = x * r
        err = nh * gv - t_ref[...]
        part_loss = 0.5 * jnp.sum(jnp.mean(err * err, axis=-1, keepdims=True), axis=0, keepdims=True)
        dy = err * (1.0 / d)
        gd = dy * gv
        dh_ref[...] = (gd - nh * jnp.mean(gd * nh, axis=-1, keepdims=True)) * r
        part_g = jnp.sum(dy * nh, axis=0, keepdims=True)
        part_l = jnp.broadcast_to(part_loss, (1, LANES))

        @pl.when(i == 0)
        def _():
            dg_ref[...] = part_g
            loss_ref[...] = part_l

        @pl.when(i > 0)
        def _():
            dg_ref[...] += part_g
            loss_ref[...] += part_l

    row = pl.BlockSpec((ts, d), lambda i: (i, 0))
    vec = pl.BlockSpec((1, d), lambda i: (0, 0))
    return _pcall(
        body, name="loss_head", grid=(s // ts,), in_specs=[row, vec, row],
        out_specs=(pl.BlockSpec((1, LANES), lambda i: (0, 0)), row, vec),
        out_shape=(jax.ShapeDtypeStruct((1, LANES), F32), jax.ShapeDtypeStruct((s, d), F32),
                   jax.ShapeDtypeStruct((1, d), F32)),
        compiler_params=_params("arbitrary"),
    )(h, g.reshape(1, d), target)


HBM_SPEC = pl.BlockSpec(memory_space=pl.ANY)


def _flat_index(px, py, pc):
    return 4 * px + 2 * py + pc


def _all_gather(shards, *, name):
    n = len(shards)

    def body(*refs):
        ins, outs = refs[:n], refs[n:2 * n]
        send_sems, recv_sems, local_sems = refs[2 * n:]
        x, y, c = lax.axis_index("x"), lax.axis_index("y"), lax.axis_index("c")
        me, sibling = (x, y, c), (x, y, 1 - c)
        chips = [(1 - x, y), (x, 1 - y), (1 - x, 1 - y)]

        def copy(a, k, block, to, src=None):
            slot = outs[a].at[_flat_index(*block)]
            return pltpu.make_async_remote_copy(
                src_ref=slot if src is None else src, dst_ref=slot,
                send_sem=send_sems.at[a, k], recv_sem=recv_sems.at[a, k],
                device_id=to, device_id_type=pl.DeviceIdType.MESH)

        mine, first, passed = [], [], []
        for a in range(n):
            cp = pltpu.make_async_copy(ins[a], outs[a].at[_flat_index(*me)], local_sems.at[a])
            cp.start()
            mine.append(cp)
            first.append(copy(a, 0, me, sibling, src=ins[a]))
            first += [copy(a, 1 + j, me, (*chip, c), src=ins[a]) for j, chip in enumerate(chips)]
        for cp in first:
            cp.start()
        for j, chip in enumerate(chips):
            for a in range(n):
                copy(a, 1 + j, (*chip, c), me).wait_recv()
                fwd = copy(a, 4 + j, (*chip, c), sibling)
                fwd.start()
                passed.append(fwd)
        for a in range(n):
            copy(a, 0, sibling, me).wait_recv()
            for j, chip in enumerate(chips):
                copy(a, 4 + j, (*chip, 1 - c), me).wait_recv()
        for cp in first + passed:
            cp.wait_send()
        for cp in mine:
            cp.wait()

    return _pcall(
        body, name=name,
        in_specs=[HBM_SPEC] * n, out_specs=[HBM_SPEC] * n,
        out_shape=[jax.ShapeDtypeStruct((N_DEV,) + a.shape, a.dtype) for a in shards],
        scratch_shapes=[pltpu.SemaphoreType.DMA((n, 7)), pltpu.SemaphoreType.DMA((n, 7)),
                        pltpu.SemaphoreType.DMA((n,))],
    )(*shards)


HBM_ONLY = pl.BlockSpec(memory_space=pltpu.HBM)
SEM_SPEC = pl.BlockSpec(memory_space=pltpu.SEMAPHORE)
N_PEERS = N_DEV - 1
PEER_FLIPS = ((0, 0, 1), (1, 0, 0), (0, 1, 0), (1, 1, 0), (1, 0, 1), (0, 1, 1), (1, 1, 1))


def _exchange_refs(gather, src_refs, land_refs, send_sems, recv_sems):
    x, y, c = lax.axis_index("x"), lax.axis_index("y"), lax.axis_index("c")
    me = _flat_index(x, y, c)
    peers = [(x ^ fx, y ^ fy, c ^ fc) for fx, fy, fc in PEER_FLIPS]

    def out_copy(a, k):
        src = src_refs[a] if gather else src_refs[a].at[_flat_index(*peers[k])]
        return pltpu.make_async_remote_copy(
            src_ref=src, dst_ref=land_refs[a].at[me], send_sem=send_sems.at[a * N_PEERS + k],
            recv_sem=recv_sems.at[a * N_PEERS + k], device_id=peers[k], device_id_type=pl.DeviceIdType.MESH)

    def in_copy(a, k):
        src = src_refs[a] if gather else src_refs[a].at[me]
        return pltpu.make_async_remote_copy(
            src_ref=src, dst_ref=land_refs[a].at[_flat_index(*peers[k])], send_sem=send_sems.at[a * N_PEERS + k],
            recv_sem=recv_sems.at[a * N_PEERS + k], device_id=peers[k], device_id_type=pl.DeviceIdType.MESH)

    return out_copy, in_copy


def _exchange_start(srcs, lands, *, gather, name, dep):
    n = len(srcs)

    def body(*refs):
        src_refs, land_refs = refs[:n], refs[n:2 * n]
        send_sems, recv_sems = refs[2 * n + 1], refs[2 * n + 2]
        token = refs[-1]
        out_copy, _ = _exchange_refs(gather, src_refs, land_refs, send_sems, recv_sems)
        for k in range(N_PEERS):
            for a in range(n):
                out_copy(a, k).start()
        token[...] = jnp.zeros_like(token)

    hbm = lambda a: pltpu.with_memory_space_constraint(a, pltpu.HBM)
    return _pcall(
        body, name=name,
        out_shape=(pltpu.SemaphoreType.DMA((n * N_PEERS,)), pltpu.SemaphoreType.DMA((n * N_PEERS,)),
                   *[pltpu.HBM(a.shape, a.dtype) for a in srcs], *[pltpu.HBM(a.shape, a.dtype) for a in lands],
                   jax.ShapeDtypeStruct((8, LANES), F32)),
        in_specs=[HBM_ONLY] * (2 * n) + [HBM_SPEC],
        out_specs=(SEM_SPEC, SEM_SPEC, *[HBM_ONLY] * (2 * n), pl.BlockSpec(memory_space=pltpu.VMEM)),
        input_output_aliases={i: 2 + i for i in range(2 * n)},
        compiler_params=pltpu.CompilerParams(has_side_effects=pltpu.SideEffectType.DATAFLOW_SIDE_EFFECTING),
    )(*[hbm(a) for a in srcs], *[hbm(a) for a in lands], dep)


def _exchange_wait(started, after, *, gather, name):
    send_sems, recv_sems = started[0], started[1]
    n = (len(started) - 3) // 2
    srcs, lands = started[2:2 + n], started[2 + n:2 + 2 * n]

    def body(*refs):
        src_refs, land_refs = refs[:n], refs[n:2 * n]
        s_sems, r_sems = refs[2 * n], refs[2 * n + 1]
        out_copy, in_copy = _exchange_refs(gather, src_refs, land_refs, s_sems, r_sems)
        for k in range(N_PEERS):
            for a in range(n):
                out_copy(a, k).wait_send()
                in_copy(a, k).wait_recv()

    res = _pcall(
        body, name=name,
        out_shape=tuple(pltpu.HBM(a.shape, a.dtype) for a in (*srcs, *lands)),
        in_specs=[HBM_ONLY] * (2 * n) + [SEM_SPEC, SEM_SPEC, HBM_SPEC],
        out_specs=tuple([HBM_ONLY] * (2 * n)),
        input_output_aliases={i: i for i in range(2 * n)},
        compiler_params=pltpu.CompilerParams(has_side_effects=pltpu.SideEffectType.DATAFLOW_SIDE_EFFECTING),
    )(*srcs, *lands, send_sems, recv_sems, after)
    return list(res[n:])


def _own_slot_filled(block, me):
    land = lax.empty((N_DEV,) + block.shape, block.dtype)
    return lax.dynamic_update_slice(land, block[None], (me,) + (0,) * block.ndim)


def _adam_math(g, w, m, v):
    m = ADAM_B1 * m + (1.0 - ADAM_B1) * g
    v = ADAM_B2 * v + (1.0 - ADAM_B2) * jnp.square(g)
    m_hat = m / (1.0 - ADAM_B1 ** ADAM_STEP)
    v_hat = v / (1.0 - ADAM_B2 ** ADAM_STEP)
    delta = -ADAM_LR * (m_hat / (jnp.sqrt(v_hat) + ADAM_EPS) + ADAM_WD * w)
    return delta, m, v


def _adamw(contrib, w, m, v, *, name):
    _, layers, r, c = contrib.shape
    tr = _tile(r, max(8, (256 * 1024 // c) // 8 * 8), 8)

    def body(g_ref, w_ref, m_ref, v_ref, go_ref, d_ref, mo_ref, vo_ref):
        g = g_ref[0].astype(F32)
        for k in range(1, N_DEV):
            g = g + g_ref[k].astype(F32)
        delta, m_new, v_new = _adam_math(g, w_ref[...], m_ref[...], v_ref[...])
        go_ref[...] = g
        d_ref[...] = delta
        mo_ref[...] = m_new
        vo_ref[...] = v_new

    blk = pl.BlockSpec((None, tr, c), lambda l, i: (l, i, 0))
    out = jax.ShapeDtypeStruct((layers, r, c), F32)
    return _pcall(
        body, name=name, grid=(layers, r // tr),
        in_specs=[pl.BlockSpec((N_DEV, None, tr, c), lambda l, i: (0, l, i, 0)), blk, blk, blk],
        out_specs=(blk, blk, blk, blk), out_shape=(out, out, out, out),
        compiler_params=_params("parallel", "parallel"),
    )(contrib, w, m, v)


def _heads_split(w, heads, first, second):
    k = w.shape[0]
    w3 = w.reshape(k, heads, first + second)
    return jnp.concatenate([w3[:, :, :first].reshape(k, heads * first),
                            w3[:, :, first:].reshape(k, heads * second)], axis=1)


def _heads_join(w, heads, first, second):
    k = w.shape[0]
    a = w[:, :heads * first].reshape(k, heads, first)
    b = w[:, heads * first:].reshape(k, heads, second)
    return jnp.concatenate([a, b], axis=2).reshape(k, heads * (first + second))


def _full_from_gathered(kind, g):
    if kind == "col":
        return jnp.transpose(g, (1, 0, 2)).reshape(g.shape[1], N_DEV * g.shape[2])
    if kind == "row":
        return g.reshape(N_DEV * g.shape[1], g.shape[2])
    return jnp.transpose(g, (1, 0, 2, 3)).reshape(g.shape[1], N_DEV * g.shape[2], g.shape[3])


def _blocks_from_full(kind, f):
    if kind == "col":
        k, n = f.shape
        return jnp.transpose(f.reshape(k, N_DEV, n // N_DEV), (1, 0, 2))
    if kind == "row":
        k, n = f.shape
        return f.reshape(N_DEV, k // N_DEV, n)
    g, c_in, c = f.shape
    return jnp.transpose(f.reshape(g, N_DEV, c_in // N_DEV, c), (1, 0, 2, 3))


def _rope_tables(positions):
    inv_freq = 1.0 / (ROPE_THETA ** (jnp.arange(0, ROPE_DIM, 2, dtype=F32) / ROPE_DIM))
    ang = positions.astype(F32)[:, None] * inv_freq
    cos, sin = jnp.cos(ang), jnp.sin(ang)
    return jnp.concatenate([cos, cos, cos, cos], axis=-1), jnp.concatenate([-sin, sin, -sin, sin], axis=-1)


def _layer_fwd(h0, p_i, w, rep, tabs, dims, dep=None):
    heads, d_pool, q_lora, d_ff = dims["heads"], dims["d_pool"], dims["q_lora"], dims["d_ff"]
    c = d_pool // POOL_GROUPS
    cos_t, sin_t = tabs
    scale = 1.0 / math.sqrt(NOPE_DIM + ROPE_DIM)
    n1 = _rms_fwd(h0, rep["norm_mix_g"], name="rms_mix_fwd", dep=dep)
    u = _mm(n1, w["w_in"], name="mm_in_fwd")
    y_pool, diff = _pool_fwd(u, w["pool_w"], rep["pool_scale"], c=c)
    nq = _rms_fwd(u, rep["q_norm_g"], name="rms_q_fwd", col_block=d_pool // q_lora)
    nkv = _rms_fwd(u, rep["kv_norm_g"], name="rms_kv_fwd", col_block=d_pool // q_lora + 1)
    q = _mm(nq, w["w_uq"], name="mm_uq_fwd")
    kv = _mm(nkv, w["w_ukv"], name="mm_ukv_fwd")
    kr = u[:, d_pool + 2 * q_lora:]
    kr2 = jnp.concatenate([kr, kr], axis=-1)
    qp, kp, v = _qkv_prep(q, kv, kr2, cos_t, sin_t, heads=heads)
    o, lse = _flash_fwd(qp, kp, v, scale=scale)
    t = _mm(y_pool, w["w_out"], name="mm_out_pool_fwd", add=h0, b_row=(0, d_pool))
    h1 = _mm(o, w["w_out"], name="mm_out_att_fwd", add=t, b_row=(d_pool, None))
    n2 = _rms_fwd(h1, rep["norm_ffn_g"], name="rms_ffn_fwd")
    gu = _mm(n2, w["w_up"], name="mm_up_fwd")
    a = _ffn_act_fwd(gu, w["conv_w"], rep["conv_b"], d_ff=d_ff)
    h2 = _mm(a, w["w_down"], name="mm_down_fwd", add=h1)
    n3 = _rms_fwd(h2, rep["norm_ple_g"], name="rms_ple_fwd")
    z = _mm(n3, w["w_ple_gate"], name="mm_pgate_fwd")
    e = _mm(p_i, w["w_ple"], name="mm_ple_fwd")
    h3 = _ple_fwd(h2, z, e)
    saved = dict(h0=h0, n1=n1, u=u, y_pool=y_pool, diff=diff, nq=nq, nkv=nkv, qp=qp, kp=kp, v=v, o=o, lse=lse,
                 h1=h1, n2=n2, gu=gu, a=a, h2=h2, n3=n3, z=z, e=e)
    return h3, saved


def _layer_bwd(dh3, p_i, w, rep, tabs, dims, sv, dep=None):
    heads, d_pool, q_lora, d_ff = dims["heads"], dims["d_pool"], dims["q_lora"], dims["d_ff"]
    c = d_pool // POOL_GROUPS
    cos_t, sin_t = tabs
    scale = 1.0 / math.sqrt(NOPE_DIM + ROPE_DIM)
    gr = {}
    de, dz = _ple_bwd(dh3, sv["z"], sv["e"], dep)
    gr["w_ple"] = _mm(p_i, de, name="mm_ple_dw", ta=True, out_dtype=BF16)
    gr["w_ple_gate"] = _mm(sv["n3"], dz, name="mm_pgate_dw", ta=True, out_dtype=BF16)
    dn3 = _mm(dz, w["w_ple_gate"], name="mm_pgate_dx", tb=True)
    dh2, gr["norm_ple_g"] = _rms_bwd(dn3, sv["h2"], rep["norm_ple_g"], name="rms_ple_bwd", res=dh3)
    gr["w_down"] = _mm(sv["a"], dh2, name="mm_down_dw", ta=True, out_dtype=BF16)
    da = _mm(dh2, w["w_down"], name="mm_down_dx", tb=True)
    dgate, dup, gr["conv_w"], gr["conv_b"] = _ffn_act_bwd(da, sv["gu"], w["conv_w"], rep["conv_b"], d_ff=d_ff)
    gr["w_up"] = jnp.concatenate([_mm(sv["n2"], dgate, name="mm_up_gate_dw", ta=True, out_dtype=BF16),
                                  _mm(sv["n2"], dup, name="mm_up_up_dw", ta=True, out_dtype=BF16)], axis=1)
    dn2 = _mm(dgate, w["w_up"], name="mm_up_gate_dx", tb=True, b_row=(0, d_ff))
    dn2 = _mm(dup, w["w_up"], name="mm_up_up_dx", tb=True, b_row=(d_ff, d_ff), add=dn2)
    dh1, gr["norm_ffn_g"] = _rms_bwd(dn2, sv["h1"], rep["norm_ffn_g"], name="rms_ffn_bwd", res=dh2)
    dw_out_pool = _mm(sv["y_pool"], dh1, name="mm_out_pool_dw", ta=True, out_dtype=BF16)
    dw_out_att = _mm(sv["o"], dh1, name="mm_out_att_dw", ta=True, out_dtype=BF16)
    gr["w_out"] = jnp.concatenate([dw_out_pool, dw_out_att], axis=0)
    dcat = _mm(dh1, w["w_out"], name="mm_out_dx", tb=True)
    do_col0 = d_pool // V_DIM
    dkp, dv = _flash_bwd_kv(sv["qp"], sv["kp"], sv["v"], dcat, sv["o"], sv["lse"], scale=scale, do_col0=do_col0)
    dqp = _flash_bwd_q(sv["qp"], sv["kp"], sv["v"], dcat, sv["o"], sv["lse"], scale=scale, do_col0=do_col0)
    dq, dkv, dkr2 = _attn_bwd_post(dqp, dkp, dv, cos_t, sin_t)
    gr["w_uq"] = _mm(sv["nq"], dq, name="mm_uq_dw", ta=True, out_dtype=BF16)
    gr["w_ukv"] = _mm(sv["nkv"], dkv, name="mm_ukv_dw", ta=True, out_dtype=BF16)
    dnq = _mm(dq, w["w_uq"], name="mm_uq_dx", tb=True)
    dnkv = _mm(dkv, w["w_ukv"], name="mm_ukv_dx", tb=True)
    dcq, gr["q_norm_g"] = _rms_bwd(dnq, sv["u"], rep["q_norm_g"], name="rms_q_bwd",
                                   col_block=d_pool // q_lora, out_dtype=BF16)
    dckv, gr["kv_norm_g"] = _rms_bwd(dnkv, sv["u"], rep["kv_norm_g"], name="rms_kv_bwd",
                                     col_block=d_pool // q_lora + 1, out_dtype=BF16)
    du_pool, gr["pool_w"], gr["pool_scale"] = _pool_bwd(dcat, sv["diff"], w["pool_w"], rep["pool_scale"], c=c)
    du = jnp.concatenate([du_pool, dcq, dckv, dkr2[:, :ROPE_DIM]], axis=-1)
    gr["w_in"] = _mm(sv["n1"], du, name="mm_in_dw", ta=True, out_dtype=BF16)
    dn1 = _mm(du, w["w_in"], name="mm_in_dx", tb=True)
    dh0, gr["norm_mix_g"] = _rms_bwd(dn1, sv["h0"], rep["norm_mix_g"], name="rms_mix_bwd", res=dh1)
    return dh0, gr


def _as2d(a):
    return a.reshape(a.shape[0], -1, a.shape[-1])


def kernel(x, p, positions, norm_mix_g, w_in, pool_w, pool_scale, q_norm_g, w_uq, kv_norm_g, w_ukv, w_out, norm_ffn_g, w_up, conv_w, conv_b, w_down, norm_ple_g, w_ple, w_ple_gate, final_norm_g, loss_target, m_norm_mix_g, m_w_in, m_pool_w, m_pool_scale, m_q_norm_g, m_w_uq, m_kv_norm_g, m_w_ukv, m_w_out, m_norm_ffn_g, m_w_up, m_conv_w, m_conv_b, m_w_down, m_norm_ple_g, m_w_ple, m_w_ple_gate, m_final_norm_g, v_norm_mix_g, v_w_in, v_pool_w, v_pool_scale, v_q_norm_g, v_w_uq, v_kv_norm_g, v_w_ukv, v_w_out, v_norm_ffn_g, v_w_up, v_conv_w, v_conv_b, v_w_down, v_norm_ple_g, v_w_ple, v_w_ple_gate, v_final_norm_g):
    weights = dict(norm_mix_g=norm_mix_g, w_in=w_in, pool_w=pool_w, pool_scale=pool_scale, q_norm_g=q_norm_g,
                   w_uq=w_uq, kv_norm_g=kv_norm_g, w_ukv=w_ukv, w_out=w_out, norm_ffn_g=norm_ffn_g, w_up=w_up,
                   conv_w=conv_w, conv_b=conv_b, w_down=w_down, norm_ple_g=norm_ple_g, w_ple=w_ple,
                   w_ple_gate=w_ple_gate, final_norm_g=final_norm_g)
    m_in = dict(norm_mix_g=m_norm_mix_g, w_in=m_w_in, pool_w=m_pool_w, pool_scale=m_pool_scale, q_norm_g=m_q_norm_g,
                w_uq=m_w_uq, kv_norm_g=m_kv_norm_g, w_ukv=m_w_ukv, w_out=m_w_out, norm_ffn_g=m_norm_ffn_g,
                w_up=m_w_up, conv_w=m_conv_w, conv_b=m_conv_b, w_down=m_w_down, norm_ple_g=m_norm_ple_g,
                w_ple=m_w_ple, w_ple_gate=m_w_ple_gate, final_norm_g=m_final_norm_g)
    v_in = dict(norm_mix_g=v_norm_mix_g, w_in=v_w_in, pool_w=v_pool_w, pool_scale=v_pool_scale, q_norm_g=v_q_norm_g,
                w_uq=v_w_uq, kv_norm_g=v_kv_norm_g, w_ukv=v_w_ukv, w_out=v_w_out, norm_ffn_g=v_norm_ffn_g,
                w_up=v_w_up, conv_w=v_conv_w, conv_b=v_conv_b, w_down=v_w_down, norm_ple_g=v_norm_ple_g,
                w_ple=v_w_ple, w_ple_gate=v_w_ple_gate, final_norm_g=v_final_norm_g)

    depth = w_in.shape[0]
    s, d_model = x.shape[1], x.shape[2]
    d_pool = pool_scale.shape[-1]
    q_lora = q_norm_g.shape[-1]
    d_ff = conv_b.shape[-1]
    heads = (w_uq.shape[-1] * N_DEV) // (NOPE_DIM + ROPE_DIM)
    dims = dict(heads=heads, d_pool=d_pool, q_lora=q_lora, d_ff=d_ff)

    me = _flat_index(lax.axis_index("x"), lax.axis_index("y"), lax.axis_index("c"))

    def layer_shards(i):
        return [weights[n][i] if n == "conv_w" else weights[n][i].astype(BF16) for n in SHARDED]

    def full_weights(i, gathered_i):
        w = {n: _full_from_gathered(SHARD_KIND[n], g) for n, g in zip(SHARDED, gathered_i)}
        w["w_uq"] = _heads_split(w["w_uq"], heads, NOPE_DIM, ROPE_DIM)
        w["w_ukv"] = _heads_split(w["w_ukv"], heads, NOPE_DIM, V_DIM)
        rep = {n: weights[n][i] for n in REPLICATED}
        return w, rep

    tabs = _rope_tables(positions[0])

    gathered = _all_gather(layer_shards(0), name="weights_all_gather_0")
    layer_w = []
    h = x[0]
    saved = []
    for i in range(depth):
        layer_w.append(full_weights(i, gathered))
        started = None
        if i + 1 < depth:
            shards = layer_shards(i + 1)
            started = _exchange_start(shards, [_own_slot_filled(a, me) for a in shards], gather=True,
                                      name=f"weights_gather_start_{i + 1}", dep=gathered[0] if i == 0 else h)
        w, rep = layer_w[i]
        h, sv = _layer_fwd(h, p[i, 0], w, rep, tabs, dims, dep=None if started is None else started[-1])
        saved.append(sv)
        if started is not None:
            gathered = _exchange_wait(started, h, gather=True, name=f"weights_gather_wait_{i + 1}")
    loss_row, dh, g_final = _loss_head(h, final_norm_g, loss_target[0])
    loss = lax.psum(loss_row[0, 0], MESH_AXES)

    layer_grads = [None] * depth
    received = [None] * depth
    started = None
    for i in reversed(range(depth)):
        w, rep = layer_w[i]
        dh, gr = _layer_bwd(dh, p[i, 0], w, rep, tabs, dims, saved[i], dep=None if started is None else started[-1])
        gr["w_uq"] = _heads_join(gr["w_uq"], heads, NOPE_DIM, ROPE_DIM)
        gr["w_ukv"] = _heads_join(gr["w_ukv"], heads, NOPE_DIM, V_DIM)
        layer_grads[i] = gr
        if started is not None:
            received[i + 1] = _exchange_wait(started, dh, gather=False, name=f"grads_exchange_wait_{i + 1}")
        blocks = [_blocks_from_full(SHARD_KIND[n], gr[n]).astype(BF16) for n in SHARDED]
        lands = [_own_slot_filled(lax.dynamic_index_in_dim(b, me, 0, keepdims=False), me) for b in blocks]
        started = _exchange_start(blocks, lands, gather=False, name=f"grads_exchange_start_{i}", dep=dh)
    received[0] = _exchange_wait(started, dh, gather=False, name="grads_exchange_wait_0")
    grad_x = dh[None]

    out = {}
    for a, n in enumerate(SHARDED):
        shape = weights[n].shape
        rec = jnp.stack([received[i][a] for i in range(depth)], axis=1)
        rec2 = rec.reshape((N_DEV, depth, -1, shape[-1]))
        res = _adamw(rec2, _as2d(weights[n]), _as2d(m_in[n]), _as2d(v_in[n]), name="adamw_" + n)
        out[n] = tuple(r.reshape(shape) for r in res)

    small_names = REPLICATED + ("final_norm_g",)

    def pack(get):
        rows = []
        for n in REPLICATED:
            for i in range(depth):
                rows.append(get(n, i).reshape(-1))
        rows.append(get("final_norm_g", None).reshape(-1))
        flat = jnp.concatenate(rows)
        return flat.reshape(1, -1, LANES)

    g_small = pack(lambda n, i: g_final if i is None else layer_grads[i][n])
    w_small = pack(lambda n, i: weights[n] if i is None else weights[n][i])
    m_small = pack(lambda n, i: m_in[n] if i is None else m_in[n][i])
    v_small = pack(lambda n, i: v_in[n] if i is None else v_in[n][i])
    (g_all,) = _all_gather([g_small], name="small_grads_all_gather")
    res_small = _adamw(g_all, w_small, m_small, v_small, name="adamw_small")

    def unpack(flat3):
        flat = flat3.reshape(-1)
        res, off = {}, 0
        for n in REPLICATED:
            width = weights[n].shape[-1]
            res[n] = flat[off:off + depth * width].reshape(depth, width)
            off += depth * width
        res["final_norm_g"] = flat[off:off + d_model]
        return res

    small = [unpack(r) for r in res_small]
    for n in small_names:
        out[n] = tuple(small[k][n] for k in range(4))

    outs = [loss, grad_x]
    for k in range(4):
        outs += [out[n][k] for n in WEIGHT_ORDER]
    return tuple(outs)
```

```python
import functools
import math

import jax
import jax.numpy as jnp
from jax import lax
from jax.experimental import pallas as pl
from jax.experimental.pallas import tpu as pltpu

F32 = jnp.float32
BF16 = jnp.bfloat16

N_DEV = 8
MESH_AXES = ("x", "y", "c")
NOPE_DIM = 128
ROPE_DIM = 64
V_DIM = 128
POOL_GROUPS = 4
CONV_TAPS = 3
ROPE_THETA = 10000.0
NORM_EPS = 1e-6
ADAM_LR = 0.001
ADAM_B1 = 0.9
ADAM_B2 = 0.999
ADAM_EPS = 1e-08
ADAM_WD = 0.01
ADAM_STEP = 10
LANES = 128
VMEM_LIMIT_BYTES = 56 * 1024 * 1024

SHARDED = ("w_in", "pool_w", "w_uq", "w_ukv", "w_out", "w_up", "conv_w", "w_down", "w_ple", "w_ple_gate")
SHARD_KIND = {"w_in": "col", "pool_w": "pool", "w_uq": "col", "w_ukv": "col", "w_out": "row", "w_up": "col",
              "conv_w": "col", "w_down": "row", "w_ple": "col", "w_ple_gate": "row"}
REPLICATED = ("norm_mix_g", "pool_scale", "q_norm_g", "kv_norm_g", "norm_ffn_g", "conv_b", "norm_ple_g")
WEIGHT_ORDER = ("norm_mix_g", "w_in", "pool_w", "pool_scale", "q_norm_g", "w_uq", "kv_norm_g", "w_ukv", "w_out",
                "norm_ffn_g", "w_up", "conv_w", "conv_b", "w_down", "norm_ple_g", "w_ple", "w_ple_gate",
                "final_norm_g")

_pcall = pl.pallas_call


def _params(*sem):
    return pltpu.CompilerParams(dimension_semantics=sem or None, vmem_limit_bytes=VMEM_LIMIT_BYTES)


def _tile(n, pref, mult=LANES):
    if n <= pref:
        return n
    t = (pref // mult) * mult
    while t >= mult:
        if n % t == 0:
            return t
        t -= mult
    return n


MM_VMEM_BUDGET_BYTES = 40 * 1024 * 1024
MM_TILE_PREFS = (1024, 512, 256)
MM_MIN_TK = 1024


def _mm_tiles(m, n, k, a_bytes, b_bytes, o_bytes, has_add):
    tn = _tile(n, MM_TILE_PREFS[0])
    k_cands = [k] + [t for t in range((k - 1) // LANES * LANES, 0, -LANES) if k % t == 0]
    best = None
    for pref in MM_TILE_PREFS:
        tm = _tile(m, pref)
        for tk in k_cands:
            need = 2 * (tm * tk * a_bytes + tk * tn * b_bytes) + 2 * tm * tn * (o_bytes + (4 if has_add else 0))
            if tk < k:
                need += tm * tn * 4
            if need <= MM_VMEM_BUDGET_BYTES:
                if tk >= min(k, MM_MIN_TK):
                    return tm, tn, tk
                if best is None or tk > best[2]:
                    best = (tm, tn, tk)
                break
    assert best is not None, (m, n, k)
    return best


def _mm(a, b, *, name, ta=False, tb=False, add=None, out_dtype=F32, a_col=(0, None), b_row=(0, None),
        b_col=(0, None), dep=None):
    if ta:
        k_a, m = a.shape
    else:
        m, k_a = a.shape
    k_off_a, k_sz = a_col if not ta else (0, None)
    if k_sz is None:
        k_sz = k_a - k_off_a
    kdim = k_sz
    if tb:
        n_b, k_b = b.shape
    else:
        k_b, n_b = b.shape
    k_off_b, kb_sz = b_row
    if kb_sz is None:
        kb_sz = k_b - k_off_b
    assert kb_sz == kdim, (name, kb_sz, kdim)
    n_off, n = b_col
    if n is None:
        n = n_b - n_off
    tm, tn, tk = _mm_tiles(m, n, kdim, a.dtype.itemsize, b.dtype.itemsize, jnp.dtype(out_dtype).itemsize,
                           add is not None)
    assert k_off_a % tk == 0 and k_off_b % tk == 0 and n_off % tn == 0, name
    ka0, kb0, n0 = k_off_a // tk, k_off_b // tk, n_off // tn
    nk = kdim // tk
    dims = (((0 if ta else 1,), (1 if tb else 0,)), ((), ()))

    def body(*refs):
        a_ref, b_ref = refs[0], refs[1]
        add_ref = refs[2] if add is not None else None
        o_ref = refs[n_in]
        part = lax.dot_general(a_ref[...].astype(BF16), b_ref[...].astype(BF16), dims, preferred_element_type=F32)

        def finish(r):
            if add_ref is not None:
                r = r + add_ref[...].astype(F32)
            o_ref[...] = r.astype(out_dtype)

        if nk == 1:
            finish(part)
            return
        acc = refs[n_in + 1]
        k = pl.program_id(2)

        @pl.when(k == 0)
        def _():
            acc[...] = part

        @pl.when(jnp.logical_and(k > 0, k < nk - 1))
        def _():
            acc[...] += part

        @pl.when(k == nk - 1)
        def _():
            finish(acc[...] + part)

    if ta:
        a_spec = pl.BlockSpec((tk, tm), lambda i, j, k: (k, i))
    else:
        a_spec = pl.BlockSpec((tm, tk), lambda i, j, k: (i, k + ka0))
    if tb:
        b_spec = pl.BlockSpec((tn, tk), lambda i, j, k: (j + n0, k + kb0))
    else:
        b_spec = pl.BlockSpec((tk, tn), lambda i, j, k: (k + kb0, j + n0))
    in_specs = [a_spec, b_spec]
    args = [a, b]
    if add is not None:
        in_specs.append(pl.BlockSpec((tm, tn), lambda i, j, k: (i, j)))
        args.append(add)
    if dep is not None:
        in_specs.append(HBM_SPEC)
        args.append(dep)
    n_in = len(args)
    return _pcall(
        body, name=name, grid=(m // tm, n // tn, nk), in_specs=in_specs,
        out_specs=pl.BlockSpec((tm, tn), lambda i, j, k: (i, j)),
        out_shape=jax.ShapeDtypeStruct((m, n), out_dtype),
        scratch_shapes=[pltpu.VMEM((tm, tn), F32)] if nk > 1 else [],
        compiler_params=_params("parallel", "parallel", "arbitrary"),
    )(*args)


def _rms_fwd(h, g, *, name, col_block=0, dep=None):
    s = h.shape[0]
    d = g.shape[-1]
    ts = _tile(s, 512, 8)

    def body(h_ref, g_ref, *rest):
        n_ref = rest[-1]
        x = h_ref[...]
        r = lax.rsqrt(jnp.mean(x * x, axis=-1, keepdims=True) + NORM_EPS)
        n_ref[...] = (x * r * g_ref[...]).astype(BF16)

    deps = [] if dep is None else [dep]
    return _pcall(
        body, name=name, grid=(s // ts,),
        in_specs=[pl.BlockSpec((ts, d), lambda i: (i, col_block)), pl.BlockSpec((1, d), lambda i: (0, 0))]
        + [HBM_SPEC] * len(deps),
        out_specs=pl.BlockSpec((ts, d), lambda i: (i, 0)),
        out_shape=jax.ShapeDtypeStruct((s, d), BF16),
        compiler_params=_params("parallel"),
    )(h, g.reshape(1, d), *deps)


def _rms_bwd(dn, h, g, *, name, res=None, col_block=0, out_dtype=F32):
    s = dn.shape[0]
    d = g.shape[-1]
    ts = _tile(s, 512, 8)

    def body(*refs):
        if res is None:
            dn_ref, h_ref, g_ref, dh_ref, dg_ref = refs
            res_ref = None
        else:
            dn_ref, h_ref, g_ref, res_ref, dh_ref, dg_ref = refs
        i = pl.program_id(0)
        x = h_ref[...]
        r = lax.rsqrt(jnp.mean(x * x, axis=-1, keepdims=True) + NORM_EPS)
        nh = x * r
        dnv = dn_ref[...]
        gd = dnv * g_ref[...]
        dh = (gd - nh * jnp.mean(gd * nh, axis=-1, keepdims=True)) * r
        if res_ref is not None:
            dh = dh + res_ref[...]
        dh_ref[...] = dh.astype(out_dtype)
        part = jnp.sum(dnv * nh, axis=0, keepdims=True)

        @pl.when(i == 0)
        def _():
            dg_ref[...] = part

        @pl.when(i > 0)
        def _():
            dg_ref[...] += part

    row = pl.BlockSpec((ts, d), lambda i: (i, 0))
    in_specs = [row, pl.BlockSpec((ts, d), lambda i: (i, col_block)), pl.BlockSpec((1, d), lambda i: (0, 0))]
    args = [dn, h, g.reshape(1, d)]
    if res is not None:
        in_specs.append(row)
        args.append(res)
    return _pcall(
        body, name=name, grid=(s // ts,), in_specs=in_specs,
        out_specs=(row, pl.BlockSpec((1, d), lambda i: (0, 0))),
        out_shape=(jax.ShapeDtypeStruct((s, d), out_dtype), jax.ShapeDtypeStruct((1, d), F32)),
        compiler_params=_params("arbitrary"),
    )(*args)


ROW_CHUNK = 512


def _rows_with_halo(ref, r, t_rows, n_chunks, before, after):
    r0 = r * t_rows
    parts = []
    if before:
        hb = ref[pl.ds(pl.multiple_of(jnp.maximum(r0 - before, 0), before), before), :]
        parts.append(jnp.where(r > 0, hb, jnp.zeros_like(hb)))
    parts.append(ref[pl.ds(pl.multiple_of(r0, t_rows), t_rows), :])
    if after:
        ha = ref[pl.ds(pl.multiple_of(jnp.minimum(r0 + t_rows, n_chunks * t_rows - after), after), after), :]
        parts.append(jnp.where(r < n_chunks - 1, ha, jnp.zeros_like(ha)))
    return jnp.concatenate(parts, axis=0)


POOL_HALO = 16


def _pool_fwd(u, pool_w, pool_scale, *, c):
    s = u.shape[0]
    g_n = POOL_GROUPS
    tr = _tile(s, ROW_CHUNK, POOL_HALO)
    n_chunks = s // tr

    def body(u_ref, pw_ref, sc_ref, y_ref, d_ref):
        r = pl.program_id(1)
        w = jnp.left_shift(2, pl.program_id(0))
        xe = _rows_with_halo(u_ref, r, tr, n_chunks, POOL_HALO, 0)
        acc = xe
        for k in (1, 2, 4, 8):
            acc = jnp.where(k < w, acc + pltpu.roll(acc, k, 0), acc)
        t = r * tr + lax.broadcasted_iota(jnp.int32, (tr, 1), 0)
        cnt = jnp.minimum(t + 1, w).astype(F32)
        diff = (acc[POOL_HALO:] / cnt - xe[POOL_HALO:]).astype(BF16)
        d_ref[...] = diff
        y = jnp.dot(diff, pw_ref[...], preferred_element_type=F32) * sc_ref[...]
        y_ref[...] = y.astype(BF16)

    out = pl.BlockSpec((tr, c), lambda g, r: (r, g))
    return _pcall(
        body, name="pool_fwd", grid=(g_n, n_chunks),
        in_specs=[pl.BlockSpec((s, c), lambda g, r: (0, g)), pl.BlockSpec((None, c, c), lambda g, r: (g, 0, 0)),
                  pl.BlockSpec((1, c), lambda g, r: (0, g))],
        out_specs=(out, out),
        out_shape=(jax.ShapeDtypeStruct((s, g_n * c), BF16), jax.ShapeDtypeStruct((s, g_n * c), BF16)),
        compiler_params=_params("parallel", "arbitrary"),
    )(u, pool_w, pool_scale.reshape(1, g_n * c))


def _pool_bwd(dcat, diff, pool_w, pool_scale, *, c):
    s = dcat.shape[0]
    g_n = POOL_GROUPS
    tr = _tile(s, ROW_CHUNK, POOL_HALO)
    n_chunks = s // tr

    def body(dy_ref, d_ref, pw_ref, sc_ref, du_ref, dpw_ref, dsc_ref):
        r = pl.program_id(1)
        w = jnp.left_shift(2, pl.program_id(0))
        dye = _rows_with_halo(dy_ref, r, tr, n_chunks, 0, POOL_HALO)
        diff = d_ref[pl.ds(pl.multiple_of(r * tr, tr), tr), :]
        pw = pw_ref[...]
        yp = jnp.dot(diff, pw, preferred_element_type=F32)
        dsc = jnp.sum(dye[:tr] * yp, axis=0, keepdims=True)
        dyp = (dye * sc_ref[...]).astype(BF16)
        ddiff = lax.dot_general(dyp, pw, (((1,), (1,)), ((), ())), preferred_element_type=F32)
        dpw = lax.dot_general(diff, dyp[:tr], (((0,), (0,)), ((), ())), preferred_element_type=F32)
        t = r * tr + lax.broadcasted_iota(jnp.int32, (tr + POOL_HALO, 1), 0)
        cnt = jnp.minimum(t + 1, w).astype(F32)
        acc = ddiff / cnt
        rows = tr + POOL_HALO
        for k in (1, 2, 4, 8):
            acc = jnp.where(k < w, acc + pltpu.roll(acc, rows - k, 0), acc)
        du_ref[...] = (acc[:tr] - ddiff[:tr]).astype(BF16)

        @pl.when(r == 0)
        def _():
            dpw_ref[...] = dpw
            dsc_ref[...] = dsc

        @pl.when(r > 0)
        def _():
            dpw_ref[...] += dpw
            dsc_ref[...] += dsc

    col = lambda g, r: (0, g)
    wspec = pl.BlockSpec((None, c, c), lambda g, r: (g, 0, 0))
    vec = pl.BlockSpec((1, c), col)
    return _pcall(
        body, name="pool_bwd", grid=(g_n, n_chunks),
        in_specs=[pl.BlockSpec((s, c), col), pl.BlockSpec((s, c), col), wspec, vec],
        out_specs=(pl.BlockSpec((tr, c), lambda g, r: (r, g)), wspec, vec),
        out_shape=(jax.ShapeDtypeStruct((s, g_n * c), BF16), jax.ShapeDtypeStruct((g_n, c, c), F32),
                   jax.ShapeDtypeStruct((1, g_n * c), F32)),
        compiler_params=_params("parallel", "arbitrary"),
    )(dcat, diff, pool_w, pool_scale.reshape(1, g_n * c))


def _swap_halves(x, lane):
    return jnp.where((lane % ROPE_DIM) < ROPE_DIM // 2, pltpu.roll(x, LANES - ROPE_DIM // 2, 1),
                     pltpu.roll(x, ROPE_DIM // 2, 1))


def _qkv_prep(q, kv, kr2, cos_t, sin_t, *, heads):
    s = q.shape[0]
    ts = _tile(s, 512, 8)

    def body(qn_ref, qr_ref, kn_ref, v_ref, kr_ref, cos_ref, sin_ref, qo_ref, ko_ref, vo_ref):
        half = pl.program_id(1) % 2
        lane = lax.broadcasted_iota(jnp.int32, (ts, LANES), 1)
        cos_v = cos_ref[...]
        sin_v = sin_ref[...]

        def rope(x):
            return x * cos_v + _swap_halves(x, lane) * sin_v

        qo_ref[:, :LANES] = qn_ref[...].astype(BF16)
        qo_ref[:, LANES:] = jnp.where(lane // ROPE_DIM == half, rope(qr_ref[...]), 0.0).astype(BF16)
        ko_ref[:, :LANES] = kn_ref[...].astype(BF16)
        ko_ref[:, LANES:] = rope(kr_ref[...]).astype(BF16)
        vo_ref[...] = v_ref[...].astype(BF16)

    blk = lambda f: pl.BlockSpec((ts, LANES), f)
    return _pcall(
        body, name="qkv_prep", grid=(s // ts, heads),
        in_specs=[blk(lambda i, h: (i, h)), blk(lambda i, h: (i, heads + h // 2)),
                  blk(lambda i, h: (i, h)), blk(lambda i, h: (i, heads + h)),
                  blk(lambda i, h: (i, 0)), blk(lambda i, h: (i, 0)), blk(lambda i, h: (i, 0))],
        out_specs=(pl.BlockSpec((None, ts, 2 * LANES), lambda i, h: (h, i, 0)),
                   pl.BlockSpec((None, ts, 2 * LANES), lambda i, h: (h, i, 0)),
                   pl.BlockSpec((None, ts, LANES), lambda i, h: (h, i, 0))),
        out_shape=(jax.ShapeDtypeStruct((heads, s, 2 * LANES), BF16),
                   jax.ShapeDtypeStruct((heads, s, 2 * LANES), BF16),
                   jax.ShapeDtypeStruct((heads, s, LANES), BF16)),
        compiler_params=_params("parallel", "parallel"),
    )(q, q, kv, kv, kr2, cos_t, sin_t)


LOG2_E = 1.4426950408889634


def _flash_fwd(qp, kp, v, *, scale, tq=512):
    heads, s, dk = qp.shape
    tq = _tile(s, tq, 16)
    tk = tq
    th = tq // 2
    c = scale * LOG2_E

    def body(q_ref, k_ref, v_ref, o_ref, lse_ref):
        i = pl.program_id(1)
        qs = [q_ref[hh * th:(hh + 1) * th, :] for hh in range(2)]

        def block(j, carry, diag):
            start = pl.multiple_of(j * tk, tk)
            kb = k_ref[pl.ds(start, tk), :]
            vb = v_ref[pl.ds(start, tk), :]
            new = []
            for hh in range(2):
                m_old, l_old, acc = carry[hh]
                sc = lax.dot_general(qs[hh], kb, (((1,), (1,)), ((), ())), preferred_element_type=F32)
                if diag:
                    rows = hh * th + lax.broadcasted_iota(jnp.int32, (th, tk), 0)
                    cols = lax.broadcasted_iota(jnp.int32, (th, tk), 1)
                    sc = jnp.where(rows >= cols, sc, -jnp.inf)
                m_new = jnp.maximum(m_old, jnp.max(sc, axis=-1, keepdims=True))
                alpha = jnp.exp2((m_old - m_new) * c)
                p = jnp.exp2((sc - m_new) * c)
                l_new = alpha * l_old + jnp.sum(p, axis=-1, keepdims=True)
                acc = alpha * acc + jnp.dot(p.astype(BF16), vb, preferred_element_type=F32)
                new.append((m_new, l_new, acc))
            return tuple(new)

        init = tuple((jnp.full((th, 1), -jnp.inf, F32), jnp.zeros((th, 1), F32), jnp.zeros((th, V_DIM), F32))
                     for _ in range(2))
        carry = lax.fori_loop(0, i, lambda j, cr: block(j, cr, False), init)
        carry = block(i, carry, True)
        for hh in range(2):
            m_fin, l_fin, acc = carry[hh]
            o_ref[hh * th:(hh + 1) * th, :] = acc / l_fin
            lse_ref[hh * th:(hh + 1) * th, :] = m_fin * scale + jnp.log(l_fin)

    return _pcall(
        body, name="flash_fwd", grid=(heads, s // tq),
        in_specs=[pl.BlockSpec((None, tq, dk), lambda h, i: (h, i, 0)),
                  pl.BlockSpec((None, s, dk), lambda h, i: (h, 0, 0)),
                  pl.BlockSpec((None, s, V_DIM), lambda h, i: (h, 0, 0))],
        out_specs=(pl.BlockSpec((tq, V_DIM), lambda h, i: (i, h)),
                   pl.BlockSpec((None, tq, 1), lambda h, i: (h, i, 0))),
        out_shape=(jax.ShapeDtypeStruct((s, heads * V_DIM), F32), jax.ShapeDtypeStruct((heads, s, 1), F32)),
        compiler_params=_params("parallel", "arbitrary"),
    )(qp, kp, v)


def _flash_bwd(qp, kp, v, dcat, o, lse, *, scale, do_col0, tq=512):
    heads, s, dk = qp.shape
    tq = _tile(s, tq, 16)
    tk = tq
    nq = s // tq

    def body(k_ref, v_ref, q_ref, do_ref, o_ref, lse_ref, dq_ref, dk_ref, dv_ref):
        j = pl.program_id(1)

        @pl.when(j == 0)
        def _():
            dq_ref[...] = jnp.zeros_like(dq_ref)

        kb = k_ref[...]
        vb = v_ref[...]

        def block(i, carry, diag):
            dk_acc, dv_acc = carry
            rows_at = pl.ds(pl.multiple_of(i * tq, tq), tq)
            qb = q_ref[rows_at, :]
            do = do_ref[rows_at, :]
            sc = lax.dot_general(qb, kb, (((1,), (1,)), ((), ())), preferred_element_type=F32) * scale
            if diag:
                rows = lax.broadcasted_iota(jnp.int32, (tq, tk), 0)
                cols = lax.broadcasted_iota(jnp.int32, (tq, tk), 1)
                sc = jnp.where(rows >= cols, sc, -jnp.inf)
            p = jnp.exp(sc - lse_ref[rows_at, :])
            dob = do.astype(BF16)
            dv_acc = dv_acc + lax.dot_general(p.astype(BF16), dob, (((0,), (0,)), ((), ())),
                                              preferred_element_type=F32)
            dp = lax.dot_general(dob, vb, (((1,), (1,)), ((), ())), preferred_element_type=F32)
            delta = jnp.sum(do * o_ref[rows_at, :], axis=-1, keepdims=True)
            ds = (p * (dp - delta) * scale).astype(BF16)
            dk_acc = dk_acc + lax.dot_general(ds, qb, (((0,), (0,)), ((), ())), preferred_element_type=F32)
            dq_ref[rows_at, :] += jnp.dot(ds, kb, preferred_element_type=F32)
            return dk_acc, dv_acc

        carry = block(j, (jnp.zeros((tk, dk), F32), jnp.zeros((tk, V_DIM), F32)), True)
        carry = lax.fori_loop(j + 1, nq, lambda i, cr: block(i, cr, False), carry)
        dk_ref[...] = carry[0]
        dv_ref[...] = carry[1]

    whole = lambda h, j: (h, 0, 0)
    return _pcall(
        body, name="flash_bwd", grid=(heads, nq),
        in_specs=[pl.BlockSpec((None, tk, dk), lambda h, j: (h, j, 0)),
                  pl.BlockSpec((None, tk, V_DIM), lambda h, j: (h, j, 0)),
                  pl.BlockSpec((None, s, dk), whole),
                  pl.BlockSpec((s, V_DIM), lambda h, j: (0, do_col0 + h)),
                  pl.BlockSpec((s, V_DIM), lambda h, j: (0, h)),
                  pl.BlockSpec((None, s, 1), whole)],
        out_specs=(pl.BlockSpec((None, s, dk), whole),
                   pl.BlockSpec((None, tk, dk), lambda h, j: (h, j, 0)),
                   pl.BlockSpec((None, tk, V_DIM), lambda h, j: (h, j, 0))),
        out_shape=(jax.ShapeDtypeStruct((heads, s, dk), F32), jax.ShapeDtypeStruct((heads, s, dk), F32),
                   jax.ShapeDtypeStruct((heads, s, V_DIM), F32)),
        compiler_params=_params("parallel", "arbitrary"),
    )(kp, v, qp, dcat, o, lse)


def _attn_bwd_post(dqp, dkp, dv, cos_t, sin_t):
    heads, s, _ = dqp.shape
    ts = _tile(s, 256, 8)

    def body(dq_ref, dk_ref, dv_ref, cos_ref, sin_ref, q_out, kv_out, kr_out):
        lane = lax.broadcasted_iota(jnp.int32, (ts, LANES), 1)
        cos_v = cos_ref[...]
        sin_v = sin_ref[...]

        def rope_t(dy):
            return dy * cos_v + _swap_halves(dy * sin_v, lane)

        kr_sum = jnp.zeros((ts, LANES), F32)
        for h in range(heads):
            q_out[:, h * LANES:(h + 1) * LANES] = dq_ref[h, :, :LANES].astype(BF16)
            kv_out[:, h * LANES:(h + 1) * LANES] = dk_ref[h, :, :LANES].astype(BF16)
            kv_out[:, (heads + h) * LANES:(heads + h + 1) * LANES] = dv_ref[h].astype(BF16)
            kr_sum = kr_sum + dk_ref[h, :, LANES:]
        for pair in range(heads // 2):
            r = jnp.where(lane < ROPE_DIM, dq_ref[2 * pair, :, LANES:], dq_ref[2 * pair + 1, :, LANES:])
            q_out[:, (heads + pair) * LANES:(heads + pair + 1) * LANES] = rope_t(r).astype(BF16)
        kr = rope_t(kr_sum)
        kr_out[...] = (kr + pltpu.roll(kr, ROPE_DIM, 1)).astype(BF16)

    wq = heads * (NOPE_DIM + ROPE_DIM)
    wkv = heads * (NOPE_DIM + V_DIM)
    tab = pl.BlockSpec((ts, LANES), lambda i: (i, 0))
    return _pcall(
        body, name="attn_bwd_post", grid=(s // ts,),
        in_specs=[pl.BlockSpec((heads, ts, 2 * LANES), lambda i: (0, i, 0)),
                  pl.BlockSpec((heads, ts, 2 * LANES), lambda i: (0, i, 0)),
                  pl.BlockSpec((heads, ts, LANES), lambda i: (0, i, 0)), tab, tab],
        out_specs=(pl.BlockSpec((ts, wq), lambda i: (i, 0)), pl.BlockSpec((ts, wkv), lambda i: (i, 0)), tab),
        out_shape=(jax.ShapeDtypeStruct((s, wq), BF16), jax.ShapeDtypeStruct((s, wkv), BF16),
                   jax.ShapeDtypeStruct((s, LANES), BF16)),
        compiler_params=_params("parallel"),
    )(dqp, dkp, dv, cos_t, sin_t)


CONV_HALO = 8


def _conv_gate(xe, cw_ref, cb_ref):
    return (cw_ref[2:3, :] * xe + cw_ref[1:2, :] * pltpu.roll(xe, 1, 0) + cw_ref[0:1, :] * pltpu.roll(xe, 2, 0)
            + cb_ref[...])


def _ffn_act_fwd(gu, conv_w, conv_b, *, d_ff, tc=256):
    s = gu.shape[0]
    tc = _tile(d_ff, tc)
    nf = d_ff // tc
    tr = _tile(s, ROW_CHUNK, CONV_HALO)
    n_chunks = s // tr

    def body(g_ref, u_ref, cw_ref, cb_ref, a_ref):
        r = pl.program_id(1)
        xe = _rows_with_halo(g_ref, r, tr, n_chunks, CONV_HALO, 0)
        gc = _conv_gate(xe, cw_ref, cb_ref)[CONV_HALO:]
        a_ref[...] = (gc * jax.nn.sigmoid(gc) * u_ref[...]).astype(BF16)

    return _pcall(
        body, name="ffn_act_fwd", grid=(nf, n_chunks),
        in_specs=[pl.BlockSpec((s, tc), lambda j, r: (0, j)), pl.BlockSpec((tr, tc), lambda j, r: (r, nf + j)),
                  pl.BlockSpec((CONV_TAPS, tc), lambda j, r: (0, j)), pl.BlockSpec((1, tc), lambda j, r: (0, j))],
        out_specs=pl.BlockSpec((tr, tc), lambda j, r: (r, j)),
        out_shape=jax.ShapeDtypeStruct((s, d_ff), BF16),
        compiler_params=_params("parallel", "arbitrary"),
    )(gu, gu, conv_w, conv_b.reshape(1, d_ff))


def _ffn_act_bwd(da, gu, conv_w, conv_b, *, d_ff, tc=256):
    s = gu.shape[0]
    tc = _tile(d_ff, tc)
    nf = d_ff // tc
    tr = _tile(s, ROW_CHUNK, CONV_HALO)
    n_chunks = s // tr
    rows = tr + 2 * CONV_HALO
    main = slice(CONV_HALO, CONV_HALO + tr)

    def body(da_ref, g_ref, u_ref, cw_ref, cb_ref, dg_ref, du_ref, dcw_ref, dcb_ref):
        r = pl.program_id(1)
        xe = _rows_with_halo(g_ref, r, tr, n_chunks, CONV_HALO, CONV_HALO)
        dae = _rows_with_halo(da_ref, r, tr, n_chunks, CONV_HALO, CONV_HALO)
        ue = _rows_with_halo(u_ref, r, tr, n_chunks, CONV_HALO, CONV_HALO)
        gc = _conv_gate(xe, cw_ref, cb_ref)
        sg = jax.nn.sigmoid(gc)
        du_ref[...] = (dae * gc * sg)[main].astype(BF16)
        dgc = dae * ue * sg * (1.0 + gc * (1.0 - sg))
        dg = (cw_ref[2:3, :] * dgc + cw_ref[1:2, :] * pltpu.roll(dgc, rows - 1, 0)
              + cw_ref[0:1, :] * pltpu.roll(dgc, rows - 2, 0))
        dg_ref[...] = dg[main].astype(BF16)
        dgc_m = dgc[main]
        dcb = jnp.sum(dgc_m, axis=0, keepdims=True)
        dcw = jnp.concatenate([jnp.sum(dgc_m * pltpu.roll(xe, 2, 0)[main], axis=0, keepdims=True),
                               jnp.sum(dgc_m * pltpu.roll(xe, 1, 0)[main], axis=0, keepdims=True),
                               jnp.sum(dgc_m * xe[main], axis=0, keepdims=True)], axis=0)

        @pl.when(r == 0)
        def _():
            dcb_ref[...] = dcb
            dcw_ref[...] = dcw

        @pl.when(r > 0)
        def _():
            dcb_ref[...] += dcb
            dcw_ref[...] += dcw

    col = pl.BlockSpec((s, tc), lambda j, r: (0, j))
    out = pl.BlockSpec((tr, tc), lambda j, r: (r, j))
    return _pcall(
        body, name="ffn_act_bwd", grid=(nf, n_chunks),
        in_specs=[col, col, pl.BlockSpec((s, tc), lambda j, r: (0, nf + j)),
                  pl.BlockSpec((CONV_TAPS, tc), lambda j, r: (0, j)), pl.BlockSpec((1, tc), lambda j, r: (0, j))],
        out_specs=(out, out, pl.BlockSpec((CONV_TAPS, tc), lambda j, r: (0, j)),
                   pl.BlockSpec((1, tc), lambda j, r: (0, j))),
        out_shape=(jax.ShapeDtypeStruct((s, d_ff), BF16), jax.ShapeDtypeStruct((s, d_ff), BF16),
                   jax.ShapeDtypeStruct((CONV_TAPS, d_ff), F32), jax.ShapeDtypeStruct((1, d_ff), F32)),
        compiler_params=_params("parallel", "arbitrary"),
    )(da, gu, gu, conv_w, conv_b.reshape(1, d_ff))


def _ple_fwd(h, z, e):
    s, d = h.shape
    ts = _tile(s, 512, 8)

    def body(h_ref, z_ref, e_ref, o_ref):
        o_ref[...] = h_ref[...] + e_ref[...] * jax.nn.sigmoid(z_ref[...])

    row = pl.BlockSpec((ts, d), lambda i: (i, 0))
    return _pcall(body, name="ple_fwd", grid=(s // ts,), in_specs=[row, row, row], out_specs=row,
                  out_shape=jax.ShapeDtypeStruct((s, d), F32), compiler_params=_params("parallel"))(h, z, e)


def _ple_bwd(dh, z, e, dep=None):
    s, d = dh.shape
    ts = _tile(s, 512, 8)

    def body(dh_ref, z_ref, e_ref, *rest):
        de_ref, dz_ref = rest[-2:]
        gt = jax.nn.sigmoid(z_ref[...])
        dhv = dh_ref[...]
        de_ref[...] = (dhv * gt).astype(BF16)
        dz_ref[...] = (dhv * e_ref[...] * gt * (1.0 - gt)).astype(BF16)

    row = pl.BlockSpec((ts, d), lambda i: (i, 0))
    deps = [] if dep is None else [dep]
    return _pcall(body, name="ple_bwd", grid=(s // ts,), in_specs=[row, row, row] + [HBM_SPEC] * len(deps),
                  out_specs=(row, row),
                  out_shape=(jax.ShapeDtypeStruct((s, d), BF16), jax.ShapeDtypeStruct((s, d), BF16)),
                  compiler_params=_params("parallel"))(dh, z, e, *deps)


def _loss_head(h, g, target):
    s, d = h.shape
    ts = _tile(s, 512, 8)

    def body(h_ref, g_ref, t_ref, loss_ref, dh_ref, dg_ref):
        i = pl.program_id(0)
        x = h_ref[...]
        gv = g_ref[...]
        r = lax.rsqrt(jnp.mean(x * x, axis=-1, keepdims=True) + NORM_EPS)
        nh = x * r
        err = nh * gv - t_ref[...]
        part_loss = 0.5 * jnp.sum(jnp.mean(err * err, axis=-1, keepdims=True), axis=0, keepdims=True)
        dy = err * (1.0 / d)
        gd = dy * gv
        dh_ref[...] = (gd - nh * jnp.mean(gd * nh, axis=-1, keepdims=True)) * r
        part_g = jnp.sum(dy * nh, axis=0, keepdims=True)
        part_l = jnp.broadcast_to(part_loss, (1, LANES))

        @pl.when(i == 0)
        def _():
            dg_ref[...] = part_g
            loss_ref[...] = part_l

        @pl.when(i > 0)
        def _():
            dg_ref[...] += part_g
            loss_ref[...] += part_l

    row = pl.BlockSpec((ts, d), lambda i: (i, 0))
    vec = pl.BlockSpec((1, d), lambda i: (0, 0))
    return _pcall(
        body, name="loss_head", grid=(s // ts,), in_specs=[row, vec, row],
        out_specs=(pl.BlockSpec((1, LANES), lambda i: (0, 0)), row, vec),
        out_shape=(jax.ShapeDtypeStruct((1, LANES), F32), jax.ShapeDtypeStruct((s, d), F32),
                   jax.ShapeDtypeStruct((1, d), F32)),
        compiler_params=_params("arbitrary"),
    )(h, g.reshape(1, d), target)


HBM_SPEC = pl.BlockSpec(memory_space=pl.ANY)


def _flat_index(px, py, pc):
    return 4 * px + 2 * py + pc


def _all_gather(shards, *, name):
    n = len(shards)

    def body(*refs):
        ins, outs = refs[:n], refs[n:2 * n]
        send_sems, recv_sems, local_sems = refs[2 * n:]
        x, y, c = lax.axis_index("x"), lax.axis_index("y"), lax.axis_index("c")
        me, sibling = (x, y, c), (x, y, 1 - c)
        chips = [(1 - x, y), (x, 1 - y), (1 - x, 1 - y)]

        def copy(a, k, block, to, src=None):
            slot = outs[a].at[_flat_index(*block)]
            return pltpu.make_async_remote_copy(
                src_ref=slot if src is None else src, dst_ref=slot,
                send_sem=send_sems.at[a, k], recv_sem=recv_sems.at[a, k],
                device_id=to, device_id_type=pl.DeviceIdType.MESH)

        mine, first, passed = [], [], []
        for a in range(n):
            cp = pltpu.make_async_copy(ins[a], outs[a].at[_flat_index(*me)], local_sems.at[a])
            cp.start()
            mine.append(cp)
            first.append(copy(a, 0, me, sibling, src=ins[a]))
            first += [copy(a, 1 + j, me, (*chip, c), src=ins[a]) for j, chip in enumerate(chips)]
        for cp in first:
            cp.start()
        for j, chip in enumerate(chips):
            for a in range(n):
                copy(a, 1 + j, (*chip, c), me).wait_recv()
                fwd = copy(a, 4 + j, (*chip, c), sibling)
                fwd.start()
                passed.append(fwd)
        for a in range(n):
            copy(a, 0, sibling, me).wait_recv()
            for j, chip in enumerate(chips):
                copy(a, 4 + j, (*chip, 1 - c), me).wait_recv()
        for cp in first + passed:
            cp.wait_send()
        for cp in mine:
            cp.wait()

    return _pcall(
        body, name=name,
        in_specs=[HBM_SPEC] * n, out_specs=[HBM_SPEC] * n,
        out_shape=[jax.ShapeDtypeStruct((N_DEV,) + a.shape, a.dtype) for a in shards],
        scratch_shapes=[pltpu.SemaphoreType.DMA((n, 7)), pltpu.SemaphoreType.DMA((n, 7)),
                        pltpu.SemaphoreType.DMA((n,))],
    )(*shards)


HBM_ONLY = pl.BlockSpec(memory_space=pltpu.HBM)
SEM_SPEC = pl.BlockSpec(memory_space=pltpu.SEMAPHORE)
N_PEERS = N_DEV - 1
PEER_FLIPS = ((0, 0, 1), (1, 0, 0), (0, 1, 0), (1, 1, 0), (1, 0, 1), (0, 1, 1), (1, 1, 1))


def _exchange_refs(gather, src_refs, land_refs, send_sems, recv_sems):
    x, y, c = lax.axis_index("x"), lax.axis_index("y"), lax.axis_index("c")
    me = _flat_index(x, y, c)
    peers = [(x ^ fx, y ^ fy, c ^ fc) for fx, fy, fc in PEER_FLIPS]

    def out_copy(a, k):
        src = src_refs[a] if gather else src_refs[a].at[_flat_index(*peers[k])]
        return pltpu.make_async_remote_copy(
            src_ref=src, dst_ref=land_refs[a].at[me], send_sem=send_sems.at[a * N_PEERS + k],
            recv_sem=recv_sems.at[a * N_PEERS + k], device_id=peers[k], device_id_type=pl.DeviceIdType.MESH)

    def in_copy(a, k):
        src = src_refs[a] if gather else src_refs[a].at[me]
        return pltpu.make_async_remote_copy(
            src_ref=src, dst_ref=land_refs[a].at[_flat_index(*peers[k])], send_sem=send_sems.at[a * N_PEERS + k],
            recv_sem=recv_sems.at[a * N_PEERS + k], device_id=peers[k], device_id_type=pl.DeviceIdType.MESH)

    return out_copy, in_copy


def _exchange_start(srcs, lands, *, gather, name, dep):
    n = len(srcs)

    def body(*refs):
        src_refs, land_refs = refs[:n], refs[n:2 * n]
        send_sems, recv_sems = refs[2 * n + 1], refs[2 * n + 2]
        token = refs[-1]
        out_copy, _ = _exchange_refs(gather, src_refs, land_refs, send_sems, recv_sems)
        for k in range(N_PEERS):
            for a in range(n):
                out_copy(a, k).start()
        token[...] = jnp.zeros_like(token)

    hbm = lambda a: pltpu.with_memory_space_constraint(a, pltpu.HBM)
    return _pcall(
        body, name=name,
        out_shape=(pltpu.SemaphoreType.DMA((n * N_PEERS,)), pltpu.SemaphoreType.DMA((n * N_PEERS,)),
                   *[pltpu.HBM(a.shape, a.dtype) for a in srcs], *[pltpu.HBM(a.shape, a.dtype) for a in lands],
                   jax.ShapeDtypeStruct((8, LANES), F32)),
        in_specs=[HBM_ONLY] * (2 * n) + [HBM_SPEC],
        out_specs=(SEM_SPEC, SEM_SPEC, *[HBM_ONLY] * (2 * n), pl.BlockSpec(memory_space=pltpu.VMEM)),
        input_output_aliases={i: 2 + i for i in range(2 * n)},
        compiler_params=pltpu.CompilerParams(has_side_effects=pltpu.SideEffectType.DATAFLOW_SIDE_EFFECTING),
    )(*[hbm(a) for a in srcs], *[hbm(a) for a in lands], dep)


def _exchange_wait(started, after, *, gather, name):
    send_sems, recv_sems = started[0], started[1]
    n = (len(started) - 3) // 2
    srcs, lands = started[2:2 + n], started[2 + n:2 + 2 * n]

    def body(*refs):
        src_refs, land_refs = refs[:n], refs[n:2 * n]
        s_sems, r_sems = refs[2 * n], refs[2 * n + 1]
        out_copy, in_copy = _exchange_refs(gather, src_refs, land_refs, s_sems, r_sems)
        for k in range(N_PEERS):
            for a in range(n):
                out_copy(a, k).wait_send()
                in_copy(a, k).wait_recv()

    res = _pcall(
        body, name=name,
        out_shape=tuple(pltpu.HBM(a.shape, a.dtype) for a in (*srcs, *lands)),
        in_specs=[HBM_ONLY] * (2 * n) + [SEM_SPEC, SEM_SPEC, HBM_SPEC],
        out_specs=tuple([HBM_ONLY] * (2 * n)),
        input_output_aliases={i: i for i in range(2 * n)},
        compiler_params=pltpu.CompilerParams(has_side_effects=pltpu.SideEffectType.DATAFLOW_SIDE_EFFECTING),
    )(*srcs, *lands, send_sems, recv_sems, after)
    return list(res[n:])


def _own_slot_filled(block, me):
    land = lax.empty((N_DEV,) + block.shape, block.dtype)
    return lax.dynamic_update_slice(land, block[None], (me,) + (0,) * block.ndim)


def _adam_math(g, w, m, v):
    m = ADAM_B1 * m + (1.0 - ADAM_B1) * g
    v = ADAM_B2 * v + (1.0 - ADAM_B2) * jnp.square(g)
    m_hat = m / (1.0 - ADAM_B1 ** ADAM_STEP)
    v_hat = v / (1.0 - ADAM_B2 ** ADAM_STEP)
    delta = -ADAM_LR * (m_hat / (jnp.sqrt(v_hat) + ADAM_EPS) + ADAM_WD * w)
    return delta, m, v


def _adamw(contrib, w, m, v, *, name, dep=None):
    _, layers, r, c = contrib.shape
    tr = _tile(r, max(8, (256 * 1024 // c) // 8 * 8), 8)

    def body(g_ref, w_ref, m_ref, v_ref, *rest):
        go_ref, d_ref, mo_ref, vo_ref = rest[-4:]
        g = g_ref[0].astype(F32)
        for k in range(1, N_DEV):
            g = g + g_ref[k].astype(F32)
        delta, m_new, v_new = _adam_math(g, w_ref[...], m_ref[...], v_ref[...])
        go_ref[...] = g
        d_ref[...] = delta
        mo_ref[...] = m_new
        vo_ref[...] = v_new

    deps = [] if dep is None else [dep]
    blk = pl.BlockSpec((None, tr, c), lambda l, i: (l, i, 0))
    out = jax.ShapeDtypeStruct((layers, r, c), F32)
    return _pcall(
        body, name=name, grid=(layers, r // tr),
        in_specs=[pl.BlockSpec((N_DEV, None, tr, c), lambda l, i: (0, l, i, 0)), blk, blk, blk]
        + [HBM_SPEC] * len(deps),
        out_specs=(blk, blk, blk, blk), out_shape=(out, out, out, out),
        compiler_params=_params("parallel", "parallel"),
    )(contrib, w, m, v, *deps)


def _heads_split(w, heads, first, second):
    k = w.shape[0]
    w3 = w.reshape(k, heads, first + second)
    return jnp.concatenate([w3[:, :, :first].reshape(k, heads * first),
                            w3[:, :, first:].reshape(k, heads * second)], axis=1)


def _heads_join(w, heads, first, second):
    k = w.shape[0]
    a = w[:, :heads * first].reshape(k, heads, first)
    b = w[:, heads * first:].reshape(k, heads, second)
    return jnp.concatenate([a, b], axis=2).reshape(k, heads * (first + second))


def _full_from_gathered(kind, g):
    if kind == "col":
        return jnp.transpose(g, (1, 0, 2)).reshape(g.shape[1], N_DEV * g.shape[2])
    if kind == "row":
        return g.reshape(N_DEV * g.shape[1], g.shape[2])
    return jnp.transpose(g, (1, 0, 2, 3)).reshape(g.shape[1], N_DEV * g.shape[2], g.shape[3])


def _blocks_from_full(kind, f):
    if kind == "col":
        k, n = f.shape
        return jnp.transpose(f.reshape(k, N_DEV, n // N_DEV), (1, 0, 2))
    if kind == "row":
        k, n = f.shape
        return f.reshape(N_DEV, k // N_DEV, n)
    g, c_in, c = f.shape
    return jnp.transpose(f.reshape(g, N_DEV, c_in // N_DEV, c), (1, 0, 2, 3))


def _rope_tables(positions):
    inv_freq = 1.0 / (ROPE_THETA ** (jnp.arange(0, ROPE_DIM, 2, dtype=F32) / ROPE_DIM))
    ang = positions.astype(F32)[:, None] * inv_freq
    cos, sin = jnp.cos(ang), jnp.sin(ang)
    return jnp.concatenate([cos, cos, cos, cos], axis=-1), jnp.concatenate([-sin, sin, -sin, sin], axis=-1)


def _layer_fwd(h0, p_i, w, rep, tabs, dims, dep=None):
    heads, d_pool, q_lora, d_ff = dims["heads"], dims["d_pool"], dims["q_lora"], dims["d_ff"]
    c = d_pool // POOL_GROUPS
    cos_t, sin_t = tabs
    scale = 1.0 / math.sqrt(NOPE_DIM + ROPE_DIM)
    n1 = _rms_fwd(h0, rep["norm_mix_g"], name="rms_mix_fwd", dep=dep)
    u = _mm(n1, w["w_in"], name="mm_in_fwd")
    y_pool, diff = _pool_fwd(u, w["pool_w"], rep["pool_scale"], c=c)
    nq = _rms_fwd(u, rep["q_norm_g"], name="rms_q_fwd", col_block=d_pool // q_lora)
    nkv = _rms_fwd(u, rep["kv_norm_g"], name="rms_kv_fwd", col_block=d_pool // q_lora + 1)
    q = _mm(nq, w["w_uq"], name="mm_uq_fwd")
    kv = _mm(nkv, w["w_ukv"], name="mm_ukv_fwd")
    kr = u[:, d_pool + 2 * q_lora:]
    kr2 = jnp.concatenate([kr, kr], axis=-1)
    qp, kp, v = _qkv_prep(q, kv, kr2, cos_t, sin_t, heads=heads)
    o, lse = _flash_fwd(qp, kp, v, scale=scale)
    t = _mm(y_pool, w["w_out"], name="mm_out_pool_fwd", add=h0, b_row=(0, d_pool))
    h1 = _mm(o, w["w_out"], name="mm_out_att_fwd", add=t, b_row=(d_pool, None))
    n2 = _rms_fwd(h1, rep["norm_ffn_g"], name="rms_ffn_fwd")
    gu = _mm(n2, w["w_up"], name="mm_up_fwd")
    a = _ffn_act_fwd(gu, w["conv_w"], rep["conv_b"], d_ff=d_ff)
    h2 = _mm(a, w["w_down"], name="mm_down_fwd", add=h1)
    n3 = _rms_fwd(h2, rep["norm_ple_g"], name="rms_ple_fwd")
    z = _mm(n3, w["w_ple_gate"], name="mm_pgate_fwd")
    e = _mm(p_i, w["w_ple"], name="mm_ple_fwd")
    h3 = _ple_fwd(h2, z, e)
    saved = dict(h0=h0, n1=n1, u=u, y_pool=y_pool, diff=diff, nq=nq, nkv=nkv, qp=qp, kp=kp, v=v, o=o, lse=lse,
                 h1=h1, n2=n2, gu=gu, a=a, h2=h2, n3=n3, z=z, e=e)
    return h3, saved


def _layer_bwd(dh3, p_i, w, rep, tabs, dims, sv, dep=None, mid=None):
    heads, d_pool, q_lora, d_ff = dims["heads"], dims["d_pool"], dims["q_lora"], dims["d_ff"]
    c = d_pool // POOL_GROUPS
    cos_t, sin_t = tabs
    scale = 1.0 / math.sqrt(NOPE_DIM + ROPE_DIM)
    gr = {}
    de, dz = _ple_bwd(dh3, sv["z"], sv["e"], dep)
    gr["w_ple"] = _mm(p_i, de, name="mm_ple_dw", ta=True, out_dtype=BF16)
    gr["w_ple_gate"] = _mm(sv["n3"], dz, name="mm_pgate_dw", ta=True, out_dtype=BF16)
    dn3 = _mm(dz, w["w_ple_gate"], name="mm_pgate_dx", tb=True)
    dh2, gr["norm_ple_g"] = _rms_bwd(dn3, sv["h2"], rep["norm_ple_g"], name="rms_ple_bwd", res=dh3)
    gr["w_down"] = _mm(sv["a"], dh2, name="mm_down_dw", ta=True, out_dtype=BF16)
    da = _mm(dh2, w["w_down"], name="mm_down_dx", tb=True)
    dgate, dup, gr["conv_w"], gr["conv_b"] = _ffn_act_bwd(da, sv["gu"], w["conv_w"], rep["conv_b"], d_ff=d_ff)
    gr["w_up"] = jnp.concatenate([_mm(sv["n2"], dgate, name="mm_up_gate_dw", ta=True, out_dtype=BF16),
                                  _mm(sv["n2"], dup, name="mm_up_up_dw", ta=True, out_dtype=BF16)], axis=1)
    dn2 = _mm(dgate, w["w_up"], name="mm_up_gate_dx", tb=True, b_row=(0, d_ff))
    dn2 = _mm(dup, w["w_up"], name="mm_up_up_dx", tb=True, b_row=(d_ff, d_ff), add=dn2)
    dh1, gr["norm_ffn_g"] = _rms_bwd(dn2, sv["h1"], rep["norm_ffn_g"], name="rms_ffn_bwd", res=dh2)
    dep_mix = None if mid is None else mid(dh1, gr)
    dw_out_pool = _mm(sv["y_pool"], dh1, name="mm_out_pool_dw", ta=True, out_dtype=BF16, dep=dep_mix)
    dw_out_att = _mm(sv["o"], dh1, name="mm_out_att_dw", ta=True, out_dtype=BF16)
    gr["w_out"] = jnp.concatenate([dw_out_pool, dw_out_att], axis=0)
    dcat = _mm(dh1, w["w_out"], name="mm_out_dx", tb=True)
    do_col0 = d_pool // V_DIM
    dqp, dkp, dv = _flash_bwd(sv["qp"], sv["kp"], sv["v"], dcat, sv["o"], sv["lse"], scale=scale, do_col0=do_col0)
    dq, dkv, dkr2 = _attn_bwd_post(dqp, dkp, dv, cos_t, sin_t)
    gr["w_uq"] = _mm(sv["nq"], dq, name="mm_uq_dw", ta=True, out_dtype=BF16)
    gr["w_ukv"] = _mm(sv["nkv"], dkv, name="mm_ukv_dw", ta=True, out_dtype=BF16)
    dnq = _mm(dq, w["w_uq"], name="mm_uq_dx", tb=True)
    dnkv = _mm(dkv, w["w_ukv"], name="mm_ukv_dx", tb=True)
    dcq, gr["q_norm_g"] = _rms_bwd(dnq, sv["u"], rep["q_norm_g"], name="rms_q_bwd",
                                   col_block=d_pool // q_lora, out_dtype=BF16)
    dckv, gr["kv_norm_g"] = _rms_bwd(dnkv, sv["u"], rep["kv_norm_g"], name="rms_kv_bwd",
                                     col_block=d_pool // q_lora + 1, out_dtype=BF16)
    du_pool, gr["pool_w"], gr["pool_scale"] = _pool_bwd(dcat, sv["diff"], w["pool_w"], rep["pool_scale"], c=c)
    du = jnp.concatenate([du_pool, dcq, dckv, dkr2[:, :ROPE_DIM]], axis=-1)
    gr["w_in"] = _mm(sv["n1"], du, name="mm_in_dw", ta=True, out_dtype=BF16)
    dn1 = _mm(du, w["w_in"], name="mm_in_dx", tb=True)
    dh0, gr["norm_mix_g"] = _rms_bwd(dn1, sv["h0"], rep["norm_mix_g"], name="rms_mix_bwd", res=dh1)
    return dh0, gr


def _as2d(a):
    return a.reshape(a.shape[0], -1, a.shape[-1])


def kernel(x, p, positions, norm_mix_g, w_in, pool_w, pool_scale, q_norm_g, w_uq, kv_norm_g, w_ukv, w_out, norm_ffn_g, w_up, conv_w, conv_b, w_down, norm_ple_g, w_ple, w_ple_gate, final_norm_g, loss_target, m_norm_mix_g, m_w_in, m_pool_w, m_pool_scale, m_q_norm_g, m_w_uq, m_kv_norm_g, m_w_ukv, m_w_out, m_norm_ffn_g, m_w_up, m_conv_w, m_conv_b, m_w_down, m_norm_ple_g, m_w_ple, m_w_ple_gate, m_final_norm_g, v_norm_mix_g, v_w_in, v_pool_w, v_pool_scale, v_q_norm_g, v_w_uq, v_kv_norm_g, v_w_ukv, v_w_out, v_norm_ffn_g, v_w_up, v_conv_w, v_conv_b, v_w_down, v_norm_ple_g, v_w_ple, v_w_ple_gate, v_final_norm_g):
    weights = dict(norm_mix_g=norm_mix_g, w_in=w_in, pool_w=pool_w, pool_scale=pool_scale, q_norm_g=q_norm_g,
                   w_uq=w_uq, kv_norm_g=kv_norm_g, w_ukv=w_ukv, w_out=w_out, norm_ffn_g=norm_ffn_g, w_up=w_up,
                   conv_w=conv_w, conv_b=conv_b, w_down=w_down, norm_ple_g=norm_ple_g, w_ple=w_ple,
                   w_ple_gate=w_ple_gate, final_norm_g=final_norm_g)
    m_in = dict(norm_mix_g=m_norm_mix_g, w_in=m_w_in, pool_w=m_pool_w, pool_scale=m_pool_scale, q_norm_g=m_q_norm_g,
                w_uq=m_w_uq, kv_norm_g=m_kv_norm_g, w_ukv=m_w_ukv, w_out=m_w_out, norm_ffn_g=m_norm_ffn_g,
                w_up=m_w_up, conv_w=m_conv_w, conv_b=m_conv_b, w_down=m_w_down, norm_ple_g=m_norm_ple_g,
                w_ple=m_w_ple, w_ple_gate=m_w_ple_gate, final_norm_g=m_final_norm_g)
    v_in = dict(norm_mix_g=v_norm_mix_g, w_in=v_w_in, pool_w=v_pool_w, pool_scale=v_pool_scale, q_norm_g=v_q_norm_g,
                w_uq=v_w_uq, kv_norm_g=v_kv_norm_g, w_ukv=v_w_ukv, w_out=v_w_out, norm_ffn_g=v_norm_ffn_g,
                w_up=v_w_up, conv_w=v_conv_w, conv_b=v_conv_b, w_down=v_w_down, norm_ple_g=v_norm_ple_g,
                w_ple=v_w_ple, w_ple_gate=v_w_ple_gate, final_norm_g=v_final_norm_g)

    depth = w_in.shape[0]
    s, d_model = x.shape[1], x.shape[2]
    d_pool = pool_scale.shape[-1]
    q_lora = q_norm_g.shape[-1]
    d_ff = conv_b.shape[-1]
    heads = (w_uq.shape[-1] * N_DEV) // (NOPE_DIM + ROPE_DIM)
    dims = dict(heads=heads, d_pool=d_pool, q_lora=q_lora, d_ff=d_ff)

    me = _flat_index(lax.axis_index("x"), lax.axis_index("y"), lax.axis_index("c"))

    def layer_shards(i):
        return [weights[n][i] if n == "conv_w" else weights[n][i].astype(BF16) for n in SHARDED]

    def full_weights(i, gathered_i):
        w = {n: _full_from_gathered(SHARD_KIND[n], g) for n, g in zip(SHARDED, gathered_i)}
        w["w_uq"] = _heads_split(w["w_uq"], heads, NOPE_DIM, ROPE_DIM)
        w["w_ukv"] = _heads_split(w["w_ukv"], heads, NOPE_DIM, V_DIM)
        rep = {n: weights[n][i] for n in REPLICATED}
        return w, rep

    tabs = _rope_tables(positions[0])

    gathered = _all_gather(layer_shards(0), name="weights_all_gather_0")
    layer_w = []
    h = x[0]
    saved = []
    for i in range(depth):
        layer_w.append(full_weights(i, gathered))
        started = None
        if i + 1 < depth:
            shards = layer_shards(i + 1)
            started = _exchange_start(shards, [_own_slot_filled(a, me) for a in shards], gather=True,
                                      name=f"weights_gather_start_{i + 1}", dep=gathered[0] if i == 0 else h)
        w, rep = layer_w[i]
        h, sv = _layer_fwd(h, p[i, 0], w, rep, tabs, dims, dep=None if started is None else started[-1])
        saved.append(sv)
        if started is not None:
            gathered = _exchange_wait(started, h, gather=True, name=f"weights_gather_wait_{i + 1}")
    loss_row, dh, g_final = _loss_head(h, final_norm_g, loss_target[0])
    loss = lax.psum(loss_row[0, 0], MESH_AXES)

    late = ("w_up", "conv_w", "w_down", "w_ple", "w_ple_gate")
    early = tuple(n for n in SHARDED if n not in late)

    def start_grads(names, gr, dep, name):
        blocks = [_blocks_from_full(SHARD_KIND[n], gr[n]).astype(BF16) for n in names]
        lands = [_own_slot_filled(lax.dynamic_index_in_dim(b, me, 0, keepdims=False), me) for b in blocks]
        return _exchange_start(blocks, lands, gather=False, name=name, dep=dep)

    layer_grads = [None] * depth
    received = [dict() for _ in range(depth)]
    pending = None
    for i in reversed(range(depth)):
        w, rep = layer_w[i]
        state = {}

        def mid(dh1, gr, i=i, pending=pending, state=state):
            if pending is not None:
                got = _exchange_wait(pending, dh1, gather=False, name=f"grads_mix_wait_{i + 1}")
                received[i + 1].update(zip(early, got))
            state["ffn"] = start_grads(late, gr, dh1, f"grads_ffn_start_{i}")
            return state["ffn"][-1]

        dh, gr = _layer_bwd(dh, p[i, 0], w, rep, tabs, dims, saved[i],
                            dep=None if pending is None else pending[-1], mid=mid)
        gr["w_uq"] = _heads_join(gr["w_uq"], heads, NOPE_DIM, ROPE_DIM)
        gr["w_ukv"] = _heads_join(gr["w_ukv"], heads, NOPE_DIM, V_DIM)
        layer_grads[i] = gr
        received[i].update(zip(late, _exchange_wait(state["ffn"], dh, gather=False, name=f"grads_ffn_wait_{i}")))
        pending = start_grads(early, gr, dh, f"grads_mix_start_{i}")
    grad_x = dh[None]

    out = {}

    def update(n, dep=None):
        shape = weights[n].shape
        rec = jnp.stack([received[i][n] for i in range(depth)], axis=1)
        rec2 = rec.reshape((N_DEV, depth, -1, shape[-1]))
        res = _adamw(rec2, _as2d(weights[n]), _as2d(m_in[n]), _as2d(v_in[n]), name="adamw_" + n, dep=dep)
        out[n] = tuple(r.reshape(shape) for r in res)

    for k, n in enumerate(late):
        update(n, dep=pending[-1] if k == 0 else None)
    received[0].update(zip(early, _exchange_wait(pending, out[late[-1]][0], gather=False, name="grads_mix_wait_0")))
    for n in early:
        update(n)

    small_names = REPLICATED + ("final_norm_g",)

    def pack(get):
        rows = []
        for n in REPLICATED:
            for i in range(depth):
                rows.append(get(n, i).reshape(-1))
        rows.append(get("final_norm_g", None).reshape(-1))
        flat = jnp.concatenate(rows)
        return flat.reshape(1, -1, LANES)

    g_small = pack(lambda n, i: g_final if i is None else layer_grads[i][n])
    w_small = pack(lambda n, i: weights[n] if i is None else weights[n][i])
    m_small = pack(lambda n, i: m_in[n] if i is None else m_in[n][i])
    v_small = pack(lambda n, i: v_in[n] if i is None else v_in[n][i])
    (g_all,) = _all_gather([g_small], name="small_grads_all_gather")
    res_small = _adamw(g_all, w_small, m_small, v_small, name="adamw_small")

    def unpack(flat3):
        flat = flat3.reshape(-1)
        res, off = {}, 0
        for n in REPLICATED:
            width = weights[n].shape[-1]
            res[n] = flat[off:off + depth * width].reshape(depth, width)
            off += depth * width
        res["final_norm_g"] = flat[off:off + d_model]
        return res

    small = [unpack(r) for r in res_small]
    for n in small_names:
        out[n] = tuple(small[k][n] for k in range(4))

    outs = [loss, grad_x]
    for k in range(4):
        outs += [out[n][k] for n in WEIGHT_ORDER]
    return tuple(outs)
```

```python
import functools
import math

import jax
import jax.numpy as jnp
from jax import lax
from jax.experimental import pallas as pl
from jax.experimental.pallas import tpu as pltpu

F32 = jnp.float32
BF16 = jnp.bfloat16

N_DEV = 8
MESH_AXES = ("x", "y", "c")
NOPE_DIM = 128
ROPE_DIM = 64
V_DIM = 128
POOL_GROUPS = 4
CONV_TAPS = 3
ROPE_THETA = 10000.0
NORM_EPS = 1e-6
ADAM_LR = 0.001
ADAM_B1 = 0.9
ADAM_B2 = 0.999
ADAM_EPS = 1e-08
ADAM_WD = 0.01
ADAM_STEP = 10
LANES = 128
VMEM_LIMIT_BYTES = 56 * 1024 * 1024

SHARDED = ("w_in", "pool_w", "w_uq", "w_ukv", "w_out", "w_up", "conv_w", "w_down", "w_ple", "w_ple_gate")
SHARD_KIND = {"w_in": "col", "pool_w": "pool", "w_uq": "col", "w_ukv": "col", "w_out": "row", "w_up": "col",
              "conv_w": "col", "w_down": "row", "w_ple": "col", "w_ple_gate": "row"}
REPLICATED = ("norm_mix_g", "pool_scale", "q_norm_g", "kv_norm_g", "norm_ffn_g", "conv_b", "norm_ple_g")
WEIGHT_ORDER = ("norm_mix_g", "w_in", "pool_w", "pool_scale", "q_norm_g", "w_uq", "kv_norm_g", "w_ukv", "w_out",
                "norm_ffn_g", "w_up", "conv_w", "conv_b", "w_down", "norm_ple_g", "w_ple", "w_ple_gate",
                "final_norm_g")

_pcall = pl.pallas_call


def _params(*sem):
    return pltpu.CompilerParams(dimension_semantics=sem or None, vmem_limit_bytes=VMEM_LIMIT_BYTES)


def _tile(n, pref, mult=LANES):
    if n <= pref:
        return n
    t = (pref // mult) * mult
    while t >= mult:
        if n % t == 0:
            return t
        t -= mult
    return n


MM_VMEM_BUDGET_BYTES = 40 * 1024 * 1024
MM_TILE_PREFS = (1024, 512, 256)
MM_MIN_TK = 1024


def _mm_tiles(m, n, k, a_bytes, b_bytes, o_bytes, has_add, tn_fixed=None, tk_fixed=None):
    tn = tn_fixed or _tile(n, MM_TILE_PREFS[0])
    if tk_fixed:
        k_cands = [tk_fixed]
    else:
        k_cands = [k] + [t for t in range((k - 1) // LANES * LANES, 0, -LANES) if k % t == 0]
    best = None
    for pref in MM_TILE_PREFS:
        tm = _tile(m, pref)
        for tk in k_cands:
            need = 2 * (tm * tk * a_bytes + tk * tn * b_bytes) + 2 * tm * tn * (o_bytes + (4 if has_add else 0))
            if tk < k:
                need += tm * tn * 4
            if need <= MM_VMEM_BUDGET_BYTES:
                if tk >= min(k, MM_MIN_TK):
                    return tm, tn, tk
                if best is None or tk > best[2]:
                    best = (tm, tn, tk)
                break
    assert best is not None, (m, n, k)
    return best


def _mm(a, b, *, name, ta=False, tb=False, add=None, out_dtype=F32, b_row=(0, None), b_blocks=None,
        out_blocks=None, out_init=None, dep=None):
    m = a.shape[1] if ta else a.shape[0]
    kdim = a.shape[0] if ta else a.shape[1]
    tn_fixed = tk_fixed = None
    k_off_b = 0
    if b_blocks is None:
        n_b, k_b = (b.shape if tb else b.shape[::-1])
        k_off_b, kb_sz = b_row
        kb_sz = k_b - k_off_b if kb_sz is None else kb_sz
        assert kb_sz == kdim, (name, kb_sz, kdim)
        n = n_b
    else:
        first_b, count_b = b_blocks
        per_b = b.shape[2]
        if tb:
            n = b.shape[1]
            assert kdim == count_b * per_b, name
            tk_fixed = per_b
        else:
            n = count_b * per_b
            assert kdim == b.shape[1], name
            tn_fixed = per_b
    if out_blocks is not None:
        nb_out, first_o, tn_fixed = out_blocks
    tm, tn, tk = _mm_tiles(m, n, kdim, a.dtype.itemsize, b.dtype.itemsize, jnp.dtype(out_dtype).itemsize,
                           add is not None, tn_fixed, tk_fixed)
    assert k_off_b % tk == 0 and kdim % tk == 0 and n % tn == 0 and m % tm == 0, name
    kb0 = k_off_b // tk
    nk = kdim // tk
    dims = (((0 if ta else 1,), (1 if tb else 0,)), ((), ()))

    def body(*refs):
        a_ref, b_ref = refs[0], refs[1]
        add_ref = refs[2] if add is not None else None
        o_ref = refs[n_in]
        part = lax.dot_general(a_ref[...].astype(BF16), b_ref[...].astype(BF16), dims, preferred_element_type=F32)

        def finish(r):
            if add_ref is not None:
                r = r + add_ref[...].astype(F32)
            o_ref[...] = r.astype(out_dtype)

        if nk == 1:
            finish(part)
            return
        acc = refs[n_in + 1]
        k = pl.program_id(2)

        @pl.when(k == 0)
        def _():
            acc[...] = part

        @pl.when(jnp.logical_and(k > 0, k < nk - 1))
        def _():
            acc[...] += part

        @pl.when(k == nk - 1)
        def _():
            finish(acc[...] + part)

    if ta:
        a_spec = pl.BlockSpec((tk, tm), lambda i, j, k: (k, i))
    else:
        a_spec = pl.BlockSpec((tm, tk), lambda i, j, k: (i, k))
    if b_blocks is not None and tb:
        b_spec = pl.BlockSpec((None, tn, tk), lambda i, j, k: (k + first_b, j, 0))
    elif b_blocks is not None:
        b_spec = pl.BlockSpec((None, tk, tn), lambda i, j, k: (j + first_b, k, 0))
    elif tb:
        b_spec = pl.BlockSpec((tn, tk), lambda i, j, k: (j, k + kb0))
    else:
        b_spec = pl.BlockSpec((tk, tn), lambda i, j, k: (k + kb0, j))
    in_specs = [a_spec, b_spec]
    args = [a, b]
    if add is not None:
        in_specs.append(pl.BlockSpec((tm, tn), lambda i, j, k: (i, j)))
        args.append(add)
    aliases = {}
    if out_init is not None:
        aliases = {len(args): 0}
        in_specs.append(HBM_SPEC)
        args.append(out_init)
    if dep is not None:
        in_specs.append(HBM_SPEC)
        args.append(dep)
    n_in = len(args)
    if out_blocks is None:
        out_spec = pl.BlockSpec((tm, tn), lambda i, j, k: (i, j))
        out_shape = jax.ShapeDtypeStruct((m, n), out_dtype)
    else:
        out_spec = pl.BlockSpec((None, tm, tn), lambda i, j, k: (j + first_o, i, 0))
        out_shape = jax.ShapeDtypeStruct((nb_out, m, tn), out_dtype)
    return _pcall(
        body, name=name, grid=(m // tm, n // tn, nk), in_specs=in_specs, out_specs=out_spec, out_shape=out_shape,
        scratch_shapes=[pltpu.VMEM((tm, tn), F32)] if nk > 1 else [], input_output_aliases=aliases,
        compiler_params=_params("parallel", "parallel", "arbitrary"),
    )(*args)


def _rms_fwd(h, g, *, name, col_block=0, dep=None):
    s = h.shape[0]
    d = g.shape[-1]
    ts = _tile(s, 512, 8)

    def body(h_ref, g_ref, *rest):
        n_ref = rest[-1]
        x = h_ref[...]
        r = lax.rsqrt(jnp.mean(x * x, axis=-1, keepdims=True) + NORM_EPS)
        n_ref[...] = (x * r * g_ref[...]).astype(BF16)

    deps = [] if dep is None else [dep]
    return _pcall(
        body, name=name, grid=(s // ts,),
        in_specs=[pl.BlockSpec((ts, d), lambda i: (i, col_block)), pl.BlockSpec((1, d), lambda i: (0, 0))]
        + [HBM_SPEC] * len(deps),
        out_specs=pl.BlockSpec((ts, d), lambda i: (i, 0)),
        out_shape=jax.ShapeDtypeStruct((s, d), BF16),
        compiler_params=_params("parallel"),
    )(h, g.reshape(1, d), *deps)


def _rms_bwd(dn, h, g, *, name, res=None, col_block=0, out_dtype=F32):
    s = dn.shape[0]
    d = g.shape[-1]
    ts = _tile(s, 512, 8)

    def body(*refs):
        if res is None:
            dn_ref, h_ref, g_ref, dh_ref, dg_ref = refs
            res_ref = None
        else:
            dn_ref, h_ref, g_ref, res_ref, dh_ref, dg_ref = refs
        i = pl.program_id(0)
        x = h_ref[...]
        r = lax.rsqrt(jnp.mean(x * x, axis=-1, keepdims=True) + NORM_EPS)
        nh = x * r
        dnv = dn_ref[...]
        gd = dnv * g_ref[...]
        dh = (gd - nh * jnp.mean(gd * nh, axis=-1, keepdims=True)) * r
        if res_ref is not None:
            dh = dh + res_ref[...]
        dh_ref[...] = dh.astype(out_dtype)
        part = jnp.sum(dnv * nh, axis=0, keepdims=True)

        @pl.when(i == 0)
        def _():
            dg_ref[...] = part

        @pl.when(i > 0)
        def _():
            dg_ref[...] += part

    row = pl.BlockSpec((ts, d), lambda i: (i, 0))
    in_specs = [row, pl.BlockSpec((ts, d), lambda i: (i, col_block)), pl.BlockSpec((1, d), lambda i: (0, 0))]
    args = [dn, h, g.reshape(1, d)]
    if res is not None:
        in_specs.append(row)
        args.append(res)
    return _pcall(
        body, name=name, grid=(s // ts,), in_specs=in_specs,
        out_specs=(row, pl.BlockSpec((1, d), lambda i: (0, 0))),
        out_shape=(jax.ShapeDtypeStruct((s, d), out_dtype), jax.ShapeDtypeStruct((1, d), F32)),
        compiler_params=_params("arbitrary"),
    )(*args)


ROW_CHUNK = 512


def _rows_with_halo(ref, r, t_rows, n_chunks, before, after):
    r0 = r * t_rows
    parts = []
    if before:
        hb = ref[pl.ds(pl.multiple_of(jnp.maximum(r0 - before, 0), before), before), :]
        parts.append(jnp.where(r > 0, hb, jnp.zeros_like(hb)))
    parts.append(ref[pl.ds(pl.multiple_of(r0, t_rows), t_rows), :])
    if after:
        ha = ref[pl.ds(pl.multiple_of(jnp.minimum(r0 + t_rows, n_chunks * t_rows - after), after), after), :]
        parts.append(jnp.where(r < n_chunks - 1, ha, jnp.zeros_like(ha)))
    return jnp.concatenate(parts, axis=0)


POOL_HALO = 16


def _pool_fwd(u, pool_w, pool_scale, *, c):
    s = u.shape[0]
    g_n = POOL_GROUPS
    tr = _tile(s, ROW_CHUNK, POOL_HALO)
    n_chunks = s // tr

    def body(u_ref, pw_ref, sc_ref, y_ref, d_ref):
        r = pl.program_id(1)
        w = jnp.left_shift(2, pl.program_id(0))
        xe = _rows_with_halo(u_ref, r, tr, n_chunks, POOL_HALO, 0)
        acc = xe
        for k in (1, 2, 4, 8):
            acc = jnp.where(k < w, acc + pltpu.roll(acc, k, 0), acc)
        t = r * tr + lax.broadcasted_iota(jnp.int32, (tr, 1), 0)
        cnt = jnp.minimum(t + 1, w).astype(F32)
        diff = (acc[POOL_HALO:] / cnt - xe[POOL_HALO:]).astype(BF16)
        d_ref[...] = diff
        y = jnp.dot(diff, pw_ref[...], preferred_element_type=F32) * sc_ref[...]
        y_ref[...] = y.astype(BF16)

    out = pl.BlockSpec((tr, c), lambda g, r: (r, g))
    return _pcall(
        body, name="pool_fwd", grid=(g_n, n_chunks),
        in_specs=[pl.BlockSpec((s, c), lambda g, r: (0, g)), pl.BlockSpec((None, c, c), lambda g, r: (g, 0, 0)),
                  pl.BlockSpec((1, c), lambda g, r: (0, g))],
        out_specs=(out, out),
        out_shape=(jax.ShapeDtypeStruct((s, g_n * c), BF16), jax.ShapeDtypeStruct((s, g_n * c), BF16)),
        compiler_params=_params("parallel", "arbitrary"),
    )(u, pool_w, pool_scale.reshape(1, g_n * c))


def _pool_bwd(dcat, diff, pool_w, pool_scale, *, c):
    s = dcat.shape[0]
    g_n = POOL_GROUPS
    tr = _tile(s, ROW_CHUNK, POOL_HALO)
    n_chunks = s // tr

    def body(dy_ref, d_ref, pw_ref, sc_ref, du_ref, dpw_ref, dsc_ref):
        r = pl.program_id(1)
        w = jnp.left_shift(2, pl.program_id(0))
        dye = _rows_with_halo(dy_ref, r, tr, n_chunks, 0, POOL_HALO)
        diff = d_ref[pl.ds(pl.multiple_of(r * tr, tr), tr), :]
        pw = pw_ref[...]
        yp = jnp.dot(diff, pw, preferred_element_type=F32)
        dsc = jnp.sum(dye[:tr] * yp, axis=0, keepdims=True)
        dyp = (dye * sc_ref[...]).astype(BF16)
        ddiff = lax.dot_general(dyp, pw, (((1,), (1,)), ((), ())), preferred_element_type=F32)
        dpw = lax.dot_general(diff, dyp[:tr], (((0,), (0,)), ((), ())), preferred_element_type=F32)
        t = r * tr + lax.broadcasted_iota(jnp.int32, (tr + POOL_HALO, 1), 0)
        cnt = jnp.minimum(t + 1, w).astype(F32)
        acc = ddiff / cnt
        rows = tr + POOL_HALO
        for k in (1, 2, 4, 8):
            acc = jnp.where(k < w, acc + pltpu.roll(acc, rows - k, 0), acc)
        du_ref[...] = (acc[:tr] - ddiff[:tr]).astype(BF16)

        @pl.when(r == 0)
        def _():
            dpw_ref[...] = dpw
            dsc_ref[...] = dsc

        @pl.when(r > 0)
        def _():
            dpw_ref[...] += dpw
            dsc_ref[...] += dsc

    col = lambda g, r: (0, g)
    wspec = pl.BlockSpec((None, c, c), lambda g, r: (g, 0, 0))
    vec = pl.BlockSpec((1, c), col)
    return _pcall(
        body, name="pool_bwd", grid=(g_n, n_chunks),
        in_specs=[pl.BlockSpec((s, c), col), pl.BlockSpec((s, c), col), wspec, vec],
        out_specs=(pl.BlockSpec((tr, c), lambda g, r: (r, g)), wspec, vec),
        out_shape=(jax.ShapeDtypeStruct((s, g_n * c), BF16), jax.ShapeDtypeStruct((g_n, c, c), F32),
                   jax.ShapeDtypeStruct((1, g_n * c), F32)),
        compiler_params=_params("parallel", "arbitrary"),
    )(dcat, diff, pool_w, pool_scale.reshape(1, g_n * c))


def _swap_halves(x, lane):
    return jnp.where((lane % ROPE_DIM) < ROPE_DIM // 2, pltpu.roll(x, LANES - ROPE_DIM // 2, 1),
                     pltpu.roll(x, ROPE_DIM // 2, 1))


def _qkv_prep(q, kv, kr2, cos_t, sin_t, *, heads):
    s = q.shape[0]
    ts = _tile(s, 512, 8)

    def body(qn_ref, qr_ref, kn_ref, v_ref, kr_ref, cos_ref, sin_ref, qo_ref, ko_ref, vo_ref):
        half = pl.program_id(1) % 2
        lane = lax.broadcasted_iota(jnp.int32, (ts, LANES), 1)
        cos_v = cos_ref[...]
        sin_v = sin_ref[...]

        def rope(x):
            return x * cos_v + _swap_halves(x, lane) * sin_v

        qo_ref[:, :LANES] = qn_ref[...].astype(BF16)
        qo_ref[:, LANES:] = jnp.where(lane // ROPE_DIM == half, rope(qr_ref[...]), 0.0).astype(BF16)
        ko_ref[:, :LANES] = kn_ref[...].astype(BF16)
        ko_ref[:, LANES:] = rope(kr_ref[...]).astype(BF16)
        vo_ref[...] = v_ref[...].astype(BF16)

    blk = lambda f: pl.BlockSpec((ts, LANES), f)
    return _pcall(
        body, name="qkv_prep", grid=(s // ts, heads),
        in_specs=[blk(lambda i, h: (i, h)), blk(lambda i, h: (i, heads + h // 2)),
                  blk(lambda i, h: (i, h)), blk(lambda i, h: (i, heads + h)),
                  blk(lambda i, h: (i, 0)), blk(lambda i, h: (i, 0)), blk(lambda i, h: (i, 0))],
        out_specs=(pl.BlockSpec((None, ts, 2 * LANES), lambda i, h: (h, i, 0)),
                   pl.BlockSpec((None, ts, 2 * LANES), lambda i, h: (h, i, 0)),
                   pl.BlockSpec((None, ts, LANES), lambda i, h: (h, i, 0))),
        out_shape=(jax.ShapeDtypeStruct((heads, s, 2 * LANES), BF16),
                   jax.ShapeDtypeStruct((heads, s, 2 * LANES), BF16),
                   jax.ShapeDtypeStruct((heads, s, LANES), BF16)),
        compiler_params=_params("parallel", "parallel"),
    )(q, q, kv, kv, kr2, cos_t, sin_t)


LOG2_E = 1.4426950408889634


def _flash_fwd(qp, kp, v, *, scale, tq=512):
    heads, s, dk = qp.shape
    tq = _tile(s, tq, 16)
    tk = tq
    th = tq // 2
    c = scale * LOG2_E

    def body(q_ref, k_ref, v_ref, o_ref, lse_ref):
        i = pl.program_id(1)
        qs = [q_ref[hh * th:(hh + 1) * th, :] for hh in range(2)]

        def block(j, carry, diag):
            start = pl.multiple_of(j * tk, tk)
            kb = k_ref[pl.ds(start, tk), :]
            vb = v_ref[pl.ds(start, tk), :]
            new = []
            for hh in range(2):
                m_old, l_old, acc = carry[hh]
                sc = lax.dot_general(qs[hh], kb, (((1,), (1,)), ((), ())), preferred_element_type=F32)
                if diag:
                    rows = hh * th + lax.broadcasted_iota(jnp.int32, (th, tk), 0)
                    cols = lax.broadcasted_iota(jnp.int32, (th, tk), 1)
                    sc = jnp.where(rows >= cols, sc, -jnp.inf)
                m_new = jnp.maximum(m_old, jnp.max(sc, axis=-1, keepdims=True))
                alpha = jnp.exp2((m_old - m_new) * c)
                p = jnp.exp2((sc - m_new) * c)
                l_new = alpha * l_old + jnp.sum(p, axis=-1, keepdims=True)
                acc = alpha * acc + jnp.dot(p.astype(BF16), vb, preferred_element_type=F32)
                new.append((m_new, l_new, acc))
            return tuple(new)

        init = tuple((jnp.full((th, 1), -jnp.inf, F32), jnp.zeros((th, 1), F32), jnp.zeros((th, V_DIM), F32))
                     for _ in range(2))
        carry = lax.fori_loop(0, i, lambda j, cr: block(j, cr, False), init)
        carry = block(i, carry, True)
        for hh in range(2):
            m_fin, l_fin, acc = carry[hh]
            o_ref[hh * th:(hh + 1) * th, :] = acc / l_fin
            lse_ref[hh * th:(hh + 1) * th, :] = m_fin * scale + jnp.log(l_fin)

    return _pcall(
        body, name="flash_fwd", grid=(heads, s // tq),
        in_specs=[pl.BlockSpec((None, tq, dk), lambda h, i: (h, i, 0)),
                  pl.BlockSpec((None, s, dk), lambda h, i: (h, 0, 0)),
                  pl.BlockSpec((None, s, V_DIM), lambda h, i: (h, 0, 0))],
        out_specs=(pl.BlockSpec((tq, V_DIM), lambda h, i: (i, h)),
                   pl.BlockSpec((None, tq, 1), lambda h, i: (h, i, 0))),
        out_shape=(jax.ShapeDtypeStruct((s, heads * V_DIM), F32), jax.ShapeDtypeStruct((heads, s, 1), F32)),
        compiler_params=_params("parallel", "arbitrary"),
    )(qp, kp, v)


def _flash_bwd(qp, kp, v, dcat, o, lse, *, scale, do_col0, tq=512):
    heads, s, dk = qp.shape
    tq = _tile(s, tq, 16)
    tk = tq
    nq = s // tq

    def body(k_ref, v_ref, q_ref, do_ref, o_ref, lse_ref, dq_ref, dk_ref, dv_ref):
        j = pl.program_id(1)

        @pl.when(j == 0)
        def _():
            dq_ref[...] = jnp.zeros_like(dq_ref)

        kb = k_ref[...]
        vb = v_ref[...]

        def block(i, carry, diag):
            dk_acc, dv_acc = carry
            rows_at = pl.ds(pl.multiple_of(i * tq, tq), tq)
            qb = q_ref[rows_at, :]
            do = do_ref[rows_at, :]
            sc = lax.dot_general(qb, kb, (((1,), (1,)), ((), ())), preferred_element_type=F32) * scale
            if diag:
                rows = lax.broadcasted_iota(jnp.int32, (tq, tk), 0)
                cols = lax.broadcasted_iota(jnp.int32, (tq, tk), 1)
                sc = jnp.where(rows >= cols, sc, -jnp.inf)
            p = jnp.exp(sc - lse_ref[rows_at, :])
            dob = do.astype(BF16)
            dv_acc = dv_acc + lax.dot_general(p.astype(BF16), dob, (((0,), (0,)), ((), ())),
                                              preferred_element_type=F32)
            dp = lax.dot_general(dob, vb, (((1,), (1,)), ((), ())), preferred_element_type=F32)
            delta = jnp.sum(do * o_ref[rows_at, :], axis=-1, keepdims=True)
            ds = (p * (dp - delta) * scale).astype(BF16)
            dk_acc = dk_acc + lax.dot_general(ds, qb, (((0,), (0,)), ((), ())), preferred_element_type=F32)
            dq_ref[rows_at, :] += jnp.dot(ds, kb, preferred_element_type=F32)
            return dk_acc, dv_acc

        carry = block(j, (jnp.zeros((tk, dk), F32), jnp.zeros((tk, V_DIM), F32)), True)
        carry = lax.fori_loop(j + 1, nq, lambda i, cr: block(i, cr, False), carry)
        dk_ref[...] = carry[0]
        dv_ref[...] = carry[1]

    whole = lambda h, j: (h, 0, 0)
    return _pcall(
        body, name="flash_bwd", grid=(heads, nq),
        in_specs=[pl.BlockSpec((None, tk, dk), lambda h, j: (h, j, 0)),
                  pl.BlockSpec((None, tk, V_DIM), lambda h, j: (h, j, 0)),
                  pl.BlockSpec((None, s, dk), whole),
                  pl.BlockSpec((s, V_DIM), lambda h, j: (0, do_col0 + h)),
                  pl.BlockSpec((s, V_DIM), lambda h, j: (0, h)),
                  pl.BlockSpec((None, s, 1), whole)],
        out_specs=(pl.BlockSpec((None, s, dk), whole),
                   pl.BlockSpec((None, tk, dk), lambda h, j: (h, j, 0)),
                   pl.BlockSpec((None, tk, V_DIM), lambda h, j: (h, j, 0))),
        out_shape=(jax.ShapeDtypeStruct((heads, s, dk), F32), jax.ShapeDtypeStruct((heads, s, dk), F32),
                   jax.ShapeDtypeStruct((heads, s, V_DIM), F32)),
        compiler_params=_params("parallel", "arbitrary"),
    )(kp, v, qp, dcat, o, lse)


def _attn_bwd_post(dqp, dkp, dv, cos_t, sin_t):
    heads, s, _ = dqp.shape
    ts = _tile(s, 256, 8)

    def body(dq_ref, dk_ref, dv_ref, cos_ref, sin_ref, q_out, kv_out, kr_out):
        lane = lax.broadcasted_iota(jnp.int32, (ts, LANES), 1)
        cos_v = cos_ref[...]
        sin_v = sin_ref[...]

        def rope_t(dy):
            return dy * cos_v + _swap_halves(dy * sin_v, lane)

        kr_sum = jnp.zeros((ts, LANES), F32)
        for h in range(heads):
            q_out[:, h * LANES:(h + 1) * LANES] = dq_ref[h, :, :LANES].astype(BF16)
            kv_out[:, h * LANES:(h + 1) * LANES] = dk_ref[h, :, :LANES].astype(BF16)
            kv_out[:, (heads + h) * LANES:(heads + h + 1) * LANES] = dv_ref[h].astype(BF16)
            kr_sum = kr_sum + dk_ref[h, :, LANES:]
        for pair in range(heads // 2):
            r = jnp.where(lane < ROPE_DIM, dq_ref[2 * pair, :, LANES:], dq_ref[2 * pair + 1, :, LANES:])
            q_out[:, (heads + pair) * LANES:(heads + pair + 1) * LANES] = rope_t(r).astype(BF16)
        kr = rope_t(kr_sum)
        kr_out[...] = (kr + pltpu.roll(kr, ROPE_DIM, 1)).astype(BF16)

    wq = heads * (NOPE_DIM + ROPE_DIM)
    wkv = heads * (NOPE_DIM + V_DIM)
    tab = pl.BlockSpec((ts, LANES), lambda i: (i, 0))
    return _pcall(
        body, name="attn_bwd_post", grid=(s // ts,),
        in_specs=[pl.BlockSpec((heads, ts, 2 * LANES), lambda i: (0, i, 0)),
                  pl.BlockSpec((heads, ts, 2 * LANES), lambda i: (0, i, 0)),
                  pl.BlockSpec((heads, ts, LANES), lambda i: (0, i, 0)), tab, tab],
        out_specs=(pl.BlockSpec((ts, wq), lambda i: (i, 0)), pl.BlockSpec((ts, wkv), lambda i: (i, 0)), tab),
        out_shape=(jax.ShapeDtypeStruct((s, wq), BF16), jax.ShapeDtypeStruct((s, wkv), BF16),
                   jax.ShapeDtypeStruct((s, LANES), BF16)),
        compiler_params=_params("parallel"),
    )(dqp, dkp, dv, cos_t, sin_t)


CONV_HALO = 16


def _conv_gate(xe, cw_ref, cb_ref):
    return (cw_ref[2:3, :] * xe + cw_ref[1:2, :] * pltpu.roll(xe, 1, 0) + cw_ref[0:1, :] * pltpu.roll(xe, 2, 0)
            + cb_ref[...])


def _ffn_act_fwd(gu, conv_w, conv_b, *, d_ff, tc=256):
    s = gu.shape[0]
    tc = _tile(d_ff, tc)
    nf = d_ff // tc
    tr = _tile(s, ROW_CHUNK, CONV_HALO)
    n_chunks = s // tr

    def body(g_ref, u_ref, cw_ref, cb_ref, a_ref):
        r = pl.program_id(1)
        xe = _rows_with_halo(g_ref, r, tr, n_chunks, CONV_HALO, 0).astype(F32)
        gc = _conv_gate(xe, cw_ref, cb_ref)[CONV_HALO:]
        a_ref[...] = (gc * jax.nn.sigmoid(gc) * u_ref[...].astype(F32)).astype(BF16)

    return _pcall(
        body, name="ffn_act_fwd", grid=(nf, n_chunks),
        in_specs=[pl.BlockSpec((s, tc), lambda j, r: (0, j)), pl.BlockSpec((tr, tc), lambda j, r: (r, nf + j)),
                  pl.BlockSpec((CONV_TAPS, tc), lambda j, r: (0, j)), pl.BlockSpec((1, tc), lambda j, r: (0, j))],
        out_specs=pl.BlockSpec((tr, tc), lambda j, r: (r, j)),
        out_shape=jax.ShapeDtypeStruct((s, d_ff), BF16),
        compiler_params=_params("parallel", "arbitrary"),
    )(gu, gu, conv_w, conv_b.reshape(1, d_ff))


def _ffn_act_bwd(da, gu, conv_w, conv_b, *, d_ff, tc=256):
    s = gu.shape[0]
    tc = _tile(d_ff, tc)
    nf = d_ff // tc
    tr = _tile(s, ROW_CHUNK, CONV_HALO)
    n_chunks = s // tr
    rows = tr + 2 * CONV_HALO
    main = slice(CONV_HALO, CONV_HALO + tr)

    def body(da_ref, g_ref, u_ref, cw_ref, cb_ref, dg_ref, du_ref, dcw_ref, dcb_ref):
        r = pl.program_id(1)
        xe = _rows_with_halo(g_ref, r, tr, n_chunks, CONV_HALO, CONV_HALO).astype(F32)
        dae = _rows_with_halo(da_ref, r, tr, n_chunks, CONV_HALO, CONV_HALO).astype(F32)
        ue = _rows_with_halo(u_ref, r, tr, n_chunks, CONV_HALO, CONV_HALO).astype(F32)
        gc = _conv_gate(xe, cw_ref, cb_ref)
        sg = jax.nn.sigmoid(gc)
        du_ref[...] = (dae * gc * sg)[main].astype(BF16)
        dgc = dae * ue * sg * (1.0 + gc * (1.0 - sg))
        dg = (cw_ref[2:3, :] * dgc + cw_ref[1:2, :] * pltpu.roll(dgc, rows - 1, 0)
              + cw_ref[0:1, :] * pltpu.roll(dgc, rows - 2, 0))
        dg_ref[...] = dg[main].astype(BF16)
        dgc_m = dgc[main]
        dcb = jnp.sum(dgc_m, axis=0, keepdims=True)
        dcw = jnp.concatenate([jnp.sum(dgc_m * pltpu.roll(xe, 2, 0)[main], axis=0, keepdims=True),
                               jnp.sum(dgc_m * pltpu.roll(xe, 1, 0)[main], axis=0, keepdims=True),
                               jnp.sum(dgc_m * xe[main], axis=0, keepdims=True)], axis=0)

        @pl.when(r == 0)
        def _():
            dcb_ref[...] = dcb
            dcw_ref[...] = dcw

        @pl.when(r > 0)
        def _():
            dcb_ref[...] += dcb
            dcw_ref[...] += dcw

    col = pl.BlockSpec((s, tc), lambda j, r: (0, j))
    out = pl.BlockSpec((tr, tc), lambda j, r: (r, j))
    return _pcall(
        body, name="ffn_act_bwd", grid=(nf, n_chunks),
        in_specs=[col, col, pl.BlockSpec((s, tc), lambda j, r: (0, nf + j)),
                  pl.BlockSpec((CONV_TAPS, tc), lambda j, r: (0, j)), pl.BlockSpec((1, tc), lambda j, r: (0, j))],
        out_specs=(out, out, pl.BlockSpec((CONV_TAPS, tc), lambda j, r: (0, j)),
                   pl.BlockSpec((1, tc), lambda j, r: (0, j))),
        out_shape=(jax.ShapeDtypeStruct((s, d_ff), BF16), jax.ShapeDtypeStruct((s, d_ff), BF16),
                   jax.ShapeDtypeStruct((CONV_TAPS, d_ff), F32), jax.ShapeDtypeStruct((1, d_ff), F32)),
        compiler_params=_params("parallel", "arbitrary"),
    )(da, gu, gu, conv_w, conv_b.reshape(1, d_ff))


def _ple_fwd(h, z, e):
    s, d = h.shape
    ts = _tile(s, 512, 8)

    def body(h_ref, z_ref, e_ref, o_ref):
        o_ref[...] = h_ref[...] + e_ref[...] * jax.nn.sigmoid(z_ref[...])

    row = pl.BlockSpec((ts, d), lambda i: (i, 0))
    return _pcall(body, name="ple_fwd", grid=(s // ts,), in_specs=[row, row, row], out_specs=row,
                  out_shape=jax.ShapeDtypeStruct((s, d), F32), compiler_params=_params("parallel"))(h, z, e)


def _ple_bwd(dh, z, e, dep=None):
    s, d = dh.shape
    ts = _tile(s, 512, 8)

    def body(dh_ref, z_ref, e_ref, *rest):
        de_ref, dz_ref = rest[-2:]
        gt = jax.nn.sigmoid(z_ref[...])
        dhv = dh_ref[...]
        de_ref[...] = (dhv * gt).astype(BF16)
        dz_ref[...] = (dhv * e_ref[...] * gt * (1.0 - gt)).astype(BF16)

    row = pl.BlockSpec((ts, d), lambda i: (i, 0))
    deps = [] if dep is None else [dep]
    return _pcall(body, name="ple_bwd", grid=(s // ts,), in_specs=[row, row, row] + [HBM_SPEC] * len(deps),
                  out_specs=(row, row),
                  out_shape=(jax.ShapeDtypeStruct((s, d), BF16), jax.ShapeDtypeStruct((s, d), BF16)),
                  compiler_params=_params("parallel"))(dh, z, e, *deps)


def _loss_head(h, g, target):
    s, d = h.shape
    ts = _tile(s, 512, 8)

    def body(h_ref, g_ref, t_ref, loss_ref, dh_ref, dg_ref):
        i = pl.program_id(0)
        x = h_ref[...]
        gv = g_ref[...]
        r = lax.rsqrt(jnp.mean(x * x, axis=-1, keepdims=True) + NORM_EPS)
        nh = x * r
        err = nh * gv - t_ref[...]
        part_loss = 0.5 * jnp.sum(jnp.mean(err * err, axis=-1, keepdims=True), axis=0, keepdims=True)
        dy = err * (1.0 / d)
        gd = dy * gv
        dh_ref[...] = (gd - nh * jnp.mean(gd * nh, axis=-1, keepdims=True)) * r
        part_g = jnp.sum(dy * nh, axis=0, keepdims=True)
        part_l = jnp.broadcast_to(part_loss, (1, LANES))

        @pl.when(i == 0)
        def _():
            dg_ref[...] = part_g
            loss_ref[...] = part_l

        @pl.when(i > 0)
        def _():
            dg_ref[...] += part_g
            loss_ref[...] += part_l

    row = pl.BlockSpec((ts, d), lambda i: (i, 0))
    vec = pl.BlockSpec((1, d), lambda i: (0, 0))
    return _pcall(
        body, name="loss_head", grid=(s // ts,), in_specs=[row, vec, row],
        out_specs=(pl.BlockSpec((1, LANES), lambda i: (0, 0)), row, vec),
        out_shape=(jax.ShapeDtypeStruct((1, LANES), F32), jax.ShapeDtypeStruct((s, d), F32),
                   jax.ShapeDtypeStruct((1, d), F32)),
        compiler_params=_params("arbitrary"),
    )(h, g.reshape(1, d), target)


HBM_SPEC = pl.BlockSpec(memory_space=pl.ANY)


def _flat_index(px, py, pc):
    return 4 * px + 2 * py + pc


def _all_gather(shards, *, name):
    n = len(shards)

    def body(*refs):
        ins, outs = refs[:n], refs[n:2 * n]
        send_sems, recv_sems, local_sems = refs[2 * n:]
        x, y, c = lax.axis_index("x"), lax.axis_index("y"), lax.axis_index("c")
        me, sibling = (x, y, c), (x, y, 1 - c)
        chips = [(1 - x, y), (x, 1 - y), (1 - x, 1 - y)]

        def copy(a, k, block, to, src=None):
            slot = outs[a].at[_flat_index(*block)]
            return pltpu.make_async_remote_copy(
                src_ref=slot if src is None else src, dst_ref=slot,
                send_sem=send_sems.at[a, k], recv_sem=recv_sems.at[a, k],
                device_id=to, device_id_type=pl.DeviceIdType.MESH)

        mine, first, passed = [], [], []
        for a in range(n):
            cp = pltpu.make_async_copy(ins[a], outs[a].at[_flat_index(*me)], local_sems.at[a])
            cp.start()
            mine.append(cp)
            first.append(copy(a, 0, me, sibling, src=ins[a]))
            first += [copy(a, 1 + j, me, (*chip, c), src=ins[a]) for j, chip in enumerate(chips)]
        for cp in first:
            cp.start()
        for j, chip in enumerate(chips):
            for a in range(n):
                copy(a, 1 + j, (*chip, c), me).wait_recv()
                fwd = copy(a, 4 + j, (*chip, c), sibling)
                fwd.start()
                passed.append(fwd)
        for a in range(n):
            copy(a, 0, sibling, me).wait_recv()
            for j, chip in enumerate(chips):
                copy(a, 4 + j, (*chip, 1 - c), me).wait_recv()
        for cp in first + passed:
            cp.wait_send()
        for cp in mine:
            cp.wait()

    return _pcall(
        body, name=name,
        in_specs=[HBM_SPEC] * n, out_specs=[HBM_SPEC] * n,
        out_shape=[jax.ShapeDtypeStruct((N_DEV,) + a.shape, a.dtype) for a in shards],
        scratch_shapes=[pltpu.SemaphoreType.DMA((n, 7)), pltpu.SemaphoreType.DMA((n, 7)),
                        pltpu.SemaphoreType.DMA((n,))],
    )(*shards)


HBM_ONLY = pl.BlockSpec(memory_space=pltpu.HBM)
SEM_SPEC = pl.BlockSpec(memory_space=pltpu.SEMAPHORE)
N_PEERS = N_DEV - 1
PEER_FLIPS = ((0, 0, 1), (1, 0, 0), (0, 1, 0), (1, 1, 0), (1, 0, 1), (0, 1, 1), (1, 1, 1))


def _exchange_refs(gather, src_refs, land_refs, send_sems, recv_sems):
    x, y, c = lax.axis_index("x"), lax.axis_index("y"), lax.axis_index("c")
    me = _flat_index(x, y, c)
    peers = [(x ^ fx, y ^ fy, c ^ fc) for fx, fy, fc in PEER_FLIPS]

    def out_copy(a, k):
        src = src_refs[a] if gather else src_refs[a].at[_flat_index(*peers[k])]
        return pltpu.make_async_remote_copy(
            src_ref=src, dst_ref=land_refs[a].at[me], send_sem=send_sems.at[a * N_PEERS + k],
            recv_sem=recv_sems.at[a * N_PEERS + k], device_id=peers[k], device_id_type=pl.DeviceIdType.MESH)

    def in_copy(a, k):
        src = src_refs[a] if gather else src_refs[a].at[me]
        return pltpu.make_async_remote_copy(
            src_ref=src, dst_ref=land_refs[a].at[_flat_index(*peers[k])], send_sem=send_sems.at[a * N_PEERS + k],
            recv_sem=recv_sems.at[a * N_PEERS + k], device_id=peers[k], device_id_type=pl.DeviceIdType.MESH)

    return out_copy, in_copy


def _exchange_start(srcs, lands, *, gather, name, dep):
    n = len(srcs)

    def body(*refs):
        src_refs, land_refs = refs[:n], refs[n:2 * n]
        send_sems, recv_sems = refs[2 * n + 1], refs[2 * n + 2]
        token = refs[-1]
        out_copy, _ = _exchange_refs(gather, src_refs, land_refs, send_sems, recv_sems)
        for k in range(N_PEERS):
            for a in range(n):
                out_copy(a, k).start()
        token[...] = jnp.zeros_like(token)

    hbm = lambda a: pltpu.with_memory_space_constraint(a, pltpu.HBM)
    return _pcall(
        body, name=name,
        out_shape=(pltpu.SemaphoreType.DMA((n * N_PEERS,)), pltpu.SemaphoreType.DMA((n * N_PEERS,)),
                   *[pltpu.HBM(a.shape, a.dtype) for a in srcs], *[pltpu.HBM(a.shape, a.dtype) for a in lands],
                   jax.ShapeDtypeStruct((8, LANES), F32)),
        in_specs=[HBM_ONLY] * (2 * n) + [HBM_SPEC],
        out_specs=(SEM_SPEC, SEM_SPEC, *[HBM_ONLY] * (2 * n), pl.BlockSpec(memory_space=pltpu.VMEM)),
        input_output_aliases={i: 2 + i for i in range(2 * n)},
        compiler_params=pltpu.CompilerParams(has_side_effects=pltpu.SideEffectType.DATAFLOW_SIDE_EFFECTING),
    )(*[hbm(a) for a in srcs], *[hbm(a) for a in lands], dep)


def _exchange_wait(started, after, *, gather, name):
    send_sems, recv_sems = started[0], started[1]
    n = (len(started) - 3) // 2
    srcs, lands = started[2:2 + n], started[2 + n:2 + 2 * n]

    def body(*refs):
        src_refs, land_refs = refs[:n], refs[n:2 * n]
        s_sems, r_sems = refs[2 * n], refs[2 * n + 1]
        out_copy, in_copy = _exchange_refs(gather, src_refs, land_refs, s_sems, r_sems)
        for k in range(N_PEERS):
            for a in range(n):
                out_copy(a, k).wait_send()
                in_copy(a, k).wait_recv()

    res = _pcall(
        body, name=name,
        out_shape=tuple(pltpu.HBM(a.shape, a.dtype) for a in (*srcs, *lands)),
        in_specs=[HBM_ONLY] * (2 * n) + [SEM_SPEC, SEM_SPEC, HBM_SPEC],
        out_specs=tuple([HBM_ONLY] * (2 * n)),
        input_output_aliases={i: i for i in range(2 * n)},
        compiler_params=pltpu.CompilerParams(has_side_effects=pltpu.SideEffectType.DATAFLOW_SIDE_EFFECTING),
    )(*srcs, *lands, send_sems, recv_sems, after)
    return list(res[n:])


def _own_slot_filled(block, me):
    land = lax.empty((N_DEV,) + block.shape, block.dtype)
    return lax.dynamic_update_slice(land, block[None], (me,) + (0,) * block.ndim)


def _adam_math(g, w, m, v):
    m = ADAM_B1 * m + (1.0 - ADAM_B1) * g
    v = ADAM_B2 * v + (1.0 - ADAM_B2) * jnp.square(g)
    m_hat = m / (1.0 - ADAM_B1 ** ADAM_STEP)
    v_hat = v / (1.0 - ADAM_B2 ** ADAM_STEP)
    delta = -ADAM_LR * (m_hat / (jnp.sqrt(v_hat) + ADAM_EPS) + ADAM_WD * w)
    return delta, m, v


def _adamw(contrib, w, m, v, *, name, dep=None):
    _, layers, r, c = contrib.shape
    tr = _tile(r, max(8, (256 * 1024 // c) // 8 * 8), 8)

    def body(g_ref, w_ref, m_ref, v_ref, *rest):
        go_ref, d_ref, mo_ref, vo_ref = rest[-4:]
        g = g_ref[0].astype(F32)
        for k in range(1, N_DEV):
            g = g + g_ref[k].astype(F32)
        delta, m_new, v_new = _adam_math(g, w_ref[...], m_ref[...], v_ref[...])
        go_ref[...] = g
        d_ref[...] = delta
        mo_ref[...] = m_new
        vo_ref[...] = v_new

    deps = [] if dep is None else [dep]
    blk = pl.BlockSpec((None, tr, c), lambda l, i: (l, i, 0))
    out = jax.ShapeDtypeStruct((layers, r, c), F32)
    return _pcall(
        body, name=name, grid=(layers, r // tr),
        in_specs=[pl.BlockSpec((N_DEV, None, tr, c), lambda l, i: (0, l, i, 0)), blk, blk, blk]
        + [HBM_SPEC] * len(deps),
        out_specs=(blk, blk, blk, blk), out_shape=(out, out, out, out),
        compiler_params=_params("parallel", "parallel"),
    )(contrib, w, m, v, *deps)


def _heads_split(w, heads, first, second):
    k = w.shape[0]
    w3 = w.reshape(k, heads, first + second)
    return jnp.concatenate([w3[:, :, :first].reshape(k, heads * first),
                            w3[:, :, first:].reshape(k, heads * second)], axis=1)


def _heads_join(w, heads, first, second):
    k = w.shape[0]
    a = w[:, :heads * first].reshape(k, heads, first)
    b = w[:, heads * first:].reshape(k, heads, second)
    return jnp.concatenate([a, b], axis=2).reshape(k, heads * (first + second))


def _full_from_gathered(kind, g):
    if kind == "col":
        return jnp.transpose(g, (1, 0, 2)).reshape(g.shape[1], N_DEV * g.shape[2])
    if kind == "row":
        return g.reshape(N_DEV * g.shape[1], g.shape[2])
    return jnp.transpose(g, (1, 0, 2, 3)).reshape(g.shape[1], N_DEV * g.shape[2], g.shape[3])


def _blocks_from_full(kind, f):
    if kind == "col":
        k, n = f.shape
        return jnp.transpose(f.reshape(k, N_DEV, n // N_DEV), (1, 0, 2))
    if kind == "row":
        k, n = f.shape
        return f.reshape(N_DEV, k // N_DEV, n)
    g, c_in, c = f.shape
    return jnp.transpose(f.reshape(g, N_DEV, c_in // N_DEV, c), (1, 0, 2, 3))


def _rope_tables(positions):
    inv_freq = 1.0 / (ROPE_THETA ** (jnp.arange(0, ROPE_DIM, 2, dtype=F32) / ROPE_DIM))
    ang = positions.astype(F32)[:, None] * inv_freq
    cos, sin = jnp.cos(ang), jnp.sin(ang)
    return jnp.concatenate([cos, cos, cos, cos], axis=-1), jnp.concatenate([-sin, sin, -sin, sin], axis=-1)


def _layer_fwd(h0, p_i, w, rep, tabs, dims, dep=None):
    heads, d_pool, q_lora, d_ff = dims["heads"], dims["d_pool"], dims["q_lora"], dims["d_ff"]
    c = d_pool // POOL_GROUPS
    cos_t, sin_t = tabs
    scale = 1.0 / math.sqrt(NOPE_DIM + ROPE_DIM)
    n1 = _rms_fwd(h0, rep["norm_mix_g"], name="rms_mix_fwd", dep=dep)
    u = _mm(n1, w["w_in"], name="mm_in_fwd")
    y_pool, diff = _pool_fwd(u, w["pool_w"], rep["pool_scale"], c=c)
    nq = _rms_fwd(u, rep["q_norm_g"], name="rms_q_fwd", col_block=d_pool // q_lora)
    nkv = _rms_fwd(u, rep["kv_norm_g"], name="rms_kv_fwd", col_block=d_pool // q_lora + 1)
    q = _mm(nq, w["w_uq"], name="mm_uq_fwd")
    kv = _mm(nkv, w["w_ukv"], name="mm_ukv_fwd")
    kr = u[:, d_pool + 2 * q_lora:]
    kr2 = jnp.concatenate([kr, kr], axis=-1)
    qp, kp, v = _qkv_prep(q, kv, kr2, cos_t, sin_t, heads=heads)
    o, lse = _flash_fwd(qp, kp, v, scale=scale)
    t = _mm(y_pool, w["w_out"], name="mm_out_pool_fwd", add=h0, b_row=(0, d_pool))
    h1 = _mm(o, w["w_out"], name="mm_out_att_fwd", add=t, b_row=(d_pool, None))
    n2 = _rms_fwd(h1, rep["norm_ffn_g"], name="rms_ffn_fwd")
    gu = _mm(n2, w["w_up"], name="mm_up_fwd", out_dtype=BF16, b_blocks=(0, N_DEV))
    a = _ffn_act_fwd(gu, w["conv_w"], rep["conv_b"], d_ff=d_ff)
    h2 = _mm(a, w["w_down"], name="mm_down_fwd", add=h1)
    n3 = _rms_fwd(h2, rep["norm_ple_g"], name="rms_ple_fwd")
    z = _mm(n3, w["w_ple_gate"], name="mm_pgate_fwd")
    e = _mm(p_i, w["w_ple"], name="mm_ple_fwd")
    h3 = _ple_fwd(h2, z, e)
    saved = dict(h0=h0, n1=n1, u=u, y_pool=y_pool, diff=diff, nq=nq, nkv=nkv, qp=qp, kp=kp, v=v, o=o, lse=lse,
                 h1=h1, n2=n2, gu=gu, a=a, h2=h2, n3=n3, z=z, e=e)
    return h3, saved


def _layer_bwd(dh3, p_i, w, rep, tabs, dims, sv, dep=None, mid=None):
    heads, d_pool, q_lora, d_ff = dims["heads"], dims["d_pool"], dims["q_lora"], dims["d_ff"]
    c = d_pool // POOL_GROUPS
    cos_t, sin_t = tabs
    scale = 1.0 / math.sqrt(NOPE_DIM + ROPE_DIM)
    gr = {}
    de, dz = _ple_bwd(dh3, sv["z"], sv["e"], dep)
    gr["w_ple"] = _mm(p_i, de, name="mm_ple_dw", ta=True, out_dtype=BF16)
    gr["w_ple_gate"] = _mm(sv["n3"], dz, name="mm_pgate_dw", ta=True, out_dtype=BF16)
    dn3 = _mm(dz, w["w_ple_gate"], name="mm_pgate_dx", tb=True)
    dh2, gr["norm_ple_g"] = _rms_bwd(dn3, sv["h2"], rep["norm_ple_g"], name="rms_ple_bwd", res=dh3)
    gr["w_down"] = _mm(sv["a"], dh2, name="mm_down_dw", ta=True, out_dtype=BF16)
    da = _mm(dh2, w["w_down"], name="mm_down_dx", tb=True, out_dtype=BF16)
    dgate, dup, gr["conv_w"], gr["conv_b"] = _ffn_act_bwd(da, sv["gu"], w["conv_w"], rep["conv_b"], d_ff=d_ff)
    half, per = N_DEV // 2, w["w_up"].shape[2]
    dw_gate = _mm(sv["n2"], dgate, name="mm_up_gate_dw", ta=True, out_dtype=BF16, out_blocks=(N_DEV, 0, per))
    gr["w_up"] = _mm(sv["n2"], dup, name="mm_up_up_dw", ta=True, out_dtype=BF16, out_blocks=(N_DEV, half, per),
                     out_init=dw_gate)
    dn2 = _mm(dgate, w["w_up"], name="mm_up_gate_dx", tb=True, b_blocks=(0, half))
    dn2 = _mm(dup, w["w_up"], name="mm_up_up_dx", tb=True, b_blocks=(half, half), add=dn2)
    dh1, gr["norm_ffn_g"] = _rms_bwd(dn2, sv["h1"], rep["norm_ffn_g"], name="rms_ffn_bwd", res=dh2)
    dep_mix = None if mid is None else mid(dh1, gr)
    dw_out_pool = _mm(sv["y_pool"], dh1, name="mm_out_pool_dw", ta=True, out_dtype=BF16, dep=dep_mix)
    dw_out_att = _mm(sv["o"], dh1, name="mm_out_att_dw", ta=True, out_dtype=BF16)
    gr["w_out"] = jnp.concatenate([dw_out_pool, dw_out_att], axis=0)
    dcat = _mm(dh1, w["w_out"], name="mm_out_dx", tb=True)
    do_col0 = d_pool // V_DIM
    dqp, dkp, dv = _flash_bwd(sv["qp"], sv["kp"], sv["v"], dcat, sv["o"], sv["lse"], scale=scale, do_col0=do_col0)
    dq, dkv, dkr2 = _attn_bwd_post(dqp, dkp, dv, cos_t, sin_t)
    gr["w_uq"] = _mm(sv["nq"], dq, name="mm_uq_dw", ta=True, out_dtype=BF16)
    gr["w_ukv"] = _mm(sv["nkv"], dkv, name="mm_ukv_dw", ta=True, out_dtype=BF16)
    dnq = _mm(dq, w["w_uq"], name="mm_uq_dx", tb=True)
    dnkv = _mm(dkv, w["w_ukv"], name="mm_ukv_dx", tb=True)
    dcq, gr["q_norm_g"] = _rms_bwd(dnq, sv["u"], rep["q_norm_g"], name="rms_q_bwd",
                                   col_block=d_pool // q_lora, out_dtype=BF16)
    dckv, gr["kv_norm_g"] = _rms_bwd(dnkv, sv["u"], rep["kv_norm_g"], name="rms_kv_bwd",
                                     col_block=d_pool // q_lora + 1, out_dtype=BF16)
    du_pool, gr["pool_w"], gr["pool_scale"] = _pool_bwd(dcat, sv["diff"], w["pool_w"], rep["pool_scale"], c=c)
    du = jnp.concatenate([du_pool, dcq, dckv, dkr2[:, :ROPE_DIM]], axis=-1)
    gr["w_in"] = _mm(sv["n1"], du, name="mm_in_dw", ta=True, out_dtype=BF16)
    dn1 = _mm(du, w["w_in"], name="mm_in_dx", tb=True)
    dh0, gr["norm_mix_g"] = _rms_bwd(dn1, sv["h0"], rep["norm_mix_g"], name="rms_mix_bwd", res=dh1)
    return dh0, gr


def _as2d(a):
    return a.reshape(a.shape[0], -1, a.shape[-1])


def kernel(x, p, positions, norm_mix_g, w_in, pool_w, pool_scale, q_norm_g, w_uq, kv_norm_g, w_ukv, w_out, norm_ffn_g, w_up, conv_w, conv_b, w_down, norm_ple_g, w_ple, w_ple_gate, final_norm_g, loss_target, m_norm_mix_g, m_w_in, m_pool_w, m_pool_scale, m_q_norm_g, m_w_uq, m_kv_norm_g, m_w_ukv, m_w_out, m_norm_ffn_g, m_w_up, m_conv_w, m_conv_b, m_w_down, m_norm_ple_g, m_w_ple, m_w_ple_gate, m_final_norm_g, v_norm_mix_g, v_w_in, v_pool_w, v_pool_scale, v_q_norm_g, v_w_uq, v_kv_norm_g, v_w_ukv, v_w_out, v_norm_ffn_g, v_w_up, v_conv_w, v_conv_b, v_w_down, v_norm_ple_g, v_w_ple, v_w_ple_gate, v_final_norm_g):
    weights = dict(norm_mix_g=norm_mix_g, w_in=w_in, pool_w=pool_w, pool_scale=pool_scale, q_norm_g=q_norm_g,
                   w_uq=w_uq, kv_norm_g=kv_norm_g, w_ukv=w_ukv, w_out=w_out, norm_ffn_g=norm_ffn_g, w_up=w_up,
                   conv_w=conv_w, conv_b=conv_b, w_down=w_down, norm_ple_g=norm_ple_g, w_ple=w_ple,
                   w_ple_gate=w_ple_gate, final_norm_g=final_norm_g)
    m_in = dict(norm_mix_g=m_norm_mix_g, w_in=m_w_in, pool_w=m_pool_w, pool_scale=m_pool_scale, q_norm_g=m_q_norm_g,
                w_uq=m_w_uq, kv_norm_g=m_kv_norm_g, w_ukv=m_w_ukv, w_out=m_w_out, norm_ffn_g=m_norm_ffn_g,
                w_up=m_w_up, conv_w=m_conv_w, conv_b=m_conv_b, w_down=m_w_down, norm_ple_g=m_norm_ple_g,
                w_ple=m_w_ple, w_ple_gate=m_w_ple_gate, final_norm_g=m_final_norm_g)
    v_in = dict(norm_mix_g=v_norm_mix_g, w_in=v_w_in, pool_w=v_pool_w, pool_scale=v_pool_scale, q_norm_g=v_q_norm_g,
                w_uq=v_w_uq, kv_norm_g=v_kv_norm_g, w_ukv=v_w_ukv, w_out=v_w_out, norm_ffn_g=v_norm_ffn_g,
                w_up=v_w_up, conv_w=v_conv_w, conv_b=v_conv_b, w_down=v_w_down, norm_ple_g=v_norm_ple_g,
                w_ple=v_w_ple, w_ple_gate=v_w_ple_gate, final_norm_g=v_final_norm_g)

    depth = w_in.shape[0]
    s, d_model = x.shape[1], x.shape[2]
    d_pool = pool_scale.shape[-1]
    q_lora = q_norm_g.shape[-1]
    d_ff = conv_b.shape[-1]
    heads = (w_uq.shape[-1] * N_DEV) // (NOPE_DIM + ROPE_DIM)
    dims = dict(heads=heads, d_pool=d_pool, q_lora=q_lora, d_ff=d_ff)

    me = _flat_index(lax.axis_index("x"), lax.axis_index("y"), lax.axis_index("c"))

    def layer_shards(i):
        return [weights[n][i] if n == "conv_w" else weights[n][i].astype(BF16) for n in SHARDED]

    def full_weights(i, gathered_i):
        w = {n: g if n == "w_up" else _full_from_gathered(SHARD_KIND[n], g) for n, g in zip(SHARDED, gathered_i)}
        w["w_uq"] = _heads_split(w["w_uq"], heads, NOPE_DIM, ROPE_DIM)
        w["w_ukv"] = _heads_split(w["w_ukv"], heads, NOPE_DIM, V_DIM)
        rep = {n: weights[n][i] for n in REPLICATED}
        return w, rep

    tabs = _rope_tables(positions[0])

    gathered = _all_gather(layer_shards(0), name="weights_all_gather_0")
    layer_w = []
    h = x[0]
    saved = []
    for i in range(depth):
        layer_w.append(full_weights(i, gathered))
        started = None
        if i + 1 < depth:
            shards = layer_shards(i + 1)
            started = _exchange_start(shards, [_own_slot_filled(a, me) for a in shards], gather=True,
                                      name=f"weights_gather_start_{i + 1}", dep=gathered[0] if i == 0 else h)
        w, rep = layer_w[i]
        h, sv = _layer_fwd(h, p[i, 0], w, rep, tabs, dims, dep=None if started is None else started[-1])
        saved.append(sv)
        if started is not None:
            gathered = _exchange_wait(started, h, gather=True, name=f"weights_gather_wait_{i + 1}")
    loss_row, dh, g_final = _loss_head(h, final_norm_g, loss_target[0])
    loss = lax.psum(loss_row[0, 0], MESH_AXES)

    late = ("w_up", "conv_w", "w_down", "w_ple", "w_ple_gate")
    early = tuple(n for n in SHARDED if n not in late)

    def start_grads(names, gr, dep, name):
        blocks = [gr[n] if n == "w_up" else _blocks_from_full(SHARD_KIND[n], gr[n]).astype(BF16) for n in names]
        lands = [_own_slot_filled(lax.dynamic_index_in_dim(b, me, 0, keepdims=False), me) for b in blocks]
        return _exchange_start(blocks, lands, gather=False, name=name, dep=dep)

    layer_grads = [None] * depth
    received = [dict() for _ in range(depth)]
    pending = None
    for i in reversed(range(depth)):
        w, rep = layer_w[i]
        state = {}

        def mid(dh1, gr, i=i, pending=pending, state=state):
            if pending is not None:
                got = _exchange_wait(pending[0], dh1, gather=False, name=f"grads_ffn_wait_{i + 1}")
                received[i + 1].update(zip(late, got))
                got = _exchange_wait(pending[1], dh1, gather=False, name=f"grads_mix_wait_{i + 1}")
                received[i + 1].update(zip(early, got))
            state["ffn"] = start_grads(late, gr, dh1, f"grads_ffn_start_{i}")
            return state["ffn"][-1]

        dh, gr = _layer_bwd(dh, p[i, 0], w, rep, tabs, dims, saved[i],
                            dep=None if pending is None else pending[1][-1], mid=mid)
        gr["w_uq"] = _heads_join(gr["w_uq"], heads, NOPE_DIM, ROPE_DIM)
        gr["w_ukv"] = _heads_join(gr["w_ukv"], heads, NOPE_DIM, V_DIM)
        layer_grads[i] = gr
        pending = (state["ffn"], start_grads(early, gr, dh, f"grads_mix_start_{i}"))
    grad_x = dh[None]

    out = {}

    def update(n, dep=None):
        shape = weights[n].shape
        rec = jnp.stack([received[i][n] for i in range(depth)], axis=1)
        rec2 = rec.reshape((N_DEV, depth, -1, shape[-1]))
        res = _adamw(rec2, _as2d(weights[n]), _as2d(m_in[n]), _as2d(v_in[n]), name="adamw_" + n, dep=dep)
        out[n] = tuple(r.reshape(shape) for r in res)

    received[0].update(zip(late, _exchange_wait(pending[0], pending[1][-1], gather=False, name="grads_ffn_wait_0")))
    for n in late:
        update(n)
    received[0].update(zip(early, _exchange_wait(pending[1], out[late[-1]][0], gather=False,
                                                 name="grads_mix_wait_0")))
    for n in early:
        update(n)

    small_names = REPLICATED + ("final_norm_g",)

    def pack(get):
        rows = []
        for n in REPLICATED:
            for i in range(depth):
                rows.append(get(n, i).reshape(-1))
        rows.append(get("final_norm_g", None).reshape(-1))
        flat = jnp.concatenate(rows)
        return flat.reshape(1, -1, LANES)

    g_small = pack(lambda n, i: g_final if i is None else layer_grads[i][n])
    w_small = pack(lambda n, i: weights[n] if i is None else weights[n][i])
    m_small = pack(lambda n, i: m_in[n] if i is None else m_in[n][i])
    v_small = pack(lambda n, i: v_in[n] if i is None else v_in[n][i])
    (g_all,) = _all_gather([g_small], name="small_grads_all_gather")
    res_small = _adamw(g_all, w_small, m_small, v_small, name="adamw_small")

    def unpack(flat3):
        flat = flat3.reshape(-1)
        res, off = {}, 0
        for n in REPLICATED:
            width = weights[n].shape[-1]
            res[n] = flat[off:off + depth * width].reshape(depth, width)
            off += depth * width
        res["final_norm_g"] = flat[off:off + d_model]
        return res

    small = [unpack(r) for r in res_small]
    for n in small_names:
        out[n] = tuple(small[k][n] for k in range(4))

    outs = [loss, grad_x]
    for k in range(4):
        outs += [out[n][k] for n in WEIGHT_ORDER]
    return tuple(outs)
```

```python
import functools
import math

import jax
import jax.numpy as jnp
from jax import lax
from jax.experimental import pallas as pl
from jax.experimental.pallas import tpu as pltpu

F32 = jnp.float32
BF16 = jnp.bfloat16

N_DEV = 8
MESH_AXES = ("x", "y", "c")
NOPE_DIM = 128
ROPE_DIM = 64
V_DIM = 128
POOL_GROUPS = 4
CONV_TAPS = 3
ROPE_THETA = 10000.0
NORM_EPS = 1e-6
ADAM_LR = 0.001
ADAM_B1 = 0.9
ADAM_B2 = 0.999
ADAM_EPS = 1e-08
ADAM_WD = 0.01
ADAM_STEP = 10
LANES = 128
VMEM_LIMIT_BYTES = 56 * 1024 * 1024

SHARDED = ("w_in", "pool_w", "w_uq", "w_ukv", "w_out", "w_up", "conv_w", "w_down", "w_ple", "w_ple_gate")
SHARD_KIND = {"w_in": "col", "pool_w": "pool", "w_uq": "col", "w_ukv": "col", "w_out": "row", "w_up": "col",
              "conv_w": "col", "w_down": "row", "w_ple": "col", "w_ple_gate": "row"}
REPLICATED = ("norm_mix_g", "pool_scale", "q_norm_g", "kv_norm_g", "norm_ffn_g", "conv_b", "norm_ple_g")
WEIGHT_ORDER = ("norm_mix_g", "w_in", "pool_w", "pool_scale", "q_norm_g", "w_uq", "kv_norm_g", "w_ukv", "w_out",
                "norm_ffn_g", "w_up", "conv_w", "conv_b", "w_down", "norm_ple_g", "w_ple", "w_ple_gate",
                "final_norm_g")

_pcall = pl.pallas_call


def _params(*sem):
    return pltpu.CompilerParams(dimension_semantics=sem or None, vmem_limit_bytes=VMEM_LIMIT_BYTES)


def _tile(n, pref, mult=LANES):
    if n <= pref:
        return n
    t = (pref // mult) * mult
    while t >= mult:
        if n % t == 0:
            return t
        t -= mult
    return n


MM_VMEM_BUDGET_BYTES = 40 * 1024 * 1024
MM_TILE_PREFS = (1024, 512, 256)
MM_MIN_TK = 1024


def _mm_tiles(m, n, k, a_bytes, b_bytes, o_bytes, has_add, tn_fixed=None, tk_fixed=None):
    tn = tn_fixed or _tile(n, MM_TILE_PREFS[0])
    if tk_fixed:
        k_cands = [tk_fixed]
    else:
        k_cands = [k] + [t for t in range((k - 1) // LANES * LANES, 0, -LANES) if k % t == 0]
    best = None
    for pref in MM_TILE_PREFS:
        tm = _tile(m, pref)
        for tk in k_cands:
            need = 2 * (tm * tk * a_bytes + tk * tn * b_bytes) + 2 * tm * tn * (o_bytes + (4 if has_add else 0))
            if tk < k:
                need += tm * tn * 4
            if need <= MM_VMEM_BUDGET_BYTES:
                if tk >= min(k, MM_MIN_TK):
                    return tm, tn, tk
                if best is None or tk > best[2]:
                    best = (tm, tn, tk)
                break
    assert best is not None, (m, n, k)
    return best


def _mm(a, b, *, name, ta=False, tb=False, add=None, out_dtype=F32, b_row=(0, None), b_blocks=None,
        out_blocks=None, out_init=None, dep=None):
    m = a.shape[1] if ta else a.shape[0]
    kdim = a.shape[0] if ta else a.shape[1]
    tn_fixed = tk_fixed = None
    k_off_b = 0
    if b_blocks is None:
        n_b, k_b = (b.shape if tb else b.shape[::-1])
        k_off_b, kb_sz = b_row
        kb_sz = k_b - k_off_b if kb_sz is None else kb_sz
        assert kb_sz == kdim, (name, kb_sz, kdim)
        n = n_b
    else:
        first_b, count_b = b_blocks
        per_b = b.shape[2]
        if tb:
            n = b.shape[1]
            assert kdim == count_b * per_b, name
            tk_fixed = per_b
        else:
            n = count_b * per_b
            assert kdim == b.shape[1], name
            tn_fixed = per_b
    if out_blocks is not None:
        nb_out, first_o, tn_fixed = out_blocks
    tm, tn, tk = _mm_tiles(m, n, kdim, a.dtype.itemsize, b.dtype.itemsize, jnp.dtype(out_dtype).itemsize,
                           add is not None, tn_fixed, tk_fixed)
    assert k_off_b % tk == 0 and kdim % tk == 0 and n % tn == 0 and m % tm == 0, name
    kb0 = k_off_b // tk
    nk = kdim // tk
    dims = (((0 if ta else 1,), (1 if tb else 0,)), ((), ()))

    def body(*refs):
        a_ref, b_ref = refs[0], refs[1]
        add_ref = refs[2] if add is not None else None
        o_ref = refs[n_in]
        part = lax.dot_general(a_ref[...].astype(BF16), b_ref[...].astype(BF16), dims, preferred_element_type=F32)

        def finish(r):
            if add_ref is not None:
                r = r + add_ref[...].astype(F32)
            o_ref[...] = r.astype(out_dtype)

        if nk == 1:
            finish(part)
            return
        acc = refs[n_in + 1]
        k = pl.program_id(2)

        @pl.when(k == 0)
        def _():
            acc[...] = part

        @pl.when(jnp.logical_and(k > 0, k < nk - 1))
        def _():
            acc[...] += part

        @pl.when(k == nk - 1)
        def _():
            finish(acc[...] + part)

    if ta:
        a_spec = pl.BlockSpec((tk, tm), lambda i, j, k: (k, i))
    else:
        a_spec = pl.BlockSpec((tm, tk), lambda i, j, k: (i, k))
    if b_blocks is not None and tb:
        b_spec = pl.BlockSpec((None, tn, tk), lambda i, j, k: (k + first_b, j, 0))
    elif b_blocks is not None:
        b_spec = pl.BlockSpec((None, tk, tn), lambda i, j, k: (j + first_b, k, 0))
    elif tb:
        b_spec = pl.BlockSpec((tn, tk), lambda i, j, k: (j, k + kb0))
    else:
        b_spec = pl.BlockSpec((tk, tn), lambda i, j, k: (k + kb0, j))
    in_specs = [a_spec, b_spec]
    args = [a, b]
    if add is not None:
        in_specs.append(pl.BlockSpec((tm, tn), lambda i, j, k: (i, j)))
        args.append(add)
    aliases = {}
    if out_init is not None:
        aliases = {len(args): 0}
        in_specs.append(HBM_SPEC)
        args.append(out_init)
    if dep is not None:
        in_specs.append(HBM_SPEC)
        args.append(dep)
    n_in = len(args)
    if out_blocks is None:
        out_spec = pl.BlockSpec((tm, tn), lambda i, j, k: (i, j))
        out_shape = jax.ShapeDtypeStruct((m, n), out_dtype)
    else:
        out_spec = pl.BlockSpec((None, tm, tn), lambda i, j, k: (j + first_o, i, 0))
        out_shape = jax.ShapeDtypeStruct((nb_out, m, tn), out_dtype)
    return _pcall(
        body, name=name, grid=(m // tm, n // tn, nk), in_specs=in_specs, out_specs=out_spec, out_shape=out_shape,
        scratch_shapes=[pltpu.VMEM((tm, tn), F32)] if nk > 1 else [], input_output_aliases=aliases,
        compiler_params=_params("parallel", "parallel", "arbitrary"),
    )(*args)


def _rms_fwd(h, g, *, name, col_block=0, dep=None):
    s = h.shape[0]
    d = g.shape[-1]
    ts = _tile(s, 512, 8)

    def body(h_ref, g_ref, *rest):
        n_ref = rest[-1]
        x = h_ref[...]
        r = lax.rsqrt(jnp.mean(x * x, axis=-1, keepdims=True) + NORM_EPS)
        n_ref[...] = (x * r * g_ref[...]).astype(BF16)

    deps = [] if dep is None else [dep]
    return _pcall(
        body, name=name, grid=(s // ts,),
        in_specs=[pl.BlockSpec((ts, d), lambda i: (i, col_block)), pl.BlockSpec((1, d), lambda i: (0, 0))]
        + [HBM_SPEC] * len(deps),
        out_specs=pl.BlockSpec((ts, d), lambda i: (i, 0)),
        out_shape=jax.ShapeDtypeStruct((s, d), BF16),
        compiler_params=_params("parallel"),
    )(h, g.reshape(1, d), *deps)


def _rms_bwd(dn, h, g, *, name, res=None, col_block=0, out_dtype=F32):
    s = dn.shape[0]
    d = g.shape[-1]
    ts = _tile(s, 512, 8)

    def body(*refs):
        if res is None:
            dn_ref, h_ref, g_ref, dh_ref, dg_ref = refs
            res_ref = None
        else:
            dn_ref, h_ref, g_ref, res_ref, dh_ref, dg_ref = refs
        i = pl.program_id(0)
        x = h_ref[...]
        r = lax.rsqrt(jnp.mean(x * x, axis=-1, keepdims=True) + NORM_EPS)
        nh = x * r
        dnv = dn_ref[...]
        gd = dnv * g_ref[...]
        dh = (gd - nh * jnp.mean(gd * nh, axis=-1, keepdims=True)) * r
        if res_ref is not None:
            dh = dh + res_ref[...]
        dh_ref[...] = dh.astype(out_dtype)
        part = jnp.sum(dnv * nh, axis=0, keepdims=True)

        @pl.when(i == 0)
        def _():
            dg_ref[...] = part

        @pl.when(i > 0)
        def _():
            dg_ref[...] += part

    row = pl.BlockSpec((ts, d), lambda i: (i, 0))
    in_specs = [row, pl.BlockSpec((ts, d), lambda i: (i, col_block)), pl.BlockSpec((1, d), lambda i: (0, 0))]
    args = [dn, h, g.reshape(1, d)]
    if res is not None:
        in_specs.append(row)
        args.append(res)
    return _pcall(
        body, name=name, grid=(s // ts,), in_specs=in_specs,
        out_specs=(row, pl.BlockSpec((1, d), lambda i: (0, 0))),
        out_shape=(jax.ShapeDtypeStruct((s, d), out_dtype), jax.ShapeDtypeStruct((1, d), F32)),
        compiler_params=_params("arbitrary"),
    )(*args)


ROW_CHUNK = 512


def _rows_with_halo(ref, r, t_rows, n_chunks, before, after):
    r0 = r * t_rows
    parts = []
    if before:
        hb = ref[pl.ds(pl.multiple_of(jnp.maximum(r0 - before, 0), before), before), :]
        parts.append(jnp.where(r > 0, hb, jnp.zeros_like(hb)))
    parts.append(ref[pl.ds(pl.multiple_of(r0, t_rows), t_rows), :])
    if after:
        ha = ref[pl.ds(pl.multiple_of(jnp.minimum(r0 + t_rows, n_chunks * t_rows - after), after), after), :]
        parts.append(jnp.where(r < n_chunks - 1, ha, jnp.zeros_like(ha)))
    return jnp.concatenate(parts, axis=0)


POOL_HALO = 16


def _pool_fwd(u, pool_w, pool_scale, *, c):
    s = u.shape[0]
    g_n = POOL_GROUPS
    tr = _tile(s, ROW_CHUNK, POOL_HALO)
    n_chunks = s // tr

    def body(u_ref, pw_ref, sc_ref, y_ref, d_ref):
        r = pl.program_id(1)
        w = jnp.left_shift(2, pl.program_id(0))
        xe = _rows_with_halo(u_ref, r, tr, n_chunks, POOL_HALO, 0)
        acc = xe
        for k in (1, 2, 4, 8):
            acc = jnp.where(k < w, acc + pltpu.roll(acc, k, 0), acc)
        t = r * tr + lax.broadcasted_iota(jnp.int32, (tr, 1), 0)
        cnt = jnp.minimum(t + 1, w).astype(F32)
        diff = (acc[POOL_HALO:] / cnt - xe[POOL_HALO:]).astype(BF16)
        d_ref[...] = diff
        y = jnp.dot(diff, pw_ref[...], preferred_element_type=F32) * sc_ref[...]
        y_ref[...] = y.astype(BF16)

    out = pl.BlockSpec((tr, c), lambda g, r: (r, g))
    return _pcall(
        body, name="pool_fwd", grid=(g_n, n_chunks),
        in_specs=[pl.BlockSpec((s, c), lambda g, r: (0, g)), pl.BlockSpec((None, c, c), lambda g, r: (g, 0, 0)),
                  pl.BlockSpec((1, c), lambda g, r: (0, g))],
        out_specs=(out, out),
        out_shape=(jax.ShapeDtypeStruct((s, g_n * c), BF16), jax.ShapeDtypeStruct((s, g_n * c), BF16)),
        compiler_params=_params("parallel", "arbitrary"),
    )(u, pool_w, pool_scale.reshape(1, g_n * c))


def _pool_bwd(dcat, diff, pool_w, pool_scale, *, c):
    s = dcat.shape[0]
    g_n = POOL_GROUPS
    tr = _tile(s, ROW_CHUNK, POOL_HALO)
    n_chunks = s // tr

    def body(dy_ref, d_ref, pw_ref, sc_ref, du_ref, dpw_ref, dsc_ref):
        r = pl.program_id(1)
        w = jnp.left_shift(2, pl.program_id(0))
        dye = _rows_with_halo(dy_ref, r, tr, n_chunks, 0, POOL_HALO)
        diff = d_ref[pl.ds(pl.multiple_of(r * tr, tr), tr), :]
        pw = pw_ref[...]
        yp = jnp.dot(diff, pw, preferred_element_type=F32)
        dsc = jnp.sum(dye[:tr] * yp, axis=0, keepdims=True)
        dyp = (dye * sc_ref[...]).astype(BF16)
        ddiff = lax.dot_general(dyp, pw, (((1,), (1,)), ((), ())), preferred_element_type=F32)
        dpw = lax.dot_general(diff, dyp[:tr], (((0,), (0,)), ((), ())), preferred_element_type=F32)
        t = r * tr + lax.broadcasted_iota(jnp.int32, (tr + POOL_HALO, 1), 0)
        cnt = jnp.minimum(t + 1, w).astype(F32)
        acc = ddiff / cnt
        rows = tr + POOL_HALO
        for k in (1, 2, 4, 8):
            acc = jnp.where(k < w, acc + pltpu.roll(acc, rows - k, 0), acc)
        du_ref[...] = (acc[:tr] - ddiff[:tr]).astype(BF16)

        @pl.when(r == 0)
        def _():
            dpw_ref[...] = dpw
            dsc_ref[...] = dsc

        @pl.when(r > 0)
        def _():
            dpw_ref[...] += dpw
            dsc_ref[...] += dsc

    col = lambda g, r: (0, g)
    wspec = pl.BlockSpec((None, c, c), lambda g, r: (g, 0, 0))
    vec = pl.BlockSpec((1, c), col)
    return _pcall(
        body, name="pool_bwd", grid=(g_n, n_chunks),
        in_specs=[pl.BlockSpec((s, c), col), pl.BlockSpec((s, c), col), wspec, vec],
        out_specs=(pl.BlockSpec((tr, c), lambda g, r: (r, g)), wspec, vec),
        out_shape=(jax.ShapeDtypeStruct((s, g_n * c), BF16), jax.ShapeDtypeStruct((g_n, c, c), F32),
                   jax.ShapeDtypeStruct((1, g_n * c), F32)),
        compiler_params=_params("parallel", "arbitrary"),
    )(dcat, diff, pool_w, pool_scale.reshape(1, g_n * c))


def _swap_halves(x, lane):
    return jnp.where((lane % ROPE_DIM) < ROPE_DIM // 2, pltpu.roll(x, LANES - ROPE_DIM // 2, 1),
                     pltpu.roll(x, ROPE_DIM // 2, 1))


def _qkv_prep(q, kv, kr2, cos_t, sin_t, *, heads):
    s = q.shape[0]
    ts = _tile(s, 512, 8)

    def body(qn_ref, qr_ref, kn_ref, v_ref, kr_ref, cos_ref, sin_ref, qo_ref, ko_ref, vo_ref):
        half = pl.program_id(1) % 2
        lane = lax.broadcasted_iota(jnp.int32, (ts, LANES), 1)
        cos_v = cos_ref[...]
        sin_v = sin_ref[...]

        def rope(x):
            return x * cos_v + _swap_halves(x, lane) * sin_v

        qo_ref[:, :LANES] = qn_ref[...].astype(BF16)
        qo_ref[:, LANES:] = jnp.where(lane // ROPE_DIM == half, rope(qr_ref[...]), 0.0).astype(BF16)
        ko_ref[:, :LANES] = kn_ref[...].astype(BF16)
        ko_ref[:, LANES:] = rope(kr_ref[...]).astype(BF16)
        vo_ref[...] = v_ref[...].astype(BF16)

    blk = lambda f: pl.BlockSpec((ts, LANES), f)
    return _pcall(
        body, name="qkv_prep", grid=(s // ts, heads),
        in_specs=[blk(lambda i, h: (i, h)), blk(lambda i, h: (i, heads + h // 2)),
                  blk(lambda i, h: (i, h)), blk(lambda i, h: (i, heads + h)),
                  blk(lambda i, h: (i, 0)), blk(lambda i, h: (i, 0)), blk(lambda i, h: (i, 0))],
        out_specs=(pl.BlockSpec((None, ts, 2 * LANES), lambda i, h: (h, i, 0)),
                   pl.BlockSpec((None, ts, 2 * LANES), lambda i, h: (h, i, 0)),
                   pl.BlockSpec((None, ts, LANES), lambda i, h: (h, i, 0))),
        out_shape=(jax.ShapeDtypeStruct((heads, s, 2 * LANES), BF16),
                   jax.ShapeDtypeStruct((heads, s, 2 * LANES), BF16),
                   jax.ShapeDtypeStruct((heads, s, LANES), BF16)),
        compiler_params=_params("parallel", "parallel"),
    )(q, q, kv, kv, kr2, cos_t, sin_t)


LOG2_E = 1.4426950408889634


def _flash_fwd(qp, kp, v, *, scale, tq=512):
    heads, s, dk = qp.shape
    tq = _tile(s, tq, 16)
    tk = tq
    th = tq // 2
    c = scale * LOG2_E

    def body(q_ref, k_ref, v_ref, o_ref, lse_ref):
        i = pl.program_id(1)
        qs = [q_ref[hh * th:(hh + 1) * th, :] for hh in range(2)]

        def block(j, carry, diag):
            start = pl.multiple_of(j * tk, tk)
            kb = k_ref[pl.ds(start, tk), :]
            vb = v_ref[pl.ds(start, tk), :]
            new = []
            for hh in range(2):
                m_old, l_old, acc = carry[hh]
                sc = lax.dot_general(qs[hh], kb, (((1,), (1,)), ((), ())), preferred_element_type=F32)
                if diag:
                    rows = hh * th + lax.broadcasted_iota(jnp.int32, (th, tk), 0)
                    cols = lax.broadcasted_iota(jnp.int32, (th, tk), 1)
                    sc = jnp.where(rows >= cols, sc, -jnp.inf)
                m_new = jnp.maximum(m_old, jnp.max(sc, axis=-1, keepdims=True))
                alpha = jnp.exp2((m_old - m_new) * c)
                p = jnp.exp2((sc - m_new) * c)
                l_new = alpha * l_old + jnp.sum(p, axis=-1, keepdims=True)
                acc = alpha * acc + jnp.dot(p.astype(BF16), vb, preferred_element_type=F32)
                new.append((m_new, l_new, acc))
            return tuple(new)

        init = tuple((jnp.full((th, 1), -jnp.inf, F32), jnp.zeros((th, 1), F32), jnp.zeros((th, V_DIM), F32))
                     for _ in range(2))
        carry = lax.fori_loop(0, i, lambda j, cr: block(j, cr, False), init)
        carry = block(i, carry, True)
        for hh in range(2):
            m_fin, l_fin, acc = carry[hh]
            o_ref[hh * th:(hh + 1) * th, :] = acc / l_fin
            lse_ref[hh * th:(hh + 1) * th, :] = m_fin * scale + jnp.log(l_fin)

    return _pcall(
        body, name="flash_fwd", grid=(heads, s // tq),
        in_specs=[pl.BlockSpec((None, tq, dk), lambda h, i: (h, i, 0)),
                  pl.BlockSpec((None, s, dk), lambda h, i: (h, 0, 0)),
                  pl.BlockSpec((None, s, V_DIM), lambda h, i: (h, 0, 0))],
        out_specs=(pl.BlockSpec((tq, V_DIM), lambda h, i: (i, h)),
                   pl.BlockSpec((None, tq, 1), lambda h, i: (h, i, 0))),
        out_shape=(jax.ShapeDtypeStruct((s, heads * V_DIM), F32), jax.ShapeDtypeStruct((heads, s, 1), F32)),
        compiler_params=_params("parallel", "arbitrary"),
    )(qp, kp, v)


def _flash_bwd(qp, kp, v, dcat, o, lse, *, scale, do_col0, tq=512):
    heads, s, dk = qp.shape
    tq = _tile(s, tq, 16)
    tk = tq
    nq = s // tq

    def body(k_ref, v_ref, q_ref, do_ref, o_ref, lse_ref, dq_ref, dk_ref, dv_ref):
        j = pl.program_id(1)

        @pl.when(j == 0)
        def _():
            dq_ref[...] = jnp.zeros_like(dq_ref)

        kb = k_ref[...]
        vb = v_ref[...]

        def block(i, carry, diag):
            dk_acc, dv_acc = carry
            rows_at = pl.ds(pl.multiple_of(i * tq, tq), tq)
            qb = q_ref[rows_at, :]
            do = do_ref[rows_at, :]
            sc = lax.dot_general(qb, kb, (((1,), (1,)), ((), ())), preferred_element_type=F32) * scale
            if diag:
                rows = lax.broadcasted_iota(jnp.int32, (tq, tk), 0)
                cols = lax.broadcasted_iota(jnp.int32, (tq, tk), 1)
                sc = jnp.where(rows >= cols, sc, -jnp.inf)
            p = jnp.exp(sc - lse_ref[rows_at, :])
            dob = do.astype(BF16)
            dv_acc = dv_acc + lax.dot_general(p.astype(BF16), dob, (((0,), (0,)), ((), ())),
                                              preferred_element_type=F32)
            dp = lax.dot_general(dob, vb, (((1,), (1,)), ((), ())), preferred_element_type=F32)
            delta = jnp.sum(do * o_ref[rows_at, :], axis=-1, keepdims=True)
            ds = (p * (dp - delta) * scale).astype(BF16)
            dk_acc = dk_acc + lax.dot_general(ds, qb, (((0,), (0,)), ((), ())), preferred_element_type=F32)
            dq_ref[rows_at, :] += jnp.dot(ds, kb, preferred_element_type=F32)
            return dk_acc, dv_acc

        carry = block(j, (jnp.zeros((tk, dk), F32), jnp.zeros((tk, V_DIM), F32)), True)
        carry = lax.fori_loop(j + 1, nq, lambda i, cr: block(i, cr, False), carry)
        dk_ref[...] = carry[0]
        dv_ref[...] = carry[1]

    whole = lambda h, j: (h, 0, 0)
    return _pcall(
        body, name="flash_bwd", grid=(heads, nq),
        in_specs=[pl.BlockSpec((None, tk, dk), lambda h, j: (h, j, 0)),
                  pl.BlockSpec((None, tk, V_DIM), lambda h, j: (h, j, 0)),
                  pl.BlockSpec((None, s, dk), whole),
                  pl.BlockSpec((s, V_DIM), lambda h, j: (0, do_col0 + h)),
                  pl.BlockSpec((s, V_DIM), lambda h, j: (0, h)),
                  pl.BlockSpec((None, s, 1), whole)],
        out_specs=(pl.BlockSpec((None, s, dk), whole),
                   pl.BlockSpec((None, tk, dk), lambda h, j: (h, j, 0)),
                   pl.BlockSpec((None, tk, V_DIM), lambda h, j: (h, j, 0))),
        out_shape=(jax.ShapeDtypeStruct((heads, s, dk), F32), jax.ShapeDtypeStruct((heads, s, dk), F32),
                   jax.ShapeDtypeStruct((heads, s, V_DIM), F32)),
        compiler_params=_params("parallel", "arbitrary"),
    )(kp, v, qp, dcat, o, lse)


def _attn_bwd_post(dqp, dkp, dv, cos_t, sin_t):
    heads, s, _ = dqp.shape
    ts = _tile(s, 256, 8)

    def body(dq_ref, dk_ref, dv_ref, cos_ref, sin_ref, q_out, kv_out, kr_out):
        lane = lax.broadcasted_iota(jnp.int32, (ts, LANES), 1)
        cos_v = cos_ref[...]
        sin_v = sin_ref[...]

        def rope_t(dy):
            return dy * cos_v + _swap_halves(dy * sin_v, lane)

        kr_sum = jnp.zeros((ts, LANES), F32)
        for h in range(heads):
            q_out[:, h * LANES:(h + 1) * LANES] = dq_ref[h, :, :LANES].astype(BF16)
            kv_out[:, h * LANES:(h + 1) * LANES] = dk_ref[h, :, :LANES].astype(BF16)
            kv_out[:, (heads + h) * LANES:(heads + h + 1) * LANES] = dv_ref[h].astype(BF16)
            kr_sum = kr_sum + dk_ref[h, :, LANES:]
        for pair in range(heads // 2):
            r = jnp.where(lane < ROPE_DIM, dq_ref[2 * pair, :, LANES:], dq_ref[2 * pair + 1, :, LANES:])
            q_out[:, (heads + pair) * LANES:(heads + pair + 1) * LANES] = rope_t(r).astype(BF16)
        kr = rope_t(kr_sum)
        kr_out[...] = (kr + pltpu.roll(kr, ROPE_DIM, 1)).astype(BF16)

    wq = heads * (NOPE_DIM + ROPE_DIM)
    wkv = heads * (NOPE_DIM + V_DIM)
    tab = pl.BlockSpec((ts, LANES), lambda i: (i, 0))
    return _pcall(
        body, name="attn_bwd_post", grid=(s // ts,),
        in_specs=[pl.BlockSpec((heads, ts, 2 * LANES), lambda i: (0, i, 0)),
                  pl.BlockSpec((heads, ts, 2 * LANES), lambda i: (0, i, 0)),
                  pl.BlockSpec((heads, ts, LANES), lambda i: (0, i, 0)), tab, tab],
        out_specs=(pl.BlockSpec((ts, wq), lambda i: (i, 0)), pl.BlockSpec((ts, wkv), lambda i: (i, 0)), tab),
        out_shape=(jax.ShapeDtypeStruct((s, wq), BF16), jax.ShapeDtypeStruct((s, wkv), BF16),
                   jax.ShapeDtypeStruct((s, LANES), BF16)),
        compiler_params=_params("parallel"),
    )(dqp, dkp, dv, cos_t, sin_t)


CONV_HALO = 16


def _conv_gate(xe, cw_ref, cb_ref):
    return (cw_ref[2:3, :] * xe + cw_ref[1:2, :] * pltpu.roll(xe, 1, 0) + cw_ref[0:1, :] * pltpu.roll(xe, 2, 0)
            + cb_ref[...])


def _ffn_act_fwd(gu, conv_w, conv_b, *, d_ff, tc=256):
    s = gu.shape[0]
    tc = _tile(d_ff, tc)
    nf = d_ff // tc
    tr = _tile(s, ROW_CHUNK, CONV_HALO)
    n_chunks = s // tr

    def body(g_ref, u_ref, cw_ref, cb_ref, a_ref):
        r = pl.program_id(1)
        xe = _rows_with_halo(g_ref, r, tr, n_chunks, CONV_HALO, 0).astype(F32)
        gc = _conv_gate(xe, cw_ref, cb_ref)[CONV_HALO:]
        a_ref[...] = (gc * jax.nn.sigmoid(gc) * u_ref[...].astype(F32)).astype(BF16)

    return _pcall(
        body, name="ffn_act_fwd", grid=(nf, n_chunks),
        in_specs=[pl.BlockSpec((s, tc), lambda j, r: (0, j)), pl.BlockSpec((tr, tc), lambda j, r: (r, nf + j)),
                  pl.BlockSpec((CONV_TAPS, tc), lambda j, r: (0, j)), pl.BlockSpec((1, tc), lambda j, r: (0, j))],
        out_specs=pl.BlockSpec((tr, tc), lambda j, r: (r, j)),
        out_shape=jax.ShapeDtypeStruct((s, d_ff), BF16),
        compiler_params=_params("parallel", "arbitrary"),
    )(gu, gu, conv_w, conv_b.reshape(1, d_ff))


def _ffn_act_bwd(da, gu, conv_w, conv_b, *, d_ff, tc=256):
    s = gu.shape[0]
    tc = _tile(d_ff, tc)
    nf = d_ff // tc
    tr = _tile(s, ROW_CHUNK, CONV_HALO)
    n_chunks = s // tr
    rows = tr + 2 * CONV_HALO
    main = slice(CONV_HALO, CONV_HALO + tr)

    def body(da_ref, g_ref, u_ref, cw_ref, cb_ref, dg_ref, du_ref, dcw_ref, dcb_ref):
        r = pl.program_id(1)
        xe = _rows_with_halo(g_ref, r, tr, n_chunks, CONV_HALO, CONV_HALO).astype(F32)
        dae = _rows_with_halo(da_ref, r, tr, n_chunks, CONV_HALO, CONV_HALO).astype(F32)
        ue = _rows_with_halo(u_ref, r, tr, n_chunks, CONV_HALO, CONV_HALO).astype(F32)
        gc = _conv_gate(xe, cw_ref, cb_ref)
        sg = jax.nn.sigmoid(gc)
        du_ref[...] = (dae * gc * sg)[main].astype(BF16)
        dgc = dae * ue * sg * (1.0 + gc * (1.0 - sg))
        dg = (cw_ref[2:3, :] * dgc + cw_ref[1:2, :] * pltpu.roll(dgc, rows - 1, 0)
              + cw_ref[0:1, :] * pltpu.roll(dgc, rows - 2, 0))
        dg_ref[...] = dg[main].astype(BF16)
        dgc_m = dgc[main]
        dcb = jnp.sum(dgc_m, axis=0, keepdims=True)
        dcw = jnp.concatenate([jnp.sum(dgc_m * pltpu.roll(xe, 2, 0)[main], axis=0, keepdims=True),
                               jnp.sum(dgc_m * pltpu.roll(xe, 1, 0)[main], axis=0, keepdims=True),
                               jnp.sum(dgc_m * xe[main], axis=0, keepdims=True)], axis=0)

        @pl.when(r == 0)
        def _():
            dcb_ref[...] = dcb
            dcw_ref[...] = dcw

        @pl.when(r > 0)
        def _():
            dcb_ref[...] += dcb
            dcw_ref[...] += dcw

    col = pl.BlockSpec((s, tc), lambda j, r: (0, j))
    out = pl.BlockSpec((tr, tc), lambda j, r: (r, j))
    return _pcall(
        body, name="ffn_act_bwd", grid=(nf, n_chunks),
        in_specs=[col, col, pl.BlockSpec((s, tc), lambda j, r: (0, nf + j)),
                  pl.BlockSpec((CONV_TAPS, tc), lambda j, r: (0, j)), pl.BlockSpec((1, tc), lambda j, r: (0, j))],
        out_specs=(out, out, pl.BlockSpec((CONV_TAPS, tc), lambda j, r: (0, j)),
                   pl.BlockSpec((1, tc), lambda j, r: (0, j))),
        out_shape=(jax.ShapeDtypeStruct((s, d_ff), BF16), jax.ShapeDtypeStruct((s, d_ff), BF16),
                   jax.ShapeDtypeStruct((CONV_TAPS, d_ff), F32), jax.ShapeDtypeStruct((1, d_ff), F32)),
        compiler_params=_params("parallel", "arbitrary"),
    )(da, gu, gu, conv_w, conv_b.reshape(1, d_ff))


def _ple_fwd(h, z, e):
    s, d = h.shape
    ts = _tile(s, 512, 8)

    def body(h_ref, z_ref, e_ref, o_ref):
        o_ref[...] = h_ref[...] + e_ref[...] * jax.nn.sigmoid(z_ref[...])

    row = pl.BlockSpec((ts, d), lambda i: (i, 0))
    return _pcall(body, name="ple_fwd", grid=(s // ts,), in_specs=[row, row, row], out_specs=row,
                  out_shape=jax.ShapeDtypeStruct((s, d), F32), compiler_params=_params("parallel"))(h, z, e)


def _ple_bwd(dh, z, e, dep=None):
    s, d = dh.shape
    ts = _tile(s, 512, 8)

    def body(dh_ref, z_ref, e_ref, *rest):
        de_ref, dz_ref = rest[-2:]
        gt = jax.nn.sigmoid(z_ref[...])
        dhv = dh_ref[...]
        de_ref[...] = (dhv * gt).astype(BF16)
        dz_ref[...] = (dhv * e_ref[...] * gt * (1.0 - gt)).astype(BF16)

    row = pl.BlockSpec((ts, d), lambda i: (i, 0))
    deps = [] if dep is None else [dep]
    return _pcall(body, name="ple_bwd", grid=(s // ts,), in_specs=[row, row, row] + [HBM_SPEC] * len(deps),
                  out_specs=(row, row),
                  out_shape=(jax.ShapeDtypeStruct((s, d), BF16), jax.ShapeDtypeStruct((s, d), BF16)),
                  compiler_params=_params("parallel"))(dh, z, e, *deps)


def _loss_head(h, g, target):
    s, d = h.shape
    ts = _tile(s, 512, 8)

    def body(h_ref, g_ref, t_ref, loss_ref, dh_ref, dg_ref):
        i = pl.program_id(0)
        x = h_ref[...]
        gv = g_ref[...]
        r = lax.rsqrt(jnp.mean(x * x, axis=-1, keepdims=True) + NORM_EPS)
        nh = x * r
        err = nh * gv - t_ref[...]
        part_loss = 0.5 * jnp.sum(jnp.mean(err * err, axis=-1, keepdims=True), axis=0, keepdims=True)
        dy = err * (1.0 / d)
        gd = dy * gv
        dh_ref[...] = (gd - nh * jnp.mean(gd * nh, axis=-1, keepdims=True)) * r
        part_g = jnp.sum(dy * nh, axis=0, keepdims=True)
        part_l = jnp.broadcast_to(part_loss, (1, LANES))

        @pl.when(i == 0)
        def _():
            dg_ref[...] = part_g
            loss_ref[...] = part_l

        @pl.when(i > 0)
        def _():
            dg_ref[...] += part_g
            loss_ref[...] += part_l

    row = pl.BlockSpec((ts, d), lambda i: (i, 0))
    vec = pl.BlockSpec((1, d), lambda i: (0, 0))
    return _pcall(
        body, name="loss_head", grid=(s // ts,), in_specs=[row, vec, row],
        out_specs=(pl.BlockSpec((1, LANES), lambda i: (0, 0)), row, vec),
        out_shape=(jax.ShapeDtypeStruct((1, LANES), F32), jax.ShapeDtypeStruct((s, d), F32),
                   jax.ShapeDtypeStruct((1, d), F32)),
        compiler_params=_params("arbitrary"),
    )(h, g.reshape(1, d), target)


HBM_SPEC = pl.BlockSpec(memory_space=pl.ANY)


def _flat_index(px, py, pc):
    return 4 * px + 2 * py + pc


def _all_gather(shards, *, name):
    n = len(shards)

    def body(*refs):
        ins, outs = refs[:n], refs[n:2 * n]
        send_sems, recv_sems, local_sems = refs[2 * n:]
        x, y, c = lax.axis_index("x"), lax.axis_index("y"), lax.axis_index("c")
        me, sibling = (x, y, c), (x, y, 1 - c)
        chips = [(1 - x, y), (x, 1 - y), (1 - x, 1 - y)]

        def copy(a, k, block, to, src=None):
            slot = outs[a].at[_flat_index(*block)]
            return pltpu.make_async_remote_copy(
                src_ref=slot if src is None else src, dst_ref=slot,
                send_sem=send_sems.at[a, k], recv_sem=recv_sems.at[a, k],
                device_id=to, device_id_type=pl.DeviceIdType.MESH)

        mine, first, passed = [], [], []
        for a in range(n):
            cp = pltpu.make_async_copy(ins[a], outs[a].at[_flat_index(*me)], local_sems.at[a])
            cp.start()
            mine.append(cp)
            first.append(copy(a, 0, me, sibling, src=ins[a]))
            first += [copy(a, 1 + j, me, (*chip, c), src=ins[a]) for j, chip in enumerate(chips)]
        for cp in first:
            cp.start()
        for j, chip in enumerate(chips):
            for a in range(n):
                copy(a, 1 + j, (*chip, c), me).wait_recv()
                fwd = copy(a, 4 + j, (*chip, c), sibling)
                fwd.start()
                passed.append(fwd)
        for a in range(n):
            copy(a, 0, sibling, me).wait_recv()
            for j, chip in enumerate(chips):
                copy(a, 4 + j, (*chip, 1 - c), me).wait_recv()
        for cp in first + passed:
            cp.wait_send()
        for cp in mine:
            cp.wait()

    return _pcall(
        body, name=name,
        in_specs=[HBM_SPEC] * n, out_specs=[HBM_SPEC] * n,
        out_shape=[jax.ShapeDtypeStruct((N_DEV,) + a.shape, a.dtype) for a in shards],
        scratch_shapes=[pltpu.SemaphoreType.DMA((n, 7)), pltpu.SemaphoreType.DMA((n, 7)),
                        pltpu.SemaphoreType.DMA((n,))],
    )(*shards)


HBM_ONLY = pl.BlockSpec(memory_space=pltpu.HBM)
SEM_SPEC = pl.BlockSpec(memory_space=pltpu.SEMAPHORE)
N_PEERS = N_DEV - 1
PEER_FLIPS = ((0, 0, 1), (1, 0, 0), (0, 1, 0), (1, 1, 0), (1, 0, 1), (0, 1, 1), (1, 1, 1))


def _exchange_refs(gather, src_refs, land_refs, send_sems, recv_sems):
    x, y, c = lax.axis_index("x"), lax.axis_index("y"), lax.axis_index("c")
    me = _flat_index(x, y, c)
    peers = [(x ^ fx, y ^ fy, c ^ fc) for fx, fy, fc in PEER_FLIPS]

    def out_copy(a, k):
        src = src_refs[a] if gather else src_refs[a].at[_flat_index(*peers[k])]
        return pltpu.make_async_remote_copy(
            src_ref=src, dst_ref=land_refs[a].at[me], send_sem=send_sems.at[a * N_PEERS + k],
            recv_sem=recv_sems.at[a * N_PEERS + k], device_id=peers[k], device_id_type=pl.DeviceIdType.MESH)

    def in_copy(a, k):
        src = src_refs[a] if gather else src_refs[a].at[me]
        return pltpu.make_async_remote_copy(
            src_ref=src, dst_ref=land_refs[a].at[_flat_index(*peers[k])], send_sem=send_sems.at[a * N_PEERS + k],
            recv_sem=recv_sems.at[a * N_PEERS + k], device_id=peers[k], device_id_type=pl.DeviceIdType.MESH)

    return out_copy, in_copy


def _exchange_start(srcs, lands, *, gather, name, dep):
    n = len(srcs)

    def body(*refs):
        src_refs, land_refs = refs[:n], refs[n:2 * n]
        send_sems, recv_sems = refs[2 * n + 1], refs[2 * n + 2]
        token = refs[-1]
        out_copy, _ = _exchange_refs(gather, src_refs, land_refs, send_sems, recv_sems)
        for k in range(N_PEERS):
            for a in range(n):
                out_copy(a, k).start()
        token[...] = jnp.zeros_like(token)

    hbm = lambda a: pltpu.with_memory_space_constraint(a, pltpu.HBM)
    return _pcall(
        body, name=name,
        out_shape=(pltpu.SemaphoreType.DMA((n * N_PEERS,)), pltpu.SemaphoreType.DMA((n * N_PEERS,)),
                   *[pltpu.HBM(a.shape, a.dtype) for a in srcs], *[pltpu.HBM(a.shape, a.dtype) for a in lands],
                   jax.ShapeDtypeStruct((8, LANES), F32)),
        in_specs=[HBM_ONLY] * (2 * n) + [HBM_SPEC],
        out_specs=(SEM_SPEC, SEM_SPEC, *[HBM_ONLY] * (2 * n), pl.BlockSpec(memory_space=pltpu.VMEM)),
        input_output_aliases={i: 2 + i for i in range(2 * n)},
        compiler_params=pltpu.CompilerParams(has_side_effects=pltpu.SideEffectType.DATAFLOW_SIDE_EFFECTING),
    )(*[hbm(a) for a in srcs], *[hbm(a) for a in lands], dep)


def _exchange_wait(started, after, *, gather, name):
    send_sems, recv_sems = started[0], started[1]
    n = (len(started) - 3) // 2
    srcs, lands = started[2:2 + n], started[2 + n:2 + 2 * n]

    def body(*refs):
        src_refs, land_refs = refs[:n], refs[n:2 * n]
        s_sems, r_sems = refs[2 * n], refs[2 * n + 1]
        out_copy, in_copy = _exchange_refs(gather, src_refs, land_refs, s_sems, r_sems)
        for k in range(N_PEERS):
            for a in range(n):
                out_copy(a, k).wait_send()
                in_copy(a, k).wait_recv()

    res = _pcall(
        body, name=name,
        out_shape=tuple(pltpu.HBM(a.shape, a.dtype) for a in (*srcs, *lands)),
        in_specs=[HBM_ONLY] * (2 * n) + [SEM_SPEC, SEM_SPEC, HBM_SPEC],
        out_specs=tuple([HBM_ONLY] * (2 * n)),
        input_output_aliases={i: i for i in range(2 * n)},
        compiler_params=pltpu.CompilerParams(has_side_effects=pltpu.SideEffectType.DATAFLOW_SIDE_EFFECTING),
    )(*srcs, *lands, send_sems, recv_sems, after)
    return list(res[n:])


def _own_slot_filled(block, me):
    land = lax.empty((N_DEV,) + block.shape, block.dtype)
    return lax.dynamic_update_slice(land, block[None], (me,) + (0,) * block.ndim)


def _adam_math(g, w, m, v):
    m = ADAM_B1 * m + (1.0 - ADAM_B1) * g
    v = ADAM_B2 * v + (1.0 - ADAM_B2) * jnp.square(g)
    m_hat = m / (1.0 - ADAM_B1 ** ADAM_STEP)
    v_hat = v / (1.0 - ADAM_B2 ** ADAM_STEP)
    delta = -ADAM_LR * (m_hat / (jnp.sqrt(v_hat) + ADAM_EPS) + ADAM_WD * w)
    return delta, m, v


def _adamw(contrib, w, m, v, *, name):
    _, layers, r, c = contrib.shape
    tr = _tile(r, max(8, (256 * 1024 // c) // 8 * 8), 8)

    def body(g_ref, w_ref, m_ref, v_ref, go_ref, d_ref, mo_ref, vo_ref):
        g = g_ref[0].astype(F32)
        for k in range(1, N_DEV):
            g = g + g_ref[k].astype(F32)
        delta, m_new, v_new = _adam_math(g, w_ref[...], m_ref[...], v_ref[...])
        go_ref[...] = g
        d_ref[...] = delta
        mo_ref[...] = m_new
        vo_ref[...] = v_new

    blk = pl.BlockSpec((None, tr, c), lambda l, i: (l, i, 0))
    out = jax.ShapeDtypeStruct((layers, r, c), F32)
    return _pcall(
        body, name=name, grid=(layers, r // tr),
        in_specs=[pl.BlockSpec((N_DEV, None, tr, c), lambda l, i: (0, l, i, 0)), blk, blk, blk],
        out_specs=(blk, blk, blk, blk), out_shape=(out, out, out, out),
        compiler_params=_params("parallel", "parallel"),
    )(contrib, w, m, v)


def _heads_split(w, heads, first, second):
    k = w.shape[0]
    w3 = w.reshape(k, heads, first + second)
    return jnp.concatenate([w3[:, :, :first].reshape(k, heads * first),
                            w3[:, :, first:].reshape(k, heads * second)], axis=1)


def _heads_join(w, heads, first, second):
    k = w.shape[0]
    a = w[:, :heads * first].reshape(k, heads, first)
    b = w[:, heads * first:].reshape(k, heads, second)
    return jnp.concatenate([a, b], axis=2).reshape(k, heads * (first + second))


def _full_from_gathered(kind, g):
    if kind == "col":
        return jnp.transpose(g, (1, 0, 2)).reshape(g.shape[1], N_DEV * g.shape[2])
    if kind == "row":
        return g.reshape(N_DEV * g.shape[1], g.shape[2])
    return jnp.transpose(g, (1, 0, 2, 3)).reshape(g.shape[1], N_DEV * g.shape[2], g.shape[3])


def _blocks_from_full(kind, f):
    if kind == "col":
        k, n = f.shape
        return jnp.transpose(f.reshape(k, N_DEV, n // N_DEV), (1, 0, 2))
    if kind == "row":
        k, n = f.shape
        return f.reshape(N_DEV, k // N_DEV, n)
    g, c_in, c = f.shape
    return jnp.transpose(f.reshape(g, N_DEV, c_in // N_DEV, c), (1, 0, 2, 3))


def _rope_tables(positions):
    inv_freq = 1.0 / (ROPE_THETA ** (jnp.arange(0, ROPE_DIM, 2, dtype=F32) / ROPE_DIM))
    ang = positions.astype(F32)[:, None] * inv_freq
    cos, sin = jnp.cos(ang), jnp.sin(ang)
    return jnp.concatenate([cos, cos, cos, cos], axis=-1), jnp.concatenate([-sin, sin, -sin, sin], axis=-1)


def _layer_fwd(h0, p_i, w, rep, tabs, dims, dep=None):
    heads, d_pool, q_lora, d_ff = dims["heads"], dims["d_pool"], dims["q_lora"], dims["d_ff"]
    c = d_pool // POOL_GROUPS
    cos_t, sin_t = tabs
    scale = 1.0 / math.sqrt(NOPE_DIM + ROPE_DIM)
    n1 = _rms_fwd(h0, rep["norm_mix_g"], name="rms_mix_fwd", dep=dep)
    u = _mm(n1, w["w_in"], name="mm_in_fwd")
    y_pool, diff = _pool_fwd(u, w["pool_w"], rep["pool_scale"], c=c)
    nq = _rms_fwd(u, rep["q_norm_g"], name="rms_q_fwd", col_block=d_pool // q_lora)
    nkv = _rms_fwd(u, rep["kv_norm_g"], name="rms_kv_fwd", col_block=d_pool // q_lora + 1)
    q = _mm(nq, w["w_uq"], name="mm_uq_fwd")
    kv = _mm(nkv, w["w_ukv"], name="mm_ukv_fwd")
    kr = u[:, d_pool + 2 * q_lora:]
    kr2 = jnp.concatenate([kr, kr], axis=-1)
    qp, kp, v = _qkv_prep(q, kv, kr2, cos_t, sin_t, heads=heads)
    o, lse = _flash_fwd(qp, kp, v, scale=scale)
    t = _mm(y_pool, w["w_out"], name="mm_out_pool_fwd", add=h0, b_row=(0, d_pool))
    h1 = _mm(o, w["w_out"], name="mm_out_att_fwd", add=t, b_row=(d_pool, None))
    n2 = _rms_fwd(h1, rep["norm_ffn_g"], name="rms_ffn_fwd")
    gu = _mm(n2, w["w_up"], name="mm_up_fwd", out_dtype=BF16, b_blocks=(0, N_DEV))
    a = _ffn_act_fwd(gu, w["conv_w"], rep["conv_b"], d_ff=d_ff)
    h2 = _mm(a, w["w_down"], name="mm_down_fwd", add=h1)
    n3 = _rms_fwd(h2, rep["norm_ple_g"], name="rms_ple_fwd")
    z = _mm(n3, w["w_ple_gate"], name="mm_pgate_fwd")
    e = _mm(p_i, w["w_ple"], name="mm_ple_fwd")
    h3 = _ple_fwd(h2, z, e)
    saved = dict(h0=h0, n1=n1, u=u, y_pool=y_pool, diff=diff, nq=nq, nkv=nkv, qp=qp, kp=kp, v=v, o=o, lse=lse,
                 h1=h1, n2=n2, gu=gu, a=a, h2=h2, n3=n3, z=z, e=e)
    return h3, saved


def _layer_bwd(dh3, p_i, w, rep, tabs, dims, sv, dep=None, hooks=None):
    hooks = hooks or {}
    heads, d_pool, q_lora, d_ff = dims["heads"], dims["d_pool"], dims["q_lora"], dims["d_ff"]
    c = d_pool // POOL_GROUPS
    cos_t, sin_t = tabs
    scale = 1.0 / math.sqrt(NOPE_DIM + ROPE_DIM)
    gr = {}
    de, dz = _ple_bwd(dh3, sv["z"], sv["e"], dep)
    gr["w_ple"] = _mm(p_i, de, name="mm_ple_dw", ta=True, out_dtype=BF16)
    gr["w_ple_gate"] = _mm(sv["n3"], dz, name="mm_pgate_dw", ta=True, out_dtype=BF16)
    dn3 = _mm(dz, w["w_ple_gate"], name="mm_pgate_dx", tb=True)
    dh2, gr["norm_ple_g"] = _rms_bwd(dn3, sv["h2"], rep["norm_ple_g"], name="rms_ple_bwd", res=dh3)
    gr["w_down"] = _mm(sv["a"], dh2, name="mm_down_dw", ta=True, out_dtype=BF16)
    dep_down = hooks["down"](dh2, gr) if "down" in hooks else None
    da = _mm(dh2, w["w_down"], name="mm_down_dx", tb=True, out_dtype=BF16, dep=dep_down)
    dgate, dup, gr["conv_w"], gr["conv_b"] = _ffn_act_bwd(da, sv["gu"], w["conv_w"], rep["conv_b"], d_ff=d_ff)
    half, per = N_DEV // 2, w["w_up"].shape[2]
    dw_gate = _mm(sv["n2"], dgate, name="mm_up_gate_dw", ta=True, out_dtype=BF16, out_blocks=(N_DEV, 0, per))
    gr["w_up"] = _mm(sv["n2"], dup, name="mm_up_up_dw", ta=True, out_dtype=BF16, out_blocks=(N_DEV, half, per),
                     out_init=dw_gate)
    dep_up = hooks["up"](gr["w_up"], gr) if "up" in hooks else None
    dn2 = _mm(dgate, w["w_up"], name="mm_up_gate_dx", tb=True, b_blocks=(0, half), dep=dep_up)
    dn2 = _mm(dup, w["w_up"], name="mm_up_up_dx", tb=True, b_blocks=(half, half), add=dn2)
    dh1, gr["norm_ffn_g"] = _rms_bwd(dn2, sv["h1"], rep["norm_ffn_g"], name="rms_ffn_bwd", res=dh2)
    dw_out_pool = _mm(sv["y_pool"], dh1, name="mm_out_pool_dw", ta=True, out_dtype=BF16)
    dw_out_att = _mm(sv["o"], dh1, name="mm_out_att_dw", ta=True, out_dtype=BF16)
    gr["w_out"] = jnp.concatenate([dw_out_pool, dw_out_att], axis=0)
    dcat = _mm(dh1, w["w_out"], name="mm_out_dx", tb=True)
    do_col0 = d_pool // V_DIM
    dqp, dkp, dv = _flash_bwd(sv["qp"], sv["kp"], sv["v"], dcat, sv["o"], sv["lse"], scale=scale, do_col0=do_col0)
    dq, dkv, dkr2 = _attn_bwd_post(dqp, dkp, dv, cos_t, sin_t)
    gr["w_uq"] = _mm(sv["nq"], dq, name="mm_uq_dw", ta=True, out_dtype=BF16)
    gr["w_ukv"] = _mm(sv["nkv"], dkv, name="mm_ukv_dw", ta=True, out_dtype=BF16)
    dnq = _mm(dq, w["w_uq"], name="mm_uq_dx", tb=True)
    dnkv = _mm(dkv, w["w_ukv"], name="mm_ukv_dx", tb=True)
    dcq, gr["q_norm_g"] = _rms_bwd(dnq, sv["u"], rep["q_norm_g"], name="rms_q_bwd",
                                   col_block=d_pool // q_lora, out_dtype=BF16)
    dckv, gr["kv_norm_g"] = _rms_bwd(dnkv, sv["u"], rep["kv_norm_g"], name="rms_kv_bwd",
                                     col_block=d_pool // q_lora + 1, out_dtype=BF16)
    du_pool, gr["pool_w"], gr["pool_scale"] = _pool_bwd(dcat, sv["diff"], w["pool_w"], rep["pool_scale"], c=c)
    du = jnp.concatenate([du_pool, dcq, dckv, dkr2[:, :ROPE_DIM]], axis=-1)
    gr["w_in"] = _mm(sv["n1"], du, name="mm_in_dw", ta=True, out_dtype=BF16)
    dn1 = _mm(du, w["w_in"], name="mm_in_dx", tb=True)
    dh0, gr["norm_mix_g"] = _rms_bwd(dn1, sv["h0"], rep["norm_mix_g"], name="rms_mix_bwd", res=dh1)
    return dh0, gr


def _as2d(a):
    return a.reshape(a.shape[0], -1, a.shape[-1])


def kernel(x, p, positions, norm_mix_g, w_in, pool_w, pool_scale, q_norm_g, w_uq, kv_norm_g, w_ukv, w_out, norm_ffn_g, w_up, conv_w, conv_b, w_down, norm_ple_g, w_ple, w_ple_gate, final_norm_g, loss_target, m_norm_mix_g, m_w_in, m_pool_w, m_pool_scale, m_q_norm_g, m_w_uq, m_kv_norm_g, m_w_ukv, m_w_out, m_norm_ffn_g, m_w_up, m_conv_w, m_conv_b, m_w_down, m_norm_ple_g, m_w_ple, m_w_ple_gate, m_final_norm_g, v_norm_mix_g, v_w_in, v_pool_w, v_pool_scale, v_q_norm_g, v_w_uq, v_kv_norm_g, v_w_ukv, v_w_out, v_norm_ffn_g, v_w_up, v_conv_w, v_conv_b, v_w_down, v_norm_ple_g, v_w_ple, v_w_ple_gate, v_final_norm_g):
    weights = dict(norm_mix_g=norm_mix_g, w_in=w_in, pool_w=pool_w, pool_scale=pool_scale, q_norm_g=q_norm_g,
                   w_uq=w_uq, kv_norm_g=kv_norm_g, w_ukv=w_ukv, w_out=w_out, norm_ffn_g=norm_ffn_g, w_up=w_up,
                   conv_w=conv_w, conv_b=conv_b, w_down=w_down, norm_ple_g=norm_ple_g, w_ple=w_ple,
                   w_ple_gate=w_ple_gate, final_norm_g=final_norm_g)
    m_in = dict(norm_mix_g=m_norm_mix_g, w_in=m_w_in, pool_w=m_pool_w, pool_scale=m_pool_scale, q_norm_g=m_q_norm_g,
                w_uq=m_w_uq, kv_norm_g=m_kv_norm_g, w_ukv=m_w_ukv, w_out=m_w_out, norm_ffn_g=m_norm_ffn_g,
                w_up=m_w_up, conv_w=m_conv_w, conv_b=m_conv_b, w_down=m_w_down, norm_ple_g=m_norm_ple_g,
                w_ple=m_w_ple, w_ple_gate=m_w_ple_gate, final_norm_g=m_final_norm_g)
    v_in = dict(norm_mix_g=v_norm_mix_g, w_in=v_w_in, pool_w=v_pool_w, pool_scale=v_pool_scale, q_norm_g=v_q_norm_g,
                w_uq=v_w_uq, kv_norm_g=v_kv_norm_g, w_ukv=v_w_ukv, w_out=v_w_out, norm_ffn_g=v_norm_ffn_g,
                w_up=v_w_up, conv_w=v_conv_w, conv_b=v_conv_b, w_down=v_w_down, norm_ple_g=v_norm_ple_g,
                w_ple=v_w_ple, w_ple_gate=v_w_ple_gate, final_norm_g=v_final_norm_g)

    depth = w_in.shape[0]
    s, d_model = x.shape[1], x.shape[2]
    d_pool = pool_scale.shape[-1]
    q_lora = q_norm_g.shape[-1]
    d_ff = conv_b.shape[-1]
    heads = (w_uq.shape[-1] * N_DEV) // (NOPE_DIM + ROPE_DIM)
    dims = dict(heads=heads, d_pool=d_pool, q_lora=q_lora, d_ff=d_ff)

    me = _flat_index(lax.axis_index("x"), lax.axis_index("y"), lax.axis_index("c"))

    def layer_shards(i):
        return [weights[n][i] if n == "conv_w" else weights[n][i].astype(BF16) for n in SHARDED]

    def full_weights(i, gathered_i):
        w = {n: g if n == "w_up" else _full_from_gathered(SHARD_KIND[n], g) for n, g in zip(SHARDED, gathered_i)}
        w["w_uq"] = _heads_split(w["w_uq"], heads, NOPE_DIM, ROPE_DIM)
        w["w_ukv"] = _heads_split(w["w_ukv"], heads, NOPE_DIM, V_DIM)
        rep = {n: weights[n][i] for n in REPLICATED}
        return w, rep

    tabs = _rope_tables(positions[0])

    gathered = _all_gather(layer_shards(0), name="weights_all_gather_0")
    layer_w = []
    h = x[0]
    saved = []
    for i in range(depth):
        layer_w.append(full_weights(i, gathered))
        started = None
        if i + 1 < depth:
            shards = layer_shards(i + 1)
            started = _exchange_start(shards, [_own_slot_filled(a, me) for a in shards], gather=True,
                                      name=f"weights_gather_start_{i + 1}", dep=gathered[0] if i == 0 else h)
        w, rep = layer_w[i]
        h, sv = _layer_fwd(h, p[i, 0], w, rep, tabs, dims, dep=None if started is None else started[-1])
        saved.append(sv)
        if started is not None:
            gathered = _exchange_wait(started, h, gather=True, name=f"weights_gather_wait_{i + 1}")
    loss_row, dh, g_final = _loss_head(h, final_norm_g, loss_target[0])
    loss = lax.psum(loss_row[0, 0], MESH_AXES)

    groups = {"down": ("w_down", "w_ple", "w_ple_gate"), "up": ("w_up", "conv_w"),
              "mix": ("w_in", "pool_w", "w_uq", "w_ukv", "w_out")}

    def start_grads(group, gr, dep, i):
        names = groups[group]
        blocks = [gr[n] if n == "w_up" else _blocks_from_full(SHARD_KIND[n], gr[n]).astype(BF16) for n in names]
        lands = [_own_slot_filled(lax.dynamic_index_in_dim(b, me, 0, keepdims=False), me) for b in blocks]
        return _exchange_start(blocks, lands, gather=False, name=f"grads_{group}_start_{i}", dep=dep)

    def end_grads(group, started, after, i):
        got = _exchange_wait(started, after, gather=False, name=f"grads_{group}_wait_{i}")
        received[i].update(zip(groups[group], got))

    layer_grads = [None] * depth
    received = [dict() for _ in range(depth)]
    pending = None
    for i in reversed(range(depth)):
        w, rep = layer_w[i]
        state = {}

        def on_down(dh2, gr, i=i, state=state):
            state["down"] = start_grads("down", gr, dh2, i)
            return state["down"][-1]

        def on_up(dw_up, gr, i=i, pending=pending, state=state):
            if pending is not None:
                for group in ("down", "up", "mix"):
                    end_grads(group, pending[group], dw_up, i + 1)
            state["up"] = start_grads("up", gr, dw_up, i)
            return state["up"][-1]

        dh, gr = _layer_bwd(dh, p[i, 0], w, rep, tabs, dims, saved[i],
                            dep=loss.reshape(1, 1) if pending is None else pending["mix"][-1],
                            hooks={"down": on_down, "up": on_up})
        gr["w_uq"] = _heads_join(gr["w_uq"], heads, NOPE_DIM, ROPE_DIM)
        gr["w_ukv"] = _heads_join(gr["w_ukv"], heads, NOPE_DIM, V_DIM)
        layer_grads[i] = gr
        state["mix"] = start_grads("mix", gr, dh, i)
        pending = state
    grad_x = dh[None]

    out = {}

    def update(n):
        shape = weights[n].shape
        rec = jnp.stack([received[i][n] for i in range(depth)], axis=1)
        rec2 = rec.reshape((N_DEV, depth, -1, shape[-1]))
        res = _adamw(rec2, _as2d(weights[n]), _as2d(m_in[n]), _as2d(v_in[n]), name="adamw_" + n)
        out[n] = tuple(r.reshape(shape) for r in res)

    end_grads("down", pending["down"], pending["mix"][-1], 0)
    end_grads("up", pending["up"], pending["mix"][-1], 0)
    for n in groups["down"] + groups["up"]:
        update(n)
    end_grads("mix", pending["mix"], out["w_up"][0], 0)
    for n in groups["mix"]:
        update(n)

    small_names = REPLICATED + ("final_norm_g",)

    def pack(get):
        rows = [jnp.stack([get(n, i).reshape(-1) for i in range(depth)]).reshape(-1) for n in REPLICATED]
        rows.append(get("final_norm_g", None).reshape(-1))
        return jnp.concatenate(rows).reshape(1, -1, LANES)

    g_small = pack(lambda n, i: g_final if i is None else layer_grads[i][n])
    w_small = pack(lambda n, i: weights[n] if i is None else weights[n][i])
    m_small = pack(lambda n, i: m_in[n] if i is None else m_in[n][i])
    v_small = pack(lambda n, i: v_in[n] if i is None else v_in[n][i])
    (g_all,) = _all_gather([g_small], name="small_grads_all_gather")
    res_small = _adamw(g_all, w_small, m_small, v_small, name="adamw_small")

    def unpack(flat3):
        flat = flat3.reshape(-1)
        res, off = {}, 0
        for n in REPLICATED:
            width = weights[n].shape[-1]
            res[n] = flat[off:off + depth * width].reshape(depth, width)
            off += depth * width
        res["final_norm_g"] = flat[off:off + d_model]
        return res

    small = [unpack(r) for r in res_small]
    for n in small_names:
        out[n] = tuple(small[k][n] for k in range(4))

    outs = [loss, grad_x]
    for k in range(4):
        outs += [out[n][k] for n in WEIGHT_ORDER]
    return tuple(outs)
```

```python
import functools
import math

import jax
import jax.numpy as jnp
from jax import lax
from jax.experimental import pallas as pl
from jax.experimental.pallas import tpu as pltpu

F32 = jnp.float32
BF16 = jnp.bfloat16

N_DEV = 8
MESH_AXES = ("x", "y", "c")
NOPE_DIM = 128
ROPE_DIM = 64
V_DIM = 128
POOL_GROUPS = 4
CONV_TAPS = 3
ROPE_THETA = 10000.0
NORM_EPS = 1e-6
ADAM_LR = 0.001
ADAM_B1 = 0.9
ADAM_B2 = 0.999
ADAM_EPS = 1e-08
ADAM_WD = 0.01
ADAM_STEP = 10
LANES = 128
VMEM_LIMIT_BYTES = 56 * 1024 * 1024

SHARDED = ("w_in", "pool_w", "w_uq", "w_ukv", "w_out", "w_up", "conv_w", "w_down", "w_ple", "w_ple_gate")
SHARD_KIND = {"w_in": "col", "pool_w": "pool", "w_uq": "col", "w_ukv": "col", "w_out": "row", "w_up": "col",
              "conv_w": "col", "w_down": "row", "w_ple": "col", "w_ple_gate": "row"}
REPLICATED = ("norm_mix_g", "pool_scale", "q_norm_g", "kv_norm_g", "norm_ffn_g", "conv_b", "norm_ple_g")
WEIGHT_ORDER = ("norm_mix_g", "w_in", "pool_w", "pool_scale", "q_norm_g", "w_uq", "kv_norm_g", "w_ukv", "w_out",
                "norm_ffn_g", "w_up", "conv_w", "conv_b", "w_down", "norm_ple_g", "w_ple", "w_ple_gate",
                "final_norm_g")

_pcall = pl.pallas_call


def _params(*sem):
    return pltpu.CompilerParams(dimension_semantics=sem or None, vmem_limit_bytes=VMEM_LIMIT_BYTES)


def _tile(n, pref, mult=LANES):
    if n <= pref:
        return n
    t = (pref // mult) * mult
    while t >= mult:
        if n % t == 0:
            return t
        t -= mult
    return n


MM_VMEM_BUDGET_BYTES = 40 * 1024 * 1024
MM_TILE_PREFS = (1024, 512, 256)
MM_MIN_TK = 1024


def _mm_tiles(m, n, k, a_bytes, b_bytes, o_bytes, has_add, tn_fixed=None, tk_fixed=None):
    tn = tn_fixed or _tile(n, MM_TILE_PREFS[0])
    if tk_fixed:
        k_cands = [tk_fixed]
    else:
        k_cands = [k] + [t for t in range((k - 1) // LANES * LANES, 0, -LANES) if k % t == 0]
    best = None
    for pref in MM_TILE_PREFS:
        tm = _tile(m, pref)
        for tk in k_cands:
            need = 2 * (tm * tk * a_bytes + tk * tn * b_bytes) + 2 * tm * tn * (o_bytes + (4 if has_add else 0))
            if tk < k:
                need += tm * tn * 4
            if need <= MM_VMEM_BUDGET_BYTES:
                if tk >= min(k, MM_MIN_TK):
                    return tm, tn, tk
                if best is None or tk > best[2]:
                    best = (tm, tn, tk)
                break
    assert best is not None, (m, n, k)
    return best


def _mm(a, b, *, name, ta=False, tb=False, add=None, out_dtype=F32, b_row=(0, None), b_blocks=None,
        out_blocks=None, out_init=None, dep=None):
    m = a.shape[1] if ta else a.shape[0]
    kdim = a.shape[0] if ta else a.shape[1]
    tn_fixed = tk_fixed = None
    k_off_b = 0
    if b_blocks is None:
        n_b, k_b = (b.shape if tb else b.shape[::-1])
        k_off_b, kb_sz = b_row
        kb_sz = k_b - k_off_b if kb_sz is None else kb_sz
        assert kb_sz == kdim, (name, kb_sz, kdim)
        n = n_b
    else:
        first_b, count_b = b_blocks
        per_b = b.shape[2]
        if tb:
            n = b.shape[1]
            assert kdim == count_b * per_b, name
            tk_fixed = per_b
        else:
            n = count_b * per_b
            assert kdim == b.shape[1], name
            tn_fixed = per_b
    if out_blocks is not None:
        nb_out, first_o, tn_fixed = out_blocks
    tm, tn, tk = _mm_tiles(m, n, kdim, a.dtype.itemsize, b.dtype.itemsize, jnp.dtype(out_dtype).itemsize,
                           add is not None, tn_fixed, tk_fixed)
    assert k_off_b % tk == 0 and kdim % tk == 0 and n % tn == 0 and m % tm == 0, name
    kb0 = k_off_b // tk
    nk = kdim // tk
    dims = (((0 if ta else 1,), (1 if tb else 0,)), ((), ()))

    def body(*refs):
        a_ref, b_ref = refs[0], refs[1]
        add_ref = refs[2] if add is not None else None
        o_ref = refs[n_in]
        part = lax.dot_general(a_ref[...].astype(BF16), b_ref[...].astype(BF16), dims, preferred_element_type=F32)

        def finish(r):
            if add_ref is not None:
                r = r + add_ref[...].astype(F32)
            o_ref[...] = r.astype(out_dtype)

        if nk == 1:
            finish(part)
            return
        acc = refs[n_in + 1]
        k = pl.program_id(2)

        @pl.when(k == 0)
        def _():
            acc[...] = part

        @pl.when(jnp.logical_and(k > 0, k < nk - 1))
        def _():
            acc[...] += part

        @pl.when(k == nk - 1)
        def _():
            finish(acc[...] + part)

    if ta:
        a_spec = pl.BlockSpec((tk, tm), lambda i, j, k: (k, i))
    else:
        a_spec = pl.BlockSpec((tm, tk), lambda i, j, k: (i, k))
    if b_blocks is not None and tb:
        b_spec = pl.BlockSpec((None, tn, tk), lambda i, j, k: (k + first_b, j, 0))
    elif b_blocks is not None:
        b_spec = pl.BlockSpec((None, tk, tn), lambda i, j, k: (j + first_b, k, 0))
    elif tb:
        b_spec = pl.BlockSpec((tn, tk), lambda i, j, k: (j, k + kb0))
    else:
        b_spec = pl.BlockSpec((tk, tn), lambda i, j, k: (k + kb0, j))
    in_specs = [a_spec, b_spec]
    args = [a, b]
    if add is not None:
        in_specs.append(pl.BlockSpec((tm, tn), lambda i, j, k: (i, j)))
        args.append(add)
    aliases = {}
    if out_init is not None:
        aliases = {len(args): 0}
        in_specs.append(HBM_SPEC)
        args.append(out_init)
    if dep is not None:
        in_specs.append(HBM_SPEC)
        args.append(dep)
    n_in = len(args)
    if out_blocks is None:
        out_spec = pl.BlockSpec((tm, tn), lambda i, j, k: (i, j))
        out_shape = jax.ShapeDtypeStruct((m, n), out_dtype)
    else:
        out_spec = pl.BlockSpec((None, tm, tn), lambda i, j, k: (j + first_o, i, 0))
        out_shape = jax.ShapeDtypeStruct((nb_out, m, tn), out_dtype)
    return _pcall(
        body, name=name, grid=(m // tm, n // tn, nk), in_specs=in_specs, out_specs=out_spec, out_shape=out_shape,
        scratch_shapes=[pltpu.VMEM((tm, tn), F32)] if nk > 1 else [], input_output_aliases=aliases,
        compiler_params=_params("parallel", "parallel", "arbitrary"),
    )(*args)


def _rms_fwd(h, g, *, name, col_block=0, dep=None):
    s = h.shape[0]
    d = g.shape[-1]
    ts = _tile(s, 512, 8)

    def body(h_ref, g_ref, *rest):
        n_ref = rest[-1]
        x = h_ref[...]
        r = lax.rsqrt(jnp.mean(x * x, axis=-1, keepdims=True) + NORM_EPS)
        n_ref[...] = (x * r * g_ref[...]).astype(BF16)

    deps = [] if dep is None else [dep]
    return _pcall(
        body, name=name, grid=(s // ts,),
        in_specs=[pl.BlockSpec((ts, d), lambda i: (i, col_block)), pl.BlockSpec((1, d), lambda i: (0, 0))]
        + [HBM_SPEC] * len(deps),
        out_specs=pl.BlockSpec((ts, d), lambda i: (i, 0)),
        out_shape=jax.ShapeDtypeStruct((s, d), BF16),
        compiler_params=_params("parallel"),
    )(h, g.reshape(1, d), *deps)


def _rms_bwd(dn, h, g, *, name, res=None, col_block=0, out_dtype=F32):
    s = dn.shape[0]
    d = g.shape[-1]
    ts = _tile(s, 512, 8)

    def body(*refs):
        if res is None:
            dn_ref, h_ref, g_ref, dh_ref, dg_ref = refs
            res_ref = None
        else:
            dn_ref, h_ref, g_ref, res_ref, dh_ref, dg_ref = refs
        i = pl.program_id(0)
        x = h_ref[...]
        r = lax.rsqrt(jnp.mean(x * x, axis=-1, keepdims=True) + NORM_EPS)
        nh = x * r
        dnv = dn_ref[...]
        gd = dnv * g_ref[...]
        dh = (gd - nh * jnp.mean(gd * nh, axis=-1, keepdims=True)) * r
        if res_ref is not None:
            dh = dh + res_ref[...]
        dh_ref[...] = dh.astype(out_dtype)
        part = jnp.sum(dnv * nh, axis=0, keepdims=True)

        @pl.when(i == 0)
        def _():
            dg_ref[...] = part

        @pl.when(i > 0)
        def _():
            dg_ref[...] += part

    row = pl.BlockSpec((ts, d), lambda i: (i, 0))
    in_specs = [row, pl.BlockSpec((ts, d), lambda i: (i, col_block)), pl.BlockSpec((1, d), lambda i: (0, 0))]
    args = [dn, h, g.reshape(1, d)]
    if res is not None:
        in_specs.append(row)
        args.append(res)
    return _pcall(
        body, name=name, grid=(s // ts,), in_specs=in_specs,
        out_specs=(row, pl.BlockSpec((1, d), lambda i: (0, 0))),
        out_shape=(jax.ShapeDtypeStruct((s, d), out_dtype), jax.ShapeDtypeStruct((1, d), F32)),
        compiler_params=_params("arbitrary"),
    )(*args)


ROW_CHUNK = 512


def _rows_with_halo(ref, r, t_rows, n_chunks, before, after):
    r0 = r * t_rows
    parts = []
    if before:
        hb = ref[pl.ds(pl.multiple_of(jnp.maximum(r0 - before, 0), before), before), :]
        parts.append(jnp.where(r > 0, hb, jnp.zeros_like(hb)))
    parts.append(ref[pl.ds(pl.multiple_of(r0, t_rows), t_rows), :])
    if after:
        ha = ref[pl.ds(pl.multiple_of(jnp.minimum(r0 + t_rows, n_chunks * t_rows - after), after), after), :]
        parts.append(jnp.where(r < n_chunks - 1, ha, jnp.zeros_like(ha)))
    return jnp.concatenate(parts, axis=0)


POOL_HALO = 16


def _pool_fwd(u, pool_w, pool_scale, *, c):
    s = u.shape[0]
    g_n = POOL_GROUPS
    tr = _tile(s, ROW_CHUNK, POOL_HALO)
    n_chunks = s // tr

    def body(u_ref, pw_ref, sc_ref, y_ref, d_ref):
        r = pl.program_id(1)
        w = jnp.left_shift(2, pl.program_id(0))
        xe = _rows_with_halo(u_ref, r, tr, n_chunks, POOL_HALO, 0)
        acc = xe
        for k in (1, 2, 4, 8):
            acc = jnp.where(k < w, acc + pltpu.roll(acc, k, 0), acc)
        t = r * tr + lax.broadcasted_iota(jnp.int32, (tr, 1), 0)
        cnt = jnp.minimum(t + 1, w).astype(F32)
        diff = (acc[POOL_HALO:] / cnt - xe[POOL_HALO:]).astype(BF16)
        d_ref[...] = diff
        y = jnp.dot(diff, pw_ref[...], preferred_element_type=F32) * sc_ref[...]
        y_ref[...] = y.astype(BF16)

    out = pl.BlockSpec((tr, c), lambda g, r: (r, g))
    return _pcall(
        body, name="pool_fwd", grid=(g_n, n_chunks),
        in_specs=[pl.BlockSpec((s, c), lambda g, r: (0, g)), pl.BlockSpec((None, c, c), lambda g, r: (g, 0, 0)),
                  pl.BlockSpec((1, c), lambda g, r: (0, g))],
        out_specs=(out, out),
        out_shape=(jax.ShapeDtypeStruct((s, g_n * c), BF16), jax.ShapeDtypeStruct((s, g_n * c), BF16)),
        compiler_params=_params("parallel", "arbitrary"),
    )(u, pool_w, pool_scale.reshape(1, g_n * c))


def _pool_bwd(dcat, diff, pool_w, pool_scale, *, c):
    s = dcat.shape[0]
    g_n = POOL_GROUPS
    tr = _tile(s, ROW_CHUNK, POOL_HALO)
    n_chunks = s // tr

    def body(dy_ref, d_ref, pw_ref, sc_ref, du_ref, dpw_ref, dsc_ref):
        r = pl.program_id(1)
        w = jnp.left_shift(2, pl.program_id(0))
        dye = _rows_with_halo(dy_ref, r, tr, n_chunks, 0, POOL_HALO)
        diff = d_ref[pl.ds(pl.multiple_of(r * tr, tr), tr), :]
        pw = pw_ref[...]
        yp = jnp.dot(diff, pw, preferred_element_type=F32)
        dsc = jnp.sum(dye[:tr] * yp, axis=0, keepdims=True)
        dyp = (dye * sc_ref[...]).astype(BF16)
        ddiff = lax.dot_general(dyp, pw, (((1,), (1,)), ((), ())), preferred_element_type=F32)
        dpw = lax.dot_general(diff, dyp[:tr], (((0,), (0,)), ((), ())), preferred_element_type=F32)
        t = r * tr + lax.broadcasted_iota(jnp.int32, (tr + POOL_HALO, 1), 0)
        cnt = jnp.minimum(t + 1, w).astype(F32)
        acc = ddiff / cnt
        rows = tr + POOL_HALO
        for k in (1, 2, 4, 8):
            acc = jnp.where(k < w, acc + pltpu.roll(acc, rows - k, 0), acc)
        du_ref[...] = (acc[:tr] - ddiff[:tr]).astype(BF16)

        @pl.when(r == 0)
        def _():
            dpw_ref[...] = dpw
            dsc_ref[...] = dsc

        @pl.when(r > 0)
        def _():
            dpw_ref[...] += dpw
            dsc_ref[...] += dsc

    col = lambda g, r: (0, g)
    wspec = pl.BlockSpec((None, c, c), lambda g, r: (g, 0, 0))
    vec = pl.BlockSpec((1, c), col)
    return _pcall(
        body, name="pool_bwd", grid=(g_n, n_chunks),
        in_specs=[pl.BlockSpec((s, c), col), pl.BlockSpec((s, c), col), wspec, vec],
        out_specs=(pl.BlockSpec((tr, c), lambda g, r: (r, g)), wspec, vec),
        out_shape=(jax.ShapeDtypeStruct((s, g_n * c), BF16), jax.ShapeDtypeStruct((g_n, c, c), F32),
                   jax.ShapeDtypeStruct((1, g_n * c), F32)),
        compiler_params=_params("parallel", "arbitrary"),
    )(dcat, diff, pool_w, pool_scale.reshape(1, g_n * c))


def _swap_halves(x, lane):
    return jnp.where((lane % ROPE_DIM) < ROPE_DIM // 2, pltpu.roll(x, LANES - ROPE_DIM // 2, 1),
                     pltpu.roll(x, ROPE_DIM // 2, 1))


def _qkv_prep(q, kv, kr2, cos_t, sin_t, *, heads):
    s = q.shape[0]
    ts = _tile(s, 256, 8)

    def body(q_ref, kv_ref, kr_ref, cos_ref, sin_ref, qo_ref, ko_ref, vo_ref):
        lane = lax.broadcasted_iota(jnp.int32, (ts, LANES), 1)
        cos_v = cos_ref[...]
        sin_v = sin_ref[...]

        def rope(x):
            return x * cos_v + _swap_halves(x, lane) * sin_v

        kr = rope(kr_ref[...]).astype(BF16)
        for pair in range(heads // 2):
            qr = rope(q_ref[:, (heads + pair) * LANES:(heads + pair + 1) * LANES])
            for half in range(2):
                h = 2 * pair + half
                qo_ref[h, :, :LANES] = q_ref[:, h * LANES:(h + 1) * LANES].astype(BF16)
                qo_ref[h, :, LANES:] = jnp.where(lane // ROPE_DIM == half, qr, 0.0).astype(BF16)
        for h in range(heads):
            ko_ref[h, :, :LANES] = kv_ref[:, h * LANES:(h + 1) * LANES].astype(BF16)
            ko_ref[h, :, LANES:] = kr
            vo_ref[h] = kv_ref[:, (heads + h) * LANES:(heads + h + 1) * LANES].astype(BF16)

    tab = pl.BlockSpec((ts, LANES), lambda i: (i, 0))
    return _pcall(
        body, name="qkv_prep", grid=(s // ts,),
        in_specs=[pl.BlockSpec((ts, q.shape[1]), lambda i: (i, 0)), pl.BlockSpec((ts, kv.shape[1]), lambda i: (i, 0)),
                  tab, tab, tab],
        out_specs=(pl.BlockSpec((heads, ts, 2 * LANES), lambda i: (0, i, 0)),
                   pl.BlockSpec((heads, ts, 2 * LANES), lambda i: (0, i, 0)),
                   pl.BlockSpec((heads, ts, LANES), lambda i: (0, i, 0))),
        out_shape=(jax.ShapeDtypeStruct((heads, s, 2 * LANES), BF16),
                   jax.ShapeDtypeStruct((heads, s, 2 * LANES), BF16),
                   jax.ShapeDtypeStruct((heads, s, LANES), BF16)),
        compiler_params=_params("parallel"),
    )(q, kv, kr2, cos_t, sin_t)


LOG2_E = 1.4426950408889634


FLASH_FWD_TK = 512
FLASH_FWD_SPLITS = 2


def _flash_fwd(qp, kp, v, *, scale, tq=512):
    heads, s, dk = qp.shape
    tq = _tile(s, tq, 16)
    tk = _tile(tq, FLASH_FWD_TK, 16)
    splits = FLASH_FWD_SPLITS
    th = tq // splits
    band = tq // tk
    c = scale * LOG2_E

    def body(q_ref, k_ref, v_ref, o_ref, lse_ref):
        i = pl.program_id(1)
        qs = [q_ref[hh * th:(hh + 1) * th, :] for hh in range(splits)]

        def skipped(hh, col0):
            return col0 is not None and col0 >= (hh + 1) * th

        def scores(start, col0):
            kb = k_ref[pl.ds(pl.multiple_of(start, tk), tk), :]
            return tuple(None if skipped(hh, col0) else
                         lax.dot_general(qs[hh], kb, (((1,), (1,)), ((), ())), preferred_element_type=F32)
                         for hh in range(splits))

        def absorb(start, scs, state, col0):
            vb = v_ref[pl.ds(pl.multiple_of(start, tk), tk), :]
            new = []
            for hh in range(splits):
                if scs[hh] is None:
                    new.append(state[hh])
                    continue
                m_old, l_old, acc = state[hh]
                sc = scs[hh]
                if col0 is not None and col0 + tk - 1 > hh * th:
                    rows = hh * th + lax.broadcasted_iota(jnp.int32, (th, tk), 0)
                    cols = col0 + lax.broadcasted_iota(jnp.int32, (th, tk), 1)
                    sc = jnp.where(rows >= cols, sc, -jnp.inf)
                m_new = jnp.maximum(m_old, jnp.max(sc, axis=-1, keepdims=True))
                alpha = jnp.exp2((m_old - m_new) * c)
                p = jnp.exp2((sc - m_new) * c)
                l_new = alpha * l_old + jnp.sum(p, axis=-1, keepdims=True)
                acc = alpha * acc + jnp.dot(p.astype(BF16), vb, preferred_element_type=F32)
                new.append((m_new, l_new, acc))
            return tuple(new)

        def step(j, carry):
            state, scs = carry
            nxt = scores((j + 1) * tk, None)
            return absorb(j * tk, scs, state, None), nxt

        init = tuple((jnp.full((th, 1), -jnp.inf, F32), jnp.zeros((th, 1), F32), jnp.zeros((th, V_DIM), F32))
                     for _ in range(splits))
        state, scs = lax.fori_loop(0, i * band, step, (init, scores(0, None)))
        for b in range(band):
            nxt = scores(i * tq + (b + 1) * tk, (b + 1) * tk) if b + 1 < band else None
            state = absorb(i * tq + b * tk, scs, state, b * tk)
            scs = nxt
        carry = state
        for hh in range(splits):
            m_fin, l_fin, acc = carry[hh]
            o_ref[hh * th:(hh + 1) * th, :] = acc / l_fin
            lse_ref[hh * th:(hh + 1) * th, :] = m_fin * scale + jnp.log(l_fin)

    return _pcall(
        body, name="flash_fwd", grid=(heads, s // tq),
        in_specs=[pl.BlockSpec((None, tq, dk), lambda h, i: (h, i, 0)),
                  pl.BlockSpec((None, s, dk), lambda h, i: (h, 0, 0)),
                  pl.BlockSpec((None, s, V_DIM), lambda h, i: (h, 0, 0))],
        out_specs=(pl.BlockSpec((tq, V_DIM), lambda h, i: (i, h)),
                   pl.BlockSpec((None, tq, 1), lambda h, i: (h, i, 0))),
        out_shape=(jax.ShapeDtypeStruct((s, heads * V_DIM), F32), jax.ShapeDtypeStruct((heads, s, 1), F32)),
        compiler_params=_params("parallel", "arbitrary"),
    )(qp, kp, v)


def _flash_bwd(qp, kp, v, dcat, o, lse, *, scale, do_col0, tq=512):
    heads, s, dk = qp.shape
    tq = _tile(s, tq, 16)
    tk = tq
    nq = s // tq

    def body(k_ref, v_ref, q_ref, do_ref, o_ref, lse_ref, dq_ref, dk_ref, dv_ref):
        j = pl.program_id(1)

        @pl.when(j == 0)
        def _():
            dq_ref[...] = jnp.zeros_like(dq_ref)

        kb = k_ref[...]
        vb = v_ref[...]

        def block(i, carry, diag):
            dk_acc, dv_acc = carry
            rows_at = pl.ds(pl.multiple_of(i * tq, tq), tq)
            qb = q_ref[rows_at, :]
            do = do_ref[rows_at, :]
            sc = lax.dot_general(qb, kb, (((1,), (1,)), ((), ())), preferred_element_type=F32) * scale
            if diag:
                rows = lax.broadcasted_iota(jnp.int32, (tq, tk), 0)
                cols = lax.broadcasted_iota(jnp.int32, (tq, tk), 1)
                sc = jnp.where(rows >= cols, sc, -jnp.inf)
            p = jnp.exp(sc - lse_ref[rows_at, :])
            dob = do.astype(BF16)
            dv_acc = dv_acc + lax.dot_general(p.astype(BF16), dob, (((0,), (0,)), ((), ())),
                                              preferred_element_type=F32)
            dp = lax.dot_general(dob, vb, (((1,), (1,)), ((), ())), preferred_element_type=F32)
            delta = jnp.sum(do * o_ref[rows_at, :], axis=-1, keepdims=True)
            ds = (p * (dp - delta) * scale).astype(BF16)
            dk_acc = dk_acc + lax.dot_general(ds, qb, (((0,), (0,)), ((), ())), preferred_element_type=F32)
            dq_ref[rows_at, :] += jnp.dot(ds, kb, preferred_element_type=F32)
            return dk_acc, dv_acc

        carry = block(j, (jnp.zeros((tk, dk), F32), jnp.zeros((tk, V_DIM), F32)), True)
        carry = lax.fori_loop(j + 1, nq, lambda i, cr: block(i, cr, False), carry)
        dk_ref[...] = carry[0]
        dv_ref[...] = carry[1]

    whole = lambda h, j: (h, 0, 0)
    return _pcall(
        body, name="flash_bwd", grid=(heads, nq),
        in_specs=[pl.BlockSpec((None, tk, dk), lambda h, j: (h, j, 0)),
                  pl.BlockSpec((None, tk, V_DIM), lambda h, j: (h, j, 0)),
                  pl.BlockSpec((None, s, dk), whole),
                  pl.BlockSpec((s, V_DIM), lambda h, j: (0, do_col0 + h)),
                  pl.BlockSpec((s, V_DIM), lambda h, j: (0, h)),
                  pl.BlockSpec((None, s, 1), whole)],
        out_specs=(pl.BlockSpec((None, s, dk), whole),
                   pl.BlockSpec((None, tk, dk), lambda h, j: (h, j, 0)),
                   pl.BlockSpec((None, tk, V_DIM), lambda h, j: (h, j, 0))),
        out_shape=(jax.ShapeDtypeStruct((heads, s, dk), F32), jax.ShapeDtypeStruct((heads, s, dk), F32),
                   jax.ShapeDtypeStruct((heads, s, V_DIM), F32)),
        compiler_params=_params("parallel", "arbitrary"),
    )(kp, v, qp, dcat, o, lse)


def _attn_bwd_post(dqp, dkp, dv, cos_t, sin_t):
    heads, s, _ = dqp.shape
    ts = _tile(s, 256, 8)

    def body(dq_ref, dk_ref, dv_ref, cos_ref, sin_ref, q_out, kv_out, kr_out):
        lane = lax.broadcasted_iota(jnp.int32, (ts, LANES), 1)
        cos_v = cos_ref[...]
        sin_v = sin_ref[...]

        def rope_t(dy):
            return dy * cos_v + _swap_halves(dy * sin_v, lane)

        kr_sum = jnp.zeros((ts, LANES), F32)
        for h in range(heads):
            q_out[:, h * LANES:(h + 1) * LANES] = dq_ref[h, :, :LANES].astype(BF16)
            kv_out[:, h * LANES:(h + 1) * LANES] = dk_ref[h, :, :LANES].astype(BF16)
            kv_out[:, (heads + h) * LANES:(heads + h + 1) * LANES] = dv_ref[h].astype(BF16)
            kr_sum = kr_sum + dk_ref[h, :, LANES:]
        for pair in range(heads // 2):
            r = jnp.where(lane < ROPE_DIM, dq_ref[2 * pair, :, LANES:], dq_ref[2 * pair + 1, :, LANES:])
            q_out[:, (heads + pair) * LANES:(heads + pair + 1) * LANES] = rope_t(r).astype(BF16)
        kr = rope_t(kr_sum)
        kr_out[...] = (kr + pltpu.roll(kr, ROPE_DIM, 1)).astype(BF16)

    wq = heads * (NOPE_DIM + ROPE_DIM)
    wkv = heads * (NOPE_DIM + V_DIM)
    tab = pl.BlockSpec((ts, LANES), lambda i: (i, 0))
    return _pcall(
        body, name="attn_bwd_post", grid=(s // ts,),
        in_specs=[pl.BlockSpec((heads, ts, 2 * LANES), lambda i: (0, i, 0)),
                  pl.BlockSpec((heads, ts, 2 * LANES), lambda i: (0, i, 0)),
                  pl.BlockSpec((heads, ts, LANES), lambda i: (0, i, 0)), tab, tab],
        out_specs=(pl.BlockSpec((ts, wq), lambda i: (i, 0)), pl.BlockSpec((ts, wkv), lambda i: (i, 0)), tab),
        out_shape=(jax.ShapeDtypeStruct((s, wq), BF16), jax.ShapeDtypeStruct((s, wkv), BF16),
                   jax.ShapeDtypeStruct((s, LANES), BF16)),
        compiler_params=_params("parallel"),
    )(dqp, dkp, dv, cos_t, sin_t)


CONV_HALO = 16


def _conv_gate(xe, cw_ref, cb_ref):
    x1 = pltpu.roll(xe, 1, 0)
    x2 = pltpu.roll(xe, 2, 0)
    return cw_ref[2:3, :] * xe + cw_ref[1:2, :] * x1 + cw_ref[0:1, :] * x2 + cb_ref[...], x1, x2


def _ffn_act_fwd(gu, conv_w, conv_b, *, d_ff, tc=256):
    s = gu.shape[0]
    tc = _tile(d_ff, tc)
    nf = d_ff // tc
    tr = _tile(s, ROW_CHUNK, CONV_HALO)
    n_chunks = s // tr

    def body(g_ref, u_ref, cw_ref, cb_ref, a_ref):
        r = pl.program_id(1)
        xe = _rows_with_halo(g_ref, r, tr, n_chunks, CONV_HALO, 0).astype(F32)
        gc = _conv_gate(xe, cw_ref, cb_ref)[0][CONV_HALO:]
        a_ref[...] = (gc * jax.nn.sigmoid(gc) * u_ref[...].astype(F32)).astype(BF16)

    return _pcall(
        body, name="ffn_act_fwd", grid=(nf, n_chunks),
        in_specs=[pl.BlockSpec((s, tc), lambda j, r: (0, j)), pl.BlockSpec((tr, tc), lambda j, r: (r, nf + j)),
                  pl.BlockSpec((CONV_TAPS, tc), lambda j, r: (0, j)), pl.BlockSpec((1, tc), lambda j, r: (0, j))],
        out_specs=pl.BlockSpec((tr, tc), lambda j, r: (r, j)),
        out_shape=jax.ShapeDtypeStruct((s, d_ff), BF16),
        compiler_params=_params("parallel", "arbitrary"),
    )(gu, gu, conv_w, conv_b.reshape(1, d_ff))


def _ffn_act_bwd(da, gu, conv_w, conv_b, *, d_ff, tc=256):
    s = gu.shape[0]
    tc = _tile(d_ff, tc)
    nf = d_ff // tc
    tr = _tile(s, ROW_CHUNK, CONV_HALO)
    n_chunks = s // tr
    rows = tr + 2 * CONV_HALO
    main = slice(CONV_HALO, CONV_HALO + tr)

    def body(da_ref, g_ref, u_ref, cw_ref, cb_ref, dg_ref, du_ref, dcw_ref, dcb_ref):
        r = pl.program_id(1)
        xe = _rows_with_halo(g_ref, r, tr, n_chunks, CONV_HALO, CONV_HALO).astype(F32)
        dae = _rows_with_halo(da_ref, r, tr, n_chunks, CONV_HALO, CONV_HALO).astype(F32)
        ue = _rows_with_halo(u_ref, r, tr, n_chunks, CONV_HALO, CONV_HALO).astype(F32)
        gc, x1, x2 = _conv_gate(xe, cw_ref, cb_ref)
        sg = jax.nn.sigmoid(gc)
        du_ref[...] = (dae * gc * sg)[main].astype(BF16)
        dgc = dae * ue * sg * (1.0 + gc * (1.0 - sg))
        dg = (cw_ref[2:3, :] * dgc + cw_ref[1:2, :] * pltpu.roll(dgc, rows - 1, 0)
              + cw_ref[0:1, :] * pltpu.roll(dgc, rows - 2, 0))
        dg_ref[...] = dg[main].astype(BF16)
        dgc_m = dgc[main]
        dcb = jnp.sum(dgc_m, axis=0, keepdims=True)
        dcw = jnp.concatenate([jnp.sum(dgc_m * x2[main], axis=0, keepdims=True),
                               jnp.sum(dgc_m * x1[main], axis=0, keepdims=True),
                               jnp.sum(dgc_m * xe[main], axis=0, keepdims=True)], axis=0)

        @pl.when(r == 0)
        def _():
            dcb_ref[...] = dcb
            dcw_ref[...] = dcw

        @pl.when(r > 0)
        def _():
            dcb_ref[...] += dcb
            dcw_ref[...] += dcw

    col = pl.BlockSpec((s, tc), lambda j, r: (0, j))
    out = pl.BlockSpec((tr, tc), lambda j, r: (r, j))
    return _pcall(
        body, name="ffn_act_bwd", grid=(nf, n_chunks),
        in_specs=[col, col, pl.BlockSpec((s, tc), lambda j, r: (0, nf + j)),
                  pl.BlockSpec((CONV_TAPS, tc), lambda j, r: (0, j)), pl.BlockSpec((1, tc), lambda j, r: (0, j))],
        out_specs=(out, out, pl.BlockSpec((CONV_TAPS, tc), lambda j, r: (0, j)),
                   pl.BlockSpec((1, tc), lambda j, r: (0, j))),
        out_shape=(jax.ShapeDtypeStruct((s, d_ff), BF16), jax.ShapeDtypeStruct((s, d_ff), BF16),
                   jax.ShapeDtypeStruct((CONV_TAPS, d_ff), F32), jax.ShapeDtypeStruct((1, d_ff), F32)),
        compiler_params=_params("parallel", "arbitrary"),
    )(da, gu, gu, conv_w, conv_b.reshape(1, d_ff))


def _ple_fwd(h, z, e):
    s, d = h.shape
    ts = _tile(s, 512, 8)

    def body(h_ref, z_ref, e_ref, o_ref):
        o_ref[...] = h_ref[...] + e_ref[...] * jax.nn.sigmoid(z_ref[...])

    row = pl.BlockSpec((ts, d), lambda i: (i, 0))
    return _pcall(body, name="ple_fwd", grid=(s // ts,), in_specs=[row, row, row], out_specs=row,
                  out_shape=jax.ShapeDtypeStruct((s, d), F32), compiler_params=_params("parallel"))(h, z, e)


def _ple_bwd(dh, z, e, dep=None):
    s, d = dh.shape
    ts = _tile(s, 512, 8)

    def body(dh_ref, z_ref, e_ref, *rest):
        de_ref, dz_ref = rest[-2:]
        gt = jax.nn.sigmoid(z_ref[...])
        dhv = dh_ref[...]
        de_ref[...] = (dhv * gt).astype(BF16)
        dz_ref[...] = (dhv * e_ref[...] * gt * (1.0 - gt)).astype(BF16)

    row = pl.BlockSpec((ts, d), lambda i: (i, 0))
    deps = [] if dep is None else [dep]
    return _pcall(body, name="ple_bwd", grid=(s // ts,), in_specs=[row, row, row] + [HBM_SPEC] * len(deps),
                  out_specs=(row, row),
                  out_shape=(jax.ShapeDtypeStruct((s, d), BF16), jax.ShapeDtypeStruct((s, d), BF16)),
                  compiler_params=_params("parallel"))(dh, z, e, *deps)


def _loss_head(h, g, target):
    s, d = h.shape
    ts = _tile(s, 512, 8)

    def body(h_ref, g_ref, t_ref, loss_ref, dh_ref, dg_ref):
        i = pl.program_id(0)
        x = h_ref[...]
        gv = g_ref[...]
        r = lax.rsqrt(jnp.mean(x * x, axis=-1, keepdims=True) + NORM_EPS)
        nh = x * r
        err = nh * gv - t_ref[...]
        part_loss = 0.5 * jnp.sum(jnp.mean(err * err, axis=-1, keepdims=True), axis=0, keepdims=True)
        dy = err * (1.0 / d)
        gd = dy * gv
        dh_ref[...] = (gd - nh * jnp.mean(gd * nh, axis=-1, keepdims=True)) * r
        part_g = jnp.sum(dy * nh, axis=0, keepdims=True)
        part_l = jnp.broadcast_to(part_loss, (1, LANES))

        @pl.when(i == 0)
        def _():
            dg_ref[...] = part_g
            loss_ref[...] = part_l

        @pl.when(i > 0)
        def _():
            dg_ref[...] += part_g
            loss_ref[...] += part_l

    row = pl.BlockSpec((ts, d), lambda i: (i, 0))
    vec = pl.BlockSpec((1, d), lambda i: (0, 0))
    return _pcall(
        body, name="loss_head", grid=(s // ts,), in_specs=[row, vec, row],
        out_specs=(pl.BlockSpec((1, LANES), lambda i: (0, 0)), row, vec),
        out_shape=(jax.ShapeDtypeStruct((1, LANES), F32), jax.ShapeDtypeStruct((s, d), F32),
                   jax.ShapeDtypeStruct((1, d), F32)),
        compiler_params=_params("arbitrary"),
    )(h, g.reshape(1, d), target)


HBM_SPEC = pl.BlockSpec(memory_space=pl.ANY)


def _flat_index(px, py, pc):
    return 4 * px + 2 * py + pc


def _all_gather(shards, *, name):
    n = len(shards)

    def body(*refs):
        ins, outs = refs[:n], refs[n:2 * n]
        send_sems, recv_sems, local_sems = refs[2 * n:]
        x, y, c = lax.axis_index("x"), lax.axis_index("y"), lax.axis_index("c")
        me, sibling = (x, y, c), (x, y, 1 - c)
        chips = [(1 - x, y), (x, 1 - y), (1 - x, 1 - y)]

        def copy(a, k, block, to, src=None):
            slot = outs[a].at[_flat_index(*block)]
            return pltpu.make_async_remote_copy(
                src_ref=slot if src is None else src, dst_ref=slot,
                send_sem=send_sems.at[a, k], recv_sem=recv_sems.at[a, k],
                device_id=to, device_id_type=pl.DeviceIdType.MESH)

        mine, first, passed = [], [], []
        for a in range(n):
            cp = pltpu.make_async_copy(ins[a], outs[a].at[_flat_index(*me)], local_sems.at[a])
            cp.start()
            mine.append(cp)
            first.append(copy(a, 0, me, sibling, src=ins[a]))
            first += [copy(a, 1 + j, me, (*chip, c), src=ins[a]) for j, chip in enumerate(chips)]
        for cp in first:
            cp.start()
        for j, chip in enumerate(chips):
            for a in range(n):
                copy(a, 1 + j, (*chip, c), me).wait_recv()
                fwd = copy(a, 4 + j, (*chip, c), sibling)
                fwd.start()
                passed.append(fwd)
        for a in range(n):
            copy(a, 0, sibling, me).wait_recv()
            for j, chip in enumerate(chips):
                copy(a, 4 + j, (*chip, 1 - c), me).wait_recv()
        for cp in first + passed:
            cp.wait_send()
        for cp in mine:
            cp.wait()

    return _pcall(
        body, name=name,
        in_specs=[HBM_SPEC] * n, out_specs=[HBM_SPEC] * n,
        out_shape=[jax.ShapeDtypeStruct((N_DEV,) + a.shape, a.dtype) for a in shards],
        scratch_shapes=[pltpu.SemaphoreType.DMA((n, 7)), pltpu.SemaphoreType.DMA((n, 7)),
                        pltpu.SemaphoreType.DMA((n,))],
    )(*shards)


HBM_ONLY = pl.BlockSpec(memory_space=pltpu.HBM)
SEM_SPEC = pl.BlockSpec(memory_space=pltpu.SEMAPHORE)
N_PEERS = N_DEV - 1
PEER_FLIPS = ((0, 0, 1), (1, 0, 0), (0, 1, 0), (1, 1, 0), (1, 0, 1), (0, 1, 1), (1, 1, 1))


def _exchange_refs(gather, src_refs, land_refs, send_sems, recv_sems):
    x, y, c = lax.axis_index("x"), lax.axis_index("y"), lax.axis_index("c")
    me = _flat_index(x, y, c)
    peers = [(x ^ fx, y ^ fy, c ^ fc) for fx, fy, fc in PEER_FLIPS]

    def out_copy(a, k):
        src = src_refs[a] if gather else src_refs[a].at[_flat_index(*peers[k])]
        return pltpu.make_async_remote_copy(
            src_ref=src, dst_ref=land_refs[a].at[me], send_sem=send_sems.at[a * N_PEERS + k],
            recv_sem=recv_sems.at[a * N_PEERS + k], device_id=peers[k], device_id_type=pl.DeviceIdType.MESH)

    def in_copy(a, k):
        src = src_refs[a] if gather else src_refs[a].at[me]
        return pltpu.make_async_remote_copy(
            src_ref=src, dst_ref=land_refs[a].at[_flat_index(*peers[k])], send_sem=send_sems.at[a * N_PEERS + k],
            recv_sem=recv_sems.at[a * N_PEERS + k], device_id=peers[k], device_id_type=pl.DeviceIdType.MESH)

    return out_copy, in_copy


def _exchange_start(srcs, lands, *, gather, name, dep):
    n = len(srcs)

    def body(*refs):
        src_refs, land_refs = refs[:n], refs[n:2 * n]
        send_sems, recv_sems = refs[2 * n + 1], refs[2 * n + 2]
        token = refs[-1]
        out_copy, _ = _exchange_refs(gather, src_refs, land_refs, send_sems, recv_sems)
        for k in range(N_PEERS):
            for a in range(n):
                out_copy(a, k).start()
        token[...] = jnp.zeros_like(token)

    hbm = lambda a: pltpu.with_memory_space_constraint(a, pltpu.HBM)
    return _pcall(
        body, name=name,
        out_shape=(pltpu.SemaphoreType.DMA((n * N_PEERS,)), pltpu.SemaphoreType.DMA((n * N_PEERS,)),
                   *[pltpu.HBM(a.shape, a.dtype) for a in srcs], *[pltpu.HBM(a.shape, a.dtype) for a in lands],
                   jax.ShapeDtypeStruct((8, LANES), F32)),
        in_specs=[HBM_ONLY] * (2 * n) + [HBM_SPEC],
        out_specs=(SEM_SPEC, SEM_SPEC, *[HBM_ONLY] * (2 * n), pl.BlockSpec(memory_space=pltpu.VMEM)),
        input_output_aliases={i: 2 + i for i in range(2 * n)},
        compiler_params=pltpu.CompilerParams(has_side_effects=pltpu.SideEffectType.DATAFLOW_SIDE_EFFECTING),
    )(*[hbm(a) for a in srcs], *[hbm(a) for a in lands], dep)


def _exchange_wait(started, after, *, gather, name):
    send_sems, recv_sems = started[0], started[1]
    n = (len(started) - 3) // 2
    srcs, lands = started[2:2 + n], started[2 + n:2 + 2 * n]

    def body(*refs):
        src_refs, land_refs = refs[:n], refs[n:2 * n]
        s_sems, r_sems = refs[2 * n], refs[2 * n + 1]
        out_copy, in_copy = _exchange_refs(gather, src_refs, land_refs, s_sems, r_sems)
        for k in range(N_PEERS):
            for a in range(n):
                out_copy(a, k).wait_send()
                in_copy(a, k).wait_recv()

    res = _pcall(
        body, name=name,
        out_shape=tuple(pltpu.HBM(a.shape, a.dtype) for a in (*srcs, *lands)),
        in_specs=[HBM_ONLY] * (2 * n) + [SEM_SPEC, SEM_SPEC, HBM_SPEC],
        out_specs=tuple([HBM_ONLY] * (2 * n)),
        input_output_aliases={i: i for i in range(2 * n)},
        compiler_params=pltpu.CompilerParams(has_side_effects=pltpu.SideEffectType.DATAFLOW_SIDE_EFFECTING),
    )(*srcs, *lands, send_sems, recv_sems, after)
    return list(res[n:])


def _own_slot_filled(block, me):
    land = lax.empty((N_DEV,) + block.shape, block.dtype)
    return lax.dynamic_update_slice(land, block[None], (me,) + (0,) * block.ndim)


def _adam_math(g, w, m, v):
    m = ADAM_B1 * m + (1.0 - ADAM_B1) * g
    v = ADAM_B2 * v + (1.0 - ADAM_B2) * jnp.square(g)
    m_hat = m / (1.0 - ADAM_B1 ** ADAM_STEP)
    v_hat = v / (1.0 - ADAM_B2 ** ADAM_STEP)
    delta = -ADAM_LR * (m_hat / (jnp.sqrt(v_hat) + ADAM_EPS) + ADAM_WD * w)
    return delta, m, v


def _adamw(contribs, w, m, v, *, name):
    layers = len(contribs)
    _, r, c = contribs[0].shape
    tr = _tile(r, max(8, (256 * 1024 // c) // 8 * 8), 8)

    def body(*refs):
        g_refs = refs[:layers]
        w_ref, m_ref, v_ref, go_ref, d_ref, mo_ref, vo_ref = refs[layers:]
        layer = pl.program_id(0)
        for l2 in range(layers):

            @pl.when(layer == l2)
            def _(g_ref=g_refs[l2]):
                g = g_ref[0].astype(F32)
                for k in range(1, N_DEV):
                    g = g + g_ref[k].astype(F32)
                delta, m_new, v_new = _adam_math(g, w_ref[...], m_ref[...], v_ref[...])
                go_ref[...] = g
                d_ref[...] = delta
                mo_ref[...] = m_new
                vo_ref[...] = v_new

    g_specs = [pl.BlockSpec((N_DEV, tr, c), lambda l, i, l2=l2: (0, jnp.where(l == l2, i, 0), 0))
               for l2 in range(layers)]
    blk = pl.BlockSpec((None, tr, c), lambda l, i: (l, i, 0))
    out = jax.ShapeDtypeStruct((layers, r, c), F32)
    return _pcall(
        body, name=name, grid=(layers, r // tr), in_specs=g_specs + [blk, blk, blk],
        out_specs=(blk, blk, blk, blk), out_shape=(out, out, out, out),
        compiler_params=_params("arbitrary", "arbitrary"),
    )(*contribs, w, m, v)


def _heads_split(w, heads, first, second):
    k = w.shape[0]
    w3 = w.reshape(k, heads, first + second)
    return jnp.concatenate([w3[:, :, :first].reshape(k, heads * first),
                            w3[:, :, first:].reshape(k, heads * second)], axis=1)


def _heads_join(w, heads, first, second):
    k = w.shape[0]
    a = w[:, :heads * first].reshape(k, heads, first)
    b = w[:, heads * first:].reshape(k, heads, second)
    return jnp.concatenate([a, b], axis=2).reshape(k, heads * (first + second))


def _full_from_gathered(kind, g):
    if kind == "col":
        return jnp.transpose(g, (1, 0, 2)).reshape(g.shape[1], N_DEV * g.shape[2])
    if kind == "row":
        return g.reshape(N_DEV * g.shape[1], g.shape[2])
    return jnp.transpose(g, (1, 0, 2, 3)).reshape(g.shape[1], N_DEV * g.shape[2], g.shape[3])


def _blocks_from_full(kind, f):
    if kind == "col":
        k, n = f.shape
        return jnp.transpose(f.reshape(k, N_DEV, n // N_DEV), (1, 0, 2))
    if kind == "row":
        k, n = f.shape
        return f.reshape(N_DEV, k // N_DEV, n)
    g, c_in, c = f.shape
    return jnp.transpose(f.reshape(g, N_DEV, c_in // N_DEV, c), (1, 0, 2, 3))


def _rope_tables(positions):
    inv_freq = 1.0 / (ROPE_THETA ** (jnp.arange(0, ROPE_DIM, 2, dtype=F32) / ROPE_DIM))
    ang = positions.astype(F32)[:, None] * inv_freq
    cos, sin = jnp.cos(ang), jnp.sin(ang)
    return jnp.concatenate([cos, cos, cos, cos], axis=-1), jnp.concatenate([-sin, sin, -sin, sin], axis=-1)


def _layer_fwd(h0, p_i, fetch, rep, tabs, dims, dep=None):
    heads, d_pool, q_lora, d_ff = dims["heads"], dims["d_pool"], dims["q_lora"], dims["d_ff"]
    c = d_pool // POOL_GROUPS
    cos_t, sin_t = tabs
    scale = 1.0 / math.sqrt(NOPE_DIM + ROPE_DIM)
    n1 = _rms_fwd(h0, rep["norm_mix_g"], name="rms_mix_fwd", dep=dep)
    w = dict(fetch("mix", n1))
    u = _mm(n1, w["w_in"], name="mm_in_fwd")
    y_pool, diff = _pool_fwd(u, w["pool_w"], rep["pool_scale"], c=c)
    nq = _rms_fwd(u, rep["q_norm_g"], name="rms_q_fwd", col_block=d_pool // q_lora)
    nkv = _rms_fwd(u, rep["kv_norm_g"], name="rms_kv_fwd", col_block=d_pool // q_lora + 1)
    q = _mm(nq, w["w_uq"], name="mm_uq_fwd")
    kv = _mm(nkv, w["w_ukv"], name="mm_ukv_fwd")
    kr = u[:, d_pool + 2 * q_lora:]
    kr2 = jnp.concatenate([kr, kr], axis=-1)
    qp, kp, v = _qkv_prep(q, kv, kr2, cos_t, sin_t, heads=heads)
    o, lse = _flash_fwd(qp, kp, v, scale=scale)
    t = _mm(y_pool, w["w_out"], name="mm_out_pool_fwd", add=h0, b_row=(0, d_pool))
    h1 = _mm(o, w["w_out"], name="mm_out_att_fwd", add=t, b_row=(d_pool, None))
    n2 = _rms_fwd(h1, rep["norm_ffn_g"], name="rms_ffn_fwd")
    w.update(fetch("up", n2))
    gu = _mm(n2, w["w_up"], name="mm_up_fwd", out_dtype=BF16, b_blocks=(0, N_DEV))
    a = _ffn_act_fwd(gu, w["conv_w"], rep["conv_b"], d_ff=d_ff)
    w.update(fetch("down", a))
    h2 = _mm(a, w["w_down"], name="mm_down_fwd", add=h1)
    n3 = _rms_fwd(h2, rep["norm_ple_g"], name="rms_ple_fwd")
    z = _mm(n3, w["w_ple_gate"], name="mm_pgate_fwd")
    e = _mm(p_i, w["w_ple"], name="mm_ple_fwd")
    h3 = _ple_fwd(h2, z, e)
    saved = dict(h0=h0, n1=n1, u=u, y_pool=y_pool, diff=diff, nq=nq, nkv=nkv, qp=qp, kp=kp, v=v, o=o, lse=lse,
                 h1=h1, n2=n2, gu=gu, a=a, h2=h2, n3=n3, z=z, e=e)
    return h3, saved, w


def _layer_bwd(dh3, p_i, w, rep, tabs, dims, sv, dep=None, hooks=None):
    hooks = hooks or {}
    heads, d_pool, q_lora, d_ff = dims["heads"], dims["d_pool"], dims["q_lora"], dims["d_ff"]
    c = d_pool // POOL_GROUPS
    cos_t, sin_t = tabs
    scale = 1.0 / math.sqrt(NOPE_DIM + ROPE_DIM)
    gr = {}
    de, dz = _ple_bwd(dh3, sv["z"], sv["e"], dep)
    gr["w_ple"] = _mm(p_i, de, name="mm_ple_dw", ta=True, out_dtype=BF16)
    gr["w_ple_gate"] = _mm(sv["n3"], dz, name="mm_pgate_dw", ta=True, out_dtype=BF16)
    dn3 = _mm(dz, w["w_ple_gate"], name="mm_pgate_dx", tb=True)
    dh2, gr["norm_ple_g"] = _rms_bwd(dn3, sv["h2"], rep["norm_ple_g"], name="rms_ple_bwd", res=dh3)
    gr["w_down"] = _mm(sv["a"], dh2, name="mm_down_dw", ta=True, out_dtype=BF16)
    dep_down = hooks["down"](dh2, gr) if "down" in hooks else None
    da = _mm(dh2, w["w_down"], name="mm_down_dx", tb=True, out_dtype=BF16, dep=dep_down)
    dgate, dup, gr["conv_w"], gr["conv_b"] = _ffn_act_bwd(da, sv["gu"], w["conv_w"], rep["conv_b"], d_ff=d_ff)
    half, per = N_DEV // 2, w["w_up"].shape[2]
    dw_gate = _mm(sv["n2"], dgate, name="mm_up_gate_dw", ta=True, out_dtype=BF16, out_blocks=(N_DEV, 0, per))
    gr["w_up"] = _mm(sv["n2"], dup, name="mm_up_up_dw", ta=True, out_dtype=BF16, out_blocks=(N_DEV, half, per),
                     out_init=dw_gate)
    dep_up = hooks["up"](gr["w_up"], gr) if "up" in hooks else None
    dn2 = _mm(dgate, w["w_up"], name="mm_up_gate_dx", tb=True, b_blocks=(0, half), dep=dep_up)
    dn2 = _mm(dup, w["w_up"], name="mm_up_up_dx", tb=True, b_blocks=(half, half), add=dn2)
    dh1, gr["norm_ffn_g"] = _rms_bwd(dn2, sv["h1"], rep["norm_ffn_g"], name="rms_ffn_bwd", res=dh2)
    dw_out_pool = _mm(sv["y_pool"], dh1, name="mm_out_pool_dw", ta=True, out_dtype=BF16)
    dw_out_att = _mm(sv["o"], dh1, name="mm_out_att_dw", ta=True, out_dtype=BF16)
    gr["w_out"] = jnp.concatenate([dw_out_pool, dw_out_att], axis=0)
    dcat = _mm(dh1, w["w_out"], name="mm_out_dx", tb=True)
    do_col0 = d_pool // V_DIM
    dqp, dkp, dv = _flash_bwd(sv["qp"], sv["kp"], sv["v"], dcat, sv["o"], sv["lse"], scale=scale, do_col0=do_col0)
    dq, dkv, dkr2 = _attn_bwd_post(dqp, dkp, dv, cos_t, sin_t)
    gr["w_uq"] = _mm(sv["nq"], dq, name="mm_uq_dw", ta=True, out_dtype=BF16)
    gr["w_ukv"] = _mm(sv["nkv"], dkv, name="mm_ukv_dw", ta=True, out_dtype=BF16)
    dnq = _mm(dq, w["w_uq"], name="mm_uq_dx", tb=True)
    dnkv = _mm(dkv, w["w_ukv"], name="mm_ukv_dx", tb=True)
    dcq, gr["q_norm_g"] = _rms_bwd(dnq, sv["u"], rep["q_norm_g"], name="rms_q_bwd",
                                   col_block=d_pool // q_lora, out_dtype=BF16)
    dckv, gr["kv_norm_g"] = _rms_bwd(dnkv, sv["u"], rep["kv_norm_g"], name="rms_kv_bwd",
                                     col_block=d_pool // q_lora + 1, out_dtype=BF16)
    du_pool, gr["pool_w"], gr["pool_scale"] = _pool_bwd(dcat, sv["diff"], w["pool_w"], rep["pool_scale"], c=c)
    du = jnp.concatenate([du_pool, dcq, dckv, dkr2[:, :ROPE_DIM]], axis=-1)
    gr["w_in"] = _mm(sv["n1"], du, name="mm_in_dw", ta=True, out_dtype=BF16)
    dn1 = _mm(du, w["w_in"], name="mm_in_dx", tb=True)
    dh0, gr["norm_mix_g"] = _rms_bwd(dn1, sv["h0"], rep["norm_mix_g"], name="rms_mix_bwd", res=dh1)
    return dh0, gr


def _as2d(a):
    return a.reshape(a.shape[0], -1, a.shape[-1])


def kernel(x, p, positions, norm_mix_g, w_in, pool_w, pool_scale, q_norm_g, w_uq, kv_norm_g, w_ukv, w_out, norm_ffn_g, w_up, conv_w, conv_b, w_down, norm_ple_g, w_ple, w_ple_gate, final_norm_g, loss_target, m_norm_mix_g, m_w_in, m_pool_w, m_pool_scale, m_q_norm_g, m_w_uq, m_kv_norm_g, m_w_ukv, m_w_out, m_norm_ffn_g, m_w_up, m_conv_w, m_conv_b, m_w_down, m_norm_ple_g, m_w_ple, m_w_ple_gate, m_final_norm_g, v_norm_mix_g, v_w_in, v_pool_w, v_pool_scale, v_q_norm_g, v_w_uq, v_kv_norm_g, v_w_ukv, v_w_out, v_norm_ffn_g, v_w_up, v_conv_w, v_conv_b, v_w_down, v_norm_ple_g, v_w_ple, v_w_ple_gate, v_final_norm_g):
    weights = dict(norm_mix_g=norm_mix_g, w_in=w_in, pool_w=pool_w, pool_scale=pool_scale, q_norm_g=q_norm_g,
                   w_uq=w_uq, kv_norm_g=kv_norm_g, w_ukv=w_ukv, w_out=w_out, norm_ffn_g=norm_ffn_g, w_up=w_up,
                   conv_w=conv_w, conv_b=conv_b, w_down=w_down, norm_ple_g=norm_ple_g, w_ple=w_ple,
                   w_ple_gate=w_ple_gate, final_norm_g=final_norm_g)
    m_in = dict(norm_mix_g=m_norm_mix_g, w_in=m_w_in, pool_w=m_pool_w, pool_scale=m_pool_scale, q_norm_g=m_q_norm_g,
                w_uq=m_w_uq, kv_norm_g=m_kv_norm_g, w_ukv=m_w_ukv, w_out=m_w_out, norm_ffn_g=m_norm_ffn_g,
                w_up=m_w_up, conv_w=m_conv_w, conv_b=m_conv_b, w_down=m_w_down, norm_ple_g=m_norm_ple_g,
                w_ple=m_w_ple, w_ple_gate=m_w_ple_gate, final_norm_g=m_final_norm_g)
    v_in = dict(norm_mix_g=v_norm_mix_g, w_in=v_w_in, pool_w=v_pool_w, pool_scale=v_pool_scale, q_norm_g=v_q_norm_g,
                w_uq=v_w_uq, kv_norm_g=v_kv_norm_g, w_ukv=v_w_ukv, w_out=v_w_out, norm_ffn_g=v_norm_ffn_g,
                w_up=v_w_up, conv_w=v_conv_w, conv_b=v_conv_b, w_down=v_w_down, norm_ple_g=v_norm_ple_g,
                w_ple=v_w_ple, w_ple_gate=v_w_ple_gate, final_norm_g=v_final_norm_g)

    depth = w_in.shape[0]
    s, d_model = x.shape[1], x.shape[2]
    d_pool = pool_scale.shape[-1]
    q_lora = q_norm_g.shape[-1]
    d_ff = conv_b.shape[-1]
    heads = (w_uq.shape[-1] * N_DEV) // (NOPE_DIM + ROPE_DIM)
    dims = dict(heads=heads, d_pool=d_pool, q_lora=q_lora, d_ff=d_ff)

    me = _flat_index(lax.axis_index("x"), lax.axis_index("y"), lax.axis_index("c"))

    groups = {"mix": ("w_in", "pool_w", "w_uq", "w_ukv", "w_out"), "up": ("w_up", "conv_w"),
              "down": ("w_down", "w_ple", "w_ple_gate")}

    def group_shards(i, group):
        return [weights[n][i] if n == "conv_w" else weights[n][i].astype(BF16) for n in groups[group]]

    def start_weights(i, group, dep):
        shards = group_shards(i, group)
        return _exchange_start(shards, [_own_slot_filled(a, me) for a in shards], gather=True,
                               name=f"weights_{group}_start_{i}", dep=dep)

    def full_group(group, gathered_g):
        w = {n: g if n == "w_up" else _full_from_gathered(SHARD_KIND[n], g) for n, g in zip(groups[group], gathered_g)}
        if group == "mix":
            w["w_uq"] = _heads_split(w["w_uq"], heads, NOPE_DIM, ROPE_DIM)
            w["w_ukv"] = _heads_split(w["w_ukv"], heads, NOPE_DIM, V_DIM)
        return w

    tabs = _rope_tables(positions[0])

    arrived = {(0, "mix"): _all_gather(group_shards(0, "mix"), name="weights_mix_gather_0")}
    travelling = {}
    token = arrived[(0, "mix")][0]
    for group in ("up", "down"):
        travelling[(0, group)] = start_weights(0, group, token)
        token = travelling[(0, group)][-1]
    layer_w = []
    h = x[0]
    saved = []
    for i in range(depth):
        if i + 1 < depth:
            for group in ("mix", "up", "down"):
                travelling[(i + 1, group)] = start_weights(i + 1, group, token if i == 0 and group == "mix" else
                                                           (h if group == "mix" else token))
                token = travelling[(i + 1, group)][-1]

        def fetch(group, after, i=i):
            if (i, group) not in arrived:
                arrived[(i, group)] = _exchange_wait(travelling[(i, group)], after, gather=True,
                                                     name=f"weights_{group}_wait_{i}")
            return full_group(group, arrived[(i, group)])

        rep = {n: weights[n][i] for n in REPLICATED}
        h, sv, w = _layer_fwd(h, p[i, 0], fetch, rep, tabs, dims, dep=token if i + 1 < depth or i == 0 else None)
        layer_w.append((w, rep))
        saved.append(sv)
    loss_row, dh, g_final = _loss_head(h, final_norm_g, loss_target[0])
    loss = lax.psum(loss_row[0, 0], MESH_AXES)

    def start_grads(group, gr, dep, i):
        names = groups[group]
        blocks = [gr[n] if n == "w_up" else _blocks_from_full(SHARD_KIND[n], gr[n]).astype(BF16) for n in names]
        lands = [_own_slot_filled(lax.dynamic_index_in_dim(b, me, 0, keepdims=False), me) for b in blocks]
        return _exchange_start(blocks, lands, gather=False, name=f"grads_{group}_start_{i}", dep=dep)

    def end_grads(group, started, after, i):
        got = _exchange_wait(started, after, gather=False, name=f"grads_{group}_wait_{i}")
        received[i].update(zip(groups[group], got))

    layer_grads = [None] * depth
    received = [dict() for _ in range(depth)]
    pending = None
    for i in reversed(range(depth)):
        w, rep = layer_w[i]
        state = {}

        def on_down(dh2, gr, i=i, state=state):
            state["down"] = start_grads("down", gr, dh2, i)
            return state["down"][-1]

        def on_up(dw_up, gr, i=i, pending=pending, state=state):
            if pending is not None:
                for group in ("down", "up", "mix"):
                    end_grads(group, pending[group], dw_up, i + 1)
            state["up"] = start_grads("up", gr, dw_up, i)
            return state["up"][-1]

        dh, gr = _layer_bwd(dh, p[i, 0], w, rep, tabs, dims, saved[i],
                            dep=loss.reshape(1, 1) if pending is None else pending["mix"][-1],
                            hooks={"down": on_down, "up": on_up})
        gr["w_uq"] = _heads_join(gr["w_uq"], heads, NOPE_DIM, ROPE_DIM)
        gr["w_ukv"] = _heads_join(gr["w_ukv"], heads, NOPE_DIM, V_DIM)
        layer_grads[i] = gr
        state["mix"] = start_grads("mix", gr, dh, i)
        pending = state
    grad_x = dh[None]

    out = {}

    def update(n):
        shape = weights[n].shape
        recs = [received[i][n].reshape((N_DEV, -1, shape[-1])) for i in range(depth)]
        res = _adamw(recs, _as2d(weights[n]), _as2d(m_in[n]), _as2d(v_in[n]), name="adamw_" + n)
        out[n] = tuple(r.reshape(shape) for r in res)

    end_grads("down", pending["down"], pending["mix"][-1], 0)
    end_grads("up", pending["up"], pending["mix"][-1], 0)
    for n in groups["down"] + groups["up"]:
        update(n)
    end_grads("mix", pending["mix"], out["w_up"][0], 0)
    for n in groups["mix"]:
        update(n)

    small_names = REPLICATED + ("final_norm_g",)

    def pack(get):
        rows = [jnp.stack([get(n, i).reshape(-1) for i in range(depth)]).reshape(-1) for n in REPLICATED]
        rows.append(get("final_norm_g", None).reshape(-1))
        return jnp.concatenate(rows).reshape(1, -1, LANES)

    g_small = pack(lambda n, i: g_final if i is None else layer_grads[i][n])
    w_small = pack(lambda n, i: weights[n] if i is None else weights[n][i])
    m_small = pack(lambda n, i: m_in[n] if i is None else m_in[n][i])
    v_small = pack(lambda n, i: v_in[n] if i is None else v_in[n][i])
    (g_all,) = _all_gather([g_small], name="small_grads_all_gather")
    res_small = _adamw([g_all[:, 0]], w_small, m_small, v_small, name="adamw_small")

    def unpack(flat3):
        flat = flat3.reshape(-1)
        res, off = {}, 0
        for n in REPLICATED:
            width = weights[n].shape[-1]
            res[n] = flat[off:off + depth * width].reshape(depth, width)
            off += depth * width
        res["final_norm_g"] = flat[off:off + d_model]
        return res

    small = [unpack(r) for r in res_small]
    for n in small_names:
        out[n] = tuple(small[k][n] for k in range(4))

    outs = [loss, grad_x]
    for k in range(4):
        outs += [out[n][k] for n in WEIGHT_ORDER]
    return tuple(outs)
```

```python
import functools
import math

import jax
import jax.numpy as jnp
from jax import lax
from jax.experimental import pallas as pl
from jax.experimental.pallas import tpu as pltpu

F32 = jnp.float32
BF16 = jnp.bfloat16

N_DEV = 8
MESH_AXES = ("x", "y", "c")
NOPE_DIM = 128
ROPE_DIM = 64
V_DIM = 128
POOL_GROUPS = 4
CONV_TAPS = 3
ROPE_THETA = 10000.0
NORM_EPS = 1e-6
ADAM_LR = 0.001
ADAM_B1 = 0.9
ADAM_B2 = 0.999
ADAM_EPS = 1e-08
ADAM_WD = 0.01
ADAM_STEP = 10
LANES = 128
VMEM_LIMIT_BYTES = 56 * 1024 * 1024

SHARDED = ("w_in", "pool_w", "w_uq", "w_ukv", "w_out", "w_up", "conv_w", "w_down", "w_ple", "w_ple_gate")
SHARD_KIND = {"w_in": "col", "pool_w": "pool", "w_uq": "col", "w_ukv": "col", "w_out": "row", "w_up": "col",
              "conv_w": "col", "w_down": "row", "w_ple": "col", "w_ple_gate": "row"}
REPLICATED = ("norm_mix_g", "pool_scale", "q_norm_g", "kv_norm_g", "norm_ffn_g", "conv_b", "norm_ple_g")
WEIGHT_ORDER = ("norm_mix_g", "w_in", "pool_w", "pool_scale", "q_norm_g", "w_uq", "kv_norm_g", "w_ukv", "w_out",
                "norm_ffn_g", "w_up", "conv_w", "conv_b", "w_down", "norm_ple_g", "w_ple", "w_ple_gate",
                "final_norm_g")

_pcall = pl.pallas_call


def _params(*sem):
    return pltpu.CompilerParams(dimension_semantics=sem or None, vmem_limit_bytes=VMEM_LIMIT_BYTES)


def _tile(n, pref, mult=LANES):
    if n <= pref:
        return n
    t = (pref // mult) * mult
    while t >= mult:
        if n % t == 0:
            return t
        t -= mult
    return n


MM_VMEM_BUDGET_BYTES = 40 * 1024 * 1024
MM_TILE_PREFS = (1024, 512, 256)
MM_MIN_TK = 1024


MM_MIN_WHOLE_K_TILE = 512 * 512


def _mm_tiles(m, n, k, a_bytes, b_bytes, o_bytes, has_add, tn_fixed=None, tk_fixed=None):
    def need(tm, tn, tk):
        nbytes = 2 * (tm * tk * a_bytes + tk * tn * b_bytes) + 2 * tm * tn * (o_bytes + (4 if has_add else 0))
        return nbytes + (tm * tn * 4 if tk < k else 0)

    tn_cands = [tn_fixed] if tn_fixed else sorted({_tile(n, pref) for pref in MM_TILE_PREFS[:2]}, reverse=True)
    if not tk_fixed or tk_fixed == k:
        whole = [(tm * tn, tm, tn) for tm in sorted({_tile(m, pref) for pref in MM_TILE_PREFS}, reverse=True)
                 for tn in tn_cands if need(tm, tn, k) <= MM_VMEM_BUDGET_BYTES]
        if whole and max(whole)[0] >= MM_MIN_WHOLE_K_TILE:
            _, tm, tn = max(whole)
            return tm, tn, k
    tn = tn_cands[0]
    if tk_fixed:
        k_cands = [tk_fixed]
    else:
        k_cands = [k] + [t for t in range((k - 1) // LANES * LANES, 0, -LANES) if k % t == 0]
    best = None
    for pref in MM_TILE_PREFS:
        tm = _tile(m, pref)
        for tk in k_cands:
            if need(tm, tn, tk) <= MM_VMEM_BUDGET_BYTES:
                if tk >= min(k, MM_MIN_TK):
                    return tm, tn, tk
                if best is None or tk > best[2]:
                    best = (tm, tn, tk)
                break
    assert best is not None, (m, n, k)
    return best


def _mm(a, b, *, name, ta=False, tb=False, add=None, out_dtype=F32, b_row=(0, None), b_blocks=None,
        out_blocks=None, out_init=None, dep=None):
    m = a.shape[1] if ta else a.shape[0]
    kdim = a.shape[0] if ta else a.shape[1]
    tn_fixed = tk_fixed = None
    k_off_b = 0
    if b_blocks is None:
        n_b, k_b = (b.shape if tb else b.shape[::-1])
        k_off_b, kb_sz = b_row
        kb_sz = k_b - k_off_b if kb_sz is None else kb_sz
        assert kb_sz == kdim, (name, kb_sz, kdim)
        n = n_b
    else:
        first_b, count_b = b_blocks
        per_b = b.shape[2]
        if tb:
            n = b.shape[1]
            assert kdim == count_b * per_b, name
            tk_fixed = per_b
        else:
            n = count_b * per_b
            assert kdim == b.shape[1], name
            tn_fixed = per_b
    if out_blocks is not None:
        nb_out, first_o, tn_fixed = out_blocks
    tm, tn, tk = _mm_tiles(m, n, kdim, a.dtype.itemsize, b.dtype.itemsize, jnp.dtype(out_dtype).itemsize,
                           add is not None, tn_fixed, tk_fixed)
    assert k_off_b % tk == 0 and kdim % tk == 0 and n % tn == 0 and m % tm == 0, name
    kb0 = k_off_b // tk
    nk = kdim // tk
    dims = (((0 if ta else 1,), (1 if tb else 0,)), ((), ()))

    def body(*refs):
        a_ref, b_ref = refs[0], refs[1]
        add_ref = refs[2] if add is not None else None
        o_ref = refs[n_in]
        part = lax.dot_general(a_ref[...].astype(BF16), b_ref[...].astype(BF16), dims, preferred_element_type=F32)

        def finish(r):
            if add_ref is not None:
                r = r + add_ref[...].astype(F32)
            o_ref[...] = r.astype(out_dtype)

        if nk == 1:
            finish(part)
            return
        acc = refs[n_in + 1]
        k = pl.program_id(2)

        @pl.when(k == 0)
        def _():
            acc[...] = part

        @pl.when(jnp.logical_and(k > 0, k < nk - 1))
        def _():
            acc[...] += part

        @pl.when(k == nk - 1)
        def _():
            finish(acc[...] + part)

    if ta:
        a_spec = pl.BlockSpec((tk, tm), lambda i, j, k: (k, i))
    else:
        a_spec = pl.BlockSpec((tm, tk), lambda i, j, k: (i, k))
    if b_blocks is not None and tb:
        b_spec = pl.BlockSpec((None, tn, tk), lambda i, j, k: (k + first_b, j, 0))
    elif b_blocks is not None:
        b_spec = pl.BlockSpec((None, tk, tn), lambda i, j, k: (j + first_b, k, 0))
    elif tb:
        b_spec = pl.BlockSpec((tn, tk), lambda i, j, k: (j, k + kb0))
    else:
        b_spec = pl.BlockSpec((tk, tn), lambda i, j, k: (k + kb0, j))
    in_specs = [a_spec, b_spec]
    args = [a, b]
    if add is not None:
        in_specs.append(pl.BlockSpec((tm, tn), lambda i, j, k: (i, j)))
        args.append(add)
    aliases = {}
    if out_init is not None:
        aliases = {len(args): 0}
        in_specs.append(HBM_SPEC)
        args.append(out_init)
    if dep is not None:
        in_specs.append(HBM_SPEC)
        args.append(dep)
    n_in = len(args)
    if out_blocks is None:
        out_spec = pl.BlockSpec((tm, tn), lambda i, j, k: (i, j))
        out_shape = jax.ShapeDtypeStruct((m, n), out_dtype)
    else:
        out_spec = pl.BlockSpec((None, tm, tn), lambda i, j, k: (j + first_o, i, 0))
        out_shape = jax.ShapeDtypeStruct((nb_out, m, tn), out_dtype)
    return _pcall(
        body, name=name, grid=(m // tm, n // tn, nk), in_specs=in_specs, out_specs=out_spec, out_shape=out_shape,
        scratch_shapes=[pltpu.VMEM((tm, tn), F32)] if nk > 1 else [], input_output_aliases=aliases,
        compiler_params=_params("parallel", "parallel", "arbitrary"),
    )(*args)


def _rms_fwd(h, g, *, name, col_block=0, dep=None):
    s = h.shape[0]
    d = g.shape[-1]
    ts = _tile(s, 512, 8)

    def body(h_ref, g_ref, *rest):
        n_ref = rest[-1]
        x = h_ref[...]
        r = lax.rsqrt(jnp.mean(x * x, axis=-1, keepdims=True) + NORM_EPS)
        n_ref[...] = (x * r * g_ref[...]).astype(BF16)

    deps = [] if dep is None else [dep]
    return _pcall(
        body, name=name, grid=(s // ts,),
        in_specs=[pl.BlockSpec((ts, d), lambda i: (i, col_block)), pl.BlockSpec((1, d), lambda i: (0, 0))]
        + [HBM_SPEC] * len(deps),
        out_specs=pl.BlockSpec((ts, d), lambda i: (i, 0)),
        out_shape=jax.ShapeDtypeStruct((s, d), BF16),
        compiler_params=_params("parallel"),
    )(h, g.reshape(1, d), *deps)


def _rms_bwd(dn, h, g, *, name, res=None, col_block=0, out_dtype=F32, matmul_copy=False):
    s = dn.shape[0]
    d = g.shape[-1]
    ts = _tile(s, 512, 8)

    def body(*refs):
        dn_ref, h_ref, g_ref = refs[:3]
        res_ref = refs[3] if res is not None else None
        dh_ref, dg_ref = refs[n_in], refs[-1]
        i = pl.program_id(0)
        x = h_ref[...]
        r = lax.rsqrt(jnp.mean(x * x, axis=-1, keepdims=True) + NORM_EPS)
        nh = x * r
        dnv = dn_ref[...]
        gd = dnv * g_ref[...]
        dh = (gd - nh * jnp.mean(gd * nh, axis=-1, keepdims=True)) * r
        if res_ref is not None:
            dh = dh + res_ref[...]
        dh_ref[...] = dh.astype(out_dtype)
        if matmul_copy:
            refs[n_in + 1][...] = dh.astype(BF16)
        part = jnp.sum(dnv * nh, axis=0, keepdims=True)

        @pl.when(i == 0)
        def _():
            dg_ref[...] = part

        @pl.when(i > 0)
        def _():
            dg_ref[...] += part

    row = pl.BlockSpec((ts, d), lambda i: (i, 0))
    in_specs = [row, pl.BlockSpec((ts, d), lambda i: (i, col_block)), pl.BlockSpec((1, d), lambda i: (0, 0))]
    args = [dn, h, g.reshape(1, d)]
    if res is not None:
        in_specs.append(row)
        args.append(res)
    n_in = len(args)
    copies = [row] if matmul_copy else []
    return _pcall(
        body, name=name, grid=(s // ts,), in_specs=in_specs,
        out_specs=(row, *copies, pl.BlockSpec((1, d), lambda i: (0, 0))),
        out_shape=(jax.ShapeDtypeStruct((s, d), out_dtype), *[jax.ShapeDtypeStruct((s, d), BF16) for _ in copies],
                   jax.ShapeDtypeStruct((1, d), F32)),
        compiler_params=_params("arbitrary"),
    )(*args)


ROW_CHUNK = 512


def _rows_with_halo(ref, r, t_rows, n_chunks, before, after):
    r0 = r * t_rows
    parts = []
    if before:
        hb = ref[pl.ds(pl.multiple_of(jnp.maximum(r0 - before, 0), before), before), :]
        parts.append(jnp.where(r > 0, hb, jnp.zeros_like(hb)))
    parts.append(ref[pl.ds(pl.multiple_of(r0, t_rows), t_rows), :])
    if after:
        ha = ref[pl.ds(pl.multiple_of(jnp.minimum(r0 + t_rows, n_chunks * t_rows - after), after), after), :]
        parts.append(jnp.where(r < n_chunks - 1, ha, jnp.zeros_like(ha)))
    return jnp.concatenate(parts, axis=0)


POOL_HALO = 16


def _pool_fwd(u, pool_w, pool_scale, *, c):
    s = u.shape[0]
    g_n = POOL_GROUPS
    tr = _tile(s, ROW_CHUNK, POOL_HALO)
    n_chunks = s // tr

    def body(u_ref, pw_ref, sc_ref, y_ref, d_ref):
        r = pl.program_id(1)
        w = jnp.left_shift(2, pl.program_id(0))
        xe = _rows_with_halo(u_ref, r, tr, n_chunks, POOL_HALO, 0)
        acc = xe
        for k in (1, 2, 4, 8):
            acc = jnp.where(k < w, acc + pltpu.roll(acc, k, 0), acc)
        t = r * tr + lax.broadcasted_iota(jnp.int32, (tr, 1), 0)
        cnt = jnp.minimum(t + 1, w).astype(F32)
        diff = (acc[POOL_HALO:] / cnt - xe[POOL_HALO:]).astype(BF16)
        d_ref[...] = diff
        y = jnp.dot(diff, pw_ref[...], preferred_element_type=F32) * sc_ref[...]
        y_ref[...] = y.astype(BF16)

    out = pl.BlockSpec((tr, c), lambda g, r: (r, g))
    return _pcall(
        body, name="pool_fwd", grid=(g_n, n_chunks),
        in_specs=[pl.BlockSpec((s, c), lambda g, r: (0, g)), pl.BlockSpec((None, c, c), lambda g, r: (g, 0, 0)),
                  pl.BlockSpec((1, c), lambda g, r: (0, g))],
        out_specs=(out, out),
        out_shape=(jax.ShapeDtypeStruct((s, g_n * c), BF16), jax.ShapeDtypeStruct((s, g_n * c), BF16)),
        compiler_params=_params("parallel", "arbitrary"),
    )(u, pool_w, pool_scale.reshape(1, g_n * c))


def _pool_bwd(dcat, diff, pool_w, pool_scale, *, c):
    s = dcat.shape[0]
    g_n = POOL_GROUPS
    tr = _tile(s, ROW_CHUNK, POOL_HALO)
    n_chunks = s // tr

    def body(dy_ref, d_ref, pw_ref, sc_ref, du_ref, dpw_ref, dsc_ref):
        r = pl.program_id(1)
        w = jnp.left_shift(2, pl.program_id(0))
        dye = _rows_with_halo(dy_ref, r, tr, n_chunks, 0, POOL_HALO)
        diff = d_ref[pl.ds(pl.multiple_of(r * tr, tr), tr), :]
        pw = pw_ref[...]
        yp = jnp.dot(diff, pw, preferred_element_type=F32)
        dsc = jnp.sum(dye[:tr] * yp, axis=0, keepdims=True)
        dyp = (dye * sc_ref[...]).astype(BF16)
        ddiff = lax.dot_general(dyp, pw, (((1,), (1,)), ((), ())), preferred_element_type=F32)
        dpw = lax.dot_general(diff, dyp[:tr], (((0,), (0,)), ((), ())), preferred_element_type=F32)
        t = r * tr + lax.broadcasted_iota(jnp.int32, (tr + POOL_HALO, 1), 0)
        cnt = jnp.minimum(t + 1, w).astype(F32)
        acc = ddiff / cnt
        rows = tr + POOL_HALO
        for k in (1, 2, 4, 8):
            acc = jnp.where(k < w, acc + pltpu.roll(acc, rows - k, 0), acc)
        du_ref[...] = (acc[:tr] - ddiff[:tr]).astype(BF16)

        @pl.when(r == 0)
        def _():
            dpw_ref[...] = dpw
            dsc_ref[...] = dsc

        @pl.when(r > 0)
        def _():
            dpw_ref[...] += dpw
            dsc_ref[...] += dsc

    col = lambda g, r: (0, g)
    wspec = pl.BlockSpec((None, c, c), lambda g, r: (g, 0, 0))
    vec = pl.BlockSpec((1, c), col)
    return _pcall(
        body, name="pool_bwd", grid=(g_n, n_chunks),
        in_specs=[pl.BlockSpec((s, c), col), pl.BlockSpec((s, c), col), wspec, vec],
        out_specs=(pl.BlockSpec((tr, c), lambda g, r: (r, g)), wspec, vec),
        out_shape=(jax.ShapeDtypeStruct((s, g_n * c), BF16), jax.ShapeDtypeStruct((g_n, c, c), F32),
                   jax.ShapeDtypeStruct((1, g_n * c), F32)),
        compiler_params=_params("parallel", "arbitrary"),
    )(dcat, diff, pool_w, pool_scale.reshape(1, g_n * c))


def _swap_halves(x, lane):
    return jnp.where((lane % ROPE_DIM) < ROPE_DIM // 2, pltpu.roll(x, LANES - ROPE_DIM // 2, 1),
                     pltpu.roll(x, ROPE_DIM // 2, 1))


def _qkv_prep(q, kv, kr2, cos_t, sin_t, *, heads):
    s = q.shape[0]
    ts = _tile(s, 256, 8)

    def body(q_ref, kv_ref, kr_ref, cos_ref, sin_ref, qo_ref, ko_ref, vo_ref):
        lane = lax.broadcasted_iota(jnp.int32, (ts, LANES), 1)
        cos_v = cos_ref[...]
        sin_v = sin_ref[...]

        def rope(x):
            return x * cos_v + _swap_halves(x, lane) * sin_v

        kr = rope(kr_ref[...]).astype(BF16)
        for pair in range(heads // 2):
            qr = rope(q_ref[:, (heads + pair) * LANES:(heads + pair + 1) * LANES])
            for half in range(2):
                h = 2 * pair + half
                qo_ref[h, :, :LANES] = q_ref[:, h * LANES:(h + 1) * LANES].astype(BF16)
                qo_ref[h, :, LANES:] = jnp.where(lane // ROPE_DIM == half, qr, 0.0).astype(BF16)
        for h in range(heads):
            ko_ref[h, :, :LANES] = kv_ref[:, h * LANES:(h + 1) * LANES].astype(BF16)
            ko_ref[h, :, LANES:] = kr
            vo_ref[h] = kv_ref[:, (heads + h) * LANES:(heads + h + 1) * LANES].astype(BF16)

    tab = pl.BlockSpec((ts, LANES), lambda i: (i, 0))
    return _pcall(
        body, name="qkv_prep", grid=(s // ts,),
        in_specs=[pl.BlockSpec((ts, q.shape[1]), lambda i: (i, 0)), pl.BlockSpec((ts, kv.shape[1]), lambda i: (i, 0)),
                  tab, tab, tab],
        out_specs=(pl.BlockSpec((heads, ts, 2 * LANES), lambda i: (0, i, 0)),
                   pl.BlockSpec((heads, ts, 2 * LANES), lambda i: (0, i, 0)),
                   pl.BlockSpec((heads, ts, LANES), lambda i: (0, i, 0))),
        out_shape=(jax.ShapeDtypeStruct((heads, s, 2 * LANES), BF16),
                   jax.ShapeDtypeStruct((heads, s, 2 * LANES), BF16),
                   jax.ShapeDtypeStruct((heads, s, LANES), BF16)),
        compiler_params=_params("parallel"),
    )(q, kv, kr2, cos_t, sin_t)


LOG2_E = 1.4426950408889634


FLASH_FWD_TK = 512
FLASH_FWD_SPLITS = 2


def _flash_fwd(qp, kp, v, *, scale, tq=512):
    heads, s, dk = qp.shape
    tq = _tile(s, tq, 16)
    tk = _tile(tq, FLASH_FWD_TK, 16)
    splits = FLASH_FWD_SPLITS
    th = tq // splits
    band = tq // tk
    c = scale * LOG2_E

    def body(q_ref, k_ref, v_ref, o_ref, lse_ref):
        i = pl.program_id(1)
        qs = [q_ref[hh * th:(hh + 1) * th, :] for hh in range(splits)]

        def skipped(hh, col0):
            return col0 is not None and col0 >= (hh + 1) * th

        def scores(start, col0):
            kb = k_ref[pl.ds(pl.multiple_of(start, tk), tk), :]
            return tuple(None if skipped(hh, col0) else
                         lax.dot_general(qs[hh], kb, (((1,), (1,)), ((), ())), preferred_element_type=F32)
                         for hh in range(splits))

        def absorb(start, scs, state, col0):
            vb = v_ref[pl.ds(pl.multiple_of(start, tk), tk), :]
            new = []
            for hh in range(splits):
                if scs[hh] is None:
                    new.append(state[hh])
                    continue
                m_old, l_old, acc = state[hh]
                sc = scs[hh]
                if col0 is not None and col0 + tk - 1 > hh * th:
                    rows = hh * th + lax.broadcasted_iota(jnp.int32, (th, tk), 0)
                    cols = col0 + lax.broadcasted_iota(jnp.int32, (th, tk), 1)
                    sc = jnp.where(rows >= cols, sc, -jnp.inf)
                m_new = jnp.maximum(m_old, jnp.max(sc, axis=-1, keepdims=True))
                alpha = jnp.exp2((m_old - m_new) * c)
                p = jnp.exp2((sc - m_new) * c)
                l_new = alpha * l_old + jnp.sum(p, axis=-1, keepdims=True)
                acc = alpha * acc + jnp.dot(p.astype(BF16), vb, preferred_element_type=F32)
                new.append((m_new, l_new, acc))
            return tuple(new)

        def step(j, carry):
            state, scs = carry
            nxt = scores((j + 1) * tk, None)
            return absorb(j * tk, scs, state, None), nxt

        init = tuple((jnp.full((th, 1), -jnp.inf, F32), jnp.zeros((th, 1), F32), jnp.zeros((th, V_DIM), F32))
                     for _ in range(splits))
        state, scs = lax.fori_loop(0, i * band, step, (init, scores(0, None)))
        for b in range(band):
            nxt = scores(i * tq + (b + 1) * tk, (b + 1) * tk) if b + 1 < band else None
            state = absorb(i * tq + b * tk, scs, state, b * tk)
            scs = nxt
        carry = state
        for hh in range(splits):
            m_fin, l_fin, acc = carry[hh]
            o_ref[hh * th:(hh + 1) * th, :] = acc / l_fin
            lse_ref[hh * th:(hh + 1) * th, :] = m_fin * scale + jnp.log(l_fin)

    return _pcall(
        body, name="flash_fwd", grid=(heads, s // tq),
        in_specs=[pl.BlockSpec((None, tq, dk), lambda h, i: (h, i, 0)),
                  pl.BlockSpec((None, s, dk), lambda h, i: (h, 0, 0)),
                  pl.BlockSpec((None, s, V_DIM), lambda h, i: (h, 0, 0))],
        out_specs=(pl.BlockSpec((tq, V_DIM), lambda h, i: (i, h)),
                   pl.BlockSpec((None, tq, 1), lambda h, i: (h, i, 0))),
        out_shape=(jax.ShapeDtypeStruct((s, heads * V_DIM), F32), jax.ShapeDtypeStruct((heads, s, 1), F32)),
        compiler_params=_params("parallel", "arbitrary"),
    )(qp, kp, v)


def _flash_bwd(qp, kp, v, dcat, o, lse, *, scale, do_col0, tq=512):
    heads, s, dk = qp.shape
    tq = _tile(s, tq, 16)
    tk = tq
    nq = s // tq

    def body(k_ref, v_ref, q_ref, do_ref, o_ref, lse_ref, dq_ref, dk_ref, dv_ref):
        j = pl.program_id(1)

        @pl.when(j == 0)
        def _():
            dq_ref[...] = jnp.zeros_like(dq_ref)

        kb = k_ref[...]
        vb = v_ref[...]

        def block(i, carry, diag):
            dk_acc, dv_acc = carry
            rows_at = pl.ds(pl.multiple_of(i * tq, tq), tq)
            qb = q_ref[rows_at, :]
            do = do_ref[rows_at, :]
            sc = lax.dot_general(qb, kb, (((1,), (1,)), ((), ())), preferred_element_type=F32) * scale
            if diag:
                rows = lax.broadcasted_iota(jnp.int32, (tq, tk), 0)
                cols = lax.broadcasted_iota(jnp.int32, (tq, tk), 1)
                sc = jnp.where(rows >= cols, sc, -jnp.inf)
            p = jnp.exp(sc - lse_ref[rows_at, :])
            dob = do.astype(BF16)
            dv_acc = dv_acc + lax.dot_general(p.astype(BF16), dob, (((0,), (0,)), ((), ())),
                                              preferred_element_type=F32)
            dp = lax.dot_general(dob, vb, (((1,), (1,)), ((), ())), preferred_element_type=F32)
            delta = jnp.sum(do * o_ref[rows_at, :], axis=-1, keepdims=True)
            ds = (p * (dp - delta) * scale).astype(BF16)
            dk_acc = dk_acc + lax.dot_general(ds, qb, (((0,), (0,)), ((), ())), preferred_element_type=F32)
            dq_ref[rows_at, :] += jnp.dot(ds, kb, preferred_element_type=F32)
            return dk_acc, dv_acc

        carry = block(j, (jnp.zeros((tk, dk), F32), jnp.zeros((tk, V_DIM), F32)), True)
        carry = lax.fori_loop(j + 1, nq, lambda i, cr: block(i, cr, False), carry)
        dk_ref[...] = carry[0]
        dv_ref[...] = carry[1]

    whole = lambda h, j: (h, 0, 0)
    return _pcall(
        body, name="flash_bwd", grid=(heads, nq),
        in_specs=[pl.BlockSpec((None, tk, dk), lambda h, j: (h, j, 0)),
                  pl.BlockSpec((None, tk, V_DIM), lambda h, j: (h, j, 0)),
                  pl.BlockSpec((None, s, dk), whole),
                  pl.BlockSpec((s, V_DIM), lambda h, j: (0, do_col0 + h)),
                  pl.BlockSpec((s, V_DIM), lambda h, j: (0, h)),
                  pl.BlockSpec((None, s, 1), whole)],
        out_specs=(pl.BlockSpec((None, s, dk), whole),
                   pl.BlockSpec((None, tk, dk), lambda h, j: (h, j, 0)),
                   pl.BlockSpec((None, tk, V_DIM), lambda h, j: (h, j, 0))),
        out_shape=(jax.ShapeDtypeStruct((heads, s, dk), F32), jax.ShapeDtypeStruct((heads, s, dk), F32),
                   jax.ShapeDtypeStruct((heads, s, V_DIM), F32)),
        compiler_params=_params("parallel", "arbitrary"),
    )(kp, v, qp, dcat, o, lse)


def _attn_bwd_post(dqp, dkp, dv, cos_t, sin_t):
    heads, s, _ = dqp.shape
    ts = _tile(s, 256, 8)

    def body(dq_ref, dk_ref, dv_ref, cos_ref, sin_ref, q_out, kv_out, kr_out):
        lane = lax.broadcasted_iota(jnp.int32, (ts, LANES), 1)
        cos_v = cos_ref[...]
        sin_v = sin_ref[...]

        def rope_t(dy):
            return dy * cos_v + _swap_halves(dy * sin_v, lane)

        kr_sum = jnp.zeros((ts, LANES), F32)
        for h in range(heads):
            q_out[:, h * LANES:(h + 1) * LANES] = dq_ref[h, :, :LANES].astype(BF16)
            kv_out[:, h * LANES:(h + 1) * LANES] = dk_ref[h, :, :LANES].astype(BF16)
            kv_out[:, (heads + h) * LANES:(heads + h + 1) * LANES] = dv_ref[h].astype(BF16)
            kr_sum = kr_sum + dk_ref[h, :, LANES:]
        for pair in range(heads // 2):
            r = jnp.where(lane < ROPE_DIM, dq_ref[2 * pair, :, LANES:], dq_ref[2 * pair + 1, :, LANES:])
            q_out[:, (heads + pair) * LANES:(heads + pair + 1) * LANES] = rope_t(r).astype(BF16)
        kr = rope_t(kr_sum)
        kr_out[...] = (kr + pltpu.roll(kr, ROPE_DIM, 1)).astype(BF16)

    wq = heads * (NOPE_DIM + ROPE_DIM)
    wkv = heads * (NOPE_DIM + V_DIM)
    tab = pl.BlockSpec((ts, LANES), lambda i: (i, 0))
    return _pcall(
        body, name="attn_bwd_post", grid=(s // ts,),
        in_specs=[pl.BlockSpec((heads, ts, 2 * LANES), lambda i: (0, i, 0)),
                  pl.BlockSpec((heads, ts, 2 * LANES), lambda i: (0, i, 0)),
                  pl.BlockSpec((heads, ts, LANES), lambda i: (0, i, 0)), tab, tab],
        out_specs=(pl.BlockSpec((ts, wq), lambda i: (i, 0)), pl.BlockSpec((ts, wkv), lambda i: (i, 0)), tab),
        out_shape=(jax.ShapeDtypeStruct((s, wq), BF16), jax.ShapeDtypeStruct((s, wkv), BF16),
                   jax.ShapeDtypeStruct((s, LANES), BF16)),
        compiler_params=_params("parallel"),
    )(dqp, dkp, dv, cos_t, sin_t)


CONV_HALO = 16


def _conv_gate(xe, cw_ref, cb_ref):
    x1 = pltpu.roll(xe, 1, 0)
    x2 = pltpu.roll(xe, 2, 0)
    return cw_ref[2:3, :] * xe + cw_ref[1:2, :] * x1 + cw_ref[0:1, :] * x2 + cb_ref[...], x1, x2


def _ffn_act_fwd(gu, conv_w, conv_b, *, d_ff, tc=256):
    s = gu.shape[0]
    tc = _tile(d_ff, tc)
    nf = d_ff // tc
    tr = _tile(s, ROW_CHUNK, CONV_HALO)
    n_chunks = s // tr

    def body(g_ref, u_ref, cw_ref, cb_ref, a_ref):
        r = pl.program_id(1)
        xe = _rows_with_halo(g_ref, r, tr, n_chunks, CONV_HALO, 0).astype(F32)
        gc = _conv_gate(xe, cw_ref, cb_ref)[0][CONV_HALO:]
        a_ref[...] = (gc * jax.nn.sigmoid(gc) * u_ref[...].astype(F32)).astype(BF16)

    return _pcall(
        body, name="ffn_act_fwd", grid=(nf, n_chunks),
        in_specs=[pl.BlockSpec((s, tc), lambda j, r: (0, j)), pl.BlockSpec((tr, tc), lambda j, r: (r, nf + j)),
                  pl.BlockSpec((CONV_TAPS, tc), lambda j, r: (0, j)), pl.BlockSpec((1, tc), lambda j, r: (0, j))],
        out_specs=pl.BlockSpec((tr, tc), lambda j, r: (r, j)),
        out_shape=jax.ShapeDtypeStruct((s, d_ff), BF16),
        compiler_params=_params("parallel", "arbitrary"),
    )(gu, gu, conv_w, conv_b.reshape(1, d_ff))


def _ffn_act_bwd(da, gu, conv_w, conv_b, *, d_ff, tc=256):
    s = gu.shape[0]
    tc = _tile(d_ff, tc)
    nf = d_ff // tc
    tr = _tile(s, ROW_CHUNK, CONV_HALO)
    n_chunks = s // tr
    rows = tr + 2 * CONV_HALO
    main = slice(CONV_HALO, CONV_HALO + tr)

    def body(da_ref, g_ref, u_ref, cw_ref, cb_ref, dg_ref, du_ref, dcw_ref, dcb_ref):
        r = pl.program_id(1)
        xe = _rows_with_halo(g_ref, r, tr, n_chunks, CONV_HALO, CONV_HALO).astype(F32)
        dae = _rows_with_halo(da_ref, r, tr, n_chunks, CONV_HALO, CONV_HALO).astype(F32)
        ue = _rows_with_halo(u_ref, r, tr, n_chunks, CONV_HALO, CONV_HALO).astype(F32)
        gc, x1, x2 = _conv_gate(xe, cw_ref, cb_ref)
        sg = jax.nn.sigmoid(gc)
        du_ref[...] = (dae * gc * sg)[main].astype(BF16)
        dgc = dae * ue * sg * (1.0 + gc * (1.0 - sg))
        dg = (cw_ref[2:3, :] * dgc + cw_ref[1:2, :] * pltpu.roll(dgc, rows - 1, 0)
              + cw_ref[0:1, :] * pltpu.roll(dgc, rows - 2, 0))
        dg_ref[...] = dg[main].astype(BF16)
        dgc_m = dgc[main]
        dcb = jnp.sum(dgc_m, axis=0, keepdims=True)
        dcw = jnp.concatenate([jnp.sum(dgc_m * x2[main], axis=0, keepdims=True),
                               jnp.sum(dgc_m * x1[main], axis=0, keepdims=True),
                               jnp.sum(dgc_m * xe[main], axis=0, keepdims=True)], axis=0)

        @pl.when(r == 0)
        def _():
            dcb_ref[...] = dcb
            dcw_ref[...] = dcw

        @pl.when(r > 0)
        def _():
            dcb_ref[...] += dcb
            dcw_ref[...] += dcw

    col = pl.BlockSpec((s, tc), lambda j, r: (0, j))
    out = pl.BlockSpec((tr, tc), lambda j, r: (r, j))
    return _pcall(
        body, name="ffn_act_bwd", grid=(nf, n_chunks),
        in_specs=[col, col, pl.BlockSpec((s, tc), lambda j, r: (0, nf + j)),
                  pl.BlockSpec((CONV_TAPS, tc), lambda j, r: (0, j)), pl.BlockSpec((1, tc), lambda j, r: (0, j))],
        out_specs=(out, out, pl.BlockSpec((CONV_TAPS, tc), lambda j, r: (0, j)),
                   pl.BlockSpec((1, tc), lambda j, r: (0, j))),
        out_shape=(jax.ShapeDtypeStruct((s, d_ff), BF16), jax.ShapeDtypeStruct((s, d_ff), BF16),
                   jax.ShapeDtypeStruct((CONV_TAPS, d_ff), F32), jax.ShapeDtypeStruct((1, d_ff), F32)),
        compiler_params=_params("parallel", "arbitrary"),
    )(da, gu, gu, conv_w, conv_b.reshape(1, d_ff))


def _ple_fwd(h, z, e):
    s, d = h.shape
    ts = _tile(s, 512, 8)

    def body(h_ref, z_ref, e_ref, o_ref):
        o_ref[...] = h_ref[...] + e_ref[...] * jax.nn.sigmoid(z_ref[...])

    row = pl.BlockSpec((ts, d), lambda i: (i, 0))
    return _pcall(body, name="ple_fwd", grid=(s // ts,), in_specs=[row, row, row], out_specs=row,
                  out_shape=jax.ShapeDtypeStruct((s, d), F32), compiler_params=_params("parallel"))(h, z, e)


def _ple_bwd(dh, z, e, dep=None):
    s, d = dh.shape
    ts = _tile(s, 512, 8)

    def body(dh_ref, z_ref, e_ref, *rest):
        de_ref, dz_ref = rest[-2:]
        gt = jax.nn.sigmoid(z_ref[...])
        dhv = dh_ref[...]
        de_ref[...] = (dhv * gt).astype(BF16)
        dz_ref[...] = (dhv * e_ref[...] * gt * (1.0 - gt)).astype(BF16)

    row = pl.BlockSpec((ts, d), lambda i: (i, 0))
    deps = [] if dep is None else [dep]
    return _pcall(body, name="ple_bwd", grid=(s // ts,), in_specs=[row, row, row] + [HBM_SPEC] * len(deps),
                  out_specs=(row, row),
                  out_shape=(jax.ShapeDtypeStruct((s, d), BF16), jax.ShapeDtypeStruct((s, d), BF16)),
                  compiler_params=_params("parallel"))(dh, z, e, *deps)


def _loss_head(h, g, target):
    s, d = h.shape
    ts = _tile(s, 512, 8)

    def body(h_ref, g_ref, t_ref, loss_ref, dh_ref, dg_ref):
        i = pl.program_id(0)
        x = h_ref[...]
        gv = g_ref[...]
        r = lax.rsqrt(jnp.mean(x * x, axis=-1, keepdims=True) + NORM_EPS)
        nh = x * r
        err = nh * gv - t_ref[...]
        part_loss = 0.5 * jnp.sum(jnp.mean(err * err, axis=-1, keepdims=True), axis=0, keepdims=True)
        dy = err * (1.0 / d)
        gd = dy * gv
        dh_ref[...] = (gd - nh * jnp.mean(gd * nh, axis=-1, keepdims=True)) * r
        part_g = jnp.sum(dy * nh, axis=0, keepdims=True)
        part_l = jnp.broadcast_to(part_loss, (1, LANES))

        @pl.when(i == 0)
        def _():
            dg_ref[...] = part_g
            loss_ref[...] = part_l

        @pl.when(i > 0)
        def _():
            dg_ref[...] += part_g
            loss_ref[...] += part_l

    row = pl.BlockSpec((ts, d), lambda i: (i, 0))
    vec = pl.BlockSpec((1, d), lambda i: (0, 0))
    return _pcall(
        body, name="loss_head", grid=(s // ts,), in_specs=[row, vec, row],
        out_specs=(pl.BlockSpec((1, LANES), lambda i: (0, 0)), row, vec),
        out_shape=(jax.ShapeDtypeStruct((1, LANES), F32), jax.ShapeDtypeStruct((s, d), F32),
                   jax.ShapeDtypeStruct((1, d), F32)),
        compiler_params=_params("arbitrary"),
    )(h, g.reshape(1, d), target)


HBM_SPEC = pl.BlockSpec(memory_space=pl.ANY)


def _flat_index(px, py, pc):
    return 4 * px + 2 * py + pc


def _all_gather(shards, *, name):
    n = len(shards)

    def body(*refs):
        ins, outs = refs[:n], refs[n:2 * n]
        send_sems, recv_sems, local_sems = refs[2 * n:]
        x, y, c = lax.axis_index("x"), lax.axis_index("y"), lax.axis_index("c")
        me, sibling = (x, y, c), (x, y, 1 - c)
        chips = [(1 - x, y), (x, 1 - y), (1 - x, 1 - y)]

        def copy(a, k, block, to, src=None):
            slot = outs[a].at[_flat_index(*block)]
            return pltpu.make_async_remote_copy(
                src_ref=slot if src is None else src, dst_ref=slot,
                send_sem=send_sems.at[a, k], recv_sem=recv_sems.at[a, k],
                device_id=to, device_id_type=pl.DeviceIdType.MESH)

        mine, first, passed = [], [], []
        for a in range(n):
            cp = pltpu.make_async_copy(ins[a], outs[a].at[_flat_index(*me)], local_sems.at[a])
            cp.start()
            mine.append(cp)
            first.append(copy(a, 0, me, sibling, src=ins[a]))
            first += [copy(a, 1 + j, me, (*chip, c), src=ins[a]) for j, chip in enumerate(chips)]
        for cp in first:
            cp.start()
        for j, chip in enumerate(chips):
            for a in range(n):
                copy(a, 1 + j, (*chip, c), me).wait_recv()
                fwd = copy(a, 4 + j, (*chip, c), sibling)
                fwd.start()
                passed.append(fwd)
        for a in range(n):
            copy(a, 0, sibling, me).wait_recv()
            for j, chip in enumerate(chips):
                copy(a, 4 + j, (*chip, 1 - c), me).wait_recv()
        for cp in first + passed:
            cp.wait_send()
        for cp in mine:
            cp.wait()

    return _pcall(
        body, name=name,
        in_specs=[HBM_SPEC] * n, out_specs=[HBM_SPEC] * n,
        out_shape=[jax.ShapeDtypeStruct((N_DEV,) + a.shape, a.dtype) for a in shards],
        scratch_shapes=[pltpu.SemaphoreType.DMA((n, 7)), pltpu.SemaphoreType.DMA((n, 7)),
                        pltpu.SemaphoreType.DMA((n,))],
    )(*shards)


HBM_ONLY = pl.BlockSpec(memory_space=pltpu.HBM)
SEM_SPEC = pl.BlockSpec(memory_space=pltpu.SEMAPHORE)
N_PEERS = N_DEV - 1
PEER_FLIPS = ((0, 0, 1), (1, 0, 0), (0, 1, 0), (1, 1, 0), (1, 0, 1), (0, 1, 1), (1, 1, 1))


def _exchange_refs(gather, src_refs, land_refs, send_sems, recv_sems):
    x, y, c = lax.axis_index("x"), lax.axis_index("y"), lax.axis_index("c")
    me = _flat_index(x, y, c)
    peers = [(x ^ fx, y ^ fy, c ^ fc) for fx, fy, fc in PEER_FLIPS]

    def out_copy(a, k):
        src = src_refs[a] if gather else src_refs[a].at[_flat_index(*peers[k])]
        return pltpu.make_async_remote_copy(
            src_ref=src, dst_ref=land_refs[a].at[me], send_sem=send_sems.at[a * N_PEERS + k],
            recv_sem=recv_sems.at[a * N_PEERS + k], device_id=peers[k], device_id_type=pl.DeviceIdType.MESH)

    def in_copy(a, k):
        src = src_refs[a] if gather else src_refs[a].at[me]
        return pltpu.make_async_remote_copy(
            src_ref=src, dst_ref=land_refs[a].at[_flat_index(*peers[k])], send_sem=send_sems.at[a * N_PEERS + k],
            recv_sem=recv_sems.at[a * N_PEERS + k], device_id=peers[k], device_id_type=pl.DeviceIdType.MESH)

    return out_copy, in_copy


def _exchange_start(srcs, *, gather, name, dep):
    n = len(srcs)
    lands = [lax.empty((N_DEV,) + a.shape if gather else a.shape, a.dtype) for a in srcs]

    def body(*refs):
        src_refs, land_refs = refs[:n], refs[n:2 * n]
        send_sems, recv_sems = refs[2 * n + 1], refs[2 * n + 2]
        token = refs[-1]
        out_copy, _ = _exchange_refs(gather, src_refs, land_refs, send_sems, recv_sems)
        for k in range(N_PEERS):
            for a in range(n):
                out_copy(a, k).start()
        token[...] = jnp.zeros_like(token)

    hbm = lambda a: pltpu.with_memory_space_constraint(a, pltpu.HBM)
    return _pcall(
        body, name=name,
        out_shape=(pltpu.SemaphoreType.DMA((n * N_PEERS,)), pltpu.SemaphoreType.DMA((n * N_PEERS,)),
                   *[pltpu.HBM(a.shape, a.dtype) for a in srcs], *[pltpu.HBM(a.shape, a.dtype) for a in lands],
                   jax.ShapeDtypeStruct((8, LANES), F32)),
        in_specs=[HBM_ONLY] * (2 * n) + [HBM_SPEC],
        out_specs=(SEM_SPEC, SEM_SPEC, *[HBM_ONLY] * (2 * n), pl.BlockSpec(memory_space=pltpu.VMEM)),
        input_output_aliases={i: 2 + i for i in range(2 * n)},
        compiler_params=pltpu.CompilerParams(has_side_effects=pltpu.SideEffectType.DATAFLOW_SIDE_EFFECTING),
    )(*[hbm(a) for a in srcs], *[hbm(a) for a in lands], dep)


def _exchange_wait(started, after, *, gather, name):
    send_sems, recv_sems = started[0], started[1]
    n = (len(started) - 3) // 2
    srcs, lands = started[2:2 + n], started[2 + n:2 + 2 * n]

    def body(*refs):
        src_refs, land_refs = refs[:n], refs[n:2 * n]
        s_sems, r_sems = refs[2 * n], refs[2 * n + 1]
        out_copy, in_copy = _exchange_refs(gather, src_refs, land_refs, s_sems, r_sems)
        for k in range(N_PEERS):
            for a in range(n):
                out_copy(a, k).wait_send()
                in_copy(a, k).wait_recv()

    res = _pcall(
        body, name=name,
        out_shape=tuple(pltpu.HBM(a.shape, a.dtype) for a in (*srcs, *lands)),
        in_specs=[HBM_ONLY] * (2 * n) + [SEM_SPEC, SEM_SPEC, HBM_SPEC],
        out_specs=tuple([HBM_ONLY] * (2 * n)),
        input_output_aliases={i: i for i in range(2 * n)},
        compiler_params=pltpu.CompilerParams(has_side_effects=pltpu.SideEffectType.DATAFLOW_SIDE_EFFECTING),
    )(*srcs, *lands, send_sems, recv_sems, after)
    return _with_own_slot(res[:n], res[n:], gather)


def _with_own_slot(srcs, lands, gather):
    me = _flat_index(lax.axis_index("x"), lax.axis_index("y"), lax.axis_index("c"))
    full = []
    for src, land in zip(srcs, lands):
        own = src if gather else lax.dynamic_index_in_dim(src, me, 0, keepdims=False)
        full.append(lax.dynamic_update_slice(land, own[None], (me,) + (0,) * own.ndim))
    return full


def _adam_math(g, w, m, v):
    m = ADAM_B1 * m + (1.0 - ADAM_B1) * g
    v = ADAM_B2 * v + (1.0 - ADAM_B2) * jnp.square(g)
    m_hat = m / (1.0 - ADAM_B1 ** ADAM_STEP)
    v_hat = v / (1.0 - ADAM_B2 ** ADAM_STEP)
    delta = -ADAM_LR * (m_hat / (jnp.sqrt(v_hat) + ADAM_EPS) + ADAM_WD * w)
    return delta, m, v


def _adamw(contribs, w, m, v, *, name):
    layers = len(contribs)
    _, r, c = contribs[0].shape
    tr = _tile(r, max(8, (256 * 1024 // c) // 8 * 8), 8)

    def body(*refs):
        g_refs = refs[:layers]
        w_ref, m_ref, v_ref, go_ref, d_ref, mo_ref, vo_ref = refs[layers:]
        layer = pl.program_id(0)
        for l2 in range(layers):

            @pl.when(layer == l2)
            def _(g_ref=g_refs[l2]):
                g = g_ref[0].astype(F32)
                for k in range(1, N_DEV):
                    g = g + g_ref[k].astype(F32)
                delta, m_new, v_new = _adam_math(g, w_ref[...], m_ref[...], v_ref[...])
                go_ref[...] = g
                d_ref[...] = delta
                mo_ref[...] = m_new
                vo_ref[...] = v_new

    g_specs = [pl.BlockSpec((N_DEV, tr, c), lambda l, i, l2=l2: (0, jnp.where(l == l2, i, 0), 0))
               for l2 in range(layers)]
    blk = pl.BlockSpec((None, tr, c), lambda l, i: (l, i, 0))
    out = jax.ShapeDtypeStruct((layers, r, c), F32)
    return _pcall(
        body, name=name, grid=(layers, r // tr), in_specs=g_specs + [blk, blk, blk],
        out_specs=(blk, blk, blk, blk), out_shape=(out, out, out, out),
        compiler_params=_params("arbitrary", "arbitrary"),
    )(*contribs, w, m, v)


def _heads_split(w, heads, first, second):
    k = w.shape[0]
    w3 = w.reshape(k, heads, first + second)
    return jnp.concatenate([w3[:, :, :first].reshape(k, heads * first),
                            w3[:, :, first:].reshape(k, heads * second)], axis=1)


def _heads_join(w, heads, first, second):
    k = w.shape[0]
    a = w[:, :heads * first].reshape(k, heads, first)
    b = w[:, heads * first:].reshape(k, heads, second)
    return jnp.concatenate([a, b], axis=2).reshape(k, heads * (first + second))


def _full_from_gathered(kind, g):
    if kind == "col":
        return jnp.transpose(g, (1, 0, 2)).reshape(g.shape[1], N_DEV * g.shape[2])
    if kind == "row":
        return g.reshape(N_DEV * g.shape[1], g.shape[2])
    return jnp.transpose(g, (1, 0, 2, 3)).reshape(g.shape[1], N_DEV * g.shape[2], g.shape[3])


def _blocks_from_full(kind, f):
    if kind == "col":
        k, n = f.shape
        return jnp.transpose(f.reshape(k, N_DEV, n // N_DEV), (1, 0, 2))
    if kind == "row":
        k, n = f.shape
        return f.reshape(N_DEV, k // N_DEV, n)
    g, c_in, c = f.shape
    return jnp.transpose(f.reshape(g, N_DEV, c_in // N_DEV, c), (1, 0, 2, 3))


def _rope_tables(positions):
    inv_freq = 1.0 / (ROPE_THETA ** (jnp.arange(0, ROPE_DIM, 2, dtype=F32) / ROPE_DIM))
    ang = positions.astype(F32)[:, None] * inv_freq
    cos, sin = jnp.cos(ang), jnp.sin(ang)
    return jnp.concatenate([cos, cos, cos, cos], axis=-1), jnp.concatenate([-sin, sin, -sin, sin], axis=-1)


def _layer_fwd(h0, p_i, fetch, rep, tabs, dims, dep=None):
    heads, d_pool, q_lora, d_ff = dims["heads"], dims["d_pool"], dims["q_lora"], dims["d_ff"]
    c = d_pool // POOL_GROUPS
    cos_t, sin_t = tabs
    scale = 1.0 / math.sqrt(NOPE_DIM + ROPE_DIM)
    n1 = _rms_fwd(h0, rep["norm_mix_g"], name="rms_mix_fwd", dep=dep)
    w = dict(fetch("mix", n1))
    u = _mm(n1, w["w_in"], name="mm_in_fwd")
    y_pool, diff = _pool_fwd(u, w["pool_w"], rep["pool_scale"], c=c)
    nq = _rms_fwd(u, rep["q_norm_g"], name="rms_q_fwd", col_block=d_pool // q_lora)
    nkv = _rms_fwd(u, rep["kv_norm_g"], name="rms_kv_fwd", col_block=d_pool // q_lora + 1)
    q = _mm(nq, w["w_uq"], name="mm_uq_fwd")
    kv = _mm(nkv, w["w_ukv"], name="mm_ukv_fwd")
    kr = u[:, d_pool + 2 * q_lora:]
    kr2 = jnp.concatenate([kr, kr], axis=-1)
    qp, kp, v = _qkv_prep(q, kv, kr2, cos_t, sin_t, heads=heads)
    o, lse = _flash_fwd(qp, kp, v, scale=scale)
    t = _mm(y_pool, w["w_out"], name="mm_out_pool_fwd", add=h0, b_row=(0, d_pool))
    h1 = _mm(o, w["w_out"], name="mm_out_att_fwd", add=t, b_row=(d_pool, None))
    n2 = _rms_fwd(h1, rep["norm_ffn_g"], name="rms_ffn_fwd")
    w.update(fetch("up", n2))
    gu = _mm(n2, w["w_up"], name="mm_up_fwd", out_dtype=BF16, b_blocks=(0, N_DEV))
    a = _ffn_act_fwd(gu, w["conv_w"], rep["conv_b"], d_ff=d_ff)
    w.update(fetch("down", a))
    h2 = _mm(a, w["w_down"], name="mm_down_fwd", add=h1)
    n3 = _rms_fwd(h2, rep["norm_ple_g"], name="rms_ple_fwd")
    z = _mm(n3, w["w_ple_gate"], name="mm_pgate_fwd")
    e = _mm(p_i, w["w_ple"], name="mm_ple_fwd")
    h3 = _ple_fwd(h2, z, e)
    saved = dict(h0=h0, n1=n1, u=u, y_pool=y_pool, diff=diff, nq=nq, nkv=nkv, qp=qp, kp=kp, v=v, o=o, lse=lse,
                 h1=h1, n2=n2, gu=gu, a=a, h2=h2, n3=n3, z=z, e=e)
    return h3, saved, w


def _layer_bwd(dh3, p_i, w, rep, tabs, dims, sv, dep=None, hooks=None):
    hooks = hooks or {}
    heads, d_pool, q_lora, d_ff = dims["heads"], dims["d_pool"], dims["q_lora"], dims["d_ff"]
    c = d_pool // POOL_GROUPS
    cos_t, sin_t = tabs
    scale = 1.0 / math.sqrt(NOPE_DIM + ROPE_DIM)
    gr = {}
    de, dz = _ple_bwd(dh3, sv["z"], sv["e"], dep)
    gr["w_ple"] = _mm(p_i, de, name="mm_ple_dw", ta=True, out_dtype=BF16)
    gr["w_ple_gate"] = _mm(sv["n3"], dz, name="mm_pgate_dw", ta=True, out_dtype=BF16)
    dn3 = _mm(dz, w["w_ple_gate"], name="mm_pgate_dx", tb=True)
    dh2, dh2_mm, gr["norm_ple_g"] = _rms_bwd(dn3, sv["h2"], rep["norm_ple_g"], name="rms_ple_bwd", res=dh3,
                                             matmul_copy=True)
    gr["w_down"] = _mm(sv["a"], dh2_mm, name="mm_down_dw", ta=True, out_dtype=BF16)
    dep_down = hooks["down"](dh2, gr) if "down" in hooks else None
    da = _mm(dh2_mm, w["w_down"], name="mm_down_dx", tb=True, out_dtype=BF16, dep=dep_down)
    dgate, dup, gr["conv_w"], gr["conv_b"] = _ffn_act_bwd(da, sv["gu"], w["conv_w"], rep["conv_b"], d_ff=d_ff)
    half, per = N_DEV // 2, w["w_up"].shape[2]
    dw_gate = _mm(sv["n2"], dgate, name="mm_up_gate_dw", ta=True, out_dtype=BF16, out_blocks=(N_DEV, 0, per))
    gr["w_up"] = _mm(sv["n2"], dup, name="mm_up_up_dw", ta=True, out_dtype=BF16, out_blocks=(N_DEV, half, per),
                     out_init=dw_gate)
    dep_up = hooks["up"](gr["w_up"], gr) if "up" in hooks else None
    dn2 = _mm(dgate, w["w_up"], name="mm_up_gate_dx", tb=True, b_blocks=(0, half), dep=dep_up)
    dn2 = _mm(dup, w["w_up"], name="mm_up_up_dx", tb=True, b_blocks=(half, half), add=dn2)
    dh1, dh1_mm, gr["norm_ffn_g"] = _rms_bwd(dn2, sv["h1"], rep["norm_ffn_g"], name="rms_ffn_bwd", res=dh2,
                                             matmul_copy=True)
    dw_out_pool = _mm(sv["y_pool"], dh1_mm, name="mm_out_pool_dw", ta=True, out_dtype=BF16)
    dw_out_att = _mm(sv["o"], dh1_mm, name="mm_out_att_dw", ta=True, out_dtype=BF16)
    gr["w_out"] = jnp.concatenate([dw_out_pool, dw_out_att], axis=0)
    dcat = _mm(dh1_mm, w["w_out"], name="mm_out_dx", tb=True)
    do_col0 = d_pool // V_DIM
    dqp, dkp, dv = _flash_bwd(sv["qp"], sv["kp"], sv["v"], dcat, sv["o"], sv["lse"], scale=scale, do_col0=do_col0)
    dq, dkv, dkr2 = _attn_bwd_post(dqp, dkp, dv, cos_t, sin_t)
    gr["w_uq"] = _mm(sv["nq"], dq, name="mm_uq_dw", ta=True, out_dtype=BF16)
    gr["w_ukv"] = _mm(sv["nkv"], dkv, name="mm_ukv_dw", ta=True, out_dtype=BF16)
    dnq = _mm(dq, w["w_uq"], name="mm_uq_dx", tb=True)
    dnkv = _mm(dkv, w["w_ukv"], name="mm_ukv_dx", tb=True)
    dcq, gr["q_norm_g"] = _rms_bwd(dnq, sv["u"], rep["q_norm_g"], name="rms_q_bwd",
                                   col_block=d_pool // q_lora, out_dtype=BF16)
    dckv, gr["kv_norm_g"] = _rms_bwd(dnkv, sv["u"], rep["kv_norm_g"], name="rms_kv_bwd",
                                     col_block=d_pool // q_lora + 1, out_dtype=BF16)
    du_pool, gr["pool_w"], gr["pool_scale"] = _pool_bwd(dcat, sv["diff"], w["pool_w"], rep["pool_scale"], c=c)
    du = jnp.concatenate([du_pool, dcq, dckv, dkr2[:, :ROPE_DIM]], axis=-1)
    gr["w_in"] = _mm(sv["n1"], du, name="mm_in_dw", ta=True, out_dtype=BF16)
    dn1 = _mm(du, w["w_in"], name="mm_in_dx", tb=True)
    dh0, gr["norm_mix_g"] = _rms_bwd(dn1, sv["h0"], rep["norm_mix_g"], name="rms_mix_bwd", res=dh1)
    return dh0, gr


def _as2d(a):
    return a.reshape(a.shape[0], -1, a.shape[-1])


def kernel(x, p, positions, norm_mix_g, w_in, pool_w, pool_scale, q_norm_g, w_uq, kv_norm_g, w_ukv, w_out, norm_ffn_g, w_up, conv_w, conv_b, w_down, norm_ple_g, w_ple, w_ple_gate, final_norm_g, loss_target, m_norm_mix_g, m_w_in, m_pool_w, m_pool_scale, m_q_norm_g, m_w_uq, m_kv_norm_g, m_w_ukv, m_w_out, m_norm_ffn_g, m_w_up, m_conv_w, m_conv_b, m_w_down, m_norm_ple_g, m_w_ple, m_w_ple_gate, m_final_norm_g, v_norm_mix_g, v_w_in, v_pool_w, v_pool_scale, v_q_norm_g, v_w_uq, v_kv_norm_g, v_w_ukv, v_w_out, v_norm_ffn_g, v_w_up, v_conv_w, v_conv_b, v_w_down, v_norm_ple_g, v_w_ple, v_w_ple_gate, v_final_norm_g):
    weights = dict(norm_mix_g=norm_mix_g, w_in=w_in, pool_w=pool_w, pool_scale=pool_scale, q_norm_g=q_norm_g,
                   w_uq=w_uq, kv_norm_g=kv_norm_g, w_ukv=w_ukv, w_out=w_out, norm_ffn_g=norm_ffn_g, w_up=w_up,
                   conv_w=conv_w, conv_b=conv_b, w_down=w_down, norm_ple_g=norm_ple_g, w_ple=w_ple,
                   w_ple_gate=w_ple_gate, final_norm_g=final_norm_g)
    m_in = dict(norm_mix_g=m_norm_mix_g, w_in=m_w_in, pool_w=m_pool_w, pool_scale=m_pool_scale, q_norm_g=m_q_norm_g,
                w_uq=m_w_uq, kv_norm_g=m_kv_norm_g, w_ukv=m_w_ukv, w_out=m_w_out, norm_ffn_g=m_norm_ffn_g,
                w_up=m_w_up, conv_w=m_conv_w, conv_b=m_conv_b, w_down=m_w_down, norm_ple_g=m_norm_ple_g,
                w_ple=m_w_ple, w_ple_gate=m_w_ple_gate, final_norm_g=m_final_norm_g)
    v_in = dict(norm_mix_g=v_norm_mix_g, w_in=v_w_in, pool_w=v_pool_w, pool_scale=v_pool_scale, q_norm_g=v_q_norm_g,
                w_uq=v_w_uq, kv_norm_g=v_kv_norm_g, w_ukv=v_w_ukv, w_out=v_w_out, norm_ffn_g=v_norm_ffn_g,
                w_up=v_w_up, conv_w=v_conv_w, conv_b=v_conv_b, w_down=v_w_down, norm_ple_g=v_norm_ple_g,
                w_ple=v_w_ple, w_ple_gate=v_w_ple_gate, final_norm_g=v_final_norm_g)

    depth = w_in.shape[0]
    s, d_model = x.shape[1], x.shape[2]
    d_pool = pool_scale.shape[-1]
    q_lora = q_norm_g.shape[-1]
    d_ff = conv_b.shape[-1]
    heads = (w_uq.shape[-1] * N_DEV) // (NOPE_DIM + ROPE_DIM)
    dims = dict(heads=heads, d_pool=d_pool, q_lora=q_lora, d_ff=d_ff)

    groups = {"mix": ("w_in", "pool_w", "w_uq", "w_ukv", "w_out"), "up": ("w_up", "conv_w"),
              "down": ("w_down", "w_ple", "w_ple_gate")}

    def group_shards(i, group):
        return [weights[n][i] if n == "conv_w" else weights[n][i].astype(BF16) for n in groups[group]]

    def start_weights(i, group, dep):
        return _exchange_start(group_shards(i, group), gather=True, name=f"weights_{group}_start_{i}", dep=dep)

    def full_group(group, gathered_g):
        w = {n: g if n == "w_up" else _full_from_gathered(SHARD_KIND[n], g) for n, g in zip(groups[group], gathered_g)}
        if group == "mix":
            w["w_uq"] = _heads_split(w["w_uq"], heads, NOPE_DIM, ROPE_DIM)
            w["w_ukv"] = _heads_split(w["w_ukv"], heads, NOPE_DIM, V_DIM)
        return w

    tabs = _rope_tables(positions[0])

    arrived = {(0, "mix"): _all_gather(group_shards(0, "mix"), name="weights_mix_gather_0")}
    travelling = {}
    token = arrived[(0, "mix")][0]
    for group in ("up", "down"):
        travelling[(0, group)] = start_weights(0, group, token)
        token = travelling[(0, group)][-1]
    layer_w = []
    h = x[0]
    saved = []
    for i in range(depth):
        if i + 1 < depth:
            for group in ("mix", "up", "down"):
                travelling[(i + 1, group)] = start_weights(i + 1, group, token if i == 0 and group == "mix" else
                                                           (h if group == "mix" else token))
                token = travelling[(i + 1, group)][-1]

        def fetch(group, after, i=i):
            if (i, group) not in arrived:
                arrived[(i, group)] = _exchange_wait(travelling[(i, group)], after, gather=True,
                                                     name=f"weights_{group}_wait_{i}")
            return full_group(group, arrived[(i, group)])

        rep = {n: weights[n][i] for n in REPLICATED}
        h, sv, w = _layer_fwd(h, p[i, 0], fetch, rep, tabs, dims, dep=token if i + 1 < depth or i == 0 else None)
        layer_w.append((w, rep))
        saved.append(sv)
    loss_row, dh, g_final = _loss_head(h, final_norm_g, loss_target[0])
    loss = lax.psum(loss_row[0, 0], MESH_AXES)

    def start_grads(group, gr, dep, i):
        names = groups[group]
        blocks = [gr[n] if n == "w_up" else _blocks_from_full(SHARD_KIND[n], gr[n]).astype(BF16) for n in names]
        return _exchange_start(blocks, gather=False, name=f"grads_{group}_start_{i}", dep=dep)

    def end_grads(group, started, after, i):
        got = _exchange_wait(started, after, gather=False, name=f"grads_{group}_wait_{i}")
        received[i].update(zip(groups[group], got))

    layer_grads = [None] * depth
    received = [dict() for _ in range(depth)]
    pending = None
    for i in reversed(range(depth)):
        w, rep = layer_w[i]
        state = {}

        def on_down(dh2, gr, i=i, state=state):
            state["down"] = start_grads("down", gr, dh2, i)
            return state["down"][-1]

        def on_up(dw_up, gr, i=i, pending=pending, state=state):
            if pending is not None:
                for group in ("down", "up", "mix"):
                    end_grads(group, pending[group], dw_up, i + 1)
            state["up"] = start_grads("up", gr, dw_up, i)
            return state["up"][-1]

        dh, gr = _layer_bwd(dh, p[i, 0], w, rep, tabs, dims, saved[i],
                            dep=loss.reshape(1, 1) if pending is None else pending["mix"][-1],
                            hooks={"down": on_down, "up": on_up})
        gr["w_uq"] = _heads_join(gr["w_uq"], heads, NOPE_DIM, ROPE_DIM)
        gr["w_ukv"] = _heads_join(gr["w_ukv"], heads, NOPE_DIM, V_DIM)
        layer_grads[i] = gr
        state["mix"] = start_grads("mix", gr, dh, i)
        pending = state
    grad_x = dh[None]

    out = {}

    def update(n):
        shape = weights[n].shape
        recs = [received[i][n].reshape((N_DEV, -1, shape[-1])) for i in range(depth)]
        res = _adamw(recs, _as2d(weights[n]), _as2d(m_in[n]), _as2d(v_in[n]), name="adamw_" + n)
        out[n] = tuple(r.reshape(shape) for r in res)

    end_grads("down", pending["down"], pending["mix"][-1], 0)
    end_grads("up", pending["up"], pending["mix"][-1], 0)
    for n in groups["down"] + groups["up"]:
        update(n)
    end_grads("mix", pending["mix"], out["w_up"][0], 0)
    for n in groups["mix"]:
        update(n)

    small_names = REPLICATED + ("final_norm_g",)

    def pack(get):
        rows = [jnp.stack([get(n, i).reshape(-1) for i in range(depth)]).reshape(-1) for n in REPLICATED]
        rows.append(get("final_norm_g", None).reshape(-1))
        return jnp.concatenate(rows).reshape(1, -1, LANES)

    g_small = pack(lambda n, i: g_final if i is None else layer_grads[i][n])
    w_small = pack(lambda n, i: weights[n] if i is None else weights[n][i])
    m_small = pack(lambda n, i: m_in[n] if i is None else m_in[n][i])
    v_small = pack(lambda n, i: v_in[n] if i is None else v_in[n][i])
    (g_all,) = _all_gather([g_small], name="small_grads_all_gather")
    res_small = _adamw([g_all[:, 0]], w_small, m_small, v_small, name="adamw_small")

    def unpack(flat3):
        flat = flat3.reshape(-1)
        res, off = {}, 0
        for n in REPLICATED:
            width = weights[n].shape[-1]
            res[n] = flat[off:off + depth * width].reshape(depth, width)
            off += depth * width
        res["final_norm_g"] = flat[off:off + d_model]
        return res

    small = [unpack(r) for r in res_small]
    for n in small_names:
        out[n] = tuple(small[k][n] for k in range(4))

    outs = [loss, grad_x]
    for k in range(4):
        outs += [out[n][k] for n in WEIGHT_ORDER]
    return tuple(outs)
```

```python
import functools
import math

import jax
import jax.numpy as jnp
from jax import lax
from jax.experimental import pallas as pl
from jax.experimental.pallas import tpu as pltpu

F32 = jnp.float32
BF16 = jnp.bfloat16

N_DEV = 8
MESH_AXES = ("x", "y", "c")
NOPE_DIM = 128
ROPE_DIM = 64
V_DIM = 128
POOL_GROUPS = 4
CONV_TAPS = 3
ROPE_THETA = 10000.0
NORM_EPS = 1e-6
ADAM_LR = 0.001
ADAM_B1 = 0.9
ADAM_B2 = 0.999
ADAM_EPS = 1e-08
ADAM_WD = 0.01
ADAM_STEP = 10
LANES = 128
VMEM_LIMIT_BYTES = 56 * 1024 * 1024

SHARDED = ("w_in", "pool_w", "w_uq", "w_ukv", "w_out", "w_up", "conv_w", "w_down", "w_ple", "w_ple_gate")
SHARD_KIND = {"w_in": "col", "pool_w": "pool", "w_uq": "col", "w_ukv": "col", "w_out": "row", "w_up": "col",
              "conv_w": "col", "w_down": "row", "w_ple": "col", "w_ple_gate": "row"}
REPLICATED = ("norm_mix_g", "pool_scale", "q_norm_g", "kv_norm_g", "norm_ffn_g", "conv_b", "norm_ple_g")
WEIGHT_ORDER = ("norm_mix_g", "w_in", "pool_w", "pool_scale", "q_norm_g", "w_uq", "kv_norm_g", "w_ukv", "w_out",
                "norm_ffn_g", "w_up", "conv_w", "conv_b", "w_down", "norm_ple_g", "w_ple", "w_ple_gate",
                "final_norm_g")

_pcall = pl.pallas_call


def _params(*sem):
    return pltpu.CompilerParams(dimension_semantics=sem or None, vmem_limit_bytes=VMEM_LIMIT_BYTES)


def _tile(n, pref, mult=LANES):
    if n <= pref:
        return n
    t = (pref // mult) * mult
    while t >= mult:
        if n % t == 0:
            return t
        t -= mult
    return n


MM_VMEM_BUDGET_BYTES = 40 * 1024 * 1024
MM_TILE_PREFS = (1024, 512, 256)
MM_MIN_TK = 1024


MM_MIN_WHOLE_K_TILE = 512 * 512


def _mm_tiles(m, n, k, a_bytes, b_bytes, o_bytes, has_add, tn_fixed=None, tk_fixed=None):
    def need(tm, tn, tk):
        nbytes = 2 * (tm * tk * a_bytes + tk * tn * b_bytes) + 2 * tm * tn * (o_bytes + (4 if has_add else 0))
        return nbytes + (tm * tn * 4 if tk < k else 0)

    tn_cands = [tn_fixed] if tn_fixed else sorted({_tile(n, pref) for pref in MM_TILE_PREFS[:2]}, reverse=True)
    if not tk_fixed or tk_fixed == k:
        whole = [(tm * tn, tm, tn) for tm in sorted({_tile(m, pref) for pref in MM_TILE_PREFS}, reverse=True)
                 for tn in tn_cands if need(tm, tn, k) <= MM_VMEM_BUDGET_BYTES]
        if whole and max(whole)[0] >= MM_MIN_WHOLE_K_TILE:
            _, tm, tn = max(whole)
            return tm, tn, k
    tn = tn_cands[0]
    if tk_fixed:
        k_cands = [tk_fixed]
    else:
        k_cands = [k] + [t for t in range((k - 1) // LANES * LANES, 0, -LANES) if k % t == 0]
    best = None
    for pref in MM_TILE_PREFS:
        tm = _tile(m, pref)
        for tk in k_cands:
            if need(tm, tn, tk) <= MM_VMEM_BUDGET_BYTES:
                if tk >= min(k, MM_MIN_TK):
                    return tm, tn, tk
                if best is None or tk > best[2]:
                    best = (tm, tn, tk)
                break
    assert best is not None, (m, n, k)
    return best


def _mm(a, b, *, name, ta=False, tb=False, add=None, out_dtype=F32, b_row=(0, None), b_blocks=None,
        out_blocks=None, out_init=None, dep=None):
    m = a.shape[1] if ta else a.shape[0]
    kdim = a.shape[0] if ta else a.shape[1]
    tn_fixed = tk_fixed = None
    k_off_b = 0
    if b_blocks is None:
        n_b, k_b = (b.shape if tb else b.shape[::-1])
        k_off_b, kb_sz = b_row
        kb_sz = k_b - k_off_b if kb_sz is None else kb_sz
        assert kb_sz == kdim, (name, kb_sz, kdim)
        n = n_b
    else:
        first_b, count_b = b_blocks
        per_b = b.shape[2]
        if tb:
            n = b.shape[1]
            assert kdim == count_b * per_b, name
            tk_fixed = per_b
        else:
            n = count_b * per_b
            assert kdim == b.shape[1], name
            tn_fixed = per_b
    if out_blocks is not None:
        nb_out, first_o, tn_fixed = out_blocks
    tm, tn, tk = _mm_tiles(m, n, kdim, a.dtype.itemsize, b.dtype.itemsize, jnp.dtype(out_dtype).itemsize,
                           add is not None, tn_fixed, tk_fixed)
    assert k_off_b % tk == 0 and kdim % tk == 0 and n % tn == 0 and m % tm == 0, name
    kb0 = k_off_b // tk
    nk = kdim // tk
    dims = (((0 if ta else 1,), (1 if tb else 0,)), ((), ()))

    def body(*refs):
        a_ref, b_ref = refs[0], refs[1]
        add_ref = refs[2] if add is not None else None
        o_ref = refs[n_in]
        part = lax.dot_general(a_ref[...].astype(BF16), b_ref[...].astype(BF16), dims, preferred_element_type=F32)

        def finish(r):
            if add_ref is not None:
                r = r + add_ref[...].astype(F32)
            o_ref[...] = r.astype(out_dtype)

        if nk == 1:
            finish(part)
            return
        acc = refs[n_in + 1]
        k = pl.program_id(2)

        @pl.when(k == 0)
        def _():
            acc[...] = part

        @pl.when(jnp.logical_and(k > 0, k < nk - 1))
        def _():
            acc[...] += part

        @pl.when(k == nk - 1)
        def _():
            finish(acc[...] + part)

    if ta:
        a_spec = pl.BlockSpec((tk, tm), lambda i, j, k: (k, i))
    else:
        a_spec = pl.BlockSpec((tm, tk), lambda i, j, k: (i, k))
    if b_blocks is not None and tb:
        b_spec = pl.BlockSpec((None, tn, tk), lambda i, j, k: (k + first_b, j, 0))
    elif b_blocks is not None:
        b_spec = pl.BlockSpec((None, tk, tn), lambda i, j, k: (j + first_b, k, 0))
    elif tb:
        b_spec = pl.BlockSpec((tn, tk), lambda i, j, k: (j, k + kb0))
    else:
        b_spec = pl.BlockSpec((tk, tn), lambda i, j, k: (k + kb0, j))
    in_specs = [a_spec, b_spec]
    args = [a, b]
    if add is not None:
        in_specs.append(pl.BlockSpec((tm, tn), lambda i, j, k: (i, j)))
        args.append(add)
    aliases = {}
    if out_init is not None:
        aliases = {len(args): 0}
        in_specs.append(HBM_SPEC)
        args.append(out_init)
    if dep is not None:
        in_specs.append(HBM_SPEC)
        args.append(dep)
    n_in = len(args)
    if out_blocks is None:
        out_spec = pl.BlockSpec((tm, tn), lambda i, j, k: (i, j))
        out_shape = jax.ShapeDtypeStruct((m, n), out_dtype)
    else:
        out_spec = pl.BlockSpec((None, tm, tn), lambda i, j, k: (j + first_o, i, 0))
        out_shape = jax.ShapeDtypeStruct((nb_out, m, tn), out_dtype)
    return _pcall(
        body, name=name, grid=(m // tm, n // tn, nk), in_specs=in_specs, out_specs=out_spec, out_shape=out_shape,
        scratch_shapes=[pltpu.VMEM((tm, tn), F32)] if nk > 1 else [], input_output_aliases=aliases,
        compiler_params=_params("parallel", "parallel", "arbitrary"),
    )(*args)


def _rms_fwd(h, g, *, name, col_block=0, dep=None):
    s = h.shape[0]
    d = g.shape[-1]
    ts = _tile(s, 512, 8)

    def body(h_ref, g_ref, *rest):
        n_ref = rest[-1]
        x = h_ref[...]
        r = lax.rsqrt(jnp.mean(x * x, axis=-1, keepdims=True) + NORM_EPS)
        n_ref[...] = (x * r * g_ref[...]).astype(BF16)

    deps = [] if dep is None else [dep]
    return _pcall(
        body, name=name, grid=(s // ts,),
        in_specs=[pl.BlockSpec((ts, d), lambda i: (i, col_block)), pl.BlockSpec((1, d), lambda i: (0, 0))]
        + [HBM_SPEC] * len(deps),
        out_specs=pl.BlockSpec((ts, d), lambda i: (i, 0)),
        out_shape=jax.ShapeDtypeStruct((s, d), BF16),
        compiler_params=_params("parallel"),
    )(h, g.reshape(1, d), *deps)


def _rms_bwd(dn, h, g, *, name, res=None, col_block=0, out_dtype=F32, matmul_copy=False):
    s = dn.shape[0]
    d = g.shape[-1]
    ts = _tile(s, 512, 8)

    def body(*refs):
        dn_ref, h_ref, g_ref = refs[:3]
        res_ref = refs[3] if res is not None else None
        dh_ref, dg_ref = refs[n_in], refs[-1]
        i = pl.program_id(0)
        x = h_ref[...]
        r = lax.rsqrt(jnp.mean(x * x, axis=-1, keepdims=True) + NORM_EPS)
        nh = x * r
        dnv = dn_ref[...]
        gd = dnv * g_ref[...]
        dh = (gd - nh * jnp.mean(gd * nh, axis=-1, keepdims=True)) * r
        if res_ref is not None:
            dh = dh + res_ref[...]
        dh_ref[...] = dh.astype(out_dtype)
        if matmul_copy:
            refs[n_in + 1][...] = dh.astype(BF16)
        part = jnp.sum(dnv * nh, axis=0, keepdims=True)

        @pl.when(i == 0)
        def _():
            dg_ref[...] = part

        @pl.when(i > 0)
        def _():
            dg_ref[...] += part

    row = pl.BlockSpec((ts, d), lambda i: (i, 0))
    in_specs = [row, pl.BlockSpec((ts, d), lambda i: (i, col_block)), pl.BlockSpec((1, d), lambda i: (0, 0))]
    args = [dn, h, g.reshape(1, d)]
    if res is not None:
        in_specs.append(row)
        args.append(res)
    n_in = len(args)
    copies = [row] if matmul_copy else []
    return _pcall(
        body, name=name, grid=(s // ts,), in_specs=in_specs,
        out_specs=(row, *copies, pl.BlockSpec((1, d), lambda i: (0, 0))),
        out_shape=(jax.ShapeDtypeStruct((s, d), out_dtype), *[jax.ShapeDtypeStruct((s, d), BF16) for _ in copies],
                   jax.ShapeDtypeStruct((1, d), F32)),
        compiler_params=_params("arbitrary"),
    )(*args)


ROW_CHUNK = 512


def _rows_with_halo(ref, r, t_rows, n_chunks, before, after):
    r0 = r * t_rows
    parts = []
    if before:
        hb = ref[pl.ds(pl.multiple_of(jnp.maximum(r0 - before, 0), before), before), :]
        parts.append(jnp.where(r > 0, hb, jnp.zeros_like(hb)))
    parts.append(ref[pl.ds(pl.multiple_of(r0, t_rows), t_rows), :])
    if after:
        ha = ref[pl.ds(pl.multiple_of(jnp.minimum(r0 + t_rows, n_chunks * t_rows - after), after), after), :]
        parts.append(jnp.where(r < n_chunks - 1, ha, jnp.zeros_like(ha)))
    return jnp.concatenate(parts, axis=0)


POOL_HALO = 16


def _pool_fwd(u, pool_w, pool_scale, *, c):
    s = u.shape[0]
    g_n = POOL_GROUPS
    tr = _tile(s, ROW_CHUNK, POOL_HALO)
    n_chunks = s // tr

    def body(u_ref, pw_ref, sc_ref, y_ref, d_ref):
        r = pl.program_id(1)
        w = jnp.left_shift(2, pl.program_id(0))
        xe = _rows_with_halo(u_ref, r, tr, n_chunks, POOL_HALO, 0)
        acc = xe
        for k in (1, 2, 4, 8):
            acc = jnp.where(k < w, acc + pltpu.roll(acc, k, 0), acc)
        t = r * tr + lax.broadcasted_iota(jnp.int32, (tr, 1), 0)
        cnt = jnp.minimum(t + 1, w).astype(F32)
        diff = (acc[POOL_HALO:] / cnt - xe[POOL_HALO:]).astype(BF16)
        d_ref[...] = diff
        y = jnp.dot(diff, pw_ref[...], preferred_element_type=F32) * sc_ref[...]
        y_ref[...] = y.astype(BF16)

    out = pl.BlockSpec((tr, c), lambda g, r: (r, g))
    return _pcall(
        body, name="pool_fwd", grid=(g_n, n_chunks),
        in_specs=[pl.BlockSpec((s, c), lambda g, r: (0, g)), pl.BlockSpec((None, c, c), lambda g, r: (g, 0, 0)),
                  pl.BlockSpec((1, c), lambda g, r: (0, g))],
        out_specs=(out, out),
        out_shape=(jax.ShapeDtypeStruct((s, g_n * c), BF16), jax.ShapeDtypeStruct((s, g_n * c), BF16)),
        compiler_params=_params("parallel", "arbitrary"),
    )(u, pool_w, pool_scale.reshape(1, g_n * c))


def _pool_bwd(dcat, diff, pool_w, pool_scale, *, c):
    s = dcat.shape[0]
    g_n = POOL_GROUPS
    tr = _tile(s, ROW_CHUNK, POOL_HALO)
    n_chunks = s // tr

    def body(dy_ref, d_ref, pw_ref, sc_ref, du_ref, dpw_ref, dsc_ref):
        r = pl.program_id(1)
        w = jnp.left_shift(2, pl.program_id(0))
        dye = _rows_with_halo(dy_ref, r, tr, n_chunks, 0, POOL_HALO)
        diff = d_ref[pl.ds(pl.multiple_of(r * tr, tr), tr), :]
        pw = pw_ref[...]
        yp = jnp.dot(diff, pw, preferred_element_type=F32)
        dsc = jnp.sum(dye[:tr] * yp, axis=0, keepdims=True)
        dyp = (dye * sc_ref[...]).astype(BF16)
        ddiff = lax.dot_general(dyp, pw, (((1,), (1,)), ((), ())), preferred_element_type=F32)
        dpw = lax.dot_general(diff, dyp[:tr], (((0,), (0,)), ((), ())), preferred_element_type=F32)
        t = r * tr + lax.broadcasted_iota(jnp.int32, (tr + POOL_HALO, 1), 0)
        cnt = jnp.minimum(t + 1, w).astype(F32)
        acc = ddiff / cnt
        rows = tr + POOL_HALO
        for k in (1, 2, 4, 8):
            acc = jnp.where(k < w, acc + pltpu.roll(acc, rows - k, 0), acc)
        du_ref[...] = (acc[:tr] - ddiff[:tr]).astype(BF16)

        @pl.when(r == 0)
        def _():
            dpw_ref[...] = dpw
            dsc_ref[...] = dsc

        @pl.when(r > 0)
        def _():
            dpw_ref[...] += dpw
            dsc_ref[...] += dsc

    col = lambda g, r: (0, g)
    wspec = pl.BlockSpec((None, c, c), lambda g, r: (g, 0, 0))
    vec = pl.BlockSpec((1, c), col)
    return _pcall(
        body, name="pool_bwd", grid=(g_n, n_chunks),
        in_specs=[pl.BlockSpec((s, c), col), pl.BlockSpec((s, c), col), wspec, vec],
        out_specs=(pl.BlockSpec((tr, c), lambda g, r: (r, g)), wspec, vec),
        out_shape=(jax.ShapeDtypeStruct((s, g_n * c), BF16), jax.ShapeDtypeStruct((g_n, c, c), F32),
                   jax.ShapeDtypeStruct((1, g_n * c), F32)),
        compiler_params=_params("parallel", "arbitrary"),
    )(dcat, diff, pool_w, pool_scale.reshape(1, g_n * c))


def _swap_halves(x, lane):
    return jnp.where((lane % ROPE_DIM) < ROPE_DIM // 2, pltpu.roll(x, LANES - ROPE_DIM // 2, 1),
                     pltpu.roll(x, ROPE_DIM // 2, 1))


def _qkv_prep(q, kv, kr2, cos_t, sin_t, *, heads):
    s = q.shape[0]
    ts = _tile(s, 256, 8)

    def body(q_ref, kv_ref, kr_ref, cos_ref, sin_ref, qo_ref, ko_ref, vo_ref):
        lane = lax.broadcasted_iota(jnp.int32, (ts, LANES), 1)
        cos_v = cos_ref[...]
        sin_v = sin_ref[...]

        def rope(x):
            return x * cos_v + _swap_halves(x, lane) * sin_v

        kr = rope(kr_ref[...]).astype(BF16)
        for pair in range(heads // 2):
            qr = rope(q_ref[:, (heads + pair) * LANES:(heads + pair + 1) * LANES])
            for half in range(2):
                h = 2 * pair + half
                qo_ref[h, :, :LANES] = q_ref[:, h * LANES:(h + 1) * LANES].astype(BF16)
                qo_ref[h, :, LANES:] = jnp.where(lane // ROPE_DIM == half, qr, 0.0).astype(BF16)
        for h in range(heads):
            ko_ref[h, :, :LANES] = kv_ref[:, h * LANES:(h + 1) * LANES].astype(BF16)
            ko_ref[h, :, LANES:] = kr
            vo_ref[h] = kv_ref[:, (heads + h) * LANES:(heads + h + 1) * LANES].astype(BF16)

    tab = pl.BlockSpec((ts, LANES), lambda i: (i, 0))
    return _pcall(
        body, name="qkv_prep", grid=(s // ts,),
        in_specs=[pl.BlockSpec((ts, q.shape[1]), lambda i: (i, 0)), pl.BlockSpec((ts, kv.shape[1]), lambda i: (i, 0)),
                  tab, tab, tab],
        out_specs=(pl.BlockSpec((heads, ts, 2 * LANES), lambda i: (0, i, 0)),
                   pl.BlockSpec((heads, ts, 2 * LANES), lambda i: (0, i, 0)),
                   pl.BlockSpec((heads, ts, LANES), lambda i: (0, i, 0))),
        out_shape=(jax.ShapeDtypeStruct((heads, s, 2 * LANES), BF16),
                   jax.ShapeDtypeStruct((heads, s, 2 * LANES), BF16),
                   jax.ShapeDtypeStruct((heads, s, LANES), BF16)),
        compiler_params=_params("parallel"),
    )(q, kv, kr2, cos_t, sin_t)


LOG2_E = 1.4426950408889634


FLASH_FWD_TK = 512
FLASH_FWD_SPLITS = 2


def _flash_fwd(qp, kp, v, *, scale, tq=512):
    heads, s, dk = qp.shape
    tq = _tile(s, tq, 16)
    tk = _tile(tq, FLASH_FWD_TK, 16)
    splits = FLASH_FWD_SPLITS
    th = tq // splits
    band = tq // tk
    c = scale * LOG2_E

    def body(q_ref, k_ref, v_ref, o_ref, lse_ref):
        i = pl.program_id(1)
        qs = [q_ref[hh * th:(hh + 1) * th, :] for hh in range(splits)]

        def skipped(hh, col0):
            return col0 is not None and col0 >= (hh + 1) * th

        def scores(start, col0):
            kb = k_ref[pl.ds(pl.multiple_of(start, tk), tk), :]
            return tuple(None if skipped(hh, col0) else
                         lax.dot_general(qs[hh], kb, (((1,), (1,)), ((), ())), preferred_element_type=F32)
                         for hh in range(splits))

        def absorb(start, scs, state, col0):
            vb = v_ref[pl.ds(pl.multiple_of(start, tk), tk), :]
            new = []
            for hh in range(splits):
                if scs[hh] is None:
                    new.append(state[hh])
                    continue
                m_old, l_old, acc = state[hh]
                sc = scs[hh]
                if col0 is not None and col0 + tk - 1 > hh * th:
                    rows = hh * th + lax.broadcasted_iota(jnp.int32, (th, tk), 0)
                    cols = col0 + lax.broadcasted_iota(jnp.int32, (th, tk), 1)
                    sc = jnp.where(rows >= cols, sc, -jnp.inf)
                m_new = jnp.maximum(m_old, jnp.max(sc, axis=-1, keepdims=True))
                alpha = jnp.exp2((m_old - m_new) * c)
                p = jnp.exp2((sc - m_new) * c)
                l_new = alpha * l_old + jnp.sum(p, axis=-1, keepdims=True)
                acc = alpha * acc + jnp.dot(p.astype(BF16), vb, preferred_element_type=F32)
                new.append((m_new, l_new, acc))
            return tuple(new)

        def step(j, carry):
            state, scs = carry
            nxt = scores((j + 1) * tk, None)
            return absorb(j * tk, scs, state, None), nxt

        init = tuple((jnp.full((th, 1), -jnp.inf, F32), jnp.zeros((th, 1), F32), jnp.zeros((th, V_DIM), F32))
                     for _ in range(splits))
        state, scs = lax.fori_loop(0, i * band, step, (init, scores(0, None)))
        for b in range(band):
            nxt = scores(i * tq + (b + 1) * tk, (b + 1) * tk) if b + 1 < band else None
            state = absorb(i * tq + b * tk, scs, state, b * tk)
            scs = nxt
        carry = state
        for hh in range(splits):
            m_fin, l_fin, acc = carry[hh]
            o_ref[hh * th:(hh + 1) * th, :] = acc / l_fin
            lse_ref[hh * th:(hh + 1) * th, :] = m_fin * scale + jnp.log(l_fin)

    return _pcall(
        body, name="flash_fwd", grid=(heads, s // tq),
        in_specs=[pl.BlockSpec((None, tq, dk), lambda h, i: (h, i, 0)),
                  pl.BlockSpec((None, s, dk), lambda h, i: (h, 0, 0)),
                  pl.BlockSpec((None, s, V_DIM), lambda h, i: (h, 0, 0))],
        out_specs=(pl.BlockSpec((tq, V_DIM), lambda h, i: (i, h)),
                   pl.BlockSpec((None, tq, 1), lambda h, i: (h, i, 0))),
        out_shape=(jax.ShapeDtypeStruct((s, heads * V_DIM), F32), jax.ShapeDtypeStruct((heads, s, 1), F32)),
        compiler_params=_params("parallel", "arbitrary"),
    )(qp, kp, v)


def _flash_bwd(qp, kp, v, dcat, o, lse, *, scale, do_col0, tq=512):
    heads, s, dk = qp.shape
    tq = _tile(s, tq, 16)
    tk = tq
    nq = s // tq

    def body(k_ref, v_ref, q_ref, do_ref, o_ref, lse_ref, dq_ref, dk_ref, dv_ref):
        j = pl.program_id(1)

        @pl.when(j == 0)
        def _():
            dq_ref[...] = jnp.zeros_like(dq_ref)

        kb = k_ref[...]
        vb = v_ref[...]

        def block(i, carry, diag):
            dk_acc, dv_acc = carry
            rows_at = pl.ds(pl.multiple_of(i * tq, tq), tq)
            qb = q_ref[rows_at, :]
            do = do_ref[rows_at, :]
            sc = lax.dot_general(qb, kb, (((1,), (1,)), ((), ())), preferred_element_type=F32) * scale
            if diag:
                rows = lax.broadcasted_iota(jnp.int32, (tq, tk), 0)
                cols = lax.broadcasted_iota(jnp.int32, (tq, tk), 1)
                sc = jnp.where(rows >= cols, sc, -jnp.inf)
            p = jnp.exp(sc - lse_ref[rows_at, :])
            dob = do.astype(BF16)
            dv_acc = dv_acc + lax.dot_general(p.astype(BF16), dob, (((0,), (0,)), ((), ())),
                                              preferred_element_type=F32)
            dp = lax.dot_general(dob, vb, (((1,), (1,)), ((), ())), preferred_element_type=F32)
            delta = jnp.sum(do * o_ref[rows_at, :], axis=-1, keepdims=True)
            ds = (p * (dp - delta) * scale).astype(BF16)
            dk_acc = dk_acc + lax.dot_general(ds, qb, (((0,), (0,)), ((), ())), preferred_element_type=F32)
            dq_ref[rows_at, :] += jnp.dot(ds, kb, preferred_element_type=F32)
            return dk_acc, dv_acc

        carry = block(j, (jnp.zeros((tk, dk), F32), jnp.zeros((tk, V_DIM), F32)), True)
        carry = lax.fori_loop(j + 1, nq, lambda i, cr: block(i, cr, False), carry)
        dk_ref[...] = carry[0]
        dv_ref[...] = carry[1]

    whole = lambda h, j: (h, 0, 0)
    return _pcall(
        body, name="flash_bwd", grid=(heads, nq),
        in_specs=[pl.BlockSpec((None, tk, dk), lambda h, j: (h, j, 0)),
                  pl.BlockSpec((None, tk, V_DIM), lambda h, j: (h, j, 0)),
                  pl.BlockSpec((None, s, dk), whole),
                  pl.BlockSpec((s, V_DIM), lambda h, j: (0, do_col0 + h)),
                  pl.BlockSpec((s, V_DIM), lambda h, j: (0, h)),
                  pl.BlockSpec((None, s, 1), whole)],
        out_specs=(pl.BlockSpec((None, s, dk), whole),
                   pl.BlockSpec((None, tk, dk), lambda h, j: (h, j, 0)),
                   pl.BlockSpec((None, tk, V_DIM), lambda h, j: (h, j, 0))),
        out_shape=(jax.ShapeDtypeStruct((heads, s, dk), F32), jax.ShapeDtypeStruct((heads, s, dk), F32),
                   jax.ShapeDtypeStruct((heads, s, V_DIM), F32)),
        compiler_params=_params("parallel", "arbitrary"),
    )(kp, v, qp, dcat, o, lse)


def _attn_bwd_post(dqp, dkp, dv, cos_t, sin_t):
    heads, s, _ = dqp.shape
    ts = _tile(s, 256, 8)

    def body(dq_ref, dk_ref, dv_ref, cos_ref, sin_ref, q_out, kv_out, kr_out):
        lane = lax.broadcasted_iota(jnp.int32, (ts, LANES), 1)
        cos_v = cos_ref[...]
        sin_v = sin_ref[...]

        def rope_t(dy):
            return dy * cos_v + _swap_halves(dy * sin_v, lane)

        kr_sum = jnp.zeros((ts, LANES), F32)
        for h in range(heads):
            q_out[:, h * LANES:(h + 1) * LANES] = dq_ref[h, :, :LANES].astype(BF16)
            kv_out[:, h * LANES:(h + 1) * LANES] = dk_ref[h, :, :LANES].astype(BF16)
            kv_out[:, (heads + h) * LANES:(heads + h + 1) * LANES] = dv_ref[h].astype(BF16)
            kr_sum = kr_sum + dk_ref[h, :, LANES:]
        for pair in range(heads // 2):
            r = jnp.where(lane < ROPE_DIM, dq_ref[2 * pair, :, LANES:], dq_ref[2 * pair + 1, :, LANES:])
            q_out[:, (heads + pair) * LANES:(heads + pair + 1) * LANES] = rope_t(r).astype(BF16)
        kr = rope_t(kr_sum)
        kr_out[...] = (kr + pltpu.roll(kr, ROPE_DIM, 1)).astype(BF16)

    wq = heads * (NOPE_DIM + ROPE_DIM)
    wkv = heads * (NOPE_DIM + V_DIM)
    tab = pl.BlockSpec((ts, LANES), lambda i: (i, 0))
    return _pcall(
        body, name="attn_bwd_post", grid=(s // ts,),
        in_specs=[pl.BlockSpec((heads, ts, 2 * LANES), lambda i: (0, i, 0)),
                  pl.BlockSpec((heads, ts, 2 * LANES), lambda i: (0, i, 0)),
                  pl.BlockSpec((heads, ts, LANES), lambda i: (0, i, 0)), tab, tab],
        out_specs=(pl.BlockSpec((ts, wq), lambda i: (i, 0)), pl.BlockSpec((ts, wkv), lambda i: (i, 0)), tab),
        out_shape=(jax.ShapeDtypeStruct((s, wq), BF16), jax.ShapeDtypeStruct((s, wkv), BF16),
                   jax.ShapeDtypeStruct((s, LANES), BF16)),
        compiler_params=_params("parallel"),
    )(dqp, dkp, dv, cos_t, sin_t)


CONV_HALO = 16


def _conv_gate(xe, cw_ref, cb_ref):
    x1 = pltpu.roll(xe, 1, 0)
    x2 = pltpu.roll(xe, 2, 0)
    return cw_ref[2:3, :] * xe + cw_ref[1:2, :] * x1 + cw_ref[0:1, :] * x2 + cb_ref[...], x1, x2


def _ffn_act_fwd(gu, conv_w, conv_b, *, d_ff, tc=256):
    s = gu.shape[0]
    tc = _tile(d_ff, tc)
    nf = d_ff // tc
    tr = _tile(s, ROW_CHUNK, CONV_HALO)
    n_chunks = s // tr

    def body(g_ref, u_ref, cw_ref, cb_ref, a_ref):
        r = pl.program_id(1)
        xe = _rows_with_halo(g_ref, r, tr, n_chunks, CONV_HALO, 0).astype(F32)
        gc = _conv_gate(xe, cw_ref, cb_ref)[0][CONV_HALO:]
        a_ref[...] = (gc * jax.nn.sigmoid(gc) * u_ref[...].astype(F32)).astype(BF16)

    return _pcall(
        body, name="ffn_act_fwd", grid=(nf, n_chunks),
        in_specs=[pl.BlockSpec((s, tc), lambda j, r: (0, j)), pl.BlockSpec((tr, tc), lambda j, r: (r, nf + j)),
                  pl.BlockSpec((CONV_TAPS, tc), lambda j, r: (0, j)), pl.BlockSpec((1, tc), lambda j, r: (0, j))],
        out_specs=pl.BlockSpec((tr, tc), lambda j, r: (r, j)),
        out_shape=jax.ShapeDtypeStruct((s, d_ff), BF16),
        compiler_params=_params("parallel", "arbitrary"),
    )(gu, gu, conv_w, conv_b.reshape(1, d_ff))


def _ffn_act_bwd(da, gu, conv_w, conv_b, *, d_ff, tc=256):
    s = gu.shape[0]
    tc = _tile(d_ff, tc)
    nf = d_ff // tc
    tr = _tile(s, ROW_CHUNK, CONV_HALO)
    n_chunks = s // tr
    rows = tr + 2 * CONV_HALO
    main = slice(CONV_HALO, CONV_HALO + tr)

    def body(da_ref, g_ref, u_ref, cw_ref, cb_ref, dg_ref, du_ref, dcw_ref, dcb_ref):
        r = pl.program_id(1)
        xe = _rows_with_halo(g_ref, r, tr, n_chunks, CONV_HALO, CONV_HALO).astype(F32)
        dae = _rows_with_halo(da_ref, r, tr, n_chunks, CONV_HALO, CONV_HALO).astype(F32)
        ue = _rows_with_halo(u_ref, r, tr, n_chunks, CONV_HALO, CONV_HALO).astype(F32)
        gc, x1, x2 = _conv_gate(xe, cw_ref, cb_ref)
        sg = jax.nn.sigmoid(gc)
        du_ref[...] = (dae * gc * sg)[main].astype(BF16)
        dgc = dae * ue * sg * (1.0 + gc * (1.0 - sg))
        dg = (cw_ref[2:3, :] * dgc + cw_ref[1:2, :] * pltpu.roll(dgc, rows - 1, 0)
              + cw_ref[0:1, :] * pltpu.roll(dgc, rows - 2, 0))
        dg_ref[...] = dg[main].astype(BF16)
        dgc_m = dgc[main]
        dcb = jnp.sum(dgc_m, axis=0, keepdims=True)
        dcw = jnp.concatenate([jnp.sum(dgc_m * x2[main], axis=0, keepdims=True),
                               jnp.sum(dgc_m * x1[main], axis=0, keepdims=True),
                               jnp.sum(dgc_m * xe[main], axis=0, keepdims=True)], axis=0)

        @pl.when(r == 0)
        def _():
            dcb_ref[...] = dcb
            dcw_ref[...] = dcw

        @pl.when(r > 0)
        def _():
            dcb_ref[...] += dcb
            dcw_ref[...] += dcw

    col = pl.BlockSpec((s, tc), lambda j, r: (0, j))
    out = pl.BlockSpec((tr, tc), lambda j, r: (r, j))
    return _pcall(
        body, name="ffn_act_bwd", grid=(nf, n_chunks),
        in_specs=[col, col, pl.BlockSpec((s, tc), lambda j, r: (0, nf + j)),
                  pl.BlockSpec((CONV_TAPS, tc), lambda j, r: (0, j)), pl.BlockSpec((1, tc), lambda j, r: (0, j))],
        out_specs=(out, out, pl.BlockSpec((CONV_TAPS, tc), lambda j, r: (0, j)),
                   pl.BlockSpec((1, tc), lambda j, r: (0, j))),
        out_shape=(jax.ShapeDtypeStruct((s, d_ff), BF16), jax.ShapeDtypeStruct((s, d_ff), BF16),
                   jax.ShapeDtypeStruct((CONV_TAPS, d_ff), F32), jax.ShapeDtypeStruct((1, d_ff), F32)),
        compiler_params=_params("parallel", "arbitrary"),
    )(da, gu, gu, conv_w, conv_b.reshape(1, d_ff))


def _ple_fwd(h, z, e):
    s, d = h.shape
    ts = _tile(s, 512, 8)

    def body(h_ref, z_ref, e_ref, o_ref):
        o_ref[...] = h_ref[...] + e_ref[...] * jax.nn.sigmoid(z_ref[...])

    row = pl.BlockSpec((ts, d), lambda i: (i, 0))
    return _pcall(body, name="ple_fwd", grid=(s // ts,), in_specs=[row, row, row], out_specs=row,
                  out_shape=jax.ShapeDtypeStruct((s, d), F32), compiler_params=_params("parallel"))(h, z, e)


def _ple_bwd(dh, z, e, dep=None):
    s, d = dh.shape
    ts = _tile(s, 512, 8)

    def body(dh_ref, z_ref, e_ref, *rest):
        de_ref, dz_ref = rest[-2:]
        gt = jax.nn.sigmoid(z_ref[...])
        dhv = dh_ref[...]
        de_ref[...] = (dhv * gt).astype(BF16)
        dz_ref[...] = (dhv * e_ref[...] * gt * (1.0 - gt)).astype(BF16)

    row = pl.BlockSpec((ts, d), lambda i: (i, 0))
    deps = [] if dep is None else [dep]
    return _pcall(body, name="ple_bwd", grid=(s // ts,), in_specs=[row, row, row] + [HBM_SPEC] * len(deps),
                  out_specs=(row, row),
                  out_shape=(jax.ShapeDtypeStruct((s, d), BF16), jax.ShapeDtypeStruct((s, d), BF16)),
                  compiler_params=_params("parallel"))(dh, z, e, *deps)


def _loss_head(h, g, target):
    s, d = h.shape
    ts = _tile(s, 512, 8)

    def body(h_ref, g_ref, t_ref, loss_ref, dh_ref, dg_ref):
        i = pl.program_id(0)
        x = h_ref[...]
        gv = g_ref[...]
        r = lax.rsqrt(jnp.mean(x * x, axis=-1, keepdims=True) + NORM_EPS)
        nh = x * r
        err = nh * gv - t_ref[...]
        part_loss = 0.5 * jnp.sum(jnp.mean(err * err, axis=-1, keepdims=True), axis=0, keepdims=True)
        dy = err * (1.0 / d)
        gd = dy * gv
        dh_ref[...] = (gd - nh * jnp.mean(gd * nh, axis=-1, keepdims=True)) * r
        part_g = jnp.sum(dy * nh, axis=0, keepdims=True)
        part_l = jnp.broadcast_to(part_loss, (1, LANES))

        @pl.when(i == 0)
        def _():
            dg_ref[...] = part_g
            loss_ref[...] = part_l

        @pl.when(i > 0)
        def _():
            dg_ref[...] += part_g
            loss_ref[...] += part_l

    row = pl.BlockSpec((ts, d), lambda i: (i, 0))
    vec = pl.BlockSpec((1, d), lambda i: (0, 0))
    return _pcall(
        body, name="loss_head", grid=(s // ts,), in_specs=[row, vec, row],
        out_specs=(pl.BlockSpec((1, LANES), lambda i: (0, 0)), row, vec),
        out_shape=(jax.ShapeDtypeStruct((1, LANES), F32), jax.ShapeDtypeStruct((s, d), F32),
                   jax.ShapeDtypeStruct((1, d), F32)),
        compiler_params=_params("arbitrary"),
    )(h, g.reshape(1, d), target)


HBM_SPEC = pl.BlockSpec(memory_space=pl.ANY)


def _flat_index(px, py, pc):
    return 4 * px + 2 * py + pc


def _all_gather(shards, *, name):
    n = len(shards)

    def body(*refs):
        ins, outs = refs[:n], refs[n:2 * n]
        send_sems, recv_sems, local_sems = refs[2 * n:]
        x, y, c = lax.axis_index("x"), lax.axis_index("y"), lax.axis_index("c")
        me, sibling = (x, y, c), (x, y, 1 - c)
        chips = [(1 - x, y), (x, 1 - y), (1 - x, 1 - y)]

        def copy(a, k, block, to, src=None):
            slot = outs[a].at[_flat_index(*block)]
            return pltpu.make_async_remote_copy(
                src_ref=slot if src is None else src, dst_ref=slot,
                send_sem=send_sems.at[a, k], recv_sem=recv_sems.at[a, k],
                device_id=to, device_id_type=pl.DeviceIdType.MESH)

        mine, first, passed = [], [], []
        for a in range(n):
            cp = pltpu.make_async_copy(ins[a], outs[a].at[_flat_index(*me)], local_sems.at[a])
            cp.start()
            mine.append(cp)
            first.append(copy(a, 0, me, sibling, src=ins[a]))
            first += [copy(a, 1 + j, me, (*chip, c), src=ins[a]) for j, chip in enumerate(chips)]
        for cp in first:
            cp.start()
        for j, chip in enumerate(chips):
            for a in range(n):
                copy(a, 1 + j, (*chip, c), me).wait_recv()
                fwd = copy(a, 4 + j, (*chip, c), sibling)
                fwd.start()
                passed.append(fwd)
        for a in range(n):
            copy(a, 0, sibling, me).wait_recv()
            for j, chip in enumerate(chips):
                copy(a, 4 + j, (*chip, 1 - c), me).wait_recv()
        for cp in first + passed:
            cp.wait_send()
        for cp in mine:
            cp.wait()

    return _pcall(
        body, name=name,
        in_specs=[HBM_SPEC] * n, out_specs=[HBM_SPEC] * n,
        out_shape=[jax.ShapeDtypeStruct((N_DEV,) + a.shape, a.dtype) for a in shards],
        scratch_shapes=[pltpu.SemaphoreType.DMA((n, 7)), pltpu.SemaphoreType.DMA((n, 7)),
                        pltpu.SemaphoreType.DMA((n,))],
    )(*shards)


HBM_ONLY = pl.BlockSpec(memory_space=pltpu.HBM)
SEM_SPEC = pl.BlockSpec(memory_space=pltpu.SEMAPHORE)
N_PEERS = N_DEV - 1
PEER_FLIPS = ((0, 0, 1), (1, 0, 0), (0, 1, 0), (1, 1, 0), (1, 0, 1), (0, 1, 1), (1, 1, 1))


def _exchange_refs(gather, src_refs, land_refs, send_sems, recv_sems):
    x, y, c = lax.axis_index("x"), lax.axis_index("y"), lax.axis_index("c")
    me = _flat_index(x, y, c)
    peers = [(x ^ fx, y ^ fy, c ^ fc) for fx, fy, fc in PEER_FLIPS]

    def out_copy(a, k):
        src = src_refs[a] if gather else src_refs[a].at[_flat_index(*peers[k])]
        return pltpu.make_async_remote_copy(
            src_ref=src, dst_ref=land_refs[a].at[me], send_sem=send_sems.at[a * N_PEERS + k],
            recv_sem=recv_sems.at[a * N_PEERS + k], device_id=peers[k], device_id_type=pl.DeviceIdType.MESH)

    def in_copy(a, k):
        src = src_refs[a] if gather else src_refs[a].at[me]
        return pltpu.make_async_remote_copy(
            src_ref=src, dst_ref=land_refs[a].at[_flat_index(*peers[k])], send_sem=send_sems.at[a * N_PEERS + k],
            recv_sem=recv_sems.at[a * N_PEERS + k], device_id=peers[k], device_id_type=pl.DeviceIdType.MESH)

    return out_copy, in_copy


def _exchange_start(srcs, *, gather, name, dep):
    n = len(srcs)
    lands = [lax.empty((N_DEV,) + a.shape if gather else a.shape, a.dtype) for a in srcs]

    def body(*refs):
        src_refs, land_refs = refs[:n], refs[n:2 * n]
        send_sems, recv_sems = refs[2 * n + 1], refs[2 * n + 2]
        token = refs[-1]
        out_copy, _ = _exchange_refs(gather, src_refs, land_refs, send_sems, recv_sems)
        for k in range(N_PEERS):
            for a in range(n):
                out_copy(a, k).start()
        token[...] = jnp.zeros_like(token)

    hbm = lambda a: pltpu.with_memory_space_constraint(a, pltpu.HBM)
    return _pcall(
        body, name=name,
        out_shape=(pltpu.SemaphoreType.DMA((n * N_PEERS,)), pltpu.SemaphoreType.DMA((n * N_PEERS,)),
                   *[pltpu.HBM(a.shape, a.dtype) for a in srcs], *[pltpu.HBM(a.shape, a.dtype) for a in lands],
                   jax.ShapeDtypeStruct((8, LANES), F32)),
        in_specs=[HBM_ONLY] * (2 * n) + [HBM_SPEC],
        out_specs=(SEM_SPEC, SEM_SPEC, *[HBM_ONLY] * (2 * n), pl.BlockSpec(memory_space=pltpu.VMEM)),
        input_output_aliases={i: 2 + i for i in range(2 * n)},
        compiler_params=pltpu.CompilerParams(has_side_effects=pltpu.SideEffectType.DATAFLOW_SIDE_EFFECTING),
    )(*[hbm(a) for a in srcs], *[hbm(a) for a in lands], dep)


def _exchange_wait(started, after, *, gather, name):
    send_sems, recv_sems = started[0], started[1]
    n = (len(started) - 3) // 2
    srcs, lands = started[2:2 + n], started[2 + n:2 + 2 * n]

    def body(*refs):
        src_refs, land_refs = refs[:n], refs[n:2 * n]
        s_sems, r_sems = refs[2 * n], refs[2 * n + 1]
        out_copy, in_copy = _exchange_refs(gather, src_refs, land_refs, s_sems, r_sems)
        for k in range(N_PEERS):
            for a in range(n):
                out_copy(a, k).wait_send()
                in_copy(a, k).wait_recv()

    res = _pcall(
        body, name=name,
        out_shape=tuple(pltpu.HBM(a.shape, a.dtype) for a in (*srcs, *lands)),
        in_specs=[HBM_ONLY] * (2 * n) + [SEM_SPEC, SEM_SPEC, HBM_SPEC],
        out_specs=tuple([HBM_ONLY] * (2 * n)),
        input_output_aliases={i: i for i in range(2 * n)},
        compiler_params=pltpu.CompilerParams(has_side_effects=pltpu.SideEffectType.DATAFLOW_SIDE_EFFECTING),
    )(*srcs, *lands, send_sems, recv_sems, after)
    return _with_own_slot(res[:n], res[n:], gather)


def _with_own_slot(srcs, lands, gather):
    me = _flat_index(lax.axis_index("x"), lax.axis_index("y"), lax.axis_index("c"))
    full = []
    for src, land in zip(srcs, lands):
        own = src if gather else lax.dynamic_index_in_dim(src, me, 0, keepdims=False)
        full.append(lax.dynamic_update_slice(land, own[None], (me,) + (0,) * own.ndim))
    return full


def _adam_math(g, w, m, v):
    m = ADAM_B1 * m + (1.0 - ADAM_B1) * g
    v = ADAM_B2 * v + (1.0 - ADAM_B2) * jnp.square(g)
    m_hat = m / (1.0 - ADAM_B1 ** ADAM_STEP)
    v_hat = v / (1.0 - ADAM_B2 ** ADAM_STEP)
    delta = -ADAM_LR * (m_hat / (jnp.sqrt(v_hat) + ADAM_EPS) + ADAM_WD * w)
    return delta, m, v


def _adamw(contribs, w, m, v, *, name):
    layers = len(contribs)
    _, r, c = contribs[0].shape
    tr = _tile(r, max(8, (256 * 1024 // c) // 8 * 8), 8)

    def body(*refs):
        g_refs = refs[:layers]
        w_ref, m_ref, v_ref, go_ref, d_ref, mo_ref, vo_ref = refs[layers:]
        layer = pl.program_id(0)
        for l2 in range(layers):

            @pl.when(layer == l2)
            def _(g_ref=g_refs[l2]):
                g = g_ref[0].astype(F32)
                for k in range(1, N_DEV):
                    g = g + g_ref[k].astype(F32)
                delta, m_new, v_new = _adam_math(g, w_ref[...], m_ref[...], v_ref[...])
                go_ref[...] = g
                d_ref[...] = delta
                mo_ref[...] = m_new
                vo_ref[...] = v_new

    g_specs = [pl.BlockSpec((N_DEV, tr, c), lambda l, i, l2=l2: (0, jnp.where(l == l2, i, 0), 0))
               for l2 in range(layers)]
    blk = pl.BlockSpec((None, tr, c), lambda l, i: (l, i, 0))
    out = jax.ShapeDtypeStruct((layers, r, c), F32)
    return _pcall(
        body, name=name, grid=(layers, r // tr), in_specs=g_specs + [blk, blk, blk],
        out_specs=(blk, blk, blk, blk), out_shape=(out, out, out, out),
        compiler_params=_params("arbitrary", "arbitrary"),
    )(*contribs, w, m, v)


def _heads_split(w, heads, first, second):
    k = w.shape[0]
    w3 = w.reshape(k, heads, first + second)
    return jnp.concatenate([w3[:, :, :first].reshape(k, heads * first),
                            w3[:, :, first:].reshape(k, heads * second)], axis=1)


def _heads_join(w, heads, first, second):
    k = w.shape[0]
    a = w[:, :heads * first].reshape(k, heads, first)
    b = w[:, heads * first:].reshape(k, heads, second)
    return jnp.concatenate([a, b], axis=2).reshape(k, heads * (first + second))


def _full_from_gathered(kind, g):
    if kind == "col":
        return jnp.transpose(g, (1, 0, 2)).reshape(g.shape[1], N_DEV * g.shape[2])
    if kind == "row":
        return g.reshape(N_DEV * g.shape[1], g.shape[2])
    return jnp.transpose(g, (1, 0, 2, 3)).reshape(g.shape[1], N_DEV * g.shape[2], g.shape[3])


def _blocks_from_full(kind, f):
    if kind == "col":
        k, n = f.shape
        return jnp.transpose(f.reshape(k, N_DEV, n // N_DEV), (1, 0, 2))
    if kind == "row":
        k, n = f.shape
        return f.reshape(N_DEV, k // N_DEV, n)
    g, c_in, c = f.shape
    return jnp.transpose(f.reshape(g, N_DEV, c_in // N_DEV, c), (1, 0, 2, 3))


def _rope_tables(positions):
    inv_freq = 1.0 / (ROPE_THETA ** (jnp.arange(0, ROPE_DIM, 2, dtype=F32) / ROPE_DIM))
    ang = positions.astype(F32)[:, None] * inv_freq
    cos, sin = jnp.cos(ang), jnp.sin(ang)
    return jnp.concatenate([cos, cos, cos, cos], axis=-1), jnp.concatenate([-sin, sin, -sin, sin], axis=-1)


def _layer_fwd(h0, p_i, fetch, rep, tabs, dims, dep=None):
    heads, d_pool, q_lora, d_ff = dims["heads"], dims["d_pool"], dims["q_lora"], dims["d_ff"]
    c = d_pool // POOL_GROUPS
    cos_t, sin_t = tabs
    scale = 1.0 / math.sqrt(NOPE_DIM + ROPE_DIM)
    n1 = _rms_fwd(h0, rep["norm_mix_g"], name="rms_mix_fwd", dep=dep)
    w = dict(fetch("mix", n1))
    u = _mm(n1, w["w_in"], name="mm_in_fwd")
    y_pool, diff = _pool_fwd(u, w["pool_w"], rep["pool_scale"], c=c)
    nq = _rms_fwd(u, rep["q_norm_g"], name="rms_q_fwd", col_block=d_pool // q_lora)
    nkv = _rms_fwd(u, rep["kv_norm_g"], name="rms_kv_fwd", col_block=d_pool // q_lora + 1)
    q = _mm(nq, w["w_uq"], name="mm_uq_fwd")
    kv = _mm(nkv, w["w_ukv"], name="mm_ukv_fwd")
    kr = u[:, d_pool + 2 * q_lora:]
    kr2 = jnp.concatenate([kr, kr], axis=-1)
    qp, kp, v = _qkv_prep(q, kv, kr2, cos_t, sin_t, heads=heads)
    o, lse = _flash_fwd(qp, kp, v, scale=scale)
    t = _mm(y_pool, w["w_out"], name="mm_out_pool_fwd", add=h0, b_row=(0, d_pool))
    h1 = _mm(o, w["w_out"], name="mm_out_att_fwd", add=t, b_row=(d_pool, None))
    n2 = _rms_fwd(h1, rep["norm_ffn_g"], name="rms_ffn_fwd")
    w.update(fetch("up", n2))
    gu = _mm(n2, w["w_up"], name="mm_up_fwd", out_dtype=BF16, b_blocks=(0, N_DEV))
    a = _ffn_act_fwd(gu, w["conv_w"], rep["conv_b"], d_ff=d_ff)
    w.update(fetch("down", a))
    h2 = _mm(a, w["w_down"], name="mm_down_fwd", add=h1)
    n3 = _rms_fwd(h2, rep["norm_ple_g"], name="rms_ple_fwd")
    z = _mm(n3, w["w_ple_gate"], name="mm_pgate_fwd")
    e = _mm(p_i, w["w_ple"], name="mm_ple_fwd")
    h3 = _ple_fwd(h2, z, e)
    saved = dict(h0=h0, n1=n1, u=u, y_pool=y_pool, diff=diff, nq=nq, nkv=nkv, qp=qp, kp=kp, v=v, o=o, lse=lse,
                 h1=h1, n2=n2, gu=gu, a=a, h2=h2, n3=n3, z=z, e=e)
    return h3, saved, w


def _layer_bwd(dh3, p_i, w, rep, tabs, dims, sv, dep=None, hooks=None):
    hooks = hooks or {}
    heads, d_pool, q_lora, d_ff = dims["heads"], dims["d_pool"], dims["q_lora"], dims["d_ff"]
    c = d_pool // POOL_GROUPS
    cos_t, sin_t = tabs
    scale = 1.0 / math.sqrt(NOPE_DIM + ROPE_DIM)
    gr = {}
    de, dz = _ple_bwd(dh3, sv["z"], sv["e"], dep)
    gr["w_ple"] = _mm(p_i, de, name="mm_ple_dw", ta=True, out_dtype=BF16)
    gr["w_ple_gate"] = _mm(sv["n3"], dz, name="mm_pgate_dw", ta=True, out_dtype=BF16)
    dn3 = _mm(dz, w["w_ple_gate"], name="mm_pgate_dx", tb=True)
    dh2, dh2_mm, gr["norm_ple_g"] = _rms_bwd(dn3, sv["h2"], rep["norm_ple_g"], name="rms_ple_bwd", res=dh3,
                                             matmul_copy=True)
    gr["w_down"] = _mm(sv["a"], dh2_mm, name="mm_down_dw", ta=True, out_dtype=BF16)
    dep_down = hooks["down"](dh2, gr) if "down" in hooks else None
    da = _mm(dh2_mm, w["w_down"], name="mm_down_dx", tb=True, out_dtype=BF16, dep=dep_down)
    dgate, dup, gr["conv_w"], gr["conv_b"] = _ffn_act_bwd(da, sv["gu"], w["conv_w"], rep["conv_b"], d_ff=d_ff)
    half, per = N_DEV // 2, w["w_up"].shape[2]
    dw_gate = _mm(sv["n2"], dgate, name="mm_up_gate_dw", ta=True, out_dtype=BF16, out_blocks=(N_DEV, 0, per))
    gr["w_up"] = _mm(sv["n2"], dup, name="mm_up_up_dw", ta=True, out_dtype=BF16, out_blocks=(N_DEV, half, per),
                     out_init=dw_gate)
    dep_up = hooks["up"](gr["w_up"], gr) if "up" in hooks else None
    dn2 = _mm(dgate, w["w_up"], name="mm_up_gate_dx", tb=True, b_blocks=(0, half), dep=dep_up)
    dn2 = _mm(dup, w["w_up"], name="mm_up_up_dx", tb=True, b_blocks=(half, half), add=dn2)
    dh1, dh1_mm, gr["norm_ffn_g"] = _rms_bwd(dn2, sv["h1"], rep["norm_ffn_g"], name="rms_ffn_bwd", res=dh2,
                                             matmul_copy=True)
    dw_out_pool = _mm(sv["y_pool"], dh1_mm, name="mm_out_pool_dw", ta=True, out_dtype=BF16)
    dw_out_att = _mm(sv["o"], dh1_mm, name="mm_out_att_dw", ta=True, out_dtype=BF16)
    gr["w_out"] = jnp.concatenate([dw_out_pool, dw_out_att], axis=0)
    dcat = _mm(dh1_mm, w["w_out"], name="mm_out_dx", tb=True)
    do_col0 = d_pool // V_DIM
    dqp, dkp, dv = _flash_bwd(sv["qp"], sv["kp"], sv["v"], dcat, sv["o"], sv["lse"], scale=scale, do_col0=do_col0)
    dq, dkv, dkr2 = _attn_bwd_post(dqp, dkp, dv, cos_t, sin_t)
    gr["w_uq"] = _mm(sv["nq"], dq, name="mm_uq_dw", ta=True, out_dtype=BF16)
    gr["w_ukv"] = _mm(sv["nkv"], dkv, name="mm_ukv_dw", ta=True, out_dtype=BF16)
    dnq = _mm(dq, w["w_uq"], name="mm_uq_dx", tb=True)
    dnkv = _mm(dkv, w["w_ukv"], name="mm_ukv_dx", tb=True)
    dcq, gr["q_norm_g"] = _rms_bwd(dnq, sv["u"], rep["q_norm_g"], name="rms_q_bwd",
                                   col_block=d_pool // q_lora, out_dtype=BF16)
    dckv, gr["kv_norm_g"] = _rms_bwd(dnkv, sv["u"], rep["kv_norm_g"], name="rms_kv_bwd",
                                     col_block=d_pool // q_lora + 1, out_dtype=BF16)
    du_pool, gr["pool_w"], gr["pool_scale"] = _pool_bwd(dcat, sv["diff"], w["pool_w"], rep["pool_scale"], c=c)
    du = jnp.concatenate([du_pool, dcq, dckv, dkr2[:, :ROPE_DIM]], axis=-1)
    gr["w_in"] = _mm(sv["n1"], du, name="mm_in_dw", ta=True, out_dtype=BF16)
    dn1 = _mm(du, w["w_in"], name="mm_in_dx", tb=True)
    dh0, gr["norm_mix_g"] = _rms_bwd(dn1, sv["h0"], rep["norm_mix_g"], name="rms_mix_bwd", res=dh1)
    return dh0, gr


def _as2d(a):
    return a.reshape(a.shape[0], -1, a.shape[-1])


def kernel(x, p, positions, norm_mix_g, w_in, pool_w, pool_scale, q_norm_g, w_uq, kv_norm_g, w_ukv, w_out, norm_ffn_g, w_up, conv_w, conv_b, w_down, norm_ple_g, w_ple, w_ple_gate, final_norm_g, loss_target, m_norm_mix_g, m_w_in, m_pool_w, m_pool_scale, m_q_norm_g, m_w_uq, m_kv_norm_g, m_w_ukv, m_w_out, m_norm_ffn_g, m_w_up, m_conv_w, m_conv_b, m_w_down, m_norm_ple_g, m_w_ple, m_w_ple_gate, m_final_norm_g, v_norm_mix_g, v_w_in, v_pool_w, v_pool_scale, v_q_norm_g, v_w_uq, v_kv_norm_g, v_w_ukv, v_w_out, v_norm_ffn_g, v_w_up, v_conv_w, v_conv_b, v_w_down, v_norm_ple_g, v_w_ple, v_w_ple_gate, v_final_norm_g):
    weights = dict(norm_mix_g=norm_mix_g, w_in=w_in, pool_w=pool_w, pool_scale=pool_scale, q_norm_g=q_norm_g,
                   w_uq=w_uq, kv_norm_g=kv_norm_g, w_ukv=w_ukv, w_out=w_out, norm_ffn_g=norm_ffn_g, w_up=w_up,
                   conv_w=conv_w, conv_b=conv_b, w_down=w_down, norm_ple_g=norm_ple_g, w_ple=w_ple,
                   w_ple_gate=w_ple_gate, final_norm_g=final_norm_g)
    m_in = dict(norm_mix_g=m_norm_mix_g, w_in=m_w_in, pool_w=m_pool_w, pool_scale=m_pool_scale, q_norm_g=m_q_norm_g,
                w_uq=m_w_uq, kv_norm_g=m_kv_norm_g, w_ukv=m_w_ukv, w_out=m_w_out, norm_ffn_g=m_norm_ffn_g,
                w_up=m_w_up, conv_w=m_conv_w, conv_b=m_conv_b, w_down=m_w_down, norm_ple_g=m_norm_ple_g,
                w_ple=m_w_ple, w_ple_gate=m_w_ple_gate, final_norm_g=m_final_norm_g)
    v_in = dict(norm_mix_g=v_norm_mix_g, w_in=v_w_in, pool_w=v_pool_w, pool_scale=v_pool_scale, q_norm_g=v_q_norm_g,
                w_uq=v_w_uq, kv_norm_g=v_kv_norm_g, w_ukv=v_w_ukv, w_out=v_w_out, norm_ffn_g=v_norm_ffn_g,
                w_up=v_w_up, conv_w=v_conv_w, conv_b=v_conv_b, w_down=v_w_down, norm_ple_g=v_norm_ple_g,
                w_ple=v_w_ple, w_ple_gate=v_w_ple_gate, final_norm_g=v_final_norm_g)

    depth = w_in.shape[0]
    s, d_model = x.shape[1], x.shape[2]
    d_pool = pool_scale.shape[-1]
    q_lora = q_norm_g.shape[-1]
    d_ff = conv_b.shape[-1]
    heads = (w_uq.shape[-1] * N_DEV) // (NOPE_DIM + ROPE_DIM)
    dims = dict(heads=heads, d_pool=d_pool, q_lora=q_lora, d_ff=d_ff)

    groups = {"mix": ("w_in", "pool_w", "w_uq", "w_ukv", "w_out"), "up": ("w_up", "conv_w"),
              "down": ("w_down", "w_ple", "w_ple_gate")}

    def group_shards(i, group):
        return [weights[n][i] if n == "conv_w" else weights[n][i].astype(BF16) for n in groups[group]]

    def start_weights(i, group, dep):
        return _exchange_start(group_shards(i, group), gather=True, name=f"weights_{group}_start_{i}", dep=dep)

    def full_group(group, gathered_g):
        w = {n: g if n == "w_up" else _full_from_gathered(SHARD_KIND[n], g) for n, g in zip(groups[group], gathered_g)}
        if group == "mix":
            w["w_uq"] = _heads_split(w["w_uq"], heads, NOPE_DIM, ROPE_DIM)
            w["w_ukv"] = _heads_split(w["w_ukv"], heads, NOPE_DIM, V_DIM)
        return w

    tabs = _rope_tables(positions[0])

    arrived = {(0, "mix"): _all_gather(group_shards(0, "mix"), name="weights_mix_gather_0")}
    travelling = {}
    token = arrived[(0, "mix")][0]
    for group in ("up", "down"):
        travelling[(0, group)] = start_weights(0, group, token)
        token = travelling[(0, group)][-1]
    layer_w = []
    h = x[0]
    saved = []
    for i in range(depth):
        if i + 1 < depth:
            for group in ("mix", "up", "down"):
                travelling[(i + 1, group)] = start_weights(i + 1, group, token if i == 0 and group == "mix" else
                                                           (h if group == "mix" else token))
                token = travelling[(i + 1, group)][-1]

        def fetch(group, after, i=i):
            if (i, group) not in arrived:
                arrived[(i, group)] = _exchange_wait(travelling[(i, group)], after, gather=True,
                                                     name=f"weights_{group}_wait_{i}")
            return full_group(group, arrived[(i, group)])

        rep = {n: weights[n][i] for n in REPLICATED}
        h, sv, w = _layer_fwd(h, p[i, 0], fetch, rep, tabs, dims, dep=token if i + 1 < depth or i == 0 else None)
        layer_w.append((w, rep))
        saved.append(sv)
    loss_row, dh, g_final = _loss_head(h, final_norm_g, loss_target[0])
    loss = lax.psum(loss_row[0, 0], MESH_AXES)

    def start_grads(group, gr, dep, i):
        names = groups[group]
        blocks = [gr[n] if n == "w_up" else _blocks_from_full(SHARD_KIND[n], gr[n]).astype(BF16) for n in names]
        return _exchange_start(blocks, gather=False, name=f"grads_{group}_start_{i}", dep=dep)

    def end_grads(group, started, after, i):
        got = _exchange_wait(started, after, gather=False, name=f"grads_{group}_wait_{i}")
        received[i].update(zip(groups[group], got))

    layer_grads = [None] * depth
    received = [dict() for _ in range(depth)]
    pending = None
    for i in reversed(range(depth)):
        w, rep = layer_w[i]
        state = {}

        def on_down(dh2, gr, i=i, state=state):
            state["down"] = start_grads("down", gr, dh2, i)
            return state["down"][-1]

        def on_up(dw_up, gr, i=i, pending=pending, state=state):
            order = gr["conv_w"]
            if pending is not None:
                for group in ("down", "up", "mix"):
                    end_grads(group, pending[group], order, i + 1)
            state["up"] = start_grads("up", gr, order, i)
            return state["up"][-1]

        dh, gr = _layer_bwd(dh, p[i, 0], w, rep, tabs, dims, saved[i],
                            dep=loss.reshape(1, 1) if pending is None else pending["mix"][-1],
                            hooks={"down": on_down, "up": on_up})
        gr["w_uq"] = _heads_join(gr["w_uq"], heads, NOPE_DIM, ROPE_DIM)
        gr["w_ukv"] = _heads_join(gr["w_ukv"], heads, NOPE_DIM, V_DIM)
        layer_grads[i] = gr
        state["mix"] = start_grads("mix", gr, dh, i)
        pending = state
    grad_x = dh[None]

    out = {}

    def update(n):
        shape = weights[n].shape
        recs = [received[i][n].reshape((N_DEV, -1, shape[-1])) for i in range(depth)]
        res = _adamw(recs, _as2d(weights[n]), _as2d(m_in[n]), _as2d(v_in[n]), name="adamw_" + n)
        out[n] = tuple(r.reshape(shape) for r in res)

    end_grads("down", pending["down"], pending["mix"][-1], 0)
    end_grads("up", pending["up"], pending["mix"][-1], 0)
    for n in groups["down"] + groups["up"]:
        update(n)
    end_grads("mix", pending["mix"], out["w_up"][0], 0)
    for n in groups["mix"]:
        update(n)

    small_names = REPLICATED + ("final_norm_g",)

    def pack(get):
        rows = [jnp.stack([get(n, i).reshape(-1) for i in range(depth)]).reshape(-1) for n in REPLICATED]
        rows.append(get("final_norm_g", None).reshape(-1))
        return jnp.concatenate(rows).reshape(1, -1, LANES)

    g_small = pack(lambda n, i: g_final if i is None else layer_grads[i][n])
    w_small = pack(lambda n, i: weights[n] if i is None else weights[n][i])
    m_small = pack(lambda n, i: m_in[n] if i is None else m_in[n][i])
    v_small = pack(lambda n, i: v_in[n] if i is None else v_in[n][i])
    (g_all,) = _all_gather([g_small], name="small_grads_all_gather")
    res_small = _adamw([g_all[:, 0]], w_small, m_small, v_small, name="adamw_small")

    def unpack(flat3):
        flat = flat3.reshape(-1)
        res, off = {}, 0
        for n in REPLICATED:
            width = weights[n].shape[-1]
            res[n] = flat[off:off + depth * width].reshape(depth, width)
            off += depth * width
        res["final_norm_g"] = flat[off:off + d_model]
        return res

    small = [unpack(r) for r in res_small]
    for n in small_names:
        out[n] = tuple(small[k][n] for k in range(4))

    outs = [loss, grad_x]
    for k in range(4):
        outs += [out[n][k] for n in WEIGHT_ORDER]
    return tuple(outs)
```

```python
import math

import jax
import jax.numpy as jnp
from jax import lax
from jax.experimental import pallas as pl
from jax.experimental.pallas import tpu as pltpu

F32 = jnp.float32
BF16 = jnp.bfloat16

N_DEV = 8
MESH_AXES = ("x", "y", "c")
NOPE_DIM = 128
ROPE_DIM = 64
V_DIM = 128
POOL_GROUPS = 4
CONV_TAPS = 3
ROPE_THETA = 10000.0
NORM_EPS = 1e-6
ADAM_LR = 0.001
ADAM_B1 = 0.9
ADAM_B2 = 0.999
ADAM_EPS = 1e-08
ADAM_WD = 0.01
ADAM_STEP = 10
LANES = 128
VMEM_LIMIT_BYTES = 56 * 1024 * 1024

SHARD_KIND = {"w_in": "col", "pool_w": "pool", "w_uq": "col", "w_ukv": "col", "w_out": "row", "w_up": "col",
              "conv_w": "col", "w_down": "row", "w_ple": "col", "w_ple_gate": "row"}
REPLICATED = ("norm_mix_g", "pool_scale", "q_norm_g", "kv_norm_g", "norm_ffn_g", "conv_b", "norm_ple_g")
WEIGHT_ORDER = ("norm_mix_g", "w_in", "pool_w", "pool_scale", "q_norm_g", "w_uq", "kv_norm_g", "w_ukv", "w_out",
                "norm_ffn_g", "w_up", "conv_w", "conv_b", "w_down", "norm_ple_g", "w_ple", "w_ple_gate",
                "final_norm_g")

_pcall = pl.pallas_call


def _params(*sem):
    return pltpu.CompilerParams(dimension_semantics=sem or None, vmem_limit_bytes=VMEM_LIMIT_BYTES)


def _tile(n, pref, mult=LANES):
    if n <= pref:
        return n
    t = (pref // mult) * mult
    while t >= mult:
        if n % t == 0:
            return t
        t -= mult
    return n


MM_VMEM_BUDGET_BYTES = 44 * 1024 * 1024
MM_TILE_PREFS = (1536, 1024, 512, 256)
MM_MIN_TK = 1024
V7X_MXU_FLOPS = 850e12
V7X_SPLIT_K_SLOWDOWN = 1.3
V7X_HBM_BYTES_PER_S = 2.8e12
GRID_STEP_S = 0.35e-6


def _mm_tiles(m, n, k, a_bytes, b_bytes, o_bytes, has_add, tn_fixed=None, tk_fixed=None, a_copied=False,
              b_copied=False):
    out_bytes = o_bytes + (4 if has_add else 0)
    tm_cands = sorted({_tile(m, pref) for pref in MM_TILE_PREFS}, reverse=True)
    tn_cands = [tn_fixed] if tn_fixed else sorted({_tile(n, pref) for pref in MM_TILE_PREFS}, reverse=True)
    if tk_fixed:
        tk_cands = [tk_fixed]
    else:
        tk_cands = [k] + [t for t in range((k - 1) // LANES * LANES, MM_MIN_TK - 1, -LANES) if k % t == 0]
    best = None
    for tm in tm_cands:
        for tn in tn_cands:
            for tk in tk_cands:
                vmem = 2 * (tm * tk * a_bytes + tk * tn * b_bytes) + 2 * tm * tn * out_bytes
                vmem += tm * tn * 4 if tk < k else 0
                vmem += (tm * tk * 2 if a_copied else 0) + (tk * tn * 2 if b_copied else 0)
                if vmem > MM_VMEM_BUDGET_BYTES:
                    continue
                steps = (m // tm) * (n // tn) * (k // tk)
                hbm = a_bytes * m * k * (1 if tk == k else n // tn) + b_bytes * k * n * (m // tm) + out_bytes * m * n
                mxu = 2 * m * n * k / V7X_MXU_FLOPS * (1.0 if tk == k else V7X_SPLIT_K_SLOWDOWN)
                cost = max(mxu, hbm / V7X_HBM_BYTES_PER_S) + steps * GRID_STEP_S
                if best is None or cost < best[0]:
                    best = (cost, tm, tn, tk)
    assert best is not None, (m, n, k)
    return best[1:]


def _mm(a, b, *, name, ta=False, tb=False, add=None, out_dtype=F32, b_row=(0, None), b_blocks=None,
        out_blocks=None, out_init=None, dep=None):
    m = a.shape[1] if ta else a.shape[0]
    kdim = a.shape[0] if ta else a.shape[1]
    tn_fixed = tk_fixed = None
    k_off_b = 0
    if b_blocks is None:
        n_b, k_b = (b.shape if tb else b.shape[::-1])
        k_off_b, kb_sz = b_row
        kb_sz = k_b - k_off_b if kb_sz is None else kb_sz
        assert kb_sz == kdim, (name, kb_sz, kdim)
        n = n_b
    else:
        first_b, count_b = b_blocks
        per_b = b.shape[2]
        if tb:
            n = b.shape[1]
            assert kdim == count_b * per_b, name
            tk_fixed = per_b
        else:
            n = count_b * per_b
            assert kdim == b.shape[1], name
            tn_fixed = per_b
    if out_blocks is not None:
        nb_out, first_o, tn_fixed = out_blocks
    tm, tn, tk = _mm_tiles(m, n, kdim, a.dtype.itemsize, b.dtype.itemsize, jnp.dtype(out_dtype).itemsize,
                           add is not None, tn_fixed, tk_fixed, a_copied=ta or a.dtype != BF16,
                           b_copied=tb or b.dtype != BF16)
    assert k_off_b % tk == 0 and kdim % tk == 0 and n % tn == 0 and m % tm == 0, name
    kb0 = k_off_b // tk
    nk = kdim // tk
    dims = (((0 if ta else 1,), (1 if tb else 0,)), ((), ()))

    def body(*refs):
        a_ref, b_ref = refs[0], refs[1]
        add_ref = refs[2] if add is not None else None
        o_ref = refs[n_in]
        part = lax.dot_general(a_ref[...].astype(BF16), b_ref[...].astype(BF16), dims, preferred_element_type=F32)

        def finish(r):
            if add_ref is not None:
                r = r + add_ref[...].astype(F32)
            o_ref[...] = r.astype(out_dtype)

        if nk == 1:
            finish(part)
            return
        acc = refs[n_in + 1]
        k = pl.program_id(2)

        @pl.when(k == 0)
        def _():
            acc[...] = part

        @pl.when(jnp.logical_and(k > 0, k < nk - 1))
        def _():
            acc[...] += part

        @pl.when(k == nk - 1)
        def _():
            finish(acc[...] + part)

    if ta:
        a_spec = pl.BlockSpec((tk, tm), lambda i, j, k: (k, i))
    else:
        a_spec = pl.BlockSpec((tm, tk), lambda i, j, k: (i, k))
    if b_blocks is not None and tb:
        b_spec = pl.BlockSpec((None, tn, tk), lambda i, j, k: (k + first_b, j, 0))
    elif b_blocks is not None:
        b_spec = pl.BlockSpec((None, tk, tn), lambda i, j, k: (j + first_b, k, 0))
    elif tb:
        b_spec = pl.BlockSpec((tn, tk), lambda i, j, k: (j, k + kb0))
    else:
        b_spec = pl.BlockSpec((tk, tn), lambda i, j, k: (k + kb0, j))
    in_specs = [a_spec, b_spec]
    args = [a, b]
    if add is not None:
        in_specs.append(pl.BlockSpec((tm, tn), lambda i, j, k: (i, j)))
        args.append(add)
    aliases = {}
    if out_init is not None:
        aliases = {len(args): 0}
        in_specs.append(HBM_SPEC)
        args.append(out_init)
    if dep is not None:
        in_specs.append(HBM_SPEC)
        args.append(dep)
    n_in = len(args)
    if out_blocks is None:
        out_spec = pl.BlockSpec((tm, tn), lambda i, j, k: (i, j))
        out_shape = jax.ShapeDtypeStruct((m, n), out_dtype)
    else:
        out_spec = pl.BlockSpec((None, tm, tn), lambda i, j, k: (j + first_o, i, 0))
        out_shape = jax.ShapeDtypeStruct((nb_out, m, tn), out_dtype)
    return _pcall(
        body, name=name, grid=(m // tm, n // tn, nk), in_specs=in_specs, out_specs=out_spec, out_shape=out_shape,
        scratch_shapes=[pltpu.VMEM((tm, tn), F32)] if nk > 1 else [], input_output_aliases=aliases,
        compiler_params=_params("parallel", "parallel", "arbitrary"),
    )(*args)


def _rms_fwd(h, g, *, name, col_block=0, dep=None):
    s = h.shape[0]
    d = g.shape[-1]
    ts = _tile(s, 512, 8)

    def body(h_ref, g_ref, *rest):
        n_ref = rest[-1]
        x = h_ref[...]
        r = lax.rsqrt(jnp.mean(x * x, axis=-1, keepdims=True) + NORM_EPS)
        n_ref[...] = (x * r * g_ref[...]).astype(BF16)

    deps = [] if dep is None else [dep]
    return _pcall(
        body, name=name, grid=(s // ts,),
        in_specs=[pl.BlockSpec((ts, d), lambda i: (i, col_block)), pl.BlockSpec((1, d), lambda i: (0, 0))]
        + [HBM_SPEC] * len(deps),
        out_specs=pl.BlockSpec((ts, d), lambda i: (i, 0)),
        out_shape=jax.ShapeDtypeStruct((s, d), BF16),
        compiler_params=_params("parallel"),
    )(h, g.reshape(1, d), *deps)


def _rms_bwd(dn, h, g, *, name, res=None, col_block=0, out_dtype=F32, matmul_copy=False):
    s = dn.shape[0]
    d = g.shape[-1]
    ts = _tile(s, 512, 8)

    def body(*refs):
        dn_ref, h_ref, g_ref = refs[:3]
        res_ref = refs[3] if res is not None else None
        dh_ref, dg_ref = refs[n_in], refs[-1]
        i = pl.program_id(0)
        x = h_ref[...]
        r = lax.rsqrt(jnp.mean(x * x, axis=-1, keepdims=True) + NORM_EPS)
        nh = x * r
        dnv = dn_ref[...]
        gd = dnv * g_ref[...]
        dh = (gd - nh * jnp.mean(gd * nh, axis=-1, keepdims=True)) * r
        if res_ref is not None:
            dh = dh + res_ref[...]
        dh_ref[...] = dh.astype(out_dtype)
        if matmul_copy:
            refs[n_in + 1][...] = dh.astype(BF16)
        part = jnp.sum(dnv * nh, axis=0, keepdims=True)

        @pl.when(i == 0)
        def _():
            dg_ref[...] = part

        @pl.when(i > 0)
        def _():
            dg_ref[...] += part

    row = pl.BlockSpec((ts, d), lambda i: (i, 0))
    in_specs = [row, pl.BlockSpec((ts, d), lambda i: (i, col_block)), pl.BlockSpec((1, d), lambda i: (0, 0))]
    args = [dn, h, g.reshape(1, d)]
    if res is not None:
        in_specs.append(row)
        args.append(res)
    n_in = len(args)
    copies = [row] if matmul_copy else []
    return _pcall(
        body, name=name, grid=(s // ts,), in_specs=in_specs,
        out_specs=(row, *copies, pl.BlockSpec((1, d), lambda i: (0, 0))),
        out_shape=(jax.ShapeDtypeStruct((s, d), out_dtype), *[jax.ShapeDtypeStruct((s, d), BF16) for _ in copies],
                   jax.ShapeDtypeStruct((1, d), F32)),
        compiler_params=_params("arbitrary"),
    )(*args)


ROW_CHUNK = 512


def _rows_with_halo(ref, r, t_rows, n_chunks, before, after):
    r0 = r * t_rows
    parts = []
    if before:
        hb = ref[pl.ds(pl.multiple_of(jnp.maximum(r0 - before, 0), before), before), :]
        parts.append(jnp.where(r > 0, hb, jnp.zeros_like(hb)))
    parts.append(ref[pl.ds(pl.multiple_of(r0, t_rows), t_rows), :])
    if after:
        ha = ref[pl.ds(pl.multiple_of(jnp.minimum(r0 + t_rows, n_chunks * t_rows - after), after), after), :]
        parts.append(jnp.where(r < n_chunks - 1, ha, jnp.zeros_like(ha)))
    return jnp.concatenate(parts, axis=0)


POOL_HALO = 16


def _pool_fwd(u, pool_w, pool_scale, *, c):
    s = u.shape[0]
    g_n = POOL_GROUPS
    tr = _tile(s, ROW_CHUNK, POOL_HALO)
    n_chunks = s // tr

    def body(u_ref, pw_ref, sc_ref, y_ref, d_ref):
        r = pl.program_id(1)
        w = jnp.left_shift(2, pl.program_id(0))
        xe = _rows_with_halo(u_ref, r, tr, n_chunks, POOL_HALO, 0)
        acc = xe
        for k in (1, 2, 4, 8):
            acc = jnp.where(k < w, acc + pltpu.roll(acc, k, 0), acc)
        t = r * tr + lax.broadcasted_iota(jnp.int32, (tr, 1), 0)
        cnt = jnp.minimum(t + 1, w).astype(F32)
        diff = (acc[POOL_HALO:] / cnt - xe[POOL_HALO:]).astype(BF16)
        d_ref[...] = diff
        y = jnp.dot(diff, pw_ref[...], preferred_element_type=F32) * sc_ref[...]
        y_ref[...] = y.astype(BF16)

    out = pl.BlockSpec((tr, c), lambda g, r: (r, g))
    return _pcall(
        body, name="pool_fwd", grid=(g_n, n_chunks),
        in_specs=[pl.BlockSpec((s, c), lambda g, r: (0, g)), pl.BlockSpec((None, c, c), lambda g, r: (g, 0, 0)),
                  pl.BlockSpec((1, c), lambda g, r: (0, g))],
        out_specs=(out, out),
        out_shape=(jax.ShapeDtypeStruct((s, g_n * c), BF16), jax.ShapeDtypeStruct((s, g_n * c), BF16)),
        compiler_params=_params("parallel", "arbitrary"),
    )(u, pool_w, pool_scale.reshape(1, g_n * c))


def _pool_bwd(dcat, diff, pool_w, pool_scale, *, c):
    s = dcat.shape[0]
    g_n = POOL_GROUPS
    tr = _tile(s, ROW_CHUNK, POOL_HALO)
    n_chunks = s // tr

    def body(dy_ref, d_ref, pw_ref, sc_ref, du_ref, dpw_ref, dsc_ref):
        r = pl.program_id(1)
        w = jnp.left_shift(2, pl.program_id(0))
        dye = _rows_with_halo(dy_ref, r, tr, n_chunks, 0, POOL_HALO)
        diff = d_ref[pl.ds(pl.multiple_of(r * tr, tr), tr), :]
        pw = pw_ref[...]
        yp = jnp.dot(diff, pw, preferred_element_type=F32)
        dsc = jnp.sum(dye[:tr] * yp, axis=0, keepdims=True)
        dyp = (dye * sc_ref[...]).astype(BF16)
        ddiff = lax.dot_general(dyp, pw, (((1,), (1,)), ((), ())), preferred_element_type=F32)
        dpw = lax.dot_general(diff, dyp[:tr], (((0,), (0,)), ((), ())), preferred_element_type=F32)
        t = r * tr + lax.broadcasted_iota(jnp.int32, (tr + POOL_HALO, 1), 0)
        cnt = jnp.minimum(t + 1, w).astype(F32)
        acc = ddiff / cnt
        rows = tr + POOL_HALO
        for k in (1, 2, 4, 8):
            acc = jnp.where(k < w, acc + pltpu.roll(acc, rows - k, 0), acc)
        du_ref[...] = (acc[:tr] - ddiff[:tr]).astype(BF16)

        @pl.when(r == 0)
        def _():
            dpw_ref[...] = dpw
            dsc_ref[...] = dsc

        @pl.when(r > 0)
        def _():
            dpw_ref[...] += dpw
            dsc_ref[...] += dsc

    col = lambda g, r: (0, g)
    wspec = pl.BlockSpec((None, c, c), lambda g, r: (g, 0, 0))
    vec = pl.BlockSpec((1, c), col)
    return _pcall(
        body, name="pool_bwd", grid=(g_n, n_chunks),
        in_specs=[pl.BlockSpec((s, c), col), pl.BlockSpec((s, c), col), wspec, vec],
        out_specs=(pl.BlockSpec((tr, c), lambda g, r: (r, g)), wspec, vec),
        out_shape=(jax.ShapeDtypeStruct((s, g_n * c), BF16), jax.ShapeDtypeStruct((g_n, c, c), F32),
                   jax.ShapeDtypeStruct((1, g_n * c), F32)),
        compiler_params=_params("parallel", "arbitrary"),
    )(dcat, diff, pool_w, pool_scale.reshape(1, g_n * c))


def _swap_halves(x, lane):
    return jnp.where((lane % ROPE_DIM) < ROPE_DIM // 2, pltpu.roll(x, LANES - ROPE_DIM // 2, 1),
                     pltpu.roll(x, ROPE_DIM // 2, 1))


def _qkv_prep(q, kv, kr2, cos_t, sin_t, *, heads):
    s = q.shape[0]
    ts = _tile(s, 256, 8)

    def body(q_ref, kv_ref, kr_ref, cos_ref, sin_ref, qo_ref, ko_ref, vo_ref):
        lane = lax.broadcasted_iota(jnp.int32, (ts, LANES), 1)
        cos_v = cos_ref[...]
        sin_v = sin_ref[...]

        def rope(x):
            return x * cos_v + _swap_halves(x, lane) * sin_v

        kr = rope(kr_ref[...]).astype(BF16)
        for pair in range(heads // 2):
            qr = rope(q_ref[:, (heads + pair) * LANES:(heads + pair + 1) * LANES])
            for half in range(2):
                h = 2 * pair + half
                qo_ref[h, :, :LANES] = q_ref[:, h * LANES:(h + 1) * LANES].astype(BF16)
                qo_ref[h, :, LANES:] = jnp.where(lane // ROPE_DIM == half, qr, 0.0).astype(BF16)
        for h in range(heads):
            ko_ref[h, :, :LANES] = kv_ref[:, h * LANES:(h + 1) * LANES].astype(BF16)
            ko_ref[h, :, LANES:] = kr
            vo_ref[h] = kv_ref[:, (heads + h) * LANES:(heads + h + 1) * LANES].astype(BF16)

    tab = pl.BlockSpec((ts, LANES), lambda i: (i, 0))
    return _pcall(
        body, name="qkv_prep", grid=(s // ts,),
        in_specs=[pl.BlockSpec((ts, q.shape[1]), lambda i: (i, 0)), pl.BlockSpec((ts, kv.shape[1]), lambda i: (i, 0)),
                  tab, tab, tab],
        out_specs=(pl.BlockSpec((heads, ts, 2 * LANES), lambda i: (0, i, 0)),
                   pl.BlockSpec((heads, ts, 2 * LANES), lambda i: (0, i, 0)),
                   pl.BlockSpec((heads, ts, LANES), lambda i: (0, i, 0))),
        out_shape=(jax.ShapeDtypeStruct((heads, s, 2 * LANES), BF16),
                   jax.ShapeDtypeStruct((heads, s, 2 * LANES), BF16),
                   jax.ShapeDtypeStruct((heads, s, LANES), BF16)),
        compiler_params=_params("parallel"),
    )(q, kv, kr2, cos_t, sin_t)


LOG2_E = 1.4426950408889634


FLASH_FWD_TK = 512
FLASH_FWD_SPLITS = 2


def _flash_fwd(qp, kp, v, *, scale, tq=512):
    heads, s, dk = qp.shape
    tq = _tile(s, tq, 16)
    tk = _tile(tq, FLASH_FWD_TK, 16)
    splits = FLASH_FWD_SPLITS
    th = tq // splits
    band = tq // tk
    c = scale * LOG2_E

    def body(q_ref, k_ref, v_ref, o_ref, lse_ref):
        i = pl.program_id(1)
        qs = [q_ref[hh * th:(hh + 1) * th, :] for hh in range(splits)]

        def skipped(hh, col0):
            return col0 is not None and col0 >= (hh + 1) * th

        def scores(start, col0):
            kb = k_ref[pl.ds(pl.multiple_of(start, tk), tk), :]
            return tuple(None if skipped(hh, col0) else
                         lax.dot_general(qs[hh], kb, (((1,), (1,)), ((), ())), preferred_element_type=F32)
                         for hh in range(splits))

        def absorb(start, scs, state, col0):
            vb = v_ref[pl.ds(pl.multiple_of(start, tk), tk), :]
            new = []
            for hh in range(splits):
                if scs[hh] is None:
                    new.append(state[hh])
                    continue
                m_old, l_old, acc = state[hh]
                sc = scs[hh]
                if col0 is not None and col0 + tk - 1 > hh * th:
                    rows = hh * th + lax.broadcasted_iota(jnp.int32, (th, tk), 0)
                    cols = col0 + lax.broadcasted_iota(jnp.int32, (th, tk), 1)
                    sc = jnp.where(rows >= cols, sc, -jnp.inf)
                m_new = jnp.maximum(m_old, jnp.max(sc, axis=-1, keepdims=True))
                alpha = jnp.exp2((m_old - m_new) * c)
                p = jnp.exp2((sc - m_new) * c)
                l_new = alpha * l_old + jnp.sum(p, axis=-1, keepdims=True)
                acc = alpha * acc + jnp.dot(p.astype(BF16), vb, preferred_element_type=F32)
                new.append((m_new, l_new, acc))
            return tuple(new)

        def step(j, carry):
            state, scs = carry
            nxt = scores((j + 1) * tk, None)
            return absorb(j * tk, scs, state, None), nxt

        init = tuple((jnp.full((th, 1), -jnp.inf, F32), jnp.zeros((th, 1), F32), jnp.zeros((th, V_DIM), F32))
                     for _ in range(splits))
        state, scs = lax.fori_loop(0, i * band, step, (init, scores(0, None)))
        for b in range(band):
            nxt = scores(i * tq + (b + 1) * tk, (b + 1) * tk) if b + 1 < band else None
            state = absorb(i * tq + b * tk, scs, state, b * tk)
            scs = nxt
        carry = state
        for hh in range(splits):
            m_fin, l_fin, acc = carry[hh]
            o_ref[hh * th:(hh + 1) * th, :] = acc / l_fin
            lse_ref[hh * th:(hh + 1) * th, :] = m_fin * scale + jnp.log(l_fin)

    return _pcall(
        body, name="flash_fwd", grid=(heads, s // tq),
        in_specs=[pl.BlockSpec((None, tq, dk), lambda h, i: (h, i, 0)),
                  pl.BlockSpec((None, s, dk), lambda h, i: (h, 0, 0)),
                  pl.BlockSpec((None, s, V_DIM), lambda h, i: (h, 0, 0))],
        out_specs=(pl.BlockSpec((tq, V_DIM), lambda h, i: (i, h)),
                   pl.BlockSpec((None, tq, 1), lambda h, i: (h, i, 0))),
        out_shape=(jax.ShapeDtypeStruct((s, heads * V_DIM), F32), jax.ShapeDtypeStruct((heads, s, 1), F32)),
        compiler_params=_params("parallel", "arbitrary"),
    )(qp, kp, v)


def _flash_bwd(qp, kp, v, dcat, o, lse, *, scale, do_col0, tq=512):
    heads, s, dk = qp.shape
    tq = _tile(s, tq, 16)
    tk = tq
    nq = s // tq

    def body(k_ref, v_ref, q_ref, do_ref, o_ref, lse_ref, dq_ref, dk_ref, dv_ref):
        j = pl.program_id(1)

        @pl.when(j == 0)
        def _():
            dq_ref[...] = jnp.zeros_like(dq_ref)

        kb = k_ref[...]
        vb = v_ref[...]

        def block(i, carry, diag):
            dk_acc, dv_acc = carry
            rows_at = pl.ds(pl.multiple_of(i * tq, tq), tq)
            qb = q_ref[rows_at, :]
            do = do_ref[rows_at, :]
            sc = lax.dot_general(qb, kb, (((1,), (1,)), ((), ())), preferred_element_type=F32) * scale
            if diag:
                rows = lax.broadcasted_iota(jnp.int32, (tq, tk), 0)
                cols = lax.broadcasted_iota(jnp.int32, (tq, tk), 1)
                sc = jnp.where(rows >= cols, sc, -jnp.inf)
            p = jnp.exp(sc - lse_ref[rows_at, :])
            dob = do.astype(BF16)
            dv_acc = dv_acc + lax.dot_general(p.astype(BF16), dob, (((0,), (0,)), ((), ())),
                                              preferred_element_type=F32)
            dp = lax.dot_general(dob, vb, (((1,), (1,)), ((), ())), preferred_element_type=F32)
            delta = jnp.sum(do * o_ref[rows_at, :], axis=-1, keepdims=True)
            ds = (p * (dp - delta) * scale).astype(BF16)
            dk_acc = dk_acc + lax.dot_general(ds, qb, (((0,), (0,)), ((), ())), preferred_element_type=F32)
            dq_ref[rows_at, :] += jnp.dot(ds, kb, preferred_element_type=F32)
            return dk_acc, dv_acc

        carry = block(j, (jnp.zeros((tk, dk), F32), jnp.zeros((tk, V_DIM), F32)), True)
        carry = lax.fori_loop(j + 1, nq, lambda i, cr: block(i, cr, False), carry)
        dk_ref[...] = carry[0]
        dv_ref[...] = carry[1]

    whole = lambda h, j: (h, 0, 0)
    return _pcall(
        body, name="flash_bwd", grid=(heads, nq),
        in_specs=[pl.BlockSpec((None, tk, dk), lambda h, j: (h, j, 0)),
                  pl.BlockSpec((None, tk, V_DIM), lambda h, j: (h, j, 0)),
                  pl.BlockSpec((None, s, dk), whole),
                  pl.BlockSpec((s, V_DIM), lambda h, j: (0, do_col0 + h)),
                  pl.BlockSpec((s, V_DIM), lambda h, j: (0, h)),
                  pl.BlockSpec((None, s, 1), whole)],
        out_specs=(pl.BlockSpec((None, s, dk), whole),
                   pl.BlockSpec((None, tk, dk), lambda h, j: (h, j, 0)),
                   pl.BlockSpec((None, tk, V_DIM), lambda h, j: (h, j, 0))),
        out_shape=(jax.ShapeDtypeStruct((heads, s, dk), F32), jax.ShapeDtypeStruct((heads, s, dk), F32),
                   jax.ShapeDtypeStruct((heads, s, V_DIM), F32)),
        compiler_params=_params("parallel", "arbitrary"),
    )(kp, v, qp, dcat, o, lse)


def _attn_bwd_post(dqp, dkp, dv, cos_t, sin_t):
    heads, s, _ = dqp.shape
    ts = _tile(s, 256, 8)

    def body(dq_ref, dk_ref, dv_ref, cos_ref, sin_ref, q_out, kv_out, kr_out):
        lane = lax.broadcasted_iota(jnp.int32, (ts, LANES), 1)
        cos_v = cos_ref[...]
        sin_v = sin_ref[...]

        def rope_t(dy):
            return dy * cos_v + _swap_halves(dy * sin_v, lane)

        kr_sum = jnp.zeros((ts, LANES), F32)
        for h in range(heads):
            q_out[:, h * LANES:(h + 1) * LANES] = dq_ref[h, :, :LANES].astype(BF16)
            kv_out[:, h * LANES:(h + 1) * LANES] = dk_ref[h, :, :LANES].astype(BF16)
            kv_out[:, (heads + h) * LANES:(heads + h + 1) * LANES] = dv_ref[h].astype(BF16)
            kr_sum = kr_sum + dk_ref[h, :, LANES:]
        for pair in range(heads // 2):
            r = jnp.where(lane < ROPE_DIM, dq_ref[2 * pair, :, LANES:], dq_ref[2 * pair + 1, :, LANES:])
            q_out[:, (heads + pair) * LANES:(heads + pair + 1) * LANES] = rope_t(r).astype(BF16)
        kr = rope_t(kr_sum)
        kr_out[...] = (kr + pltpu.roll(kr, ROPE_DIM, 1)).astype(BF16)

    wq = heads * (NOPE_DIM + ROPE_DIM)
    wkv = heads * (NOPE_DIM + V_DIM)
    tab = pl.BlockSpec((ts, LANES), lambda i: (i, 0))
    return _pcall(
        body, name="attn_bwd_post", grid=(s // ts,),
        in_specs=[pl.BlockSpec((heads, ts, 2 * LANES), lambda i: (0, i, 0)),
                  pl.BlockSpec((heads, ts, 2 * LANES), lambda i: (0, i, 0)),
                  pl.BlockSpec((heads, ts, LANES), lambda i: (0, i, 0)), tab, tab],
        out_specs=(pl.BlockSpec((ts, wq), lambda i: (i, 0)), pl.BlockSpec((ts, wkv), lambda i: (i, 0)), tab),
        out_shape=(jax.ShapeDtypeStruct((s, wq), BF16), jax.ShapeDtypeStruct((s, wkv), BF16),
                   jax.ShapeDtypeStruct((s, LANES), BF16)),
        compiler_params=_params("parallel"),
    )(dqp, dkp, dv, cos_t, sin_t)


CONV_HALO = 16


def _conv_gate(xe, cw_ref, cb_ref):
    x1 = pltpu.roll(xe, 1, 0)
    x2 = pltpu.roll(xe, 2, 0)
    return cw_ref[2:3, :] * xe + cw_ref[1:2, :] * x1 + cw_ref[0:1, :] * x2 + cb_ref[...], x1, x2


def _ffn_act_fwd(gu, conv_w, conv_b, *, d_ff, tc=256):
    s = gu.shape[0]
    tc = _tile(d_ff, tc)
    nf = d_ff // tc
    tr = _tile(s, ROW_CHUNK, CONV_HALO)
    n_chunks = s // tr

    def body(g_ref, u_ref, cw_ref, cb_ref, a_ref):
        r = pl.program_id(1)
        xe = _rows_with_halo(g_ref, r, tr, n_chunks, CONV_HALO, 0).astype(F32)
        gc = _conv_gate(xe, cw_ref, cb_ref)[0][CONV_HALO:]
        a_ref[...] = (gc * jax.nn.sigmoid(gc) * u_ref[...].astype(F32)).astype(BF16)

    return _pcall(
        body, name="ffn_act_fwd", grid=(nf, n_chunks),
        in_specs=[pl.BlockSpec((s, tc), lambda j, r: (0, j)), pl.BlockSpec((tr, tc), lambda j, r: (r, nf + j)),
                  pl.BlockSpec((CONV_TAPS, tc), lambda j, r: (0, j)), pl.BlockSpec((1, tc), lambda j, r: (0, j))],
        out_specs=pl.BlockSpec((tr, tc), lambda j, r: (r, j)),
        out_shape=jax.ShapeDtypeStruct((s, d_ff), BF16),
        compiler_params=_params("parallel", "arbitrary"),
    )(gu, gu, conv_w, conv_b.reshape(1, d_ff))


def _ffn_act_bwd(da, gu, conv_w, conv_b, *, d_ff, tc=256):
    s = gu.shape[0]
    tc = _tile(d_ff, tc)
    nf = d_ff // tc
    tr = _tile(s, ROW_CHUNK, CONV_HALO)
    n_chunks = s // tr
    rows = tr + 2 * CONV_HALO
    main = slice(CONV_HALO, CONV_HALO + tr)

    def body(da_ref, g_ref, u_ref, cw_ref, cb_ref, dg_ref, du_ref, dcw_ref, dcb_ref):
        r = pl.program_id(1)
        xe = _rows_with_halo(g_ref, r, tr, n_chunks, CONV_HALO, CONV_HALO).astype(F32)
        dae = _rows_with_halo(da_ref, r, tr, n_chunks, CONV_HALO, CONV_HALO).astype(F32)
        ue = _rows_with_halo(u_ref, r, tr, n_chunks, CONV_HALO, CONV_HALO).astype(F32)
        gc, x1, x2 = _conv_gate(xe, cw_ref, cb_ref)
        sg = jax.nn.sigmoid(gc)
        du_ref[...] = (dae * gc * sg)[main].astype(BF16)
        dgc = dae * ue * sg * (1.0 + gc * (1.0 - sg))
        dg = (cw_ref[2:3, :] * dgc + cw_ref[1:2, :] * pltpu.roll(dgc, rows - 1, 0)
              + cw_ref[0:1, :] * pltpu.roll(dgc, rows - 2, 0))
        dg_ref[...] = dg[main].astype(BF16)
        dgc_m = dgc[main]
        dcb = jnp.sum(dgc_m, axis=0, keepdims=True)
        dcw = jnp.concatenate([jnp.sum(dgc_m * x2[main], axis=0, keepdims=True),
                               jnp.sum(dgc_m * x1[main], axis=0, keepdims=True),
                               jnp.sum(dgc_m * xe[main], axis=0, keepdims=True)], axis=0)

        @pl.when(r == 0)
        def _():
            dcb_ref[...] = dcb
            dcw_ref[...] = dcw

        @pl.when(r > 0)
        def _():
            dcb_ref[...] += dcb
            dcw_ref[...] += dcw

    col = pl.BlockSpec((s, tc), lambda j, r: (0, j))
    out = pl.BlockSpec((tr, tc), lambda j, r: (r, j))
    return _pcall(
        body, name="ffn_act_bwd", grid=(nf, n_chunks),
        in_specs=[col, col, pl.BlockSpec((s, tc), lambda j, r: (0, nf + j)),
                  pl.BlockSpec((CONV_TAPS, tc), lambda j, r: (0, j)), pl.BlockSpec((1, tc), lambda j, r: (0, j))],
        out_specs=(out, out, pl.BlockSpec((CONV_TAPS, tc), lambda j, r: (0, j)),
                   pl.BlockSpec((1, tc), lambda j, r: (0, j))),
        out_shape=(jax.ShapeDtypeStruct((s, d_ff), BF16), jax.ShapeDtypeStruct((s, d_ff), BF16),
                   jax.ShapeDtypeStruct((CONV_TAPS, d_ff), F32), jax.ShapeDtypeStruct((1, d_ff), F32)),
        compiler_params=_params("parallel", "arbitrary"),
    )(da, gu, gu, conv_w, conv_b.reshape(1, d_ff))


def _ple_fwd(h, z, e):
    s, d = h.shape
    ts = _tile(s, 512, 8)

    def body(h_ref, z_ref, e_ref, o_ref):
        o_ref[...] = h_ref[...] + e_ref[...] * jax.nn.sigmoid(z_ref[...])

    row = pl.BlockSpec((ts, d), lambda i: (i, 0))
    return _pcall(body, name="ple_fwd", grid=(s // ts,), in_specs=[row, row, row], out_specs=row,
                  out_shape=jax.ShapeDtypeStruct((s, d), F32), compiler_params=_params("parallel"))(h, z, e)


def _ple_bwd(dh, z, e, dep=None):
    s, d = dh.shape
    ts = _tile(s, 512, 8)

    def body(dh_ref, z_ref, e_ref, *rest):
        de_ref, dz_ref = rest[-2:]
        gt = jax.nn.sigmoid(z_ref[...])
        dhv = dh_ref[...]
        de_ref[...] = (dhv * gt).astype(BF16)
        dz_ref[...] = (dhv * e_ref[...] * gt * (1.0 - gt)).astype(BF16)

    row = pl.BlockSpec((ts, d), lambda i: (i, 0))
    deps = [] if dep is None else [dep]
    return _pcall(body, name="ple_bwd", grid=(s // ts,), in_specs=[row, row, row] + [HBM_SPEC] * len(deps),
                  out_specs=(row, row),
                  out_shape=(jax.ShapeDtypeStruct((s, d), BF16), jax.ShapeDtypeStruct((s, d), BF16)),
                  compiler_params=_params("parallel"))(dh, z, e, *deps)


def _loss_head(h, g, target):
    s, d = h.shape
    ts = _tile(s, 512, 8)

    def body(h_ref, g_ref, t_ref, loss_ref, dh_ref, dg_ref):
        i = pl.program_id(0)
        x = h_ref[...]
        gv = g_ref[...]
        r = lax.rsqrt(jnp.mean(x * x, axis=-1, keepdims=True) + NORM_EPS)
        nh = x * r
        err = nh * gv - t_ref[...]
        part_loss = 0.5 * jnp.sum(jnp.mean(err * err, axis=-1, keepdims=True), axis=0, keepdims=True)
        dy = err * (1.0 / d)
        gd = dy * gv
        dh_ref[...] = (gd - nh * jnp.mean(gd * nh, axis=-1, keepdims=True)) * r
        part_g = jnp.sum(dy * nh, axis=0, keepdims=True)
        part_l = jnp.broadcast_to(part_loss, (1, LANES))

        @pl.when(i == 0)
        def _():
            dg_ref[...] = part_g
            loss_ref[...] = part_l

        @pl.when(i > 0)
        def _():
            dg_ref[...] += part_g
            loss_ref[...] += part_l

    row = pl.BlockSpec((ts, d), lambda i: (i, 0))
    vec = pl.BlockSpec((1, d), lambda i: (0, 0))
    return _pcall(
        body, name="loss_head", grid=(s // ts,), in_specs=[row, vec, row],
        out_specs=(pl.BlockSpec((1, LANES), lambda i: (0, 0)), row, vec),
        out_shape=(jax.ShapeDtypeStruct((1, LANES), F32), jax.ShapeDtypeStruct((s, d), F32),
                   jax.ShapeDtypeStruct((1, d), F32)),
        compiler_params=_params("arbitrary"),
    )(h, g.reshape(1, d), target)


HBM_SPEC = pl.BlockSpec(memory_space=pl.ANY)


def _flat_index(px, py, pc):
    return 4 * px + 2 * py + pc


def _all_gather(shards, *, name):
    n = len(shards)

    def body(*refs):
        ins, outs = refs[:n], refs[n:2 * n]
        send_sems, recv_sems, local_sems = refs[2 * n:]
        x, y, c = lax.axis_index("x"), lax.axis_index("y"), lax.axis_index("c")
        me, sibling = (x, y, c), (x, y, 1 - c)
        chips = [(1 - x, y), (x, 1 - y), (1 - x, 1 - y)]

        def copy(a, k, block, to, src=None):
            slot = outs[a].at[_flat_index(*block)]
            return pltpu.make_async_remote_copy(
                src_ref=slot if src is None else src, dst_ref=slot,
                send_sem=send_sems.at[a, k], recv_sem=recv_sems.at[a, k],
                device_id=to, device_id_type=pl.DeviceIdType.MESH)

        mine, first, passed = [], [], []
        for a in range(n):
            cp = pltpu.make_async_copy(ins[a], outs[a].at[_flat_index(*me)], local_sems.at[a])
            cp.start()
            mine.append(cp)
            first.append(copy(a, 0, me, sibling, src=ins[a]))
            first += [copy(a, 1 + j, me, (*chip, c), src=ins[a]) for j, chip in enumerate(chips)]
        for cp in first:
            cp.start()
        for j, chip in enumerate(chips):
            for a in range(n):
                copy(a, 1 + j, (*chip, c), me).wait_recv()
                fwd = copy(a, 4 + j, (*chip, c), sibling)
                fwd.start()
                passed.append(fwd)
        for a in range(n):
            copy(a, 0, sibling, me).wait_recv()
            for j, chip in enumerate(chips):
                copy(a, 4 + j, (*chip, 1 - c), me).wait_recv()
        for cp in first + passed:
            cp.wait_send()
        for cp in mine:
            cp.wait()

    return _pcall(
        body, name=name,
        in_specs=[HBM_SPEC] * n, out_specs=[HBM_SPEC] * n,
        out_shape=[jax.ShapeDtypeStruct((N_DEV,) + a.shape, a.dtype) for a in shards],
        scratch_shapes=[pltpu.SemaphoreType.DMA((n, 7)), pltpu.SemaphoreType.DMA((n, 7)),
                        pltpu.SemaphoreType.DMA((n,))],
    )(*shards)


HBM_ONLY = pl.BlockSpec(memory_space=pltpu.HBM)
SEM_SPEC = pl.BlockSpec(memory_space=pltpu.SEMAPHORE)
N_PEERS = N_DEV - 1
PEER_FLIPS = ((0, 0, 1), (1, 0, 0), (0, 1, 0), (1, 1, 0), (1, 0, 1), (0, 1, 1), (1, 1, 1))


def _exchange_refs(gather, src_refs, land_refs, send_sems, recv_sems):
    x, y, c = lax.axis_index("x"), lax.axis_index("y"), lax.axis_index("c")
    me = _flat_index(x, y, c)
    peers = [(x ^ fx, y ^ fy, c ^ fc) for fx, fy, fc in PEER_FLIPS]

    def out_copy(a, k):
        src = src_refs[a] if gather else src_refs[a].at[_flat_index(*peers[k])]
        return pltpu.make_async_remote_copy(
            src_ref=src, dst_ref=land_refs[a].at[me], send_sem=send_sems.at[a * N_PEERS + k],
            recv_sem=recv_sems.at[a * N_PEERS + k], device_id=peers[k], device_id_type=pl.DeviceIdType.MESH)

    def in_copy(a, k):
        src = src_refs[a] if gather else src_refs[a].at[me]
        return pltpu.make_async_remote_copy(
            src_ref=src, dst_ref=land_refs[a].at[_flat_index(*peers[k])], send_sem=send_sems.at[a * N_PEERS + k],
            recv_sem=recv_sems.at[a * N_PEERS + k], device_id=peers[k], device_id_type=pl.DeviceIdType.MESH)

    return out_copy, in_copy


def _exchange_start(srcs, *, gather, name, dep):
    n = len(srcs)
    lands = [lax.empty((N_DEV,) + a.shape if gather else a.shape, a.dtype) for a in srcs]

    def body(*refs):
        src_refs, land_refs = refs[:n], refs[n:2 * n]
        send_sems, recv_sems = refs[2 * n + 1], refs[2 * n + 2]
        token = refs[-1]
        out_copy, _ = _exchange_refs(gather, src_refs, land_refs, send_sems, recv_sems)
        for k in range(N_PEERS):
            for a in range(n):
                out_copy(a, k).start()
        token[...] = jnp.zeros_like(token)

    hbm = lambda a: pltpu.with_memory_space_constraint(a, pltpu.HBM)
    return _pcall(
        body, name=name,
        out_shape=(pltpu.SemaphoreType.DMA((n * N_PEERS,)), pltpu.SemaphoreType.DMA((n * N_PEERS,)),
                   *[pltpu.HBM(a.shape, a.dtype) for a in srcs], *[pltpu.HBM(a.shape, a.dtype) for a in lands],
                   jax.ShapeDtypeStruct((8, LANES), F32)),
        in_specs=[HBM_ONLY] * (2 * n) + [HBM_SPEC],
        out_specs=(SEM_SPEC, SEM_SPEC, *[HBM_ONLY] * (2 * n), pl.BlockSpec(memory_space=pltpu.VMEM)),
        input_output_aliases={i: 2 + i for i in range(2 * n)},
        compiler_params=pltpu.CompilerParams(has_side_effects=pltpu.SideEffectType.DATAFLOW_SIDE_EFFECTING),
    )(*[hbm(a) for a in srcs], *[hbm(a) for a in lands], dep)


def _exchange_wait(started, after, *, gather, name):
    send_sems, recv_sems = started[0], started[1]
    n = (len(started) - 3) // 2
    srcs, lands = started[2:2 + n], started[2 + n:2 + 2 * n]

    def body(*refs):
        src_refs, land_refs = refs[:n], refs[n:2 * n]
        s_sems, r_sems = refs[2 * n], refs[2 * n + 1]
        out_copy, in_copy = _exchange_refs(gather, src_refs, land_refs, s_sems, r_sems)
        for k in range(N_PEERS):
            for a in range(n):
                out_copy(a, k).wait_send()
                in_copy(a, k).wait_recv()

    res = _pcall(
        body, name=name,
        out_shape=tuple(pltpu.HBM(a.shape, a.dtype) for a in (*srcs, *lands)),
        in_specs=[HBM_ONLY] * (2 * n) + [SEM_SPEC, SEM_SPEC, HBM_SPEC],
        out_specs=tuple([HBM_ONLY] * (2 * n)),
        input_output_aliases={i: i for i in range(2 * n)},
        compiler_params=pltpu.CompilerParams(has_side_effects=pltpu.SideEffectType.DATAFLOW_SIDE_EFFECTING),
    )(*srcs, *lands, send_sems, recv_sems, after)
    return _with_own_slot(res[:n], res[n:], gather)


def _with_own_slot(srcs, lands, gather):
    me = _flat_index(lax.axis_index("x"), lax.axis_index("y"), lax.axis_index("c"))
    full = []
    for src, land in zip(srcs, lands):
        own = src if gather else lax.dynamic_index_in_dim(src, me, 0, keepdims=False)
        full.append(lax.dynamic_update_slice(land, own[None], (me,) + (0,) * own.ndim))
    return full


def _adam_math(g, w, m, v):
    m = ADAM_B1 * m + (1.0 - ADAM_B1) * g
    v = ADAM_B2 * v + (1.0 - ADAM_B2) * jnp.square(g)
    m_hat = m / (1.0 - ADAM_B1 ** ADAM_STEP)
    v_hat = v / (1.0 - ADAM_B2 ** ADAM_STEP)
    delta = -ADAM_LR * (m_hat / (jnp.sqrt(v_hat) + ADAM_EPS) + ADAM_WD * w)
    return delta, m, v


def _adamw(contribs, w, m, v, *, name):
    layers = len(contribs)
    _, r, c = contribs[0].shape
    tr = _tile(r, max(8, (256 * 1024 // c) // 8 * 8), 8)

    def body(*refs):
        g_refs = refs[:layers]
        w_ref, m_ref, v_ref, go_ref, d_ref, mo_ref, vo_ref = refs[layers:]
        layer = pl.program_id(0)
        for l2 in range(layers):

            @pl.when(layer == l2)
            def _(g_ref=g_refs[l2]):
                g = g_ref[0].astype(F32)
                for k in range(1, N_DEV):
                    g = g + g_ref[k].astype(F32)
                delta, m_new, v_new = _adam_math(g, w_ref[...], m_ref[...], v_ref[...])
                go_ref[...] = g
                d_ref[...] = delta
                mo_ref[...] = m_new
                vo_ref[...] = v_new

    g_specs = [pl.BlockSpec((N_DEV, tr, c), lambda l, i, l2=l2: (0, jnp.where(l == l2, i, 0), 0))
               for l2 in range(layers)]
    blk = pl.BlockSpec((None, tr, c), lambda l, i: (l, i, 0))
    out = jax.ShapeDtypeStruct((layers, r, c), F32)
    return _pcall(
        body, name=name, grid=(layers, r // tr), in_specs=g_specs + [blk, blk, blk],
        out_specs=(blk, blk, blk, blk), out_shape=(out, out, out, out),
        compiler_params=_params("arbitrary", "arbitrary"),
    )(*contribs, w, m, v)


def _heads_split(w, heads, first, second):
    k = w.shape[0]
    w3 = w.reshape(k, heads, first + second)
    return jnp.concatenate([w3[:, :, :first].reshape(k, heads * first),
                            w3[:, :, first:].reshape(k, heads * second)], axis=1)


def _heads_join(w, heads, first, second):
    k = w.shape[0]
    a = w[:, :heads * first].reshape(k, heads, first)
    b = w[:, heads * first:].reshape(k, heads, second)
    return jnp.concatenate([a, b], axis=2).reshape(k, heads * (first + second))


def _full_from_gathered(kind, g):
    if kind == "col":
        return jnp.transpose(g, (1, 0, 2)).reshape(g.shape[1], N_DEV * g.shape[2])
    if kind == "row":
        return g.reshape(N_DEV * g.shape[1], g.shape[2])
    return jnp.transpose(g, (1, 0, 2, 3)).reshape(g.shape[1], N_DEV * g.shape[2], g.shape[3])


def _blocks_from_full(kind, f):
    if kind == "col":
        k, n = f.shape
        return jnp.transpose(f.reshape(k, N_DEV, n // N_DEV), (1, 0, 2))
    if kind == "row":
        k, n = f.shape
        return f.reshape(N_DEV, k // N_DEV, n)
    g, c_in, c = f.shape
    return jnp.transpose(f.reshape(g, N_DEV, c_in // N_DEV, c), (1, 0, 2, 3))


def _rope_tables(positions):
    inv_freq = 1.0 / (ROPE_THETA ** (jnp.arange(0, ROPE_DIM, 2, dtype=F32) / ROPE_DIM))
    ang = positions.astype(F32)[:, None] * inv_freq
    cos, sin = jnp.cos(ang), jnp.sin(ang)
    return jnp.concatenate([cos, cos, cos, cos], axis=-1), jnp.concatenate([-sin, sin, -sin, sin], axis=-1)


def _layer_fwd(h0, p_i, fetch, rep, tabs, dims, dep=None):
    heads, d_pool, q_lora, d_ff = dims["heads"], dims["d_pool"], dims["q_lora"], dims["d_ff"]
    c = d_pool // POOL_GROUPS
    cos_t, sin_t = tabs
    scale = 1.0 / math.sqrt(NOPE_DIM + ROPE_DIM)
    n1 = _rms_fwd(h0, rep["norm_mix_g"], name="rms_mix_fwd", dep=dep)
    w = dict(fetch("mix", n1))
    u = _mm(n1, w["w_in"], name="mm_in_fwd")
    y_pool, diff = _pool_fwd(u, w["pool_w"], rep["pool_scale"], c=c)
    nq = _rms_fwd(u, rep["q_norm_g"], name="rms_q_fwd", col_block=d_pool // q_lora)
    nkv = _rms_fwd(u, rep["kv_norm_g"], name="rms_kv_fwd", col_block=d_pool // q_lora + 1)
    q = _mm(nq, w["w_uq"], name="mm_uq_fwd")
    kv = _mm(nkv, w["w_ukv"], name="mm_ukv_fwd")
    kr = u[:, d_pool + 2 * q_lora:]
    kr2 = jnp.concatenate([kr, kr], axis=-1)
    qp, kp, v = _qkv_prep(q, kv, kr2, cos_t, sin_t, heads=heads)
    o, lse = _flash_fwd(qp, kp, v, scale=scale)
    t = _mm(y_pool, w["w_out"], name="mm_out_pool_fwd", add=h0, b_row=(0, d_pool))
    h1 = _mm(o, w["w_out"], name="mm_out_att_fwd", add=t, b_row=(d_pool, None))
    n2 = _rms_fwd(h1, rep["norm_ffn_g"], name="rms_ffn_fwd")
    w.update(fetch("up", n2))
    gu = _mm(n2, w["w_up"], name="mm_up_fwd", out_dtype=BF16, b_blocks=(0, N_DEV))
    a = _ffn_act_fwd(gu, w["conv_w"], rep["conv_b"], d_ff=d_ff)
    w.update(fetch("down", a))
    h2 = _mm(a, w["w_down"], name="mm_down_fwd", add=h1)
    n3 = _rms_fwd(h2, rep["norm_ple_g"], name="rms_ple_fwd")
    z = _mm(n3, w["w_ple_gate"], name="mm_pgate_fwd")
    e = _mm(p_i, w["w_ple"], name="mm_ple_fwd")
    h3 = _ple_fwd(h2, z, e)
    saved = dict(h0=h0, n1=n1, u=u, y_pool=y_pool, diff=diff, nq=nq, nkv=nkv, qp=qp, kp=kp, v=v, o=o, lse=lse,
                 h1=h1, n2=n2, gu=gu, a=a, h2=h2, n3=n3, z=z, e=e)
    return h3, saved, w


def _layer_bwd(dh3, p_i, w, rep, tabs, dims, sv, dep=None, hooks=None):
    hooks = hooks or {}
    heads, d_pool, q_lora, d_ff = dims["heads"], dims["d_pool"], dims["q_lora"], dims["d_ff"]
    c = d_pool // POOL_GROUPS
    cos_t, sin_t = tabs
    scale = 1.0 / math.sqrt(NOPE_DIM + ROPE_DIM)
    gr = {}
    de, dz = _ple_bwd(dh3, sv["z"], sv["e"], dep)
    gr["w_ple"] = _mm(p_i, de, name="mm_ple_dw", ta=True, out_dtype=BF16)
    gr["w_ple_gate"] = _mm(sv["n3"], dz, name="mm_pgate_dw", ta=True, out_dtype=BF16)
    dn3 = _mm(dz, w["w_ple_gate"], name="mm_pgate_dx", tb=True)
    dh2, dh2_mm, gr["norm_ple_g"] = _rms_bwd(dn3, sv["h2"], rep["norm_ple_g"], name="rms_ple_bwd", res=dh3,
                                             matmul_copy=True)
    gr["w_down"] = _mm(sv["a"], dh2_mm, name="mm_down_dw", ta=True, out_dtype=BF16)
    dep_down = hooks["down"](dh2, gr) if "down" in hooks else None
    da = _mm(dh2_mm, w["w_down"], name="mm_down_dx", tb=True, out_dtype=BF16, dep=dep_down)
    dgate, dup, gr["conv_w"], gr["conv_b"] = _ffn_act_bwd(da, sv["gu"], w["conv_w"], rep["conv_b"], d_ff=d_ff)
    half, per = N_DEV // 2, w["w_up"].shape[2]
    dw_gate = _mm(sv["n2"], dgate, name="mm_up_gate_dw", ta=True, out_dtype=BF16, out_blocks=(N_DEV, 0, per))
    gr["w_up"] = _mm(sv["n2"], dup, name="mm_up_up_dw", ta=True, out_dtype=BF16, out_blocks=(N_DEV, half, per),
                     out_init=dw_gate)
    dep_up = hooks["up"](gr["w_up"], gr) if "up" in hooks else None
    dn2 = _mm(dgate, w["w_up"], name="mm_up_gate_dx", tb=True, b_blocks=(0, half), dep=dep_up)
    dn2 = _mm(dup, w["w_up"], name="mm_up_up_dx", tb=True, b_blocks=(half, half), add=dn2)
    dh1, dh1_mm, gr["norm_ffn_g"] = _rms_bwd(dn2, sv["h1"], rep["norm_ffn_g"], name="rms_ffn_bwd", res=dh2,
                                             matmul_copy=True)
    dw_out_pool = _mm(sv["y_pool"], dh1_mm, name="mm_out_pool_dw", ta=True, out_dtype=BF16)
    dw_out_att = _mm(sv["o"], dh1_mm, name="mm_out_att_dw", ta=True, out_dtype=BF16)
    gr["w_out"] = jnp.concatenate([dw_out_pool, dw_out_att], axis=0)
    dcat = _mm(dh1_mm, w["w_out"], name="mm_out_dx", tb=True)
    do_col0 = d_pool // V_DIM
    dqp, dkp, dv = _flash_bwd(sv["qp"], sv["kp"], sv["v"], dcat, sv["o"], sv["lse"], scale=scale, do_col0=do_col0)
    dq, dkv, dkr2 = _attn_bwd_post(dqp, dkp, dv, cos_t, sin_t)
    gr["w_uq"] = _mm(sv["nq"], dq, name="mm_uq_dw", ta=True, out_dtype=BF16)
    gr["w_ukv"] = _mm(sv["nkv"], dkv, name="mm_ukv_dw", ta=True, out_dtype=BF16)
    dnq = _mm(dq, w["w_uq"], name="mm_uq_dx", tb=True)
    dnkv = _mm(dkv, w["w_ukv"], name="mm_ukv_dx", tb=True)
    dcq, gr["q_norm_g"] = _rms_bwd(dnq, sv["u"], rep["q_norm_g"], name="rms_q_bwd",
                                   col_block=d_pool // q_lora, out_dtype=BF16)
    dckv, gr["kv_norm_g"] = _rms_bwd(dnkv, sv["u"], rep["kv_norm_g"], name="rms_kv_bwd",
                                     col_block=d_pool // q_lora + 1, out_dtype=BF16)
    du_pool, gr["pool_w"], gr["pool_scale"] = _pool_bwd(dcat, sv["diff"], w["pool_w"], rep["pool_scale"], c=c)
    du = jnp.concatenate([du_pool, dcq, dckv, dkr2[:, :ROPE_DIM]], axis=-1)
    gr["w_in"] = _mm(sv["n1"], du, name="mm_in_dw", ta=True, out_dtype=BF16)
    dn1 = _mm(du, w["w_in"], name="mm_in_dx", tb=True)
    dh0, gr["norm_mix_g"] = _rms_bwd(dn1, sv["h0"], rep["norm_mix_g"], name="rms_mix_bwd", res=dh1)
    return dh0, gr


def _as2d(a):
    return a.reshape(a.shape[0], -1, a.shape[-1])


def kernel(x, p, positions, norm_mix_g, w_in, pool_w, pool_scale, q_norm_g, w_uq, kv_norm_g, w_ukv, w_out, norm_ffn_g, w_up, conv_w, conv_b, w_down, norm_ple_g, w_ple, w_ple_gate, final_norm_g, loss_target, m_norm_mix_g, m_w_in, m_pool_w, m_pool_scale, m_q_norm_g, m_w_uq, m_kv_norm_g, m_w_ukv, m_w_out, m_norm_ffn_g, m_w_up, m_conv_w, m_conv_b, m_w_down, m_norm_ple_g, m_w_ple, m_w_ple_gate, m_final_norm_g, v_norm_mix_g, v_w_in, v_pool_w, v_pool_scale, v_q_norm_g, v_w_uq, v_kv_norm_g, v_w_ukv, v_w_out, v_norm_ffn_g, v_w_up, v_conv_w, v_conv_b, v_w_down, v_norm_ple_g, v_w_ple, v_w_ple_gate, v_final_norm_g):
    weights = dict(norm_mix_g=norm_mix_g, w_in=w_in, pool_w=pool_w, pool_scale=pool_scale, q_norm_g=q_norm_g,
                   w_uq=w_uq, kv_norm_g=kv_norm_g, w_ukv=w_ukv, w_out=w_out, norm_ffn_g=norm_ffn_g, w_up=w_up,
                   conv_w=conv_w, conv_b=conv_b, w_down=w_down, norm_ple_g=norm_ple_g, w_ple=w_ple,
                   w_ple_gate=w_ple_gate, final_norm_g=final_norm_g)
    m_in = dict(norm_mix_g=m_norm_mix_g, w_in=m_w_in, pool_w=m_pool_w, pool_scale=m_pool_scale, q_norm_g=m_q_norm_g,
                w_uq=m_w_uq, kv_norm_g=m_kv_norm_g, w_ukv=m_w_ukv, w_out=m_w_out, norm_ffn_g=m_norm_ffn_g,
                w_up=m_w_up, conv_w=m_conv_w, conv_b=m_conv_b, w_down=m_w_down, norm_ple_g=m_norm_ple_g,
                w_ple=m_w_ple, w_ple_gate=m_w_ple_gate, final_norm_g=m_final_norm_g)
    v_in = dict(norm_mix_g=v_norm_mix_g, w_in=v_w_in, pool_w=v_pool_w, pool_scale=v_pool_scale, q_norm_g=v_q_norm_g,
                w_uq=v_w_uq, kv_norm_g=v_kv_norm_g, w_ukv=v_w_ukv, w_out=v_w_out, norm_ffn_g=v_norm_ffn_g,
                w_up=v_w_up, conv_w=v_conv_w, conv_b=v_conv_b, w_down=v_w_down, norm_ple_g=v_norm_ple_g,
                w_ple=v_w_ple, w_ple_gate=v_w_ple_gate, final_norm_g=v_final_norm_g)

    depth = w_in.shape[0]
    s, d_model = x.shape[1], x.shape[2]
    d_pool = pool_scale.shape[-1]
    q_lora = q_norm_g.shape[-1]
    d_ff = conv_b.shape[-1]
    heads = (w_uq.shape[-1] * N_DEV) // (NOPE_DIM + ROPE_DIM)
    dims = dict(heads=heads, d_pool=d_pool, q_lora=q_lora, d_ff=d_ff)

    groups = {"mix": ("w_in", "pool_w", "w_uq", "w_ukv", "w_out"), "up": ("w_up", "conv_w"),
              "down": ("w_down", "w_ple", "w_ple_gate")}

    def group_shards(i, group):
        return [weights[n][i] if n == "conv_w" else weights[n][i].astype(BF16) for n in groups[group]]

    def start_weights(i, group, dep):
        return _exchange_start(group_shards(i, group), gather=True, name=f"weights_{group}_start_{i}", dep=dep)

    def full_group(group, gathered_g):
        w = {n: g if n == "w_up" else _full_from_gathered(SHARD_KIND[n], g) for n, g in zip(groups[group], gathered_g)}
        if group == "mix":
            w["w_uq"] = _heads_split(w["w_uq"], heads, NOPE_DIM, ROPE_DIM)
            w["w_ukv"] = _heads_split(w["w_ukv"], heads, NOPE_DIM, V_DIM)
        return w

    tabs = _rope_tables(positions[0])

    arrived = {(0, "mix"): _all_gather(group_shards(0, "mix"), name="weights_mix_gather_0")}
    travelling = {}
    token = arrived[(0, "mix")][0]
    for group in ("up", "down"):
        travelling[(0, group)] = start_weights(0, group, token)
        token = travelling[(0, group)][-1]
    layer_w = []
    h = x[0]
    saved = []
    for i in range(depth):
        if i + 1 < depth:
            for group in ("mix", "up", "down"):
                travelling[(i + 1, group)] = start_weights(i + 1, group, token if i == 0 and group == "mix" else
                                                           (h if group == "mix" else token))
                token = travelling[(i + 1, group)][-1]

        def fetch(group, after, i=i):
            if (i, group) not in arrived:
                arrived[(i, group)] = _exchange_wait(travelling[(i, group)], after, gather=True,
                                                     name=f"weights_{group}_wait_{i}")
            return full_group(group, arrived[(i, group)])

        rep = {n: weights[n][i] for n in REPLICATED}
        h, sv, w = _layer_fwd(h, p[i, 0], fetch, rep, tabs, dims, dep=token if i + 1 < depth or i == 0 else None)
        layer_w.append((w, rep))
        saved.append(sv)
    loss_row, dh, g_final = _loss_head(h, final_norm_g, loss_target[0])
    loss = lax.psum(loss_row[0, 0], MESH_AXES)

    def start_grads(group, gr, dep, i):
        names = groups[group]
        blocks = [gr[n] if n == "w_up" else _blocks_from_full(SHARD_KIND[n], gr[n]).astype(BF16) for n in names]
        return _exchange_start(blocks, gather=False, name=f"grads_{group}_start_{i}", dep=dep)

    def end_grads(group, started, after, i):
        got = _exchange_wait(started, after, gather=False, name=f"grads_{group}_wait_{i}")
        received[i].update(zip(groups[group], got))

    layer_grads = [None] * depth
    received = [dict() for _ in range(depth)]
    pending = None
    for i in reversed(range(depth)):
        w, rep = layer_w[i]
        state = {}

        def on_down(dh2, gr, i=i, state=state):
            state["down"] = start_grads("down", gr, dh2, i)
            return state["down"][-1]

        def on_up(dw_up, gr, i=i, pending=pending, state=state):
            order = gr["conv_w"]
            if pending is not None:
                for group in ("down", "up", "mix"):
                    end_grads(group, pending[group], order, i + 1)
            state["up"] = start_grads("up", gr, order, i)
            return state["up"][-1]

        dh, gr = _layer_bwd(dh, p[i, 0], w, rep, tabs, dims, saved[i],
                            dep=loss.reshape(1, 1) if pending is None else pending["mix"][-1],
                            hooks={"down": on_down, "up": on_up})
        gr["w_uq"] = _heads_join(gr["w_uq"], heads, NOPE_DIM, ROPE_DIM)
        gr["w_ukv"] = _heads_join(gr["w_ukv"], heads, NOPE_DIM, V_DIM)
        layer_grads[i] = gr
        state["mix"] = start_grads("mix", gr, dh, i)
        pending = state
    grad_x = dh[None]

    out = {}

    def update(n):
        shape = weights[n].shape
        recs = [received[i][n].reshape((N_DEV, -1, shape[-1])) for i in range(depth)]
        res = _adamw(recs, _as2d(weights[n]), _as2d(m_in[n]), _as2d(v_in[n]), name="adamw_" + n)
        out[n] = tuple(r.reshape(shape) for r in res)

    end_grads("down", pending["down"], pending["mix"][-1], 0)
    end_grads("up", pending["up"], pending["mix"][-1], 0)
    for n in groups["down"] + groups["up"]:
        update(n)
    end_grads("mix", pending["mix"], out["w_up"][0], 0)
    for n in groups["mix"]:
        update(n)

    small_names = REPLICATED + ("final_norm_g",)

    def pack(get):
        rows = [jnp.stack([get(n, i).reshape(-1) for i in range(depth)]).reshape(-1) for n in REPLICATED]
        rows.append(get("final_norm_g", None).reshape(-1))
        return jnp.concatenate(rows).reshape(1, -1, LANES)

    g_small = pack(lambda n, i: g_final if i is None else layer_grads[i][n])
    w_small = pack(lambda n, i: weights[n] if i is None else weights[n][i])
    m_small = pack(lambda n, i: m_in[n] if i is None else m_in[n][i])
    v_small = pack(lambda n, i: v_in[n] if i is None else v_in[n][i])
    (g_all,) = _all_gather([g_small], name="small_grads_all_gather")
    res_small = _adamw([g_all[:, 0]], w_small, m_small, v_small, name="adamw_small")

    def unpack(flat3):
        flat = flat3.reshape(-1)
        res, off = {}, 0
        for n in REPLICATED:
            width = weights[n].shape[-1]
            res[n] = flat[off:off + depth * width].reshape(depth, width)
            off += depth * width
        res["final_norm_g"] = flat[off:off + d_model]
        return res

    small = [unpack(r) for r in res_small]
    for n in small_names:
        out[n] = tuple(small[k][n] for k in range(4))

    outs = [loss, grad_x]
    for k in range(4):
        outs += [out[n][k] for n in WEIGHT_ORDER]
    return tuple(outs)
```

```python
import math

import jax
import jax.numpy as jnp
from jax import lax
from jax.experimental import pallas as pl
from jax.experimental.pallas import tpu as pltpu

F32 = jnp.float32
BF16 = jnp.bfloat16

N_DEV = 8
MESH_AXES = ("x", "y", "c")
NOPE_DIM = 128
ROPE_DIM = 64
V_DIM = 128
POOL_GROUPS = 4
CONV_TAPS = 3
ROPE_THETA = 10000.0
NORM_EPS = 1e-6
ADAM_LR = 0.001
ADAM_B1 = 0.9
ADAM_B2 = 0.999
ADAM_EPS = 1e-08
ADAM_WD = 0.01
ADAM_STEP = 10
LANES = 128
VMEM_LIMIT_BYTES = 56 * 1024 * 1024

SHARD_KIND = {"w_in": "col", "pool_w": "pool", "w_uq": "col", "w_ukv": "col", "w_out": "row", "w_up": "col",
              "conv_w": "col", "w_down": "row", "w_ple": "col", "w_ple_gate": "row"}
REPLICATED = ("norm_mix_g", "pool_scale", "q_norm_g", "kv_norm_g", "norm_ffn_g", "conv_b", "norm_ple_g")
WEIGHT_ORDER = ("norm_mix_g", "w_in", "pool_w", "pool_scale", "q_norm_g", "w_uq", "kv_norm_g", "w_ukv", "w_out",
                "norm_ffn_g", "w_up", "conv_w", "conv_b", "w_down", "norm_ple_g", "w_ple", "w_ple_gate",
                "final_norm_g")

_pcall = pl.pallas_call


def _params(*sem):
    return pltpu.CompilerParams(dimension_semantics=sem or None, vmem_limit_bytes=VMEM_LIMIT_BYTES)


def _tile(n, pref, mult=LANES):
    if n <= pref:
        return n
    t = (pref // mult) * mult
    while t >= mult:
        if n % t == 0:
            return t
        t -= mult
    return n


MM_VMEM_BUDGET_BYTES = 44 * 1024 * 1024
MM_TILE_PREFS = (1536, 1024, 512, 256)
MM_MIN_TK = 1024
V7X_MXU_FLOPS = 850e12
V7X_SPLIT_K_SLOWDOWN = 1.3
V7X_HBM_BYTES_PER_S = 2.8e12
GRID_STEP_S = 0.35e-6


def _mm_tiles(m, n, k, a_bytes, b_bytes, o_bytes, has_add, tn_fixed=None, tk_fixed=None, a_copied=False,
              b_copied=False):
    out_bytes = o_bytes + (4 if has_add else 0)
    tm_cands = sorted({_tile(m, pref) for pref in MM_TILE_PREFS}, reverse=True)
    tn_cands = [tn_fixed] if tn_fixed else sorted({_tile(n, pref) for pref in MM_TILE_PREFS}, reverse=True)
    if tk_fixed:
        tk_cands = [tk_fixed]
    else:
        tk_cands = [k] + [t for t in range((k - 1) // LANES * LANES, MM_MIN_TK - 1, -LANES) if k % t == 0]
    best = None
    for tm in tm_cands:
        for tn in tn_cands:
            for tk in tk_cands:
                vmem = 2 * (tm * tk * a_bytes + tk * tn * b_bytes) + 2 * tm * tn * out_bytes
                vmem += tm * tn * 4 if tk < k else 0
                vmem += (tm * tk * 2 if a_copied else 0) + (tk * tn * 2 if b_copied else 0)
                if vmem > MM_VMEM_BUDGET_BYTES:
                    continue
                steps = (m // tm) * (n // tn) * (k // tk)
                hbm = a_bytes * m * k * (1 if tk == k else n // tn) + b_bytes * k * n * (m // tm) + out_bytes * m * n
                mxu = 2 * m * n * k / V7X_MXU_FLOPS * (1.0 if tk == k else V7X_SPLIT_K_SLOWDOWN)
                cost = max(mxu, hbm / V7X_HBM_BYTES_PER_S) + steps * GRID_STEP_S
                if best is None or cost < best[0]:
                    best = (cost, tm, tn, tk)
    assert best is not None, (m, n, k)
    return best[1:]


def _mm(a, b, *, name, ta=False, tb=False, add=None, out_dtype=F32, b_blocks=None, out_blocks=None,
        out_init=None, dep=None):
    m = a.shape[1] if ta else a.shape[0]
    kdim = a.shape[0] if ta else a.shape[1]
    tn_fixed = tk_fixed = None
    if b_blocks is None:
        n, k_b = (b.shape if tb else b.shape[::-1])
        assert k_b == kdim, (name, k_b, kdim)
    else:
        first_b, count_b = b_blocks
        per_b = b.shape[2]
        if tb:
            n = b.shape[1]
            assert kdim == count_b * per_b, name
            tk_fixed = per_b
        else:
            n = count_b * per_b
            assert kdim == b.shape[1], name
            tn_fixed = per_b
    if out_blocks is not None:
        nb_out, first_o, tn_fixed = out_blocks
    tm, tn, tk = _mm_tiles(m, n, kdim, a.dtype.itemsize, b.dtype.itemsize, jnp.dtype(out_dtype).itemsize,
                           add is not None, tn_fixed, tk_fixed, a_copied=ta or a.dtype != BF16,
                           b_copied=tb or b.dtype != BF16)
    assert kdim % tk == 0 and n % tn == 0 and m % tm == 0, name
    nk = kdim // tk
    dims = (((0 if ta else 1,), (1 if tb else 0,)), ((), ()))

    def body(*refs):
        a_ref, b_ref = refs[0], refs[1]
        add_ref = refs[2] if add is not None else None
        o_ref = refs[n_in]
        part = lax.dot_general(a_ref[...].astype(BF16), b_ref[...].astype(BF16), dims, preferred_element_type=F32)

        def finish(r):
            if add_ref is not None:
                r = r + add_ref[...].astype(F32)
            o_ref[...] = r.astype(out_dtype)

        if nk == 1:
            finish(part)
            return
        acc = refs[n_in + 1]
        k = pl.program_id(2)

        @pl.when(k == 0)
        def _():
            acc[...] = part

        @pl.when(jnp.logical_and(k > 0, k < nk - 1))
        def _():
            acc[...] += part

        @pl.when(k == nk - 1)
        def _():
            finish(acc[...] + part)

    if ta:
        a_spec = pl.BlockSpec((tk, tm), lambda i, j, k: (k, i))
    else:
        a_spec = pl.BlockSpec((tm, tk), lambda i, j, k: (i, k))
    if b_blocks is not None and tb:
        b_spec = pl.BlockSpec((None, tn, tk), lambda i, j, k: (k + first_b, j, 0))
    elif b_blocks is not None:
        b_spec = pl.BlockSpec((None, tk, tn), lambda i, j, k: (j + first_b, k, 0))
    elif tb:
        b_spec = pl.BlockSpec((tn, tk), lambda i, j, k: (j, k))
    else:
        b_spec = pl.BlockSpec((tk, tn), lambda i, j, k: (k, j))
    in_specs = [a_spec, b_spec]
    args = [a, b]
    if add is not None:
        in_specs.append(pl.BlockSpec((tm, tn), lambda i, j, k: (i, j)))
        args.append(add)
    aliases = {}
    if out_init is not None:
        aliases = {len(args): 0}
        in_specs.append(HBM_SPEC)
        args.append(out_init)
    if dep is not None:
        in_specs.append(HBM_SPEC)
        args.append(dep)
    n_in = len(args)
    if out_blocks is None:
        out_spec = pl.BlockSpec((tm, tn), lambda i, j, k: (i, j))
        out_shape = jax.ShapeDtypeStruct((m, n), out_dtype)
    else:
        out_spec = pl.BlockSpec((None, tm, tn), lambda i, j, k: (j + first_o, i, 0))
        out_shape = jax.ShapeDtypeStruct((nb_out, m, tn), out_dtype)
    return _pcall(
        body, name=name, grid=(m // tm, n // tn, nk), in_specs=in_specs, out_specs=out_spec, out_shape=out_shape,
        scratch_shapes=[pltpu.VMEM((tm, tn), F32)] if nk > 1 else [], input_output_aliases=aliases,
        compiler_params=_params("parallel", "parallel", "arbitrary"),
    )(*args)


def _rms_fwd(h, g, *, name, col_block=0, dep=None):
    s = h.shape[0]
    d = g.shape[-1]
    ts = _tile(s, 512, 8)

    def body(h_ref, g_ref, *rest):
        n_ref = rest[-1]
        x = h_ref[...]
        r = lax.rsqrt(jnp.mean(x * x, axis=-1, keepdims=True) + NORM_EPS)
        n_ref[...] = (x * r * g_ref[...]).astype(BF16)

    deps = [] if dep is None else [dep]
    return _pcall(
        body, name=name, grid=(s // ts,),
        in_specs=[pl.BlockSpec((ts, d), lambda i: (i, col_block)), pl.BlockSpec((1, d), lambda i: (0, 0))]
        + [HBM_SPEC] * len(deps),
        out_specs=pl.BlockSpec((ts, d), lambda i: (i, 0)),
        out_shape=jax.ShapeDtypeStruct((s, d), BF16),
        compiler_params=_params("parallel"),
    )(h, g.reshape(1, d), *deps)


def _rms_bwd(dn, h, g, *, name, res=None, col_block=0, out_dtype=F32, matmul_copy=False):
    s = dn.shape[0]
    d = g.shape[-1]
    ts = _tile(s, 512, 8)

    def body(*refs):
        dn_ref, h_ref, g_ref = refs[:3]
        res_ref = refs[3] if res is not None else None
        dh_ref, dg_ref = refs[n_in], refs[-1]
        i = pl.program_id(0)
        x = h_ref[...]
        r = lax.rsqrt(jnp.mean(x * x, axis=-1, keepdims=True) + NORM_EPS)
        nh = x * r
        dnv = dn_ref[...]
        gd = dnv * g_ref[...]
        dh = (gd - nh * jnp.mean(gd * nh, axis=-1, keepdims=True)) * r
        if res_ref is not None:
            dh = dh + res_ref[...]
        dh_ref[...] = dh.astype(out_dtype)
        if matmul_copy:
            refs[n_in + 1][...] = dh.astype(BF16)
        part = jnp.sum(dnv * nh, axis=0, keepdims=True)

        @pl.when(i == 0)
        def _():
            dg_ref[...] = part

        @pl.when(i > 0)
        def _():
            dg_ref[...] += part

    row = pl.BlockSpec((ts, d), lambda i: (i, 0))
    in_specs = [row, pl.BlockSpec((ts, d), lambda i: (i, col_block)), pl.BlockSpec((1, d), lambda i: (0, 0))]
    args = [dn, h, g.reshape(1, d)]
    if res is not None:
        in_specs.append(row)
        args.append(res)
    n_in = len(args)
    copies = [row] if matmul_copy else []
    return _pcall(
        body, name=name, grid=(s // ts,), in_specs=in_specs,
        out_specs=(row, *copies, pl.BlockSpec((1, d), lambda i: (0, 0))),
        out_shape=(jax.ShapeDtypeStruct((s, d), out_dtype), *[jax.ShapeDtypeStruct((s, d), BF16) for _ in copies],
                   jax.ShapeDtypeStruct((1, d), F32)),
        compiler_params=_params("arbitrary"),
    )(*args)


ROW_CHUNK = 512


def _rows_with_halo(ref, r, t_rows, n_chunks, before, after):
    r0 = r * t_rows
    parts = []
    if before:
        hb = ref[pl.ds(pl.multiple_of(jnp.maximum(r0 - before, 0), before), before), :]
        parts.append(jnp.where(r > 0, hb, jnp.zeros_like(hb)))
    parts.append(ref[pl.ds(pl.multiple_of(r0, t_rows), t_rows), :])
    if after:
        ha = ref[pl.ds(pl.multiple_of(jnp.minimum(r0 + t_rows, n_chunks * t_rows - after), after), after), :]
        parts.append(jnp.where(r < n_chunks - 1, ha, jnp.zeros_like(ha)))
    return jnp.concatenate(parts, axis=0)


POOL_HALO = 16


def _pool_fwd(u, pool_w, pool_scale, *, c):
    s = u.shape[0]
    g_n = POOL_GROUPS
    tr = _tile(s, ROW_CHUNK, POOL_HALO)
    n_chunks = s // tr

    def body(u_ref, pw_ref, sc_ref, y_ref, d_ref):
        r = pl.program_id(1)
        w = jnp.left_shift(2, pl.program_id(0))
        xe = _rows_with_halo(u_ref, r, tr, n_chunks, POOL_HALO, 0)
        acc = xe
        for k in (1, 2, 4, 8):
            acc = jnp.where(k < w, acc + pltpu.roll(acc, k, 0), acc)
        t = r * tr + lax.broadcasted_iota(jnp.int32, (tr, 1), 0)
        cnt = jnp.minimum(t + 1, w).astype(F32)
        diff = (acc[POOL_HALO:] / cnt - xe[POOL_HALO:]).astype(BF16)
        d_ref[...] = diff
        y = jnp.dot(diff, pw_ref[...], preferred_element_type=F32) * sc_ref[...]
        y_ref[...] = y.astype(BF16)

    out = pl.BlockSpec((tr, c), lambda g, r: (r, g))
    return _pcall(
        body, name="pool_fwd", grid=(g_n, n_chunks),
        in_specs=[pl.BlockSpec((s, c), lambda g, r: (0, g)), pl.BlockSpec((None, c, c), lambda g, r: (g, 0, 0)),
                  pl.BlockSpec((1, c), lambda g, r: (0, g))],
        out_specs=(out, out),
        out_shape=(jax.ShapeDtypeStruct((s, g_n * c), BF16), jax.ShapeDtypeStruct((s, g_n * c), BF16)),
        compiler_params=_params("parallel", "arbitrary"),
    )(u, pool_w, pool_scale.reshape(1, g_n * c))


def _pool_bwd(dcat, diff, pool_w, pool_scale, *, c):
    s = dcat.shape[0]
    g_n = POOL_GROUPS
    tr = _tile(s, ROW_CHUNK, POOL_HALO)
    n_chunks = s // tr

    def body(dy_ref, d_ref, pw_ref, sc_ref, du_ref, dpw_ref, dsc_ref):
        r = pl.program_id(1)
        w = jnp.left_shift(2, pl.program_id(0))
        dye = _rows_with_halo(dy_ref, r, tr, n_chunks, 0, POOL_HALO)
        diff = d_ref[pl.ds(pl.multiple_of(r * tr, tr), tr), :]
        pw = pw_ref[...]
        yp = jnp.dot(diff, pw, preferred_element_type=F32)
        dsc = jnp.sum(dye[:tr] * yp, axis=0, keepdims=True)
        dyp = (dye * sc_ref[...]).astype(BF16)
        ddiff = lax.dot_general(dyp, pw, (((1,), (1,)), ((), ())), preferred_element_type=F32)
        dpw = lax.dot_general(diff, dyp[:tr], (((0,), (0,)), ((), ())), preferred_element_type=F32)
        t = r * tr + lax.broadcasted_iota(jnp.int32, (tr + POOL_HALO, 1), 0)
        cnt = jnp.minimum(t + 1, w).astype(F32)
        acc = ddiff / cnt
        rows = tr + POOL_HALO
        for k in (1, 2, 4, 8):
            acc = jnp.where(k < w, acc + pltpu.roll(acc, rows - k, 0), acc)
        du_ref[...] = (acc[:tr] - ddiff[:tr]).astype(BF16)

        @pl.when(r == 0)
        def _():
            dpw_ref[...] = dpw
            dsc_ref[...] = dsc

        @pl.when(r > 0)
        def _():
            dpw_ref[...] += dpw
            dsc_ref[...] += dsc

    col = lambda g, r: (0, g)
    wspec = pl.BlockSpec((None, c, c), lambda g, r: (g, 0, 0))
    vec = pl.BlockSpec((1, c), col)
    return _pcall(
        body, name="pool_bwd", grid=(g_n, n_chunks),
        in_specs=[pl.BlockSpec((s, c), col), pl.BlockSpec((s, c), col), wspec, vec],
        out_specs=(pl.BlockSpec((tr, c), lambda g, r: (r, g)), wspec, vec),
        out_shape=(jax.ShapeDtypeStruct((s, g_n * c), BF16), jax.ShapeDtypeStruct((g_n, c, c), F32),
                   jax.ShapeDtypeStruct((1, g_n * c), F32)),
        compiler_params=_params("parallel", "arbitrary"),
    )(dcat, diff, pool_w, pool_scale.reshape(1, g_n * c))


def _swap_halves(x, lane):
    return jnp.where((lane % ROPE_DIM) < ROPE_DIM // 2, pltpu.roll(x, LANES - ROPE_DIM // 2, 1),
                     pltpu.roll(x, ROPE_DIM // 2, 1))


def _qkv_prep(q, kv, kr2, cos_t, sin_t, *, heads):
    s = q.shape[0]
    ts = _tile(s, 256, 8)

    def body(q_ref, kv_ref, kr_ref, cos_ref, sin_ref, qo_ref, ko_ref, vo_ref):
        lane = lax.broadcasted_iota(jnp.int32, (ts, LANES), 1)
        cos_v = cos_ref[...]
        sin_v = sin_ref[...]

        def rope(x):
            return x * cos_v + _swap_halves(x, lane) * sin_v

        kr = rope(kr_ref[...]).astype(BF16)
        for pair in range(heads // 2):
            qr = rope(q_ref[:, (heads + pair) * LANES:(heads + pair + 1) * LANES])
            for half in range(2):
                h = 2 * pair + half
                qo_ref[h, :, :LANES] = q_ref[:, h * LANES:(h + 1) * LANES].astype(BF16)
                qo_ref[h, :, LANES:] = jnp.where(lane // ROPE_DIM == half, qr, 0.0).astype(BF16)
        for h in range(heads):
            ko_ref[h, :, :LANES] = kv_ref[:, h * LANES:(h + 1) * LANES].astype(BF16)
            ko_ref[h, :, LANES:] = kr
            vo_ref[h] = kv_ref[:, (heads + h) * LANES:(heads + h + 1) * LANES].astype(BF16)

    tab = pl.BlockSpec((ts, LANES), lambda i: (i, 0))
    return _pcall(
        body, name="qkv_prep", grid=(s // ts,),
        in_specs=[pl.BlockSpec((ts, q.shape[1]), lambda i: (i, 0)), pl.BlockSpec((ts, kv.shape[1]), lambda i: (i, 0)),
                  tab, tab, tab],
        out_specs=(pl.BlockSpec((heads, ts, 2 * LANES), lambda i: (0, i, 0)),
                   pl.BlockSpec((heads, ts, 2 * LANES), lambda i: (0, i, 0)),
                   pl.BlockSpec((heads, ts, LANES), lambda i: (0, i, 0))),
        out_shape=(jax.ShapeDtypeStruct((heads, s, 2 * LANES), BF16),
                   jax.ShapeDtypeStruct((heads, s, 2 * LANES), BF16),
                   jax.ShapeDtypeStruct((heads, s, LANES), BF16)),
        compiler_params=_params("parallel"),
    )(q, kv, kr2, cos_t, sin_t)


LOG2_E = 1.4426950408889634


FLASH_FWD_TK = 512
FLASH_FWD_SPLITS = 2


def _flash_fwd(qp, kp, v, *, scale, tq=512):
    heads, s, dk = qp.shape
    tq = _tile(s, tq, 16)
    tk = _tile(tq, FLASH_FWD_TK, 16)
    splits = FLASH_FWD_SPLITS
    th = tq // splits
    band = tq // tk
    c = scale * LOG2_E

    def body(q_ref, k_ref, v_ref, o_ref, ob_ref, lse_ref):
        i = pl.program_id(1)
        qs = [q_ref[hh * th:(hh + 1) * th, :] for hh in range(splits)]

        def skipped(hh, col0):
            return col0 is not None and col0 >= (hh + 1) * th

        def scores(start, col0):
            kb = k_ref[pl.ds(pl.multiple_of(start, tk), tk), :]
            return tuple(None if skipped(hh, col0) else
                         lax.dot_general(qs[hh], kb, (((1,), (1,)), ((), ())), preferred_element_type=F32)
                         for hh in range(splits))

        def absorb(start, scs, state, col0):
            vb = v_ref[pl.ds(pl.multiple_of(start, tk), tk), :]
            new = []
            for hh in range(splits):
                if scs[hh] is None:
                    new.append(state[hh])
                    continue
                m_old, l_old, acc = state[hh]
                sc = scs[hh]
                if col0 is not None and col0 + tk - 1 > hh * th:
                    rows = hh * th + lax.broadcasted_iota(jnp.int32, (th, tk), 0)
                    cols = col0 + lax.broadcasted_iota(jnp.int32, (th, tk), 1)
                    sc = jnp.where(rows >= cols, sc, -jnp.inf)
                m_new = jnp.maximum(m_old, jnp.max(sc, axis=-1, keepdims=True))
                alpha = jnp.exp2((m_old - m_new) * c)
                p = jnp.exp2((sc - m_new) * c)
                l_new = alpha * l_old + jnp.sum(p, axis=-1, keepdims=True)
                acc = alpha * acc + jnp.dot(p.astype(BF16), vb, preferred_element_type=F32)
                new.append((m_new, l_new, acc))
            return tuple(new)

        def step(j, carry):
            state, scs = carry
            nxt = scores((j + 1) * tk, None)
            return absorb(j * tk, scs, state, None), nxt

        init = tuple((jnp.full((th, 1), -jnp.inf, F32), jnp.zeros((th, 1), F32), jnp.zeros((th, V_DIM), F32))
                     for _ in range(splits))
        state, scs = lax.fori_loop(0, i * band, step, (init, scores(0, None)))
        for b in range(band):
            nxt = scores(i * tq + (b + 1) * tk, (b + 1) * tk) if b + 1 < band else None
            state = absorb(i * tq + b * tk, scs, state, b * tk)
            scs = nxt
        carry = state
        for hh in range(splits):
            m_fin, l_fin, acc = carry[hh]
            out = acc / l_fin
            o_ref[hh * th:(hh + 1) * th, :] = out
            ob_ref[hh * th:(hh + 1) * th, :] = out.astype(BF16)
            lse_ref[hh * th:(hh + 1) * th, :] = m_fin * scale + jnp.log(l_fin)

    return _pcall(
        body, name="flash_fwd", grid=(heads, s // tq),
        in_specs=[pl.BlockSpec((None, tq, dk), lambda h, i: (h, i, 0)),
                  pl.BlockSpec((None, s, dk), lambda h, i: (h, 0, 0)),
                  pl.BlockSpec((None, s, V_DIM), lambda h, i: (h, 0, 0))],
        out_specs=(pl.BlockSpec((tq, V_DIM), lambda h, i: (i, h)), pl.BlockSpec((tq, V_DIM), lambda h, i: (i, h)),
                   pl.BlockSpec((None, tq, 1), lambda h, i: (h, i, 0))),
        out_shape=(jax.ShapeDtypeStruct((s, heads * V_DIM), F32), jax.ShapeDtypeStruct((s, heads * V_DIM), BF16),
                   jax.ShapeDtypeStruct((heads, s, 1), F32)),
        compiler_params=_params("parallel", "arbitrary"),
    )(qp, kp, v)


def _flash_bwd(qp, kp, v, dcat, o, lse, *, scale, do_col0, tq=512):
    heads, s, dk = qp.shape
    tq = _tile(s, tq, 16)
    tk = tq
    nq = s // tq

    def body(k_ref, v_ref, q_ref, do_ref, o_ref, lse_ref, dq_ref, dk_ref, dv_ref):
        j = pl.program_id(1)

        @pl.when(j == 0)
        def _():
            dq_ref[...] = jnp.zeros_like(dq_ref)

        kb = k_ref[...]
        vb = v_ref[...]

        def block(i, carry, diag):
            dk_acc, dv_acc = carry
            rows_at = pl.ds(pl.multiple_of(i * tq, tq), tq)
            qb = q_ref[rows_at, :]
            do = do_ref[rows_at, :]
            sc = lax.dot_general(qb, kb, (((1,), (1,)), ((), ())), preferred_element_type=F32) * scale
            if diag:
                rows = lax.broadcasted_iota(jnp.int32, (tq, tk), 0)
                cols = lax.broadcasted_iota(jnp.int32, (tq, tk), 1)
                sc = jnp.where(rows >= cols, sc, -jnp.inf)
            p = jnp.exp(sc - lse_ref[rows_at, :])
            dob = do.astype(BF16)
            dv_acc = dv_acc + lax.dot_general(p.astype(BF16), dob, (((0,), (0,)), ((), ())),
                                              preferred_element_type=F32)
            dp = lax.dot_general(dob, vb, (((1,), (1,)), ((), ())), preferred_element_type=F32)
            delta = jnp.sum(do * o_ref[rows_at, :], axis=-1, keepdims=True)
            ds = (p * (dp - delta) * scale).astype(BF16)
            dk_acc = dk_acc + lax.dot_general(ds, qb, (((0,), (0,)), ((), ())), preferred_element_type=F32)
            dq_ref[rows_at, :] += jnp.dot(ds, kb, preferred_element_type=F32)
            return dk_acc, dv_acc

        carry = block(j, (jnp.zeros((tk, dk), F32), jnp.zeros((tk, V_DIM), F32)), True)
        carry = lax.fori_loop(j + 1, nq, lambda i, cr: block(i, cr, False), carry)
        dk_ref[...] = carry[0]
        dv_ref[...] = carry[1]

    whole = lambda h, j: (h, 0, 0)
    return _pcall(
        body, name="flash_bwd", grid=(heads, nq),
        in_specs=[pl.BlockSpec((None, tk, dk), lambda h, j: (h, j, 0)),
                  pl.BlockSpec((None, tk, V_DIM), lambda h, j: (h, j, 0)),
                  pl.BlockSpec((None, s, dk), whole),
                  pl.BlockSpec((s, V_DIM), lambda h, j: (0, do_col0 + h)),
                  pl.BlockSpec((s, V_DIM), lambda h, j: (0, h)),
                  pl.BlockSpec((None, s, 1), whole)],
        out_specs=(pl.BlockSpec((None, s, dk), whole),
                   pl.BlockSpec((None, tk, dk), lambda h, j: (h, j, 0)),
                   pl.BlockSpec((None, tk, V_DIM), lambda h, j: (h, j, 0))),
        out_shape=(jax.ShapeDtypeStruct((heads, s, dk), F32), jax.ShapeDtypeStruct((heads, s, dk), F32),
                   jax.ShapeDtypeStruct((heads, s, V_DIM), F32)),
        compiler_params=_params("parallel", "arbitrary"),
    )(kp, v, qp, dcat, o, lse)


def _attn_bwd_post(dqp, dkp, dv, cos_t, sin_t):
    heads, s, _ = dqp.shape
    ts = _tile(s, 256, 8)

    def body(dq_ref, dk_ref, dv_ref, cos_ref, sin_ref, q_out, kv_out, kr_out):
        lane = lax.broadcasted_iota(jnp.int32, (ts, LANES), 1)
        cos_v = cos_ref[...]
        sin_v = sin_ref[...]

        def rope_t(dy):
            return dy * cos_v + _swap_halves(dy * sin_v, lane)

        kr_sum = jnp.zeros((ts, LANES), F32)
        for h in range(heads):
            q_out[:, h * LANES:(h + 1) * LANES] = dq_ref[h, :, :LANES].astype(BF16)
            kv_out[:, h * LANES:(h + 1) * LANES] = dk_ref[h, :, :LANES].astype(BF16)
            kv_out[:, (heads + h) * LANES:(heads + h + 1) * LANES] = dv_ref[h].astype(BF16)
            kr_sum = kr_sum + dk_ref[h, :, LANES:]
        for pair in range(heads // 2):
            r = jnp.where(lane < ROPE_DIM, dq_ref[2 * pair, :, LANES:], dq_ref[2 * pair + 1, :, LANES:])
            q_out[:, (heads + pair) * LANES:(heads + pair + 1) * LANES] = rope_t(r).astype(BF16)
        kr = rope_t(kr_sum)
        kr_out[...] = (kr + pltpu.roll(kr, ROPE_DIM, 1)).astype(BF16)

    wq = heads * (NOPE_DIM + ROPE_DIM)
    wkv = heads * (NOPE_DIM + V_DIM)
    tab = pl.BlockSpec((ts, LANES), lambda i: (i, 0))
    return _pcall(
        body, name="attn_bwd_post", grid=(s // ts,),
        in_specs=[pl.BlockSpec((heads, ts, 2 * LANES), lambda i: (0, i, 0)),
                  pl.BlockSpec((heads, ts, 2 * LANES), lambda i: (0, i, 0)),
                  pl.BlockSpec((heads, ts, LANES), lambda i: (0, i, 0)), tab, tab],
        out_specs=(pl.BlockSpec((ts, wq), lambda i: (i, 0)), pl.BlockSpec((ts, wkv), lambda i: (i, 0)), tab),
        out_shape=(jax.ShapeDtypeStruct((s, wq), BF16), jax.ShapeDtypeStruct((s, wkv), BF16),
                   jax.ShapeDtypeStruct((s, LANES), BF16)),
        compiler_params=_params("parallel"),
    )(dqp, dkp, dv, cos_t, sin_t)


CONV_HALO = 16


def _conv_gate(xe, cw_ref, cb_ref):
    x1 = pltpu.roll(xe, 1, 0)
    x2 = pltpu.roll(xe, 2, 0)
    return cw_ref[2:3, :] * xe + cw_ref[1:2, :] * x1 + cw_ref[0:1, :] * x2 + cb_ref[...], x1, x2


def _ffn_act_fwd(gu, conv_w, conv_b, *, d_ff, tc=256):
    s = gu.shape[0]
    tc = _tile(d_ff, tc)
    nf = d_ff // tc
    tr = _tile(s, ROW_CHUNK, CONV_HALO)
    n_chunks = s // tr

    def body(g_ref, u_ref, cw_ref, cb_ref, a_ref):
        r = pl.program_id(1)
        xe = _rows_with_halo(g_ref, r, tr, n_chunks, CONV_HALO, 0).astype(F32)
        gc = _conv_gate(xe, cw_ref, cb_ref)[0][CONV_HALO:]
        a_ref[...] = (gc * jax.nn.sigmoid(gc) * u_ref[...].astype(F32)).astype(BF16)

    return _pcall(
        body, name="ffn_act_fwd", grid=(nf, n_chunks),
        in_specs=[pl.BlockSpec((s, tc), lambda j, r: (0, j)), pl.BlockSpec((tr, tc), lambda j, r: (r, nf + j)),
                  pl.BlockSpec((CONV_TAPS, tc), lambda j, r: (0, j)), pl.BlockSpec((1, tc), lambda j, r: (0, j))],
        out_specs=pl.BlockSpec((tr, tc), lambda j, r: (r, j)),
        out_shape=jax.ShapeDtypeStruct((s, d_ff), BF16),
        compiler_params=_params("parallel", "arbitrary"),
    )(gu, gu, conv_w, conv_b.reshape(1, d_ff))


def _ffn_act_bwd(da, gu, conv_w, conv_b, *, d_ff, tc=256):
    s = gu.shape[0]
    tc = _tile(d_ff, tc)
    nf = d_ff // tc
    tr = _tile(s, ROW_CHUNK, CONV_HALO)
    n_chunks = s // tr
    rows = tr + 2 * CONV_HALO
    main = slice(CONV_HALO, CONV_HALO + tr)

    def body(da_ref, g_ref, u_ref, cw_ref, cb_ref, dg_ref, du_ref, dcw_ref, dcb_ref):
        r = pl.program_id(1)
        xe = _rows_with_halo(g_ref, r, tr, n_chunks, CONV_HALO, CONV_HALO).astype(F32)
        dae = _rows_with_halo(da_ref, r, tr, n_chunks, CONV_HALO, CONV_HALO).astype(F32)
        ue = _rows_with_halo(u_ref, r, tr, n_chunks, CONV_HALO, CONV_HALO).astype(F32)
        gc, x1, x2 = _conv_gate(xe, cw_ref, cb_ref)
        sg = jax.nn.sigmoid(gc)
        du_ref[...] = (dae * gc * sg)[main].astype(BF16)
        dgc = dae * ue * sg * (1.0 + gc * (1.0 - sg))
        dg = (cw_ref[2:3, :] * dgc + cw_ref[1:2, :] * pltpu.roll(dgc, rows - 1, 0)
              + cw_ref[0:1, :] * pltpu.roll(dgc, rows - 2, 0))
        dg_ref[...] = dg[main].astype(BF16)
        dgc_m = dgc[main]
        dcb = jnp.sum(dgc_m, axis=0, keepdims=True)
        dcw = jnp.concatenate([jnp.sum(dgc_m * x2[main], axis=0, keepdims=True),
                               jnp.sum(dgc_m * x1[main], axis=0, keepdims=True),
                               jnp.sum(dgc_m * xe[main], axis=0, keepdims=True)], axis=0)

        @pl.when(r == 0)
        def _():
            dcb_ref[...] = dcb
            dcw_ref[...] = dcw

        @pl.when(r > 0)
        def _():
            dcb_ref[...] += dcb
            dcw_ref[...] += dcw

    col = pl.BlockSpec((s, tc), lambda j, r: (0, j))
    out = pl.BlockSpec((tr, tc), lambda j, r: (r, j))
    return _pcall(
        body, name="ffn_act_bwd", grid=(nf, n_chunks),
        in_specs=[col, col, pl.BlockSpec((s, tc), lambda j, r: (0, nf + j)),
                  pl.BlockSpec((CONV_TAPS, tc), lambda j, r: (0, j)), pl.BlockSpec((1, tc), lambda j, r: (0, j))],
        out_specs=(out, out, pl.BlockSpec((CONV_TAPS, tc), lambda j, r: (0, j)),
                   pl.BlockSpec((1, tc), lambda j, r: (0, j))),
        out_shape=(jax.ShapeDtypeStruct((s, d_ff), BF16), jax.ShapeDtypeStruct((s, d_ff), BF16),
                   jax.ShapeDtypeStruct((CONV_TAPS, d_ff), F32), jax.ShapeDtypeStruct((1, d_ff), F32)),
        compiler_params=_params("parallel", "arbitrary"),
    )(da, gu, gu, conv_w, conv_b.reshape(1, d_ff))


def _ple_fwd(h, z, e):
    s, d = h.shape
    ts = _tile(s, 512, 8)

    def body(h_ref, z_ref, e_ref, o_ref):
        o_ref[...] = h_ref[...] + e_ref[...] * jax.nn.sigmoid(z_ref[...])

    row = pl.BlockSpec((ts, d), lambda i: (i, 0))
    return _pcall(body, name="ple_fwd", grid=(s // ts,), in_specs=[row, row, row], out_specs=row,
                  out_shape=jax.ShapeDtypeStruct((s, d), F32), compiler_params=_params("parallel"))(h, z, e)


def _ple_bwd(dh, z, e, dep=None):
    s, d = dh.shape
    ts = _tile(s, 512, 8)

    def body(dh_ref, z_ref, e_ref, *rest):
        de_ref, dz_ref = rest[-2:]
        gt = jax.nn.sigmoid(z_ref[...])
        dhv = dh_ref[...]
        de_ref[...] = (dhv * gt).astype(BF16)
        dz_ref[...] = (dhv * e_ref[...] * gt * (1.0 - gt)).astype(BF16)

    row = pl.BlockSpec((ts, d), lambda i: (i, 0))
    deps = [] if dep is None else [dep]
    return _pcall(body, name="ple_bwd", grid=(s // ts,), in_specs=[row, row, row] + [HBM_SPEC] * len(deps),
                  out_specs=(row, row),
                  out_shape=(jax.ShapeDtypeStruct((s, d), BF16), jax.ShapeDtypeStruct((s, d), BF16)),
                  compiler_params=_params("parallel"))(dh, z, e, *deps)


def _loss_head(h, g, target):
    s, d = h.shape
    ts = _tile(s, 512, 8)

    def body(h_ref, g_ref, t_ref, loss_ref, dh_ref, dg_ref):
        i = pl.program_id(0)
        x = h_ref[...]
        gv = g_ref[...]
        r = lax.rsqrt(jnp.mean(x * x, axis=-1, keepdims=True) + NORM_EPS)
        nh = x * r
        err = nh * gv - t_ref[...]
        part_loss = 0.5 * jnp.sum(jnp.mean(err * err, axis=-1, keepdims=True), axis=0, keepdims=True)
        dy = err * (1.0 / d)
        gd = dy * gv
        dh_ref[...] = (gd - nh * jnp.mean(gd * nh, axis=-1, keepdims=True)) * r
        part_g = jnp.sum(dy * nh, axis=0, keepdims=True)
        part_l = jnp.broadcast_to(part_loss, (1, LANES))

        @pl.when(i == 0)
        def _():
            dg_ref[...] = part_g
            loss_ref[...] = part_l

        @pl.when(i > 0)
        def _():
            dg_ref[...] += part_g
            loss_ref[...] += part_l

    row = pl.BlockSpec((ts, d), lambda i: (i, 0))
    vec = pl.BlockSpec((1, d), lambda i: (0, 0))
    return _pcall(
        body, name="loss_head", grid=(s // ts,), in_specs=[row, vec, row],
        out_specs=(pl.BlockSpec((1, LANES), lambda i: (0, 0)), row, vec),
        out_shape=(jax.ShapeDtypeStruct((1, LANES), F32), jax.ShapeDtypeStruct((s, d), F32),
                   jax.ShapeDtypeStruct((1, d), F32)),
        compiler_params=_params("arbitrary"),
    )(h, g.reshape(1, d), target)


HBM_SPEC = pl.BlockSpec(memory_space=pl.ANY)


def _flat_index(px, py, pc):
    return 4 * px + 2 * py + pc


def _all_gather(shards, *, name):
    n = len(shards)

    def body(*refs):
        ins, outs = refs[:n], refs[n:2 * n]
        send_sems, recv_sems, local_sems = refs[2 * n:]
        x, y, c = lax.axis_index("x"), lax.axis_index("y"), lax.axis_index("c")
        me, sibling = (x, y, c), (x, y, 1 - c)
        chips = [(1 - x, y), (x, 1 - y), (1 - x, 1 - y)]

        def copy(a, k, block, to, src=None):
            slot = outs[a].at[_flat_index(*block)]
            return pltpu.make_async_remote_copy(
                src_ref=slot if src is None else src, dst_ref=slot,
                send_sem=send_sems.at[a, k], recv_sem=recv_sems.at[a, k],
                device_id=to, device_id_type=pl.DeviceIdType.MESH)

        mine, first, passed = [], [], []
        for a in range(n):
            cp = pltpu.make_async_copy(ins[a], outs[a].at[_flat_index(*me)], local_sems.at[a])
            cp.start()
            mine.append(cp)
            first.append(copy(a, 0, me, sibling, src=ins[a]))
            first += [copy(a, 1 + j, me, (*chip, c), src=ins[a]) for j, chip in enumerate(chips)]
        for cp in first:
            cp.start()
        for j, chip in enumerate(chips):
            for a in range(n):
                copy(a, 1 + j, (*chip, c), me).wait_recv()
                fwd = copy(a, 4 + j, (*chip, c), sibling)
                fwd.start()
                passed.append(fwd)
        for a in range(n):
            copy(a, 0, sibling, me).wait_recv()
            for j, chip in enumerate(chips):
                copy(a, 4 + j, (*chip, 1 - c), me).wait_recv()
        for cp in first + passed:
            cp.wait_send()
        for cp in mine:
            cp.wait()

    return _pcall(
        body, name=name,
        in_specs=[HBM_SPEC] * n, out_specs=[HBM_SPEC] * n,
        out_shape=[jax.ShapeDtypeStruct((N_DEV,) + a.shape, a.dtype) for a in shards],
        scratch_shapes=[pltpu.SemaphoreType.DMA((n, 7)), pltpu.SemaphoreType.DMA((n, 7)),
                        pltpu.SemaphoreType.DMA((n,))],
    )(*shards)


HBM_ONLY = pl.BlockSpec(memory_space=pltpu.HBM)
SEM_SPEC = pl.BlockSpec(memory_space=pltpu.SEMAPHORE)
N_PEERS = N_DEV - 1
PEER_FLIPS = ((0, 0, 1), (1, 0, 0), (0, 1, 0), (1, 1, 0), (1, 0, 1), (0, 1, 1), (1, 1, 1))


def _exchange_refs(gather, src_refs, land_refs, send_sems, recv_sems):
    x, y, c = lax.axis_index("x"), lax.axis_index("y"), lax.axis_index("c")
    me = _flat_index(x, y, c)
    peers = [(x ^ fx, y ^ fy, c ^ fc) for fx, fy, fc in PEER_FLIPS]

    def out_copy(a, k):
        src = src_refs[a] if gather else src_refs[a].at[_flat_index(*peers[k])]
        return pltpu.make_async_remote_copy(
            src_ref=src, dst_ref=land_refs[a].at[me], send_sem=send_sems.at[a * N_PEERS + k],
            recv_sem=recv_sems.at[a * N_PEERS + k], device_id=peers[k], device_id_type=pl.DeviceIdType.MESH)

    def in_copy(a, k):
        src = src_refs[a] if gather else src_refs[a].at[me]
        return pltpu.make_async_remote_copy(
            src_ref=src, dst_ref=land_refs[a].at[_flat_index(*peers[k])], send_sem=send_sems.at[a * N_PEERS + k],
            recv_sem=recv_sems.at[a * N_PEERS + k], device_id=peers[k], device_id_type=pl.DeviceIdType.MESH)

    return out_copy, in_copy


def _exchange_start(srcs, *, gather, name, dep):
    n = len(srcs)
    lands = [lax.empty((N_DEV,) + a.shape if gather else a.shape, a.dtype) for a in srcs]

    def body(*refs):
        src_refs, land_refs = refs[:n], refs[n:2 * n]
        send_sems, recv_sems = refs[2 * n + 1], refs[2 * n + 2]
        token = refs[-1]
        out_copy, _ = _exchange_refs(gather, src_refs, land_refs, send_sems, recv_sems)
        for k in range(N_PEERS):
            for a in range(n):
                out_copy(a, k).start()
        token[...] = jnp.zeros_like(token)

    hbm = lambda a: pltpu.with_memory_space_constraint(a, pltpu.HBM)
    return _pcall(
        body, name=name,
        out_shape=(pltpu.SemaphoreType.DMA((n * N_PEERS,)), pltpu.SemaphoreType.DMA((n * N_PEERS,)),
                   *[pltpu.HBM(a.shape, a.dtype) for a in srcs], *[pltpu.HBM(a.shape, a.dtype) for a in lands],
                   jax.ShapeDtypeStruct((8, LANES), F32)),
        in_specs=[HBM_ONLY] * (2 * n) + [HBM_SPEC],
        out_specs=(SEM_SPEC, SEM_SPEC, *[HBM_ONLY] * (2 * n), pl.BlockSpec(memory_space=pltpu.VMEM)),
        input_output_aliases={i: 2 + i for i in range(2 * n)},
        compiler_params=pltpu.CompilerParams(has_side_effects=pltpu.SideEffectType.DATAFLOW_SIDE_EFFECTING),
    )(*[hbm(a) for a in srcs], *[hbm(a) for a in lands], dep)


def _exchange_wait(started, after, *, gather, name):
    send_sems, recv_sems = started[0], started[1]
    n = (len(started) - 3) // 2
    srcs, lands = started[2:2 + n], started[2 + n:2 + 2 * n]

    def body(*refs):
        src_refs, land_refs = refs[:n], refs[n:2 * n]
        s_sems, r_sems = refs[2 * n], refs[2 * n + 1]
        out_copy, in_copy = _exchange_refs(gather, src_refs, land_refs, s_sems, r_sems)
        for k in range(N_PEERS):
            for a in range(n):
                out_copy(a, k).wait_send()
                in_copy(a, k).wait_recv()

    res = _pcall(
        body, name=name,
        out_shape=tuple(pltpu.HBM(a.shape, a.dtype) for a in (*srcs, *lands)),
        in_specs=[HBM_ONLY] * (2 * n) + [SEM_SPEC, SEM_SPEC, HBM_SPEC],
        out_specs=tuple([HBM_ONLY] * (2 * n)),
        input_output_aliases={i: i for i in range(2 * n)},
        compiler_params=pltpu.CompilerParams(has_side_effects=pltpu.SideEffectType.DATAFLOW_SIDE_EFFECTING),
    )(*srcs, *lands, send_sems, recv_sems, after)
    return _with_own_slot(res[:n], res[n:], gather)


def _with_own_slot(srcs, lands, gather):
    me = _flat_index(lax.axis_index("x"), lax.axis_index("y"), lax.axis_index("c"))
    full = []
    for src, land in zip(srcs, lands):
        own = src if gather else lax.dynamic_index_in_dim(src, me, 0, keepdims=False)
        full.append(lax.dynamic_update_slice(land, own[None], (me,) + (0,) * own.ndim))
    return full


def _adam_math(g, w, m, v):
    m = ADAM_B1 * m + (1.0 - ADAM_B1) * g
    v = ADAM_B2 * v + (1.0 - ADAM_B2) * jnp.square(g)
    m_hat = m / (1.0 - ADAM_B1 ** ADAM_STEP)
    v_hat = v / (1.0 - ADAM_B2 ** ADAM_STEP)
    delta = -ADAM_LR * (m_hat / (jnp.sqrt(v_hat) + ADAM_EPS) + ADAM_WD * w)
    return delta, m, v


def _adamw(contribs, w, m, v, *, name):
    layers = len(contribs)
    _, r, c = contribs[0].shape
    tr = _tile(r, max(8, (256 * 1024 // c) // 8 * 8), 8)

    def body(*refs):
        g_refs = refs[:layers]
        w_ref, m_ref, v_ref, go_ref, d_ref, mo_ref, vo_ref = refs[layers:]
        layer = pl.program_id(0)
        for l2 in range(layers):

            @pl.when(layer == l2)
            def _(g_ref=g_refs[l2]):
                g = g_ref[0].astype(F32)
                for k in range(1, N_DEV):
                    g = g + g_ref[k].astype(F32)
                delta, m_new, v_new = _adam_math(g, w_ref[...], m_ref[...], v_ref[...])
                go_ref[...] = g
                d_ref[...] = delta
                mo_ref[...] = m_new
                vo_ref[...] = v_new

    g_specs = [pl.BlockSpec((N_DEV, tr, c), lambda l, i, l2=l2: (0, jnp.where(l == l2, i, 0), 0))
               for l2 in range(layers)]
    blk = pl.BlockSpec((None, tr, c), lambda l, i: (l, i, 0))
    out = jax.ShapeDtypeStruct((layers, r, c), F32)
    return _pcall(
        body, name=name, grid=(layers, r // tr), in_specs=g_specs + [blk, blk, blk],
        out_specs=(blk, blk, blk, blk), out_shape=(out, out, out, out),
        compiler_params=_params("arbitrary", "arbitrary"),
    )(*contribs, w, m, v)


def _heads_split(w, heads, first, second):
    k = w.shape[0]
    w3 = w.reshape(k, heads, first + second)
    return jnp.concatenate([w3[:, :, :first].reshape(k, heads * first),
                            w3[:, :, first:].reshape(k, heads * second)], axis=1)


def _heads_join(w, heads, first, second):
    k = w.shape[0]
    a = w[:, :heads * first].reshape(k, heads, first)
    b = w[:, heads * first:].reshape(k, heads, second)
    return jnp.concatenate([a, b], axis=2).reshape(k, heads * (first + second))


def _full_from_gathered(kind, g):
    if kind == "col":
        return jnp.transpose(g, (1, 0, 2)).reshape(g.shape[1], N_DEV * g.shape[2])
    if kind == "row":
        return g.reshape(N_DEV * g.shape[1], g.shape[2])
    return jnp.transpose(g, (1, 0, 2, 3)).reshape(g.shape[1], N_DEV * g.shape[2], g.shape[3])


def _blocks_from_full(kind, f):
    if kind == "col":
        k, n = f.shape
        return jnp.transpose(f.reshape(k, N_DEV, n // N_DEV), (1, 0, 2))
    if kind == "row":
        k, n = f.shape
        return f.reshape(N_DEV, k // N_DEV, n)
    g, c_in, c = f.shape
    return jnp.transpose(f.reshape(g, N_DEV, c_in // N_DEV, c), (1, 0, 2, 3))


def _rope_tables(positions):
    inv_freq = 1.0 / (ROPE_THETA ** (jnp.arange(0, ROPE_DIM, 2, dtype=F32) / ROPE_DIM))
    ang = positions.astype(F32)[:, None] * inv_freq
    cos, sin = jnp.cos(ang), jnp.sin(ang)
    return jnp.concatenate([cos, cos, cos, cos], axis=-1), jnp.concatenate([-sin, sin, -sin, sin], axis=-1)


def _layer_fwd(h0, p_i, fetch, rep, tabs, dims, dep=None):
    heads, d_pool, q_lora, d_ff = dims["heads"], dims["d_pool"], dims["q_lora"], dims["d_ff"]
    c = d_pool // POOL_GROUPS
    cos_t, sin_t = tabs
    scale = 1.0 / math.sqrt(NOPE_DIM + ROPE_DIM)
    n1 = _rms_fwd(h0, rep["norm_mix_g"], name="rms_mix_fwd", dep=dep)
    w = dict(fetch("mix", n1))
    u = _mm(n1, w["w_in"], name="mm_in_fwd")
    y_pool, diff = _pool_fwd(u, w["pool_w"], rep["pool_scale"], c=c)
    nq = _rms_fwd(u, rep["q_norm_g"], name="rms_q_fwd", col_block=d_pool // q_lora)
    nkv = _rms_fwd(u, rep["kv_norm_g"], name="rms_kv_fwd", col_block=d_pool // q_lora + 1)
    q = _mm(nq, w["w_uq"], name="mm_uq_fwd")
    kv = _mm(nkv, w["w_ukv"], name="mm_ukv_fwd")
    kr = u[:, d_pool + 2 * q_lora:]
    kr2 = jnp.concatenate([kr, kr], axis=-1)
    qp, kp, v = _qkv_prep(q, kv, kr2, cos_t, sin_t, heads=heads)
    o, o_mm, lse = _flash_fwd(qp, kp, v, scale=scale)
    cat = jnp.concatenate([y_pool, o_mm], axis=1)
    h1 = _mm(cat, w["w_out"], name="mm_out_fwd", add=h0)
    n2 = _rms_fwd(h1, rep["norm_ffn_g"], name="rms_ffn_fwd")
    w.update(fetch("up", n2))
    gu = _mm(n2, w["w_up"], name="mm_up_fwd", out_dtype=BF16, b_blocks=(0, N_DEV))
    a = _ffn_act_fwd(gu, w["conv_w"], rep["conv_b"], d_ff=d_ff)
    w.update(fetch("down", a))
    h2 = _mm(a, w["w_down"], name="mm_down_fwd", add=h1)
    n3 = _rms_fwd(h2, rep["norm_ple_g"], name="rms_ple_fwd")
    z = _mm(n3, w["w_ple_gate"], name="mm_pgate_fwd")
    e = _mm(p_i, w["w_ple"], name="mm_ple_fwd")
    h3 = _ple_fwd(h2, z, e)
    saved = dict(h0=h0, n1=n1, u=u, cat=cat, diff=diff, nq=nq, nkv=nkv, qp=qp, kp=kp, v=v, o=o, lse=lse,
                 h1=h1, n2=n2, gu=gu, a=a, h2=h2, n3=n3, z=z, e=e)
    return h3, saved, w


def _layer_bwd(dh3, p_i, w, rep, tabs, dims, sv, dep=None, hooks=None):
    hooks = hooks or {}
    heads, d_pool, q_lora, d_ff = dims["heads"], dims["d_pool"], dims["q_lora"], dims["d_ff"]
    c = d_pool // POOL_GROUPS
    cos_t, sin_t = tabs
    scale = 1.0 / math.sqrt(NOPE_DIM + ROPE_DIM)
    gr = {}
    de, dz = _ple_bwd(dh3, sv["z"], sv["e"], dep)
    gr["w_ple"] = _mm(p_i, de, name="mm_ple_dw", ta=True, out_dtype=BF16)
    gr["w_ple_gate"] = _mm(sv["n3"], dz, name="mm_pgate_dw", ta=True, out_dtype=BF16)
    dn3 = _mm(dz, w["w_ple_gate"], name="mm_pgate_dx", tb=True)
    dh2, dh2_mm, gr["norm_ple_g"] = _rms_bwd(dn3, sv["h2"], rep["norm_ple_g"], name="rms_ple_bwd", res=dh3,
                                             matmul_copy=True)
    gr["w_down"] = _mm(sv["a"], dh2_mm, name="mm_down_dw", ta=True, out_dtype=BF16)
    dep_down = hooks["down"](dh2, gr) if "down" in hooks else None
    da = _mm(dh2_mm, w["w_down"], name="mm_down_dx", tb=True, out_dtype=BF16, dep=dep_down)
    dgate, dup, gr["conv_w"], gr["conv_b"] = _ffn_act_bwd(da, sv["gu"], w["conv_w"], rep["conv_b"], d_ff=d_ff)
    half, per = N_DEV // 2, w["w_up"].shape[2]
    dw_gate = _mm(sv["n2"], dgate, name="mm_up_gate_dw", ta=True, out_dtype=BF16, out_blocks=(N_DEV, 0, per))
    gr["w_up"] = _mm(sv["n2"], dup, name="mm_up_up_dw", ta=True, out_dtype=BF16, out_blocks=(N_DEV, half, per),
                     out_init=dw_gate)
    dep_up = hooks["up"](gr["w_up"], gr) if "up" in hooks else None
    dn2 = _mm(dgate, w["w_up"], name="mm_up_gate_dx", tb=True, b_blocks=(0, half), dep=dep_up)
    dn2 = _mm(dup, w["w_up"], name="mm_up_up_dx", tb=True, b_blocks=(half, half), add=dn2)
    dh1, dh1_mm, gr["norm_ffn_g"] = _rms_bwd(dn2, sv["h1"], rep["norm_ffn_g"], name="rms_ffn_bwd", res=dh2,
                                             matmul_copy=True)
    gr["w_out"] = _mm(sv["cat"], dh1_mm, name="mm_out_dw", ta=True, out_dtype=BF16)
    dcat = _mm(dh1_mm, w["w_out"], name="mm_out_dx", tb=True)
    do_col0 = d_pool // V_DIM
    dqp, dkp, dv = _flash_bwd(sv["qp"], sv["kp"], sv["v"], dcat, sv["o"], sv["lse"], scale=scale, do_col0=do_col0)
    dq, dkv, dkr2 = _attn_bwd_post(dqp, dkp, dv, cos_t, sin_t)
    gr["w_uq"] = _mm(sv["nq"], dq, name="mm_uq_dw", ta=True, out_dtype=BF16)
    gr["w_ukv"] = _mm(sv["nkv"], dkv, name="mm_ukv_dw", ta=True, out_dtype=BF16)
    dnq = _mm(dq, w["w_uq"], name="mm_uq_dx", tb=True)
    dnkv = _mm(dkv, w["w_ukv"], name="mm_ukv_dx", tb=True)
    dcq, gr["q_norm_g"] = _rms_bwd(dnq, sv["u"], rep["q_norm_g"], name="rms_q_bwd",
                                   col_block=d_pool // q_lora, out_dtype=BF16)
    dckv, gr["kv_norm_g"] = _rms_bwd(dnkv, sv["u"], rep["kv_norm_g"], name="rms_kv_bwd",
                                     col_block=d_pool // q_lora + 1, out_dtype=BF16)
    du_pool, gr["pool_w"], gr["pool_scale"] = _pool_bwd(dcat, sv["diff"], w["pool_w"], rep["pool_scale"], c=c)
    du = jnp.concatenate([du_pool, dcq, dckv, dkr2[:, :ROPE_DIM]], axis=-1)
    gr["w_in"] = _mm(sv["n1"], du, name="mm_in_dw", ta=True, out_dtype=BF16)
    dn1 = _mm(du, w["w_in"], name="mm_in_dx", tb=True)
    dh0, gr["norm_mix_g"] = _rms_bwd(dn1, sv["h0"], rep["norm_mix_g"], name="rms_mix_bwd", res=dh1)
    return dh0, gr


def _as2d(a):
    return a.reshape(a.shape[0], -1, a.shape[-1])


def kernel(x, p, positions, norm_mix_g, w_in, pool_w, pool_scale, q_norm_g, w_uq, kv_norm_g, w_ukv, w_out, norm_ffn_g, w_up, conv_w, conv_b, w_down, norm_ple_g, w_ple, w_ple_gate, final_norm_g, loss_target, m_norm_mix_g, m_w_in, m_pool_w, m_pool_scale, m_q_norm_g, m_w_uq, m_kv_norm_g, m_w_ukv, m_w_out, m_norm_ffn_g, m_w_up, m_conv_w, m_conv_b, m_w_down, m_norm_ple_g, m_w_ple, m_w_ple_gate, m_final_norm_g, v_norm_mix_g, v_w_in, v_pool_w, v_pool_scale, v_q_norm_g, v_w_uq, v_kv_norm_g, v_w_ukv, v_w_out, v_norm_ffn_g, v_w_up, v_conv_w, v_conv_b, v_w_down, v_norm_ple_g, v_w_ple, v_w_ple_gate, v_final_norm_g):
    weights = dict(norm_mix_g=norm_mix_g, w_in=w_in, pool_w=pool_w, pool_scale=pool_scale, q_norm_g=q_norm_g,
                   w_uq=w_uq, kv_norm_g=kv_norm_g, w_ukv=w_ukv, w_out=w_out, norm_ffn_g=norm_ffn_g, w_up=w_up,
                   conv_w=conv_w, conv_b=conv_b, w_down=w_down, norm_ple_g=norm_ple_g, w_ple=w_ple,
                   w_ple_gate=w_ple_gate, final_norm_g=final_norm_g)
    m_in = dict(norm_mix_g=m_norm_mix_g, w_in=m_w_in, pool_w=m_pool_w, pool_scale=m_pool_scale, q_norm_g=m_q_norm_g,
                w_uq=m_w_uq, kv_norm_g=m_kv_norm_g, w_ukv=m_w_ukv, w_out=m_w_out, norm_ffn_g=m_norm_ffn_g,
                w_up=m_w_up, conv_w=m_conv_w, conv_b=m_conv_b, w_down=m_w_down, norm_ple_g=m_norm_ple_g,
                w_ple=m_w_ple, w_ple_gate=m_w_ple_gate, final_norm_g=m_final_norm_g)
    v_in = dict(norm_mix_g=v_norm_mix_g, w_in=v_w_in, pool_w=v_pool_w, pool_scale=v_pool_scale, q_norm_g=v_q_norm_g,
                w_uq=v_w_uq, kv_norm_g=v_kv_norm_g, w_ukv=v_w_ukv, w_out=v_w_out, norm_ffn_g=v_norm_ffn_g,
                w_up=v_w_up, conv_w=v_conv_w, conv_b=v_conv_b, w_down=v_w_down, norm_ple_g=v_norm_ple_g,
                w_ple=v_w_ple, w_ple_gate=v_w_ple_gate, final_norm_g=v_final_norm_g)

    depth = w_in.shape[0]
    s, d_model = x.shape[1], x.shape[2]
    d_pool = pool_scale.shape[-1]
    q_lora = q_norm_g.shape[-1]
    d_ff = conv_b.shape[-1]
    heads = (w_uq.shape[-1] * N_DEV) // (NOPE_DIM + ROPE_DIM)
    dims = dict(heads=heads, d_pool=d_pool, q_lora=q_lora, d_ff=d_ff)

    groups = {"mix": ("w_in", "pool_w", "w_uq", "w_ukv", "w_out"), "up": ("w_up", "conv_w"),
              "down": ("w_down", "w_ple", "w_ple_gate")}

    def group_shards(i, group):
        return [weights[n][i] if n == "conv_w" else weights[n][i].astype(BF16) for n in groups[group]]

    def start_weights(i, group, dep):
        return _exchange_start(group_shards(i, group), gather=True, name=f"weights_{group}_start_{i}", dep=dep)

    def full_group(group, gathered_g):
        w = {n: g if n == "w_up" else _full_from_gathered(SHARD_KIND[n], g) for n, g in zip(groups[group], gathered_g)}
        if group == "mix":
            w["w_uq"] = _heads_split(w["w_uq"], heads, NOPE_DIM, ROPE_DIM)
            w["w_ukv"] = _heads_split(w["w_ukv"], heads, NOPE_DIM, V_DIM)
        return w

    tabs = _rope_tables(positions[0])

    arrived = {(0, "mix"): _all_gather(group_shards(0, "mix"), name="weights_mix_gather_0")}
    travelling = {}
    token = arrived[(0, "mix")][0]
    for group in ("up", "down"):
        travelling[(0, group)] = start_weights(0, group, token)
        token = travelling[(0, group)][-1]
    layer_w = []
    h = x[0]
    saved = []
    for i in range(depth):
        if i + 1 < depth:
            for group in ("mix", "up", "down"):
                travelling[(i + 1, group)] = start_weights(i + 1, group, token if i == 0 and group == "mix" else
                                                           (h if group == "mix" else token))
                token = travelling[(i + 1, group)][-1]

        def fetch(group, after, i=i):
            if (i, group) not in arrived:
                arrived[(i, group)] = _exchange_wait(travelling[(i, group)], after, gather=True,
                                                     name=f"weights_{group}_wait_{i}")
            return full_group(group, arrived[(i, group)])

        rep = {n: weights[n][i] for n in REPLICATED}
        h, sv, w = _layer_fwd(h, p[i, 0], fetch, rep, tabs, dims, dep=token if i + 1 < depth or i == 0 else None)
        layer_w.append((w, rep))
        saved.append(sv)
    loss_row, dh, g_final = _loss_head(h, final_norm_g, loss_target[0])
    loss = lax.psum(loss_row[0, 0], MESH_AXES)

    def start_grads(group, gr, dep, i):
        names = groups[group]
        blocks = [gr[n] if n == "w_up" else _blocks_from_full(SHARD_KIND[n], gr[n]).astype(BF16) for n in names]
        return _exchange_start(blocks, gather=False, name=f"grads_{group}_start_{i}", dep=dep)

    def end_grads(group, started, after, i):
        got = _exchange_wait(started, after, gather=False, name=f"grads_{group}_wait_{i}")
        received[i].update(zip(groups[group], got))

    layer_grads = [None] * depth
    received = [dict() for _ in range(depth)]
    pending = None
    for i in reversed(range(depth)):
        w, rep = layer_w[i]
        state = {}

        def on_down(dh2, gr, i=i, state=state):
            state["down"] = start_grads("down", gr, dh2, i)
            return state["down"][-1]

        def on_up(dw_up, gr, i=i, pending=pending, state=state):
            order = gr["conv_w"]
            if pending is not None:
                for group in ("down", "up", "mix"):
                    end_grads(group, pending[group], order, i + 1)
            state["up"] = start_grads("up", gr, order, i)
            return state["up"][-1]

        dh, gr = _layer_bwd(dh, p[i, 0], w, rep, tabs, dims, saved[i],
                            dep=loss.reshape(1, 1) if pending is None else pending["mix"][-1],
                            hooks={"down": on_down, "up": on_up})
        gr["w_uq"] = _heads_join(gr["w_uq"], heads, NOPE_DIM, ROPE_DIM)
        gr["w_ukv"] = _heads_join(gr["w_ukv"], heads, NOPE_DIM, V_DIM)
        layer_grads[i] = gr
        state["mix"] = start_grads("mix", gr, dh, i)
        pending = state
    grad_x = dh[None]

    out = {}

    def update(n):
        shape = weights[n].shape
        recs = [received[i][n].reshape((N_DEV, -1, shape[-1])) for i in range(depth)]
        res = _adamw(recs, _as2d(weights[n]), _as2d(m_in[n]), _as2d(v_in[n]), name="adamw_" + n)
        out[n] = tuple(r.reshape(shape) for r in res)

    end_grads("down", pending["down"], pending["mix"][-1], 0)
    end_grads("up", pending["up"], pending["mix"][-1], 0)
    for n in groups["down"] + groups["up"]:
        update(n)
    end_grads("mix", pending["mix"], out["w_up"][0], 0)
    for n in groups["mix"]:
        update(n)

    small_names = REPLICATED + ("final_norm_g",)

    def pack(get):
        rows = [jnp.stack([get(n, i).reshape(-1) for i in range(depth)]).reshape(-1) for n in REPLICATED]
        rows.append(get("final_norm_g", None).reshape(-1))
        return jnp.concatenate(rows).reshape(1, -1, LANES)

    g_small = pack(lambda n, i: g_final if i is None else layer_grads[i][n])
    w_small = pack(lambda n, i: weights[n] if i is None else weights[n][i])
    m_small = pack(lambda n, i: m_in[n] if i is None else m_in[n][i])
    v_small = pack(lambda n, i: v_in[n] if i is None else v_in[n][i])
    (g_all,) = _all_gather([g_small], name="small_grads_all_gather")
    res_small = _adamw([g_all[:, 0]], w_small, m_small, v_small, name="adamw_small")

    def unpack(flat3):
        flat = flat3.reshape(-1)
        res, off = {}, 0
        for n in REPLICATED:
            width = weights[n].shape[-1]
            res[n] = flat[off:off + depth * width].reshape(depth, width)
            off += depth * width
        res["final_norm_g"] = flat[off:off + d_model]
        return res

    small = [unpack(r) for r in res_small]
    for n in small_names:
        out[n] = tuple(small[k][n] for k in range(4))

    outs = [loss, grad_x]
    for k in range(4):
        outs += [out[n][k] for n in WEIGHT_ORDER]
    return tuple(outs)
```

```python
import math

import jax
import jax.numpy as jnp
from jax import lax
from jax.experimental import pallas as pl
from jax.experimental.pallas import tpu as pltpu

F32 = jnp.float32
BF16 = jnp.bfloat16

N_DEV = 8
MESH_AXES = ("x", "y", "c")
NOPE_DIM = 128
ROPE_DIM = 64
V_DIM = 128
POOL_GROUPS = 4
CONV_TAPS = 3
ROPE_THETA = 10000.0
NORM_EPS = 1e-6
ADAM_LR = 0.001
ADAM_B1 = 0.9
ADAM_B2 = 0.999
ADAM_EPS = 1e-08
ADAM_WD = 0.01
ADAM_STEP = 10
LANES = 128
VMEM_LIMIT_BYTES = 56 * 1024 * 1024

SHARD_KIND = {"w_in": "col", "pool_w": "pool", "w_uq": "col", "w_ukv": "col", "w_out": "row", "w_up": "col",
              "conv_w": "col", "w_down": "row", "w_ple": "col", "w_ple_gate": "row"}
REPLICATED = ("norm_mix_g", "pool_scale", "q_norm_g", "kv_norm_g", "norm_ffn_g", "conv_b", "norm_ple_g")
WEIGHT_ORDER = ("norm_mix_g", "w_in", "pool_w", "pool_scale", "q_norm_g", "w_uq", "kv_norm_g", "w_ukv", "w_out",
                "norm_ffn_g", "w_up", "conv_w", "conv_b", "w_down", "norm_ple_g", "w_ple", "w_ple_gate",
                "final_norm_g")

_pcall = pl.pallas_call


def _params(*sem):
    return pltpu.CompilerParams(dimension_semantics=sem or None, vmem_limit_bytes=VMEM_LIMIT_BYTES)


def _tile(n, pref, mult=LANES):
    if n <= pref:
        return n
    t = (pref // mult) * mult
    while t >= mult:
        if n % t == 0:
            return t
        t -= mult
    return n


MM_VMEM_BUDGET_BYTES = 44 * 1024 * 1024
MM_TILE_PREFS = (1536, 1024, 512, 256)
MM_MIN_TK = 1024
V7X_MXU_FLOPS = 850e12
V7X_SPLIT_K_SLOWDOWN = 1.3
V7X_HBM_BYTES_PER_S = 2.8e12
GRID_STEP_S = 0.35e-6


def _mm_tiles(m, n, k, a_bytes, b_bytes, o_bytes, has_add, tn_fixed=None, tk_fixed=None, a_copied=False,
              b_copied=False):
    out_bytes = o_bytes + (4 if has_add else 0)
    tm_cands = sorted({_tile(m, pref) for pref in MM_TILE_PREFS}, reverse=True)
    tn_cands = [tn_fixed] if tn_fixed else sorted({_tile(n, pref) for pref in MM_TILE_PREFS}, reverse=True)
    if tk_fixed:
        tk_cands = [tk_fixed]
    else:
        tk_cands = [k] + [t for t in range((k - 1) // LANES * LANES, MM_MIN_TK - 1, -LANES) if k % t == 0]
    best = None
    for tm in tm_cands:
        for tn in tn_cands:
            for tk in tk_cands:
                vmem = 2 * (tm * tk * a_bytes + tk * tn * b_bytes) + 2 * tm * tn * out_bytes
                vmem += tm * tn * 4 if tk < k else 0
                vmem += (tm * tk * 2 if a_copied else 0) + (tk * tn * 2 if b_copied else 0)
                if vmem > MM_VMEM_BUDGET_BYTES:
                    continue
                steps = (m // tm) * (n // tn) * (k // tk)
                hbm = a_bytes * m * k * (1 if tk == k else n // tn) + b_bytes * k * n * (m // tm) + out_bytes * m * n
                mxu = 2 * m * n * k / V7X_MXU_FLOPS * (1.0 if tk == k else V7X_SPLIT_K_SLOWDOWN)
                cost = max(mxu, hbm / V7X_HBM_BYTES_PER_S) + steps * GRID_STEP_S
                if best is None or cost < best[0]:
                    best = (cost, tm, tn, tk)
    assert best is not None, (m, n, k)
    return best[1:]


def _mm(a, b, *, name, ta=False, tb=False, add=None, out_dtype=F32, b_blocks=None, out_blocks=None,
        out_init=None, dep=None):
    m = a.shape[1] if ta else a.shape[0]
    kdim = a.shape[0] if ta else a.shape[1]
    tn_fixed = tk_fixed = None
    if b_blocks is None:
        n, k_b = (b.shape if tb else b.shape[::-1])
        assert k_b == kdim, (name, k_b, kdim)
    else:
        first_b, count_b = b_blocks
        per_b = b.shape[2]
        if tb:
            n = b.shape[1]
            assert kdim == count_b * per_b, name
            tk_fixed = per_b
        else:
            n = count_b * per_b
            assert kdim == b.shape[1], name
            tn_fixed = per_b
    if out_blocks is not None:
        nb_out, first_o, tn_fixed = out_blocks
    tm, tn, tk = _mm_tiles(m, n, kdim, a.dtype.itemsize, b.dtype.itemsize, jnp.dtype(out_dtype).itemsize,
                           add is not None, tn_fixed, tk_fixed, a_copied=ta or a.dtype != BF16,
                           b_copied=tb or b.dtype != BF16)
    assert kdim % tk == 0 and n % tn == 0 and m % tm == 0, name
    nk = kdim // tk
    dims = (((0 if ta else 1,), (1 if tb else 0,)), ((), ()))

    def body(*refs):
        a_ref, b_ref = refs[0], refs[1]
        add_ref = refs[2] if add is not None else None
        o_ref = refs[n_in]
        part = lax.dot_general(a_ref[...].astype(BF16), b_ref[...].astype(BF16), dims, preferred_element_type=F32)

        def finish(r):
            if add_ref is not None:
                r = r + add_ref[...].astype(F32)
            o_ref[...] = r.astype(out_dtype)

        if nk == 1:
            finish(part)
            return
        acc = refs[n_in + 1]
        k = pl.program_id(2)

        @pl.when(k == 0)
        def _():
            acc[...] = part

        @pl.when(jnp.logical_and(k > 0, k < nk - 1))
        def _():
            acc[...] += part

        @pl.when(k == nk - 1)
        def _():
            finish(acc[...] + part)

    if ta:
        a_spec = pl.BlockSpec((tk, tm), lambda i, j, k: (k, i))
    else:
        a_spec = pl.BlockSpec((tm, tk), lambda i, j, k: (i, k))
    if b_blocks is not None and tb:
        b_spec = pl.BlockSpec((None, tn, tk), lambda i, j, k: (k + first_b, j, 0))
    elif b_blocks is not None:
        b_spec = pl.BlockSpec((None, tk, tn), lambda i, j, k: (j + first_b, k, 0))
    elif tb:
        b_spec = pl.BlockSpec((tn, tk), lambda i, j, k: (j, k))
    else:
        b_spec = pl.BlockSpec((tk, tn), lambda i, j, k: (k, j))
    in_specs = [a_spec, b_spec]
    args = [a, b]
    if add is not None:
        in_specs.append(pl.BlockSpec((tm, tn), lambda i, j, k: (i, j)))
        args.append(add)
    aliases = {}
    if out_init is not None:
        aliases = {len(args): 0}
        in_specs.append(HBM_SPEC)
        args.append(out_init)
    if dep is not None:
        in_specs.append(HBM_SPEC)
        args.append(dep)
    n_in = len(args)
    if out_blocks is None:
        out_spec = pl.BlockSpec((tm, tn), lambda i, j, k: (i, j))
        out_shape = jax.ShapeDtypeStruct((m, n), out_dtype)
    else:
        out_spec = pl.BlockSpec((None, tm, tn), lambda i, j, k: (j + first_o, i, 0))
        out_shape = jax.ShapeDtypeStruct((nb_out, m, tn), out_dtype)
    return _pcall(
        body, name=name, grid=(m // tm, n // tn, nk), in_specs=in_specs, out_specs=out_spec, out_shape=out_shape,
        scratch_shapes=[pltpu.VMEM((tm, tn), F32)] if nk > 1 else [], input_output_aliases=aliases,
        compiler_params=_params("parallel", "parallel", "arbitrary"),
    )(*args)


def _rms_fwd(h, g, *, name, col_block=0, dep=None):
    s = h.shape[0]
    d = g.shape[-1]
    ts = _tile(s, 512, 8)

    def body(h_ref, g_ref, *rest):
        n_ref = rest[-1]
        x = h_ref[...]
        r = lax.rsqrt(jnp.mean(x * x, axis=-1, keepdims=True) + NORM_EPS)
        n_ref[...] = (x * r * g_ref[...]).astype(BF16)

    deps = [] if dep is None else [dep]
    return _pcall(
        body, name=name, grid=(s // ts,),
        in_specs=[pl.BlockSpec((ts, d), lambda i: (i, col_block)), pl.BlockSpec((1, d), lambda i: (0, 0))]
        + [HBM_SPEC] * len(deps),
        out_specs=pl.BlockSpec((ts, d), lambda i: (i, 0)),
        out_shape=jax.ShapeDtypeStruct((s, d), BF16),
        compiler_params=_params("parallel"),
    )(h, g.reshape(1, d), *deps)


def _rms_bwd(dn, h, g, *, name, res=None, col_block=0, out_dtype=F32, matmul_copy=False):
    s = dn.shape[0]
    d = g.shape[-1]
    ts = _tile(s, 512, 8)

    def body(*refs):
        dn_ref, h_ref, g_ref = refs[:3]
        res_ref = refs[3] if res is not None else None
        dh_ref, dg_ref = refs[n_in], refs[-1]
        i = pl.program_id(0)
        x = h_ref[...]
        r = lax.rsqrt(jnp.mean(x * x, axis=-1, keepdims=True) + NORM_EPS)
        nh = x * r
        dnv = dn_ref[...]
        gd = dnv * g_ref[...]
        dh = (gd - nh * jnp.mean(gd * nh, axis=-1, keepdims=True)) * r
        if res_ref is not None:
            dh = dh + res_ref[...]
        dh_ref[...] = dh.astype(out_dtype)
        if matmul_copy:
            refs[n_in + 1][...] = dh.astype(BF16)
        part = jnp.sum(dnv * nh, axis=0, keepdims=True)

        @pl.when(i == 0)
        def _():
            dg_ref[...] = part

        @pl.when(i > 0)
        def _():
            dg_ref[...] += part

    row = pl.BlockSpec((ts, d), lambda i: (i, 0))
    in_specs = [row, pl.BlockSpec((ts, d), lambda i: (i, col_block)), pl.BlockSpec((1, d), lambda i: (0, 0))]
    args = [dn, h, g.reshape(1, d)]
    if res is not None:
        in_specs.append(row)
        args.append(res)
    n_in = len(args)
    copies = [row] if matmul_copy else []
    return _pcall(
        body, name=name, grid=(s // ts,), in_specs=in_specs,
        out_specs=(row, *copies, pl.BlockSpec((1, d), lambda i: (0, 0))),
        out_shape=(jax.ShapeDtypeStruct((s, d), out_dtype), *[jax.ShapeDtypeStruct((s, d), BF16) for _ in copies],
                   jax.ShapeDtypeStruct((1, d), F32)),
        compiler_params=_params("arbitrary"),
    )(*args)


ROW_CHUNK = 512


def _rows_with_halo(ref, r, t_rows, n_chunks, before, after):
    r0 = r * t_rows
    parts = []
    if before:
        hb = ref[pl.ds(pl.multiple_of(jnp.maximum(r0 - before, 0), before), before), :]
        parts.append(jnp.where(r > 0, hb, jnp.zeros_like(hb)))
    parts.append(ref[pl.ds(pl.multiple_of(r0, t_rows), t_rows), :])
    if after:
        ha = ref[pl.ds(pl.multiple_of(jnp.minimum(r0 + t_rows, n_chunks * t_rows - after), after), after), :]
        parts.append(jnp.where(r < n_chunks - 1, ha, jnp.zeros_like(ha)))
    return jnp.concatenate(parts, axis=0)


POOL_HALO = 16


def _pool_fwd(u, pool_w, pool_scale, *, c):
    s = u.shape[0]
    g_n = POOL_GROUPS
    tr = _tile(s, ROW_CHUNK, POOL_HALO)
    n_chunks = s // tr

    def body(u_ref, pw_ref, sc_ref, y_ref, d_ref):
        r = pl.program_id(1)
        w = jnp.left_shift(2, pl.program_id(0))
        xe = _rows_with_halo(u_ref, r, tr, n_chunks, POOL_HALO, 0)
        acc = xe
        for k in (1, 2, 4, 8):
            acc = jnp.where(k < w, acc + pltpu.roll(acc, k, 0), acc)
        t = r * tr + lax.broadcasted_iota(jnp.int32, (tr, 1), 0)
        cnt = jnp.minimum(t + 1, w).astype(F32)
        diff = (acc[POOL_HALO:] / cnt - xe[POOL_HALO:]).astype(BF16)
        d_ref[...] = diff
        y = jnp.dot(diff, pw_ref[...], preferred_element_type=F32) * sc_ref[...]
        y_ref[...] = y.astype(BF16)

    out = pl.BlockSpec((tr, c), lambda g, r: (r, g))
    return _pcall(
        body, name="pool_fwd", grid=(g_n, n_chunks),
        in_specs=[pl.BlockSpec((s, c), lambda g, r: (0, g)), pl.BlockSpec((None, c, c), lambda g, r: (g, 0, 0)),
                  pl.BlockSpec((1, c), lambda g, r: (0, g))],
        out_specs=(out, out),
        out_shape=(jax.ShapeDtypeStruct((s, g_n * c), BF16), jax.ShapeDtypeStruct((s, g_n * c), BF16)),
        compiler_params=_params("parallel", "arbitrary"),
    )(u, pool_w, pool_scale.reshape(1, g_n * c))


def _pool_bwd(dcat, diff, pool_w, pool_scale, *, c):
    s = dcat.shape[0]
    g_n = POOL_GROUPS
    tr = _tile(s, ROW_CHUNK, POOL_HALO)
    n_chunks = s // tr

    def body(dy_ref, d_ref, pw_ref, sc_ref, du_ref, dpw_ref, dsc_ref):
        r = pl.program_id(1)
        w = jnp.left_shift(2, pl.program_id(0))
        dye = _rows_with_halo(dy_ref, r, tr, n_chunks, 0, POOL_HALO)
        diff = d_ref[pl.ds(pl.multiple_of(r * tr, tr), tr), :]
        pw = pw_ref[...]
        yp = jnp.dot(diff, pw, preferred_element_type=F32)
        dsc = jnp.sum(dye[:tr] * yp, axis=0, keepdims=True)
        dyp = (dye * sc_ref[...]).astype(BF16)
        ddiff = lax.dot_general(dyp, pw, (((1,), (1,)), ((), ())), preferred_element_type=F32)
        dpw = lax.dot_general(diff, dyp[:tr], (((0,), (0,)), ((), ())), preferred_element_type=F32)
        t = r * tr + lax.broadcasted_iota(jnp.int32, (tr + POOL_HALO, 1), 0)
        cnt = jnp.minimum(t + 1, w).astype(F32)
        acc = ddiff / cnt
        rows = tr + POOL_HALO
        for k in (1, 2, 4, 8):
            acc = jnp.where(k < w, acc + pltpu.roll(acc, rows - k, 0), acc)
        du_ref[...] = (acc[:tr] - ddiff[:tr]).astype(BF16)

        @pl.when(r == 0)
        def _():
            dpw_ref[...] = dpw
            dsc_ref[...] = dsc

        @pl.when(r > 0)
        def _():
            dpw_ref[...] += dpw
            dsc_ref[...] += dsc

    col = lambda g, r: (0, g)
    wspec = pl.BlockSpec((None, c, c), lambda g, r: (g, 0, 0))
    vec = pl.BlockSpec((1, c), col)
    return _pcall(
        body, name="pool_bwd", grid=(g_n, n_chunks),
        in_specs=[pl.BlockSpec((s, c), col), pl.BlockSpec((s, c), col), wspec, vec],
        out_specs=(pl.BlockSpec((tr, c), lambda g, r: (r, g)), wspec, vec),
        out_shape=(jax.ShapeDtypeStruct((s, g_n * c), BF16), jax.ShapeDtypeStruct((g_n, c, c), F32),
                   jax.ShapeDtypeStruct((1, g_n * c), F32)),
        compiler_params=_params("parallel", "arbitrary"),
    )(dcat, diff, pool_w, pool_scale.reshape(1, g_n * c))


def _swap_halves(x, lane):
    return jnp.where((lane % ROPE_DIM) < ROPE_DIM // 2, pltpu.roll(x, LANES - ROPE_DIM // 2, 1),
                     pltpu.roll(x, ROPE_DIM // 2, 1))


def _qkv_prep(q, kv, kr2, cos_t, sin_t, *, heads):
    s = q.shape[0]
    ts = _tile(s, 256, 8)

    def body(q_ref, kv_ref, kr_ref, cos_ref, sin_ref, qo_ref, ko_ref, vo_ref):
        lane = lax.broadcasted_iota(jnp.int32, (ts, LANES), 1)
        cos_v = cos_ref[...]
        sin_v = sin_ref[...]

        def rope(x):
            return x * cos_v + _swap_halves(x, lane) * sin_v

        kr = rope(kr_ref[...]).astype(BF16)
        for pair in range(heads // 2):
            qr = rope(q_ref[:, (heads + pair) * LANES:(heads + pair + 1) * LANES])
            for half in range(2):
                h = 2 * pair + half
                qo_ref[h, :, :LANES] = q_ref[:, h * LANES:(h + 1) * LANES].astype(BF16)
                qo_ref[h, :, LANES:] = jnp.where(lane // ROPE_DIM == half, qr, 0.0).astype(BF16)
        for h in range(heads):
            ko_ref[h, :, :LANES] = kv_ref[:, h * LANES:(h + 1) * LANES].astype(BF16)
            ko_ref[h, :, LANES:] = kr
            vo_ref[h] = kv_ref[:, (heads + h) * LANES:(heads + h + 1) * LANES].astype(BF16)

    tab = pl.BlockSpec((ts, LANES), lambda i: (i, 0))
    return _pcall(
        body, name="qkv_prep", grid=(s // ts,),
        in_specs=[pl.BlockSpec((ts, q.shape[1]), lambda i: (i, 0)), pl.BlockSpec((ts, kv.shape[1]), lambda i: (i, 0)),
                  tab, tab, tab],
        out_specs=(pl.BlockSpec((heads, ts, 2 * LANES), lambda i: (0, i, 0)),
                   pl.BlockSpec((heads, ts, 2 * LANES), lambda i: (0, i, 0)),
                   pl.BlockSpec((heads, ts, LANES), lambda i: (0, i, 0))),
        out_shape=(jax.ShapeDtypeStruct((heads, s, 2 * LANES), BF16),
                   jax.ShapeDtypeStruct((heads, s, 2 * LANES), BF16),
                   jax.ShapeDtypeStruct((heads, s, LANES), BF16)),
        compiler_params=_params("parallel"),
    )(q, kv, kr2, cos_t, sin_t)


LOG2_E = 1.4426950408889634


FLASH_FWD_TK = 512
FLASH_FWD_SPLITS = 2


def _flash_fwd(qp, kp, v, *, scale, tq=512):
    heads, s, dk = qp.shape
    tq = _tile(s, tq, 16)
    tk = _tile(tq, FLASH_FWD_TK, 16)
    splits = FLASH_FWD_SPLITS
    th = tq // splits
    band = tq // tk
    c = scale * LOG2_E

    def body(q_ref, k_ref, v_ref, o_ref, ob_ref, lse_ref):
        i = pl.program_id(1)
        qs = [q_ref[hh * th:(hh + 1) * th, :] for hh in range(splits)]

        def skipped(hh, col0):
            return col0 is not None and col0 >= (hh + 1) * th

        def scores(start, col0):
            kb = k_ref[pl.ds(pl.multiple_of(start, tk), tk), :]
            return tuple(None if skipped(hh, col0) else
                         lax.dot_general(qs[hh], kb, (((1,), (1,)), ((), ())), preferred_element_type=F32)
                         for hh in range(splits))

        def absorb(start, scs, state, col0):
            vb = v_ref[pl.ds(pl.multiple_of(start, tk), tk), :]
            new = []
            for hh in range(splits):
                if scs[hh] is None:
                    new.append(state[hh])
                    continue
                m_old, l_old, acc = state[hh]
                sc = scs[hh]
                if col0 is not None and col0 + tk - 1 > hh * th:
                    rows = hh * th + lax.broadcasted_iota(jnp.int32, (th, tk), 0)
                    cols = col0 + lax.broadcasted_iota(jnp.int32, (th, tk), 1)
                    sc = jnp.where(rows >= cols, sc, -jnp.inf)
                m_new = jnp.maximum(m_old, jnp.max(sc, axis=-1, keepdims=True))
                alpha = jnp.exp2((m_old - m_new) * c)
                p = jnp.exp2((sc - m_new) * c)
                l_new = alpha * l_old + jnp.sum(p, axis=-1, keepdims=True)
                acc = alpha * acc + jnp.dot(p.astype(BF16), vb, preferred_element_type=F32)
                new.append((m_new, l_new, acc))
            return tuple(new)

        def step(j, carry):
            state, scs = carry
            nxt = scores((j + 1) * tk, None)
            return absorb(j * tk, scs, state, None), nxt

        init = tuple((jnp.full((th, 1), -jnp.inf, F32), jnp.zeros((th, 1), F32), jnp.zeros((th, V_DIM), F32))
                     for _ in range(splits))
        state, scs = lax.fori_loop(0, i * band, step, (init, scores(0, None)))
        for b in range(band):
            nxt = scores(i * tq + (b + 1) * tk, (b + 1) * tk) if b + 1 < band else None
            state = absorb(i * tq + b * tk, scs, state, b * tk)
            scs = nxt
        carry = state
        for hh in range(splits):
            m_fin, l_fin, acc = carry[hh]
            out = acc / l_fin
            o_ref[hh * th:(hh + 1) * th, :] = out
            ob_ref[hh * th:(hh + 1) * th, :] = out.astype(BF16)
            lse_ref[hh * th:(hh + 1) * th, :] = m_fin * scale + jnp.log(l_fin)

    return _pcall(
        body, name="flash_fwd", grid=(heads, s // tq),
        in_specs=[pl.BlockSpec((None, tq, dk), lambda h, i: (h, i, 0)),
                  pl.BlockSpec((None, s, dk), lambda h, i: (h, 0, 0)),
                  pl.BlockSpec((None, s, V_DIM), lambda h, i: (h, 0, 0))],
        out_specs=(pl.BlockSpec((tq, V_DIM), lambda h, i: (i, h)), pl.BlockSpec((tq, V_DIM), lambda h, i: (i, h)),
                   pl.BlockSpec((None, tq, 1), lambda h, i: (h, i, 0))),
        out_shape=(jax.ShapeDtypeStruct((s, heads * V_DIM), F32), jax.ShapeDtypeStruct((s, heads * V_DIM), BF16),
                   jax.ShapeDtypeStruct((heads, s, 1), F32)),
        compiler_params=_params("parallel", "arbitrary"),
    )(qp, kp, v)


def _flash_bwd(qp, kp, v, dcat, o, lse, *, scale, do_col0, tq=512):
    heads, s, dk = qp.shape
    tq = _tile(s, tq, 16)
    tk = tq
    nq = s // tq

    def body(k_ref, v_ref, q_ref, do_ref, o_ref, lse_ref, dq_ref, dk_ref, dv_ref):
        j = pl.program_id(1)

        @pl.when(j == 0)
        def _():
            dq_ref[...] = jnp.zeros_like(dq_ref)

        kb = k_ref[...]
        vb = v_ref[...]

        def block(i, carry, diag):
            dk_acc, dv_acc = carry
            rows_at = pl.ds(pl.multiple_of(i * tq, tq), tq)
            qb = q_ref[rows_at, :]
            do = do_ref[rows_at, :]
            sc = lax.dot_general(qb, kb, (((1,), (1,)), ((), ())), preferred_element_type=F32) * scale
            if diag:
                rows = lax.broadcasted_iota(jnp.int32, (tq, tk), 0)
                cols = lax.broadcasted_iota(jnp.int32, (tq, tk), 1)
                sc = jnp.where(rows >= cols, sc, -jnp.inf)
            p = jnp.exp(sc - lse_ref[rows_at, :])
            dob = do.astype(BF16)
            dv_acc = dv_acc + lax.dot_general(p.astype(BF16), dob, (((0,), (0,)), ((), ())),
                                              preferred_element_type=F32)
            dp = lax.dot_general(dob, vb, (((1,), (1,)), ((), ())), preferred_element_type=F32)
            delta = jnp.sum(do * o_ref[rows_at, :], axis=-1, keepdims=True)
            ds = (p * (dp - delta) * scale).astype(BF16)
            dk_acc = dk_acc + lax.dot_general(ds, qb, (((0,), (0,)), ((), ())), preferred_element_type=F32)
            dq_ref[rows_at, :] += jnp.dot(ds, kb, preferred_element_type=F32)
            return dk_acc, dv_acc

        carry = block(j, (jnp.zeros((tk, dk), F32), jnp.zeros((tk, V_DIM), F32)), True)
        carry = lax.fori_loop(j + 1, nq, lambda i, cr: block(i, cr, False), carry)
        dk_ref[...] = carry[0]
        dv_ref[...] = carry[1]

    whole = lambda h, j: (h, 0, 0)
    return _pcall(
        body, name="flash_bwd", grid=(heads, nq),
        in_specs=[pl.BlockSpec((None, tk, dk), lambda h, j: (h, j, 0)),
                  pl.BlockSpec((None, tk, V_DIM), lambda h, j: (h, j, 0)),
                  pl.BlockSpec((None, s, dk), whole),
                  pl.BlockSpec((s, V_DIM), lambda h, j: (0, do_col0 + h)),
                  pl.BlockSpec((s, V_DIM), lambda h, j: (0, h)),
                  pl.BlockSpec((None, s, 1), whole)],
        out_specs=(pl.BlockSpec((None, s, dk), whole),
                   pl.BlockSpec((None, tk, dk), lambda h, j: (h, j, 0)),
                   pl.BlockSpec((None, tk, V_DIM), lambda h, j: (h, j, 0))),
        out_shape=(jax.ShapeDtypeStruct((heads, s, dk), F32), jax.ShapeDtypeStruct((heads, s, dk), F32),
                   jax.ShapeDtypeStruct((heads, s, V_DIM), F32)),
        compiler_params=_params("parallel", "arbitrary"),
    )(kp, v, qp, dcat, o, lse)


def _attn_bwd_post(dqp, dkp, dv, cos_t, sin_t):
    heads, s, _ = dqp.shape
    ts = _tile(s, 256, 8)

    def body(dq_ref, dk_ref, dv_ref, cos_ref, sin_ref, q_out, kv_out, kr_out):
        lane = lax.broadcasted_iota(jnp.int32, (ts, LANES), 1)
        cos_v = cos_ref[...]
        sin_v = sin_ref[...]

        def rope_t(dy):
            return dy * cos_v + _swap_halves(dy * sin_v, lane)

        kr_sum = jnp.zeros((ts, LANES), F32)
        for h in range(heads):
            q_out[:, h * LANES:(h + 1) * LANES] = dq_ref[h, :, :LANES].astype(BF16)
            kv_out[:, h * LANES:(h + 1) * LANES] = dk_ref[h, :, :LANES].astype(BF16)
            kv_out[:, (heads + h) * LANES:(heads + h + 1) * LANES] = dv_ref[h].astype(BF16)
            kr_sum = kr_sum + dk_ref[h, :, LANES:]
        for pair in range(heads // 2):
            r = jnp.where(lane < ROPE_DIM, dq_ref[2 * pair, :, LANES:], dq_ref[2 * pair + 1, :, LANES:])
            q_out[:, (heads + pair) * LANES:(heads + pair + 1) * LANES] = rope_t(r).astype(BF16)
        kr = rope_t(kr_sum)
        kr_out[...] = (kr + pltpu.roll(kr, ROPE_DIM, 1)).astype(BF16)

    wq = heads * (NOPE_DIM + ROPE_DIM)
    wkv = heads * (NOPE_DIM + V_DIM)
    tab = pl.BlockSpec((ts, LANES), lambda i: (i, 0))
    return _pcall(
        body, name="attn_bwd_post", grid=(s // ts,),
        in_specs=[pl.BlockSpec((heads, ts, 2 * LANES), lambda i: (0, i, 0)),
                  pl.BlockSpec((heads, ts, 2 * LANES), lambda i: (0, i, 0)),
                  pl.BlockSpec((heads, ts, LANES), lambda i: (0, i, 0)), tab, tab],
        out_specs=(pl.BlockSpec((ts, wq), lambda i: (i, 0)), pl.BlockSpec((ts, wkv), lambda i: (i, 0)), tab),
        out_shape=(jax.ShapeDtypeStruct((s, wq), BF16), jax.ShapeDtypeStruct((s, wkv), BF16),
                   jax.ShapeDtypeStruct((s, LANES), BF16)),
        compiler_params=_params("parallel"),
    )(dqp, dkp, dv, cos_t, sin_t)


CONV_HALO = 16


def _conv_gate(xe, cw_ref, cb_ref):
    x1 = pltpu.roll(xe, 1, 0)
    x2 = pltpu.roll(xe, 2, 0)
    return cw_ref[2:3, :] * xe + cw_ref[1:2, :] * x1 + cw_ref[0:1, :] * x2 + cb_ref[...], x1, x2


def _ffn_act_fwd(gu, conv_w, conv_b, *, d_ff, tc=256):
    s = gu.shape[0]
    tc = _tile(d_ff, tc)
    nf = d_ff // tc
    tr = _tile(s, ROW_CHUNK, CONV_HALO)
    n_chunks = s // tr

    def body(g_ref, u_ref, cw_ref, cb_ref, a_ref):
        r = pl.program_id(1)
        xe = _rows_with_halo(g_ref, r, tr, n_chunks, CONV_HALO, 0).astype(F32)
        gc = _conv_gate(xe, cw_ref, cb_ref)[0][CONV_HALO:]
        a_ref[...] = (gc * jax.nn.sigmoid(gc) * u_ref[...].astype(F32)).astype(BF16)

    return _pcall(
        body, name="ffn_act_fwd", grid=(nf, n_chunks),
        in_specs=[pl.BlockSpec((s, tc), lambda j, r: (0, j)), pl.BlockSpec((tr, tc), lambda j, r: (r, nf + j)),
                  pl.BlockSpec((CONV_TAPS, tc), lambda j, r: (0, j)), pl.BlockSpec((1, tc), lambda j, r: (0, j))],
        out_specs=pl.BlockSpec((tr, tc), lambda j, r: (r, j)),
        out_shape=jax.ShapeDtypeStruct((s, d_ff), BF16),
        compiler_params=_params("parallel", "arbitrary"),
    )(gu, gu, conv_w, conv_b.reshape(1, d_ff))


def _ffn_act_bwd(da, gu, conv_w, conv_b, *, d_ff, tc=256):
    s = gu.shape[0]
    tc = _tile(d_ff, tc)
    nf = d_ff // tc
    tr = _tile(s, ROW_CHUNK, CONV_HALO)
    n_chunks = s // tr
    rows = tr + 2 * CONV_HALO
    main = slice(CONV_HALO, CONV_HALO + tr)

    def body(da_ref, g_ref, u_ref, cw_ref, cb_ref, dg_ref, du_ref, dcw_ref, dcb_ref):
        r = pl.program_id(1)
        xe = _rows_with_halo(g_ref, r, tr, n_chunks, CONV_HALO, CONV_HALO).astype(F32)
        dae = _rows_with_halo(da_ref, r, tr, n_chunks, CONV_HALO, CONV_HALO).astype(F32)
        ue = _rows_with_halo(u_ref, r, tr, n_chunks, CONV_HALO, CONV_HALO).astype(F32)
        gc, x1, x2 = _conv_gate(xe, cw_ref, cb_ref)
        sg = jax.nn.sigmoid(gc)
        du_ref[...] = (dae * gc * sg)[main].astype(BF16)
        dgc = dae * ue * sg * (1.0 + gc * (1.0 - sg))
        dg = (cw_ref[2:3, :] * dgc + cw_ref[1:2, :] * pltpu.roll(dgc, rows - 1, 0)
              + cw_ref[0:1, :] * pltpu.roll(dgc, rows - 2, 0))
        dg_ref[...] = dg[main].astype(BF16)
        dgc_m = dgc[main]
        dcb = jnp.sum(dgc_m, axis=0, keepdims=True)
        dcw = jnp.concatenate([jnp.sum(dgc_m * x2[main], axis=0, keepdims=True),
                               jnp.sum(dgc_m * x1[main], axis=0, keepdims=True),
                               jnp.sum(dgc_m * xe[main], axis=0, keepdims=True)], axis=0)

        @pl.when(r == 0)
        def _():
            dcb_ref[...] = dcb
            dcw_ref[...] = dcw

        @pl.when(r > 0)
        def _():
            dcb_ref[...] += dcb
            dcw_ref[...] += dcw

    col = pl.BlockSpec((s, tc), lambda j, r: (0, j))
    out = pl.BlockSpec((tr, tc), lambda j, r: (r, j))
    return _pcall(
        body, name="ffn_act_bwd", grid=(nf, n_chunks),
        in_specs=[col, col, pl.BlockSpec((s, tc), lambda j, r: (0, nf + j)),
                  pl.BlockSpec((CONV_TAPS, tc), lambda j, r: (0, j)), pl.BlockSpec((1, tc), lambda j, r: (0, j))],
        out_specs=(out, out, pl.BlockSpec((CONV_TAPS, tc), lambda j, r: (0, j)),
                   pl.BlockSpec((1, tc), lambda j, r: (0, j))),
        out_shape=(jax.ShapeDtypeStruct((s, d_ff), BF16), jax.ShapeDtypeStruct((s, d_ff), BF16),
                   jax.ShapeDtypeStruct((CONV_TAPS, d_ff), F32), jax.ShapeDtypeStruct((1, d_ff), F32)),
        compiler_params=_params("parallel", "arbitrary"),
    )(da, gu, gu, conv_w, conv_b.reshape(1, d_ff))


def _ple_embed(p_ref, w_ref):
    return jnp.dot(p_ref[...].astype(BF16), w_ref[...], preferred_element_type=F32)


def _ple_fwd(h, z, p_i, w_ple):
    s, d = h.shape
    ts = _tile(s, 512, 8)

    def body(h_ref, z_ref, p_ref, w_ref, o_ref):
        o_ref[...] = h_ref[...] + _ple_embed(p_ref, w_ref) * jax.nn.sigmoid(z_ref[...])

    row = pl.BlockSpec((ts, d), lambda i: (i, 0))
    return _pcall(body, name="ple_fwd", grid=(s // ts,),
                  in_specs=[row, row, pl.BlockSpec((ts, p_i.shape[1]), lambda i: (i, 0)),
                            pl.BlockSpec(w_ple.shape, lambda i: (0, 0))],
                  out_specs=row, out_shape=jax.ShapeDtypeStruct((s, d), F32),
                  compiler_params=_params("parallel"))(h, z, p_i, w_ple)


def _ple_bwd(dh, z, p_i, w_ple, dep=None):
    s, d = dh.shape
    ts = _tile(s, 512, 8)

    def body(dh_ref, z_ref, p_ref, w_ref, *rest):
        de_ref, dz_ref = rest[-2:]
        gt = jax.nn.sigmoid(z_ref[...])
        dhv = dh_ref[...]
        de_ref[...] = (dhv * gt).astype(BF16)
        dz_ref[...] = (dhv * _ple_embed(p_ref, w_ref) * gt * (1.0 - gt)).astype(BF16)

    row = pl.BlockSpec((ts, d), lambda i: (i, 0))
    deps = [] if dep is None else [dep]
    return _pcall(body, name="ple_bwd", grid=(s // ts,),
                  in_specs=[row, row, pl.BlockSpec((ts, p_i.shape[1]), lambda i: (i, 0)),
                            pl.BlockSpec(w_ple.shape, lambda i: (0, 0))] + [HBM_SPEC] * len(deps),
                  out_specs=(row, row),
                  out_shape=(jax.ShapeDtypeStruct((s, d), BF16), jax.ShapeDtypeStruct((s, d), BF16)),
                  compiler_params=_params("parallel"))(dh, z, p_i, w_ple, *deps)


def _loss_head(h, g, target):
    s, d = h.shape
    ts = _tile(s, 512, 8)

    def body(h_ref, g_ref, t_ref, loss_ref, dh_ref, dg_ref):
        i = pl.program_id(0)
        x = h_ref[...]
        gv = g_ref[...]
        r = lax.rsqrt(jnp.mean(x * x, axis=-1, keepdims=True) + NORM_EPS)
        nh = x * r
        err = nh * gv - t_ref[...]
        part_loss = 0.5 * jnp.sum(jnp.mean(err * err, axis=-1, keepdims=True), axis=0, keepdims=True)
        dy = err * (1.0 / d)
        gd = dy * gv
        dh_ref[...] = (gd - nh * jnp.mean(gd * nh, axis=-1, keepdims=True)) * r
        part_g = jnp.sum(dy * nh, axis=0, keepdims=True)
        part_l = jnp.broadcast_to(part_loss, (1, LANES))

        @pl.when(i == 0)
        def _():
            dg_ref[...] = part_g
            loss_ref[...] = part_l

        @pl.when(i > 0)
        def _():
            dg_ref[...] += part_g
            loss_ref[...] += part_l

    row = pl.BlockSpec((ts, d), lambda i: (i, 0))
    vec = pl.BlockSpec((1, d), lambda i: (0, 0))
    return _pcall(
        body, name="loss_head", grid=(s // ts,), in_specs=[row, vec, row],
        out_specs=(pl.BlockSpec((1, LANES), lambda i: (0, 0)), row, vec),
        out_shape=(jax.ShapeDtypeStruct((1, LANES), F32), jax.ShapeDtypeStruct((s, d), F32),
                   jax.ShapeDtypeStruct((1, d), F32)),
        compiler_params=_params("arbitrary"),
    )(h, g.reshape(1, d), target)


HBM_SPEC = pl.BlockSpec(memory_space=pl.ANY)


def _flat_index(px, py, pc):
    return 4 * px + 2 * py + pc


def _all_gather(shards, *, name):
    n = len(shards)

    def body(*refs):
        ins, outs = refs[:n], refs[n:2 * n]
        send_sems, recv_sems, local_sems = refs[2 * n:]
        x, y, c = lax.axis_index("x"), lax.axis_index("y"), lax.axis_index("c")
        me, sibling = (x, y, c), (x, y, 1 - c)
        chips = [(1 - x, y), (x, 1 - y), (1 - x, 1 - y)]

        def copy(a, k, block, to, src=None):
            slot = outs[a].at[_flat_index(*block)]
            return pltpu.make_async_remote_copy(
                src_ref=slot if src is None else src, dst_ref=slot,
                send_sem=send_sems.at[a, k], recv_sem=recv_sems.at[a, k],
                device_id=to, device_id_type=pl.DeviceIdType.MESH)

        mine, first, passed = [], [], []
        for a in range(n):
            cp = pltpu.make_async_copy(ins[a], outs[a].at[_flat_index(*me)], local_sems.at[a])
            cp.start()
            mine.append(cp)
            first.append(copy(a, 0, me, sibling, src=ins[a]))
            first += [copy(a, 1 + j, me, (*chip, c), src=ins[a]) for j, chip in enumerate(chips)]
        for cp in first:
            cp.start()
        for j, chip in enumerate(chips):
            for a in range(n):
                copy(a, 1 + j, (*chip, c), me).wait_recv()
                fwd = copy(a, 4 + j, (*chip, c), sibling)
                fwd.start()
                passed.append(fwd)
        for a in range(n):
            copy(a, 0, sibling, me).wait_recv()
            for j, chip in enumerate(chips):
                copy(a, 4 + j, (*chip, 1 - c), me).wait_recv()
        for cp in first + passed:
            cp.wait_send()
        for cp in mine:
            cp.wait()

    return _pcall(
        body, name=name,
        in_specs=[HBM_SPEC] * n, out_specs=[HBM_SPEC] * n,
        out_shape=[jax.ShapeDtypeStruct((N_DEV,) + a.shape, a.dtype) for a in shards],
        scratch_shapes=[pltpu.SemaphoreType.DMA((n, 7)), pltpu.SemaphoreType.DMA((n, 7)),
                        pltpu.SemaphoreType.DMA((n,))],
    )(*shards)


HBM_ONLY = pl.BlockSpec(memory_space=pltpu.HBM)
SEM_SPEC = pl.BlockSpec(memory_space=pltpu.SEMAPHORE)
N_PEERS = N_DEV - 1
PEER_FLIPS = ((0, 0, 1), (1, 0, 0), (0, 1, 0), (1, 1, 0), (1, 0, 1), (0, 1, 1), (1, 1, 1))


def _exchange_refs(gather, src_refs, land_refs, send_sems, recv_sems):
    x, y, c = lax.axis_index("x"), lax.axis_index("y"), lax.axis_index("c")
    me = _flat_index(x, y, c)
    peers = [(x ^ fx, y ^ fy, c ^ fc) for fx, fy, fc in PEER_FLIPS]

    def out_copy(a, k):
        src = src_refs[a] if gather else src_refs[a].at[_flat_index(*peers[k])]
        return pltpu.make_async_remote_copy(
            src_ref=src, dst_ref=land_refs[a].at[me], send_sem=send_sems.at[a * N_PEERS + k],
            recv_sem=recv_sems.at[a * N_PEERS + k], device_id=peers[k], device_id_type=pl.DeviceIdType.MESH)

    def in_copy(a, k):
        src = src_refs[a] if gather else src_refs[a].at[me]
        return pltpu.make_async_remote_copy(
            src_ref=src, dst_ref=land_refs[a].at[_flat_index(*peers[k])], send_sem=send_sems.at[a * N_PEERS + k],
            recv_sem=recv_sems.at[a * N_PEERS + k], device_id=peers[k], device_id_type=pl.DeviceIdType.MESH)

    return out_copy, in_copy


def _exchange_start(srcs, *, gather, name, dep):
    n = len(srcs)
    lands = [lax.empty((N_DEV,) + a.shape if gather else a.shape, a.dtype) for a in srcs]

    def body(*refs):
        src_refs, land_refs = refs[:n], refs[n:2 * n]
        send_sems, recv_sems = refs[2 * n + 1], refs[2 * n + 2]
        token = refs[-1]
        out_copy, _ = _exchange_refs(gather, src_refs, land_refs, send_sems, recv_sems)
        for k in range(N_PEERS):
            for a in range(n):
                out_copy(a, k).start()
        token[...] = jnp.zeros_like(token)

    hbm = lambda a: pltpu.with_memory_space_constraint(a, pltpu.HBM)
    return _pcall(
        body, name=name,
        out_shape=(pltpu.SemaphoreType.DMA((n * N_PEERS,)), pltpu.SemaphoreType.DMA((n * N_PEERS,)),
                   *[pltpu.HBM(a.shape, a.dtype) for a in srcs], *[pltpu.HBM(a.shape, a.dtype) for a in lands],
                   jax.ShapeDtypeStruct((8, LANES), F32)),
        in_specs=[HBM_ONLY] * (2 * n) + [HBM_SPEC],
        out_specs=(SEM_SPEC, SEM_SPEC, *[HBM_ONLY] * (2 * n), pl.BlockSpec(memory_space=pltpu.VMEM)),
        input_output_aliases={i: 2 + i for i in range(2 * n)},
        compiler_params=pltpu.CompilerParams(has_side_effects=pltpu.SideEffectType.DATAFLOW_SIDE_EFFECTING),
    )(*[hbm(a) for a in srcs], *[hbm(a) for a in lands], dep)


def _exchange_wait(started, after, *, gather, name):
    send_sems, recv_sems = started[0], started[1]
    n = (len(started) - 3) // 2
    srcs, lands = started[2:2 + n], started[2 + n:2 + 2 * n]

    def body(*refs):
        src_refs, land_refs = refs[:n], refs[n:2 * n]
        s_sems, r_sems = refs[2 * n], refs[2 * n + 1]
        out_copy, in_copy = _exchange_refs(gather, src_refs, land_refs, s_sems, r_sems)
        for k in range(N_PEERS):
            for a in range(n):
                out_copy(a, k).wait_send()
                in_copy(a, k).wait_recv()

    res = _pcall(
        body, name=name,
        out_shape=tuple(pltpu.HBM(a.shape, a.dtype) for a in (*srcs, *lands)),
        in_specs=[HBM_ONLY] * (2 * n) + [SEM_SPEC, SEM_SPEC, HBM_SPEC],
        out_specs=tuple([HBM_ONLY] * (2 * n)),
        input_output_aliases={i: i for i in range(2 * n)},
        compiler_params=pltpu.CompilerParams(has_side_effects=pltpu.SideEffectType.DATAFLOW_SIDE_EFFECTING),
    )(*srcs, *lands, send_sems, recv_sems, after)
    return _with_own_slot(res[:n], res[n:], gather)


def _with_own_slot(srcs, lands, gather):
    me = _flat_index(lax.axis_index("x"), lax.axis_index("y"), lax.axis_index("c"))
    full = []
    for src, land in zip(srcs, lands):
        own = src if gather else lax.dynamic_index_in_dim(src, me, 0, keepdims=False)
        full.append(lax.dynamic_update_slice(land, own[None], (me,) + (0,) * own.ndim))
    return full


def _adam_math(g, w, m, v):
    m = ADAM_B1 * m + (1.0 - ADAM_B1) * g
    v = ADAM_B2 * v + (1.0 - ADAM_B2) * jnp.square(g)
    m_hat = m / (1.0 - ADAM_B1 ** ADAM_STEP)
    v_hat = v / (1.0 - ADAM_B2 ** ADAM_STEP)
    delta = -ADAM_LR * (m_hat / (jnp.sqrt(v_hat) + ADAM_EPS) + ADAM_WD * w)
    return delta, m, v


def _adamw(contribs, w, m, v, *, name):
    layers = len(contribs)
    _, r, c = contribs[0].shape
    tr = _tile(r, max(8, (256 * 1024 // c) // 8 * 8), 8)

    def body(*refs):
        g_refs = refs[:layers]
        w_ref, m_ref, v_ref, go_ref, d_ref, mo_ref, vo_ref = refs[layers:]
        layer = pl.program_id(0)
        for l2 in range(layers):

            @pl.when(layer == l2)
            def _(g_ref=g_refs[l2]):
                g = g_ref[0].astype(F32)
                for k in range(1, N_DEV):
                    g = g + g_ref[k].astype(F32)
                delta, m_new, v_new = _adam_math(g, w_ref[...], m_ref[...], v_ref[...])
                go_ref[...] = g
                d_ref[...] = delta
                mo_ref[...] = m_new
                vo_ref[...] = v_new

    g_specs = [pl.BlockSpec((N_DEV, tr, c), lambda l, i, l2=l2: (0, jnp.where(l == l2, i, 0), 0))
               for l2 in range(layers)]
    blk = pl.BlockSpec((None, tr, c), lambda l, i: (l, i, 0))
    out = jax.ShapeDtypeStruct((layers, r, c), F32)
    return _pcall(
        body, name=name, grid=(layers, r // tr), in_specs=g_specs + [blk, blk, blk],
        out_specs=(blk, blk, blk, blk), out_shape=(out, out, out, out),
        compiler_params=_params("arbitrary", "arbitrary"),
    )(*contribs, w, m, v)


def _heads_split(w, heads, first, second):
    k = w.shape[0]
    w3 = w.reshape(k, heads, first + second)
    return jnp.concatenate([w3[:, :, :first].reshape(k, heads * first),
                            w3[:, :, first:].reshape(k, heads * second)], axis=1)


def _heads_join(w, heads, first, second):
    k = w.shape[0]
    a = w[:, :heads * first].reshape(k, heads, first)
    b = w[:, heads * first:].reshape(k, heads, second)
    return jnp.concatenate([a, b], axis=2).reshape(k, heads * (first + second))


def _full_from_gathered(kind, g):
    if kind == "col":
        return jnp.transpose(g, (1, 0, 2)).reshape(g.shape[1], N_DEV * g.shape[2])
    if kind == "row":
        return g.reshape(N_DEV * g.shape[1], g.shape[2])
    return jnp.transpose(g, (1, 0, 2, 3)).reshape(g.shape[1], N_DEV * g.shape[2], g.shape[3])


def _blocks_from_full(kind, f):
    if kind == "col":
        k, n = f.shape
        return jnp.transpose(f.reshape(k, N_DEV, n // N_DEV), (1, 0, 2))
    if kind == "row":
        k, n = f.shape
        return f.reshape(N_DEV, k // N_DEV, n)
    g, c_in, c = f.shape
    return jnp.transpose(f.reshape(g, N_DEV, c_in // N_DEV, c), (1, 0, 2, 3))


def _rope_tables(positions):
    inv_freq = 1.0 / (ROPE_THETA ** (jnp.arange(0, ROPE_DIM, 2, dtype=F32) / ROPE_DIM))
    ang = positions.astype(F32)[:, None] * inv_freq
    cos, sin = jnp.cos(ang), jnp.sin(ang)
    return jnp.concatenate([cos, cos, cos, cos], axis=-1), jnp.concatenate([-sin, sin, -sin, sin], axis=-1)


def _layer_fwd(h0, p_i, fetch, rep, tabs, dims, dep=None):
    heads, d_pool, q_lora, d_ff = dims["heads"], dims["d_pool"], dims["q_lora"], dims["d_ff"]
    c = d_pool // POOL_GROUPS
    cos_t, sin_t = tabs
    scale = 1.0 / math.sqrt(NOPE_DIM + ROPE_DIM)
    n1 = _rms_fwd(h0, rep["norm_mix_g"], name="rms_mix_fwd", dep=dep)
    w = dict(fetch("mix", n1))
    u = _mm(n1, w["w_in"], name="mm_in_fwd")
    y_pool, diff = _pool_fwd(u, w["pool_w"], rep["pool_scale"], c=c)
    nq = _rms_fwd(u, rep["q_norm_g"], name="rms_q_fwd", col_block=d_pool // q_lora)
    nkv = _rms_fwd(u, rep["kv_norm_g"], name="rms_kv_fwd", col_block=d_pool // q_lora + 1)
    q = _mm(nq, w["w_uq"], name="mm_uq_fwd")
    kv = _mm(nkv, w["w_ukv"], name="mm_ukv_fwd")
    kr = u[:, d_pool + 2 * q_lora:]
    kr2 = jnp.concatenate([kr, kr], axis=-1)
    qp, kp, v = _qkv_prep(q, kv, kr2, cos_t, sin_t, heads=heads)
    o, o_mm, lse = _flash_fwd(qp, kp, v, scale=scale)
    cat = jnp.concatenate([y_pool, o_mm], axis=1)
    h1 = _mm(cat, w["w_out"], name="mm_out_fwd", add=h0)
    n2 = _rms_fwd(h1, rep["norm_ffn_g"], name="rms_ffn_fwd")
    w.update(fetch("up", n2))
    gu = _mm(n2, w["w_up"], name="mm_up_fwd", out_dtype=BF16, b_blocks=(0, N_DEV))
    a = _ffn_act_fwd(gu, w["conv_w"], rep["conv_b"], d_ff=d_ff)
    w.update(fetch("down", a))
    h2 = _mm(a, w["w_down"], name="mm_down_fwd", add=h1)
    n3 = _rms_fwd(h2, rep["norm_ple_g"], name="rms_ple_fwd")
    z = _mm(n3, w["w_ple_gate"], name="mm_pgate_fwd")
    h3 = _ple_fwd(h2, z, p_i, w["w_ple"])
    saved = dict(h0=h0, n1=n1, u=u, cat=cat, diff=diff, nq=nq, nkv=nkv, qp=qp, kp=kp, v=v, o=o, lse=lse,
                 h1=h1, n2=n2, gu=gu, a=a, h2=h2, n3=n3, z=z)
    return h3, saved, w


def _layer_bwd(dh3, p_i, w, rep, tabs, dims, sv, dep=None, hooks=None):
    hooks = hooks or {}
    heads, d_pool, q_lora, d_ff = dims["heads"], dims["d_pool"], dims["q_lora"], dims["d_ff"]
    c = d_pool // POOL_GROUPS
    cos_t, sin_t = tabs
    scale = 1.0 / math.sqrt(NOPE_DIM + ROPE_DIM)
    gr = {}
    de, dz = _ple_bwd(dh3, sv["z"], p_i, w["w_ple"], dep)
    gr["w_ple"] = _mm(p_i, de, name="mm_ple_dw", ta=True, out_dtype=BF16)
    gr["w_ple_gate"] = _mm(sv["n3"], dz, name="mm_pgate_dw", ta=True, out_dtype=BF16)
    dn3 = _mm(dz, w["w_ple_gate"], name="mm_pgate_dx", tb=True)
    dh2, dh2_mm, gr["norm_ple_g"] = _rms_bwd(dn3, sv["h2"], rep["norm_ple_g"], name="rms_ple_bwd", res=dh3,
                                             matmul_copy=True)
    gr["w_down"] = _mm(sv["a"], dh2_mm, name="mm_down_dw", ta=True, out_dtype=BF16)
    dep_down = hooks["down"](dh2, gr) if "down" in hooks else None
    da = _mm(dh2_mm, w["w_down"], name="mm_down_dx", tb=True, out_dtype=BF16, dep=dep_down)
    dgate, dup, gr["conv_w"], gr["conv_b"] = _ffn_act_bwd(da, sv["gu"], w["conv_w"], rep["conv_b"], d_ff=d_ff)
    half, per = N_DEV // 2, w["w_up"].shape[2]
    dw_gate = _mm(sv["n2"], dgate, name="mm_up_gate_dw", ta=True, out_dtype=BF16, out_blocks=(N_DEV, 0, per))
    gr["w_up"] = _mm(sv["n2"], dup, name="mm_up_up_dw", ta=True, out_dtype=BF16, out_blocks=(N_DEV, half, per),
                     out_init=dw_gate)
    dep_up = hooks["up"](gr["w_up"], gr) if "up" in hooks else None
    dn2 = _mm(dgate, w["w_up"], name="mm_up_gate_dx", tb=True, b_blocks=(0, half), dep=dep_up)
    dn2 = _mm(dup, w["w_up"], name="mm_up_up_dx", tb=True, b_blocks=(half, half), add=dn2)
    dh1, dh1_mm, gr["norm_ffn_g"] = _rms_bwd(dn2, sv["h1"], rep["norm_ffn_g"], name="rms_ffn_bwd", res=dh2,
                                             matmul_copy=True)
    gr["w_out"] = _mm(sv["cat"], dh1_mm, name="mm_out_dw", ta=True, out_dtype=BF16)
    dcat = _mm(dh1_mm, w["w_out"], name="mm_out_dx", tb=True)
    do_col0 = d_pool // V_DIM
    dqp, dkp, dv = _flash_bwd(sv["qp"], sv["kp"], sv["v"], dcat, sv["o"], sv["lse"], scale=scale, do_col0=do_col0)
    dq, dkv, dkr2 = _attn_bwd_post(dqp, dkp, dv, cos_t, sin_t)
    gr["w_uq"] = _mm(sv["nq"], dq, name="mm_uq_dw", ta=True, out_dtype=BF16)
    gr["w_ukv"] = _mm(sv["nkv"], dkv, name="mm_ukv_dw", ta=True, out_dtype=BF16)
    dnq = _mm(dq, w["w_uq"], name="mm_uq_dx", tb=True)
    dnkv = _mm(dkv, w["w_ukv"], name="mm_ukv_dx", tb=True)
    dcq, gr["q_norm_g"] = _rms_bwd(dnq, sv["u"], rep["q_norm_g"], name="rms_q_bwd",
                                   col_block=d_pool // q_lora, out_dtype=BF16)
    dckv, gr["kv_norm_g"] = _rms_bwd(dnkv, sv["u"], rep["kv_norm_g"], name="rms_kv_bwd",
                                     col_block=d_pool // q_lora + 1, out_dtype=BF16)
    du_pool, gr["pool_w"], gr["pool_scale"] = _pool_bwd(dcat, sv["diff"], w["pool_w"], rep["pool_scale"], c=c)
    du = jnp.concatenate([du_pool, dcq, dckv, dkr2[:, :ROPE_DIM]], axis=-1)
    gr["w_in"] = _mm(sv["n1"], du, name="mm_in_dw", ta=True, out_dtype=BF16)
    dn1 = _mm(du, w["w_in"], name="mm_in_dx", tb=True)
    dh0, gr["norm_mix_g"] = _rms_bwd(dn1, sv["h0"], rep["norm_mix_g"], name="rms_mix_bwd", res=dh1)
    return dh0, gr


def _as2d(a):
    return a.reshape(a.shape[0], -1, a.shape[-1])


def kernel(x, p, positions, norm_mix_g, w_in, pool_w, pool_scale, q_norm_g, w_uq, kv_norm_g, w_ukv, w_out, norm_ffn_g, w_up, conv_w, conv_b, w_down, norm_ple_g, w_ple, w_ple_gate, final_norm_g, loss_target, m_norm_mix_g, m_w_in, m_pool_w, m_pool_scale, m_q_norm_g, m_w_uq, m_kv_norm_g, m_w_ukv, m_w_out, m_norm_ffn_g, m_w_up, m_conv_w, m_conv_b, m_w_down, m_norm_ple_g, m_w_ple, m_w_ple_gate, m_final_norm_g, v_norm_mix_g, v_w_in, v_pool_w, v_pool_scale, v_q_norm_g, v_w_uq, v_kv_norm_g, v_w_ukv, v_w_out, v_norm_ffn_g, v_w_up, v_conv_w, v_conv_b, v_w_down, v_norm_ple_g, v_w_ple, v_w_ple_gate, v_final_norm_g):
    weights = dict(norm_mix_g=norm_mix_g, w_in=w_in, pool_w=pool_w, pool_scale=pool_scale, q_norm_g=q_norm_g,
                   w_uq=w_uq, kv_norm_g=kv_norm_g, w_ukv=w_ukv, w_out=w_out, norm_ffn_g=norm_ffn_g, w_up=w_up,
                   conv_w=conv_w, conv_b=conv_b, w_down=w_down, norm_ple_g=norm_ple_g, w_ple=w_ple,
                   w_ple_gate=w_ple_gate, final_norm_g=final_norm_g)
    m_in = dict(norm_mix_g=m_norm_mix_g, w_in=m_w_in, pool_w=m_pool_w, pool_scale=m_pool_scale, q_norm_g=m_q_norm_g,
                w_uq=m_w_uq, kv_norm_g=m_kv_norm_g, w_ukv=m_w_ukv, w_out=m_w_out, norm_ffn_g=m_norm_ffn_g,
                w_up=m_w_up, conv_w=m_conv_w, conv_b=m_conv_b, w_down=m_w_down, norm_ple_g=m_norm_ple_g,
                w_ple=m_w_ple, w_ple_gate=m_w_ple_gate, final_norm_g=m_final_norm_g)
    v_in = dict(norm_mix_g=v_norm_mix_g, w_in=v_w_in, pool_w=v_pool_w, pool_scale=v_pool_scale, q_norm_g=v_q_norm_g,
                w_uq=v_w_uq, kv_norm_g=v_kv_norm_g, w_ukv=v_w_ukv, w_out=v_w_out, norm_ffn_g=v_norm_ffn_g,
                w_up=v_w_up, conv_w=v_conv_w, conv_b=v_conv_b, w_down=v_w_down, norm_ple_g=v_norm_ple_g,
                w_ple=v_w_ple, w_ple_gate=v_w_ple_gate, final_norm_g=v_final_norm_g)

    depth = w_in.shape[0]
    s, d_model = x.shape[1], x.shape[2]
    d_pool = pool_scale.shape[-1]
    q_lora = q_norm_g.shape[-1]
    d_ff = conv_b.shape[-1]
    heads = (w_uq.shape[-1] * N_DEV) // (NOPE_DIM + ROPE_DIM)
    dims = dict(heads=heads, d_pool=d_pool, q_lora=q_lora, d_ff=d_ff)

    groups = {"mix": ("w_in", "pool_w", "w_uq", "w_ukv", "w_out"), "up": ("w_up", "conv_w"),
              "down": ("w_down", "w_ple", "w_ple_gate")}

    def group_shards(i, group):
        return [weights[n][i] if n == "conv_w" else weights[n][i].astype(BF16) for n in groups[group]]

    def start_weights(i, group, dep):
        return _exchange_start(group_shards(i, group), gather=True, name=f"weights_{group}_start_{i}", dep=dep)

    def full_group(group, gathered_g):
        w = {n: g if n == "w_up" else _full_from_gathered(SHARD_KIND[n], g) for n, g in zip(groups[group], gathered_g)}
        if group == "mix":
            w["w_uq"] = _heads_split(w["w_uq"], heads, NOPE_DIM, ROPE_DIM)
            w["w_ukv"] = _heads_split(w["w_ukv"], heads, NOPE_DIM, V_DIM)
        return w

    tabs = _rope_tables(positions[0])

    arrived = {(0, "mix"): _all_gather(group_shards(0, "mix"), name="weights_mix_gather_0")}
    travelling = {}
    token = arrived[(0, "mix")][0]
    for group in ("up", "down"):
        travelling[(0, group)] = start_weights(0, group, token)
        token = travelling[(0, group)][-1]
    layer_w = []
    h = x[0]
    saved = []
    for i in range(depth):
        if i + 1 < depth:
            for group in ("mix", "up", "down"):
                travelling[(i + 1, group)] = start_weights(i + 1, group, token if i == 0 and group == "mix" else
                                                           (h if group == "mix" else token))
                token = travelling[(i + 1, group)][-1]

        def fetch(group, after, i=i):
            if (i, group) not in arrived:
                arrived[(i, group)] = _exchange_wait(travelling[(i, group)], after, gather=True,
                                                     name=f"weights_{group}_wait_{i}")
            return full_group(group, arrived[(i, group)])

        rep = {n: weights[n][i] for n in REPLICATED}
        h, sv, w = _layer_fwd(h, p[i, 0], fetch, rep, tabs, dims, dep=token if i + 1 < depth or i == 0 else None)
        layer_w.append((w, rep))
        saved.append(sv)
    loss_row, dh, g_final = _loss_head(h, final_norm_g, loss_target[0])
    loss = lax.psum(loss_row[0, 0], MESH_AXES)

    def start_grads(group, gr, dep, i):
        names = groups[group]
        blocks = [gr[n] if n == "w_up" else _blocks_from_full(SHARD_KIND[n], gr[n]).astype(BF16) for n in names]
        return _exchange_start(blocks, gather=False, name=f"grads_{group}_start_{i}", dep=dep)

    def end_grads(group, started, after, i):
        got = _exchange_wait(started, after, gather=False, name=f"grads_{group}_wait_{i}")
        received[i].update(zip(groups[group], got))

    layer_grads = [None] * depth
    received = [dict() for _ in range(depth)]
    pending = None
    for i in reversed(range(depth)):
        w, rep = layer_w[i]
        state = {}

        def on_down(dh2, gr, i=i, state=state):
            state["down"] = start_grads("down", gr, dh2, i)
            return state["down"][-1]

        def on_up(dw_up, gr, i=i, pending=pending, state=state):
            order = gr["conv_w"]
            if pending is not None:
                for group in ("down", "up", "mix"):
                    end_grads(group, pending[group], order, i + 1)
            state["up"] = start_grads("up", gr, order, i)
            return state["up"][-1]

        dh, gr = _layer_bwd(dh, p[i, 0], w, rep, tabs, dims, saved[i],
                            dep=loss.reshape(1, 1) if pending is None else pending["mix"][-1],
                            hooks={"down": on_down, "up": on_up})
        gr["w_uq"] = _heads_join(gr["w_uq"], heads, NOPE_DIM, ROPE_DIM)
        gr["w_ukv"] = _heads_join(gr["w_ukv"], heads, NOPE_DIM, V_DIM)
        layer_grads[i] = gr
        state["mix"] = start_grads("mix", gr, dh, i)
        pending = state
    grad_x = dh[None]

    out = {}

    def update(n):
        shape = weights[n].shape
        recs = [received[i][n].reshape((N_DEV, -1, shape[-1])) for i in range(depth)]
        res = _adamw(recs, _as2d(weights[n]), _as2d(m_in[n]), _as2d(v_in[n]), name="adamw_" + n)
        out[n] = tuple(r.reshape(shape) for r in res)

    end_grads("down", pending["down"], pending["mix"][-1], 0)
    end_grads("up", pending["up"], pending["mix"][-1], 0)
    for n in groups["down"] + groups["up"]:
        update(n)
    end_grads("mix", pending["mix"], out["w_up"][0], 0)
    for n in groups["mix"]:
        update(n)

    small_names = REPLICATED + ("final_norm_g",)

    def pack(get):
        rows = [jnp.stack([get(n, i).reshape(-1) for i in range(depth)]).reshape(-1) for n in REPLICATED]
        rows.append(get("final_norm_g", None).reshape(-1))
        return jnp.concatenate(rows).reshape(1, -1, LANES)

    g_small = pack(lambda n, i: g_final if i is None else layer_grads[i][n])
    w_small = pack(lambda n, i: weights[n] if i is None else weights[n][i])
    m_small = pack(lambda n, i: m_in[n] if i is None else m_in[n][i])
    v_small = pack(lambda n, i: v_in[n] if i is None else v_in[n][i])
    (g_all,) = _all_gather([g_small], name="small_grads_all_gather")
    res_small = _adamw([g_all[:, 0]], w_small, m_small, v_small, name="adamw_small")

    def unpack(flat3):
        flat = flat3.reshape(-1)
        res, off = {}, 0
        for n in REPLICATED:
            width = weights[n].shape[-1]
            res[n] = flat[off:off + depth * width].reshape(depth, width)
            off += depth * width
        res["final_norm_g"] = flat[off:off + d_model]
        return res

    small = [unpack(r) for r in res_small]
    for n in small_names:
        out[n] = tuple(small[k][n] for k in range(4))

    outs = [loss, grad_x]
    for k in range(4):
        outs += [out[n][k] for n in WEIGHT_ORDER]
    return tuple(outs)
```

```python
import math

import jax
import jax.numpy as jnp
from jax import lax
from jax.experimental import pallas as pl
from jax.experimental.pallas import tpu as pltpu

F32 = jnp.float32
BF16 = jnp.bfloat16

N_DEV = 8
MESH_AXES = ("x", "y", "c")
NOPE_DIM = 128
ROPE_DIM = 64
V_DIM = 128
POOL_GROUPS = 4
CONV_TAPS = 3
ROPE_THETA = 10000.0
NORM_EPS = 1e-6
ADAM_LR = 0.001
ADAM_B1 = 0.9
ADAM_B2 = 0.999
ADAM_EPS = 1e-08
ADAM_WD = 0.01
ADAM_STEP = 10
LANES = 128
VMEM_LIMIT_BYTES = 56 * 1024 * 1024

SHARD_KIND = {"w_in": "col", "pool_w": "pool", "w_uq": "col", "w_ukv": "col", "w_out": "row", "w_up": "col",
              "conv_w": "col", "w_down": "row", "w_ple": "col", "w_ple_gate": "row"}
REPLICATED = ("norm_mix_g", "pool_scale", "q_norm_g", "kv_norm_g", "norm_ffn_g", "conv_b", "norm_ple_g")
WEIGHT_ORDER = ("norm_mix_g", "w_in", "pool_w", "pool_scale", "q_norm_g", "w_uq", "kv_norm_g", "w_ukv", "w_out",
                "norm_ffn_g", "w_up", "conv_w", "conv_b", "w_down", "norm_ple_g", "w_ple", "w_ple_gate",
                "final_norm_g")

_pcall = pl.pallas_call


def _params(*sem):
    return pltpu.CompilerParams(dimension_semantics=sem or None, vmem_limit_bytes=VMEM_LIMIT_BYTES)


def _tile(n, pref, mult=LANES):
    if n <= pref:
        return n
    t = (pref // mult) * mult
    while t >= mult:
        if n % t == 0:
            return t
        t -= mult
    return n


MM_VMEM_BUDGET_BYTES = 44 * 1024 * 1024
MM_TILE_PREFS = (1536, 1024, 512, 256)
MM_MIN_TK = 1024
V7X_MXU_FLOPS = 850e12
V7X_SPLIT_K_SLOWDOWN = 1.3
V7X_HBM_BYTES_PER_S = 2.8e12
GRID_STEP_S = 0.35e-6


def _mm_tiles(m, n, k, a_bytes, b_bytes, o_bytes, has_add, tn_fixed=None, tk_fixed=None, a_copied=False,
              b_copied=False):
    out_bytes = o_bytes + (4 if has_add else 0)
    tm_cands = sorted({_tile(m, pref) for pref in MM_TILE_PREFS}, reverse=True)
    tn_cands = [tn_fixed] if tn_fixed else sorted({_tile(n, pref) for pref in MM_TILE_PREFS}, reverse=True)
    if tk_fixed:
        tk_cands = [tk_fixed]
    else:
        tk_cands = [k] + [t for t in range((k - 1) // LANES * LANES, MM_MIN_TK - 1, -LANES) if k % t == 0]
    best = None
    for tm in tm_cands:
        for tn in tn_cands:
            for tk in tk_cands:
                vmem = 2 * (tm * tk * a_bytes + tk * tn * b_bytes) + 2 * tm * tn * out_bytes
                vmem += tm * tn * 4 if tk < k else 0
                vmem += (tm * tk * 2 if a_copied else 0) + (tk * tn * 2 if b_copied else 0)
                if vmem > MM_VMEM_BUDGET_BYTES:
                    continue
                steps = (m // tm) * (n // tn) * (k // tk)
                hbm = a_bytes * m * k * (1 if tk == k else n // tn) + b_bytes * k * n * (m // tm) + out_bytes * m * n
                mxu = 2 * m * n * k / V7X_MXU_FLOPS * (1.0 if tk == k else V7X_SPLIT_K_SLOWDOWN)
                cost = max(mxu, hbm / V7X_HBM_BYTES_PER_S) + steps * GRID_STEP_S
                if best is None or cost < best[0]:
                    best = (cost, tm, tn, tk)
    assert best is not None, (m, n, k)
    return best[1:]


def _mm(a, b, *, name, ta=False, tb=False, add=None, out_dtype=F32, b_blocks=None, out_blocks=None,
        out_init=None, dep=None):
    m = a.shape[1] if ta else a.shape[0]
    kdim = a.shape[0] if ta else a.shape[1]
    tn_fixed = tk_fixed = None
    if b_blocks is None:
        n, k_b = (b.shape if tb else b.shape[::-1])
        assert k_b == kdim, (name, k_b, kdim)
    else:
        first_b, count_b = b_blocks
        per_b = b.shape[2]
        if tb:
            n = b.shape[1]
            assert kdim == count_b * per_b, name
            tk_fixed = per_b
        else:
            n = count_b * per_b
            assert kdim == b.shape[1], name
            tn_fixed = per_b
    if out_blocks is not None:
        nb_out, first_o, tn_fixed = out_blocks
    tm, tn, tk = _mm_tiles(m, n, kdim, a.dtype.itemsize, b.dtype.itemsize, jnp.dtype(out_dtype).itemsize,
                           add is not None, tn_fixed, tk_fixed, a_copied=ta or a.dtype != BF16,
                           b_copied=tb or b.dtype != BF16)
    assert kdim % tk == 0 and n % tn == 0 and m % tm == 0, name
    nk = kdim // tk
    dims = (((0 if ta else 1,), (1 if tb else 0,)), ((), ()))

    def body(*refs):
        a_ref, b_ref = refs[0], refs[1]
        add_ref = refs[2] if add is not None else None
        o_ref = refs[n_in]
        part = lax.dot_general(a_ref[...].astype(BF16), b_ref[...].astype(BF16), dims, preferred_element_type=F32)

        def finish(r):
            if add_ref is not None:
                r = r + add_ref[...].astype(F32)
            o_ref[...] = r.astype(out_dtype)

        if nk == 1:
            finish(part)
            return
        acc = refs[n_in + 1]
        k = pl.program_id(2)

        @pl.when(k == 0)
        def _():
            acc[...] = part

        @pl.when(jnp.logical_and(k > 0, k < nk - 1))
        def _():
            acc[...] += part

        @pl.when(k == nk - 1)
        def _():
            finish(acc[...] + part)

    if ta:
        a_spec = pl.BlockSpec((tk, tm), lambda i, j, k: (k, i))
    else:
        a_spec = pl.BlockSpec((tm, tk), lambda i, j, k: (i, k))
    if b_blocks is not None and tb:
        b_spec = pl.BlockSpec((None, tn, tk), lambda i, j, k: (k + first_b, j, 0))
    elif b_blocks is not None:
        b_spec = pl.BlockSpec((None, tk, tn), lambda i, j, k: (j + first_b, k, 0))
    elif tb:
        b_spec = pl.BlockSpec((tn, tk), lambda i, j, k: (j, k))
    else:
        b_spec = pl.BlockSpec((tk, tn), lambda i, j, k: (k, j))
    in_specs = [a_spec, b_spec]
    args = [a, b]
    if add is not None:
        in_specs.append(pl.BlockSpec((tm, tn), lambda i, j, k: (i, j)))
        args.append(add)
    aliases = {}
    if out_init is not None:
        aliases = {len(args): 0}
        in_specs.append(HBM_SPEC)
        args.append(out_init)
    if dep is not None:
        in_specs.append(HBM_SPEC)
        args.append(dep)
    n_in = len(args)
    if out_blocks is None:
        out_spec = pl.BlockSpec((tm, tn), lambda i, j, k: (i, j))
        out_shape = jax.ShapeDtypeStruct((m, n), out_dtype)
    else:
        out_spec = pl.BlockSpec((None, tm, tn), lambda i, j, k: (j + first_o, i, 0))
        out_shape = jax.ShapeDtypeStruct((nb_out, m, tn), out_dtype)
    return _pcall(
        body, name=name, grid=(m // tm, n // tn, nk), in_specs=in_specs, out_specs=out_spec, out_shape=out_shape,
        scratch_shapes=[pltpu.VMEM((tm, tn), F32)] if nk > 1 else [], input_output_aliases=aliases,
        compiler_params=_params("parallel", "parallel", "arbitrary"),
    )(*args)


def _rms_fwd(h, g, *, name, col_block=0, dep=None):
    s = h.shape[0]
    d = g.shape[-1]
    ts = _tile(s, 512, 8)

    def body(h_ref, g_ref, *rest):
        n_ref = rest[-1]
        x = h_ref[...]
        r = lax.rsqrt(jnp.mean(x * x, axis=-1, keepdims=True) + NORM_EPS)
        n_ref[...] = (x * r * g_ref[...]).astype(BF16)

    deps = [] if dep is None else [dep]
    return _pcall(
        body, name=name, grid=(s // ts,),
        in_specs=[pl.BlockSpec((ts, d), lambda i: (i, col_block)), pl.BlockSpec((1, d), lambda i: (0, 0))]
        + [HBM_SPEC] * len(deps),
        out_specs=pl.BlockSpec((ts, d), lambda i: (i, 0)),
        out_shape=jax.ShapeDtypeStruct((s, d), BF16),
        compiler_params=_params("parallel"),
    )(h, g.reshape(1, d), *deps)


def _rms_bwd(dn, h, g, *, name, res=None, col_block=0, out_dtype=F32, matmul_copy=False):
    s = dn.shape[0]
    d = g.shape[-1]
    ts = _tile(s, 512, 8)

    def body(*refs):
        dn_ref, h_ref, g_ref = refs[:3]
        res_ref = refs[3] if res is not None else None
        dh_ref, dg_ref = refs[n_in], refs[-1]
        i = pl.program_id(0)
        x = h_ref[...]
        r = lax.rsqrt(jnp.mean(x * x, axis=-1, keepdims=True) + NORM_EPS)
        nh = x * r
        dnv = dn_ref[...]
        gd = dnv * g_ref[...]
        dh = (gd - nh * jnp.mean(gd * nh, axis=-1, keepdims=True)) * r
        if res_ref is not None:
            dh = dh + res_ref[...]
        dh_ref[...] = dh.astype(out_dtype)
        if matmul_copy:
            refs[n_in + 1][...] = dh.astype(BF16)
        part = jnp.sum(dnv * nh, axis=0, keepdims=True)

        @pl.when(i == 0)
        def _():
            dg_ref[...] = part

        @pl.when(i > 0)
        def _():
            dg_ref[...] += part

    row = pl.BlockSpec((ts, d), lambda i: (i, 0))
    in_specs = [row, pl.BlockSpec((ts, d), lambda i: (i, col_block)), pl.BlockSpec((1, d), lambda i: (0, 0))]
    args = [dn, h, g.reshape(1, d)]
    if res is not None:
        in_specs.append(row)
        args.append(res)
    n_in = len(args)
    copies = [row] if matmul_copy else []
    return _pcall(
        body, name=name, grid=(s // ts,), in_specs=in_specs,
        out_specs=(row, *copies, pl.BlockSpec((1, d), lambda i: (0, 0))),
        out_shape=(jax.ShapeDtypeStruct((s, d), out_dtype), *[jax.ShapeDtypeStruct((s, d), BF16) for _ in copies],
                   jax.ShapeDtypeStruct((1, d), F32)),
        compiler_params=_params("arbitrary"),
    )(*args)


ROW_CHUNK = 1024


def _rows_with_halo(ref, r, t_rows, n_chunks, before, after):
    r0 = r * t_rows
    parts = []
    if before:
        hb = ref[pl.ds(pl.multiple_of(jnp.maximum(r0 - before, 0), before), before), :]
        parts.append(jnp.where(r > 0, hb, jnp.zeros_like(hb)))
    parts.append(ref[pl.ds(pl.multiple_of(r0, t_rows), t_rows), :])
    if after:
        ha = ref[pl.ds(pl.multiple_of(jnp.minimum(r0 + t_rows, n_chunks * t_rows - after), after), after), :]
        parts.append(jnp.where(r < n_chunks - 1, ha, jnp.zeros_like(ha)))
    return jnp.concatenate(parts, axis=0)


POOL_HALO = 16


def _pool_fwd(u, pool_w, pool_scale, *, c):
    s = u.shape[0]
    g_n = POOL_GROUPS
    tr = _tile(s, ROW_CHUNK, POOL_HALO)
    n_chunks = s // tr

    def body(u_ref, pw_ref, sc_ref, y_ref, d_ref):
        r = pl.program_id(1)
        w = jnp.left_shift(2, pl.program_id(0))
        xe = _rows_with_halo(u_ref, r, tr, n_chunks, POOL_HALO, 0)
        acc = xe
        for k in (1, 2, 4, 8):
            acc = jnp.where(k < w, acc + pltpu.roll(acc, k, 0), acc)
        t = r * tr + lax.broadcasted_iota(jnp.int32, (tr, 1), 0)
        cnt = jnp.minimum(t + 1, w).astype(F32)
        diff = (acc[POOL_HALO:] / cnt - xe[POOL_HALO:]).astype(BF16)
        d_ref[...] = diff
        y = jnp.dot(diff, pw_ref[...], preferred_element_type=F32) * sc_ref[...]
        y_ref[...] = y.astype(BF16)

    out = pl.BlockSpec((tr, c), lambda g, r: (r, g))
    return _pcall(
        body, name="pool_fwd", grid=(g_n, n_chunks),
        in_specs=[pl.BlockSpec((s, c), lambda g, r: (0, g)), pl.BlockSpec((None, c, c), lambda g, r: (g, 0, 0)),
                  pl.BlockSpec((1, c), lambda g, r: (0, g))],
        out_specs=(out, out),
        out_shape=(jax.ShapeDtypeStruct((s, g_n * c), BF16), jax.ShapeDtypeStruct((s, g_n * c), BF16)),
        compiler_params=_params("parallel", "arbitrary"),
    )(u, pool_w, pool_scale.reshape(1, g_n * c))


def _pool_bwd(dcat, diff, pool_w, pool_scale, *, c):
    s = dcat.shape[0]
    g_n = POOL_GROUPS
    tr = _tile(s, ROW_CHUNK, POOL_HALO)
    n_chunks = s // tr

    def body(dy_ref, d_ref, pw_ref, sc_ref, du_ref, dpw_ref, dsc_ref):
        r = pl.program_id(1)
        w = jnp.left_shift(2, pl.program_id(0))
        dye = _rows_with_halo(dy_ref, r, tr, n_chunks, 0, POOL_HALO)
        diff = d_ref[pl.ds(pl.multiple_of(r * tr, tr), tr), :]
        pw = pw_ref[...]
        yp = jnp.dot(diff, pw, preferred_element_type=F32)
        dsc = jnp.sum(dye[:tr] * yp, axis=0, keepdims=True)
        dyp = (dye * sc_ref[...]).astype(BF16)
        ddiff = lax.dot_general(dyp, pw, (((1,), (1,)), ((), ())), preferred_element_type=F32)
        dpw = lax.dot_general(diff, dyp[:tr], (((0,), (0,)), ((), ())), preferred_element_type=F32)
        t = r * tr + lax.broadcasted_iota(jnp.int32, (tr + POOL_HALO, 1), 0)
        cnt = jnp.minimum(t + 1, w).astype(F32)
        acc = ddiff / cnt
        rows = tr + POOL_HALO
        for k in (1, 2, 4, 8):
            acc = jnp.where(k < w, acc + pltpu.roll(acc, rows - k, 0), acc)
        du_ref[...] = (acc[:tr] - ddiff[:tr]).astype(BF16)

        @pl.when(r == 0)
        def _():
            dpw_ref[...] = dpw
            dsc_ref[...] = dsc

        @pl.when(r > 0)
        def _():
            dpw_ref[...] += dpw
            dsc_ref[...] += dsc

    col = lambda g, r: (0, g)
    wspec = pl.BlockSpec((None, c, c), lambda g, r: (g, 0, 0))
    vec = pl.BlockSpec((1, c), col)
    return _pcall(
        body, name="pool_bwd", grid=(g_n, n_chunks),
        in_specs=[pl.BlockSpec((s, c), col), pl.BlockSpec((s, c), col), wspec, vec],
        out_specs=(pl.BlockSpec((tr, c), lambda g, r: (r, g)), wspec, vec),
        out_shape=(jax.ShapeDtypeStruct((s, g_n * c), BF16), jax.ShapeDtypeStruct((g_n, c, c), F32),
                   jax.ShapeDtypeStruct((1, g_n * c), F32)),
        compiler_params=_params("parallel", "arbitrary"),
    )(dcat, diff, pool_w, pool_scale.reshape(1, g_n * c))


def _swap_halves(x, lane):
    return jnp.where((lane % ROPE_DIM) < ROPE_DIM // 2, pltpu.roll(x, LANES - ROPE_DIM // 2, 1),
                     pltpu.roll(x, ROPE_DIM // 2, 1))


def _qkv_prep(q, kv, kr2, cos_t, sin_t, *, heads):
    s = q.shape[0]
    ts = _tile(s, 256, 8)

    def body(q_ref, kv_ref, kr_ref, cos_ref, sin_ref, qo_ref, ko_ref, vo_ref):
        lane = lax.broadcasted_iota(jnp.int32, (ts, LANES), 1)
        cos_v = cos_ref[...]
        sin_v = sin_ref[...]

        def rope(x):
            return x * cos_v + _swap_halves(x, lane) * sin_v

        kr = rope(kr_ref[...]).astype(BF16)
        for pair in range(heads // 2):
            qr = rope(q_ref[:, (heads + pair) * LANES:(heads + pair + 1) * LANES])
            for half in range(2):
                h = 2 * pair + half
                qo_ref[h, :, :LANES] = q_ref[:, h * LANES:(h + 1) * LANES].astype(BF16)
                qo_ref[h, :, LANES:] = jnp.where(lane // ROPE_DIM == half, qr, 0.0).astype(BF16)
        for h in range(heads):
            ko_ref[h, :, :LANES] = kv_ref[:, h * LANES:(h + 1) * LANES].astype(BF16)
            ko_ref[h, :, LANES:] = kr
            vo_ref[h] = kv_ref[:, (heads + h) * LANES:(heads + h + 1) * LANES].astype(BF16)

    tab = pl.BlockSpec((ts, LANES), lambda i: (i, 0))
    return _pcall(
        body, name="qkv_prep", grid=(s // ts,),
        in_specs=[pl.BlockSpec((ts, q.shape[1]), lambda i: (i, 0)), pl.BlockSpec((ts, kv.shape[1]), lambda i: (i, 0)),
                  tab, tab, tab],
        out_specs=(pl.BlockSpec((heads, ts, 2 * LANES), lambda i: (0, i, 0)),
                   pl.BlockSpec((heads, ts, 2 * LANES), lambda i: (0, i, 0)),
                   pl.BlockSpec((heads, ts, LANES), lambda i: (0, i, 0))),
        out_shape=(jax.ShapeDtypeStruct((heads, s, 2 * LANES), BF16),
                   jax.ShapeDtypeStruct((heads, s, 2 * LANES), BF16),
                   jax.ShapeDtypeStruct((heads, s, LANES), BF16)),
        compiler_params=_params("parallel"),
    )(q, kv, kr2, cos_t, sin_t)


LOG2_E = 1.4426950408889634


FLASH_FWD_TK = 512
FLASH_FWD_SPLITS = 2


def _flash_fwd(qp, kp, v, *, scale, tq=512):
    heads, s, dk = qp.shape
    tq = _tile(s, tq, 16)
    tk = _tile(tq, FLASH_FWD_TK, 16)
    splits = FLASH_FWD_SPLITS
    th = tq // splits
    band = tq // tk
    c = scale * LOG2_E

    def body(q_ref, k_ref, v_ref, o_ref, ob_ref, lse_ref):
        i = pl.program_id(1)
        qs = [q_ref[hh * th:(hh + 1) * th, :] for hh in range(splits)]

        def skipped(hh, col0):
            return col0 is not None and col0 >= (hh + 1) * th

        def scores(start, col0):
            kb = k_ref[pl.ds(pl.multiple_of(start, tk), tk), :]
            return tuple(None if skipped(hh, col0) else
                         lax.dot_general(qs[hh], kb, (((1,), (1,)), ((), ())), preferred_element_type=F32)
                         for hh in range(splits))

        def absorb(start, scs, state, col0):
            vb = v_ref[pl.ds(pl.multiple_of(start, tk), tk), :]
            new = []
            for hh in range(splits):
                if scs[hh] is None:
                    new.append(state[hh])
                    continue
                m_old, l_old, acc = state[hh]
                sc = scs[hh]
                if col0 is not None and col0 + tk - 1 > hh * th:
                    rows = hh * th + lax.broadcasted_iota(jnp.int32, (th, tk), 0)
                    cols = col0 + lax.broadcasted_iota(jnp.int32, (th, tk), 1)
                    sc = jnp.where(rows >= cols, sc, -jnp.inf)
                m_new = jnp.maximum(m_old, jnp.max(sc, axis=-1, keepdims=True))
                alpha = jnp.exp2((m_old - m_new) * c)
                p = jnp.exp2((sc - m_new) * c)
                l_new = alpha * l_old + jnp.sum(p, axis=-1, keepdims=True)
                acc = alpha * acc + jnp.dot(p.astype(BF16), vb, preferred_element_type=F32)
                new.append((m_new, l_new, acc))
            return tuple(new)

        def step(j, carry):
            state, scs = carry
            nxt = scores((j + 1) * tk, None)
            return absorb(j * tk, scs, state, None), nxt

        init = tuple((jnp.full((th, 1), -jnp.inf, F32), jnp.zeros((th, 1), F32), jnp.zeros((th, V_DIM), F32))
                     for _ in range(splits))
        state, scs = lax.fori_loop(0, i * band, step, (init, scores(0, None)))
        for b in range(band):
            nxt = scores(i * tq + (b + 1) * tk, (b + 1) * tk) if b + 1 < band else None
            state = absorb(i * tq + b * tk, scs, state, b * tk)
            scs = nxt
        carry = state
        for hh in range(splits):
            m_fin, l_fin, acc = carry[hh]
            out = acc / l_fin
            o_ref[hh * th:(hh + 1) * th, :] = out
            ob_ref[hh * th:(hh + 1) * th, :] = out.astype(BF16)
            lse_ref[hh * th:(hh + 1) * th, :] = m_fin * scale + jnp.log(l_fin)

    return _pcall(
        body, name="flash_fwd", grid=(heads, s // tq),
        in_specs=[pl.BlockSpec((None, tq, dk), lambda h, i: (h, i, 0)),
                  pl.BlockSpec((None, s, dk), lambda h, i: (h, 0, 0)),
                  pl.BlockSpec((None, s, V_DIM), lambda h, i: (h, 0, 0))],
        out_specs=(pl.BlockSpec((tq, V_DIM), lambda h, i: (i, h)), pl.BlockSpec((tq, V_DIM), lambda h, i: (i, h)),
                   pl.BlockSpec((None, tq, 1), lambda h, i: (h, i, 0))),
        out_shape=(jax.ShapeDtypeStruct((s, heads * V_DIM), F32), jax.ShapeDtypeStruct((s, heads * V_DIM), BF16),
                   jax.ShapeDtypeStruct((heads, s, 1), F32)),
        compiler_params=_params("parallel", "arbitrary"),
    )(qp, kp, v)


def _flash_bwd(qp, kp, v, dcat, o, lse, *, scale, do_col0, tq=512):
    heads, s, dk = qp.shape
    tq = _tile(s, tq, 16)
    tk = tq
    nq = s // tq

    def body(k_ref, v_ref, q_ref, do_ref, o_ref, lse_ref, dq_ref, dk_ref, dv_ref):
        j = pl.program_id(1)

        @pl.when(j == 0)
        def _():
            dq_ref[...] = jnp.zeros_like(dq_ref)

        kb = k_ref[...]
        vb = v_ref[...]

        def block(i, carry, diag):
            dk_acc, dv_acc = carry
            rows_at = pl.ds(pl.multiple_of(i * tq, tq), tq)
            qb = q_ref[rows_at, :]
            do = do_ref[rows_at, :]
            sc = lax.dot_general(qb, kb, (((1,), (1,)), ((), ())), preferred_element_type=F32) * scale
            if diag:
                rows = lax.broadcasted_iota(jnp.int32, (tq, tk), 0)
                cols = lax.broadcasted_iota(jnp.int32, (tq, tk), 1)
                sc = jnp.where(rows >= cols, sc, -jnp.inf)
            p = jnp.exp(sc - lse_ref[rows_at, :])
            dob = do.astype(BF16)
            dv_acc = dv_acc + lax.dot_general(p.astype(BF16), dob, (((0,), (0,)), ((), ())),
                                              preferred_element_type=F32)
            dp = lax.dot_general(dob, vb, (((1,), (1,)), ((), ())), preferred_element_type=F32)
            delta = jnp.sum(do * o_ref[rows_at, :], axis=-1, keepdims=True)
            ds = (p * (dp - delta) * scale).astype(BF16)
            dk_acc = dk_acc + lax.dot_general(ds, qb, (((0,), (0,)), ((), ())), preferred_element_type=F32)
            dq_ref[rows_at, :] += jnp.dot(ds, kb, preferred_element_type=F32)
            return dk_acc, dv_acc

        carry = block(j, (jnp.zeros((tk, dk), F32), jnp.zeros((tk, V_DIM), F32)), True)
        carry = lax.fori_loop(j + 1, nq, lambda i, cr: block(i, cr, False), carry)
        dk_ref[...] = carry[0]
        dv_ref[...] = carry[1]

    whole = lambda h, j: (h, 0, 0)
    return _pcall(
        body, name="flash_bwd", grid=(heads, nq),
        in_specs=[pl.BlockSpec((None, tk, dk), lambda h, j: (h, j, 0)),
                  pl.BlockSpec((None, tk, V_DIM), lambda h, j: (h, j, 0)),
                  pl.BlockSpec((None, s, dk), whole),
                  pl.BlockSpec((s, V_DIM), lambda h, j: (0, do_col0 + h)),
                  pl.BlockSpec((s, V_DIM), lambda h, j: (0, h)),
                  pl.BlockSpec((None, s, 1), whole)],
        out_specs=(pl.BlockSpec((None, s, dk), whole),
                   pl.BlockSpec((None, tk, dk), lambda h, j: (h, j, 0)),
                   pl.BlockSpec((None, tk, V_DIM), lambda h, j: (h, j, 0))),
        out_shape=(jax.ShapeDtypeStruct((heads, s, dk), F32), jax.ShapeDtypeStruct((heads, s, dk), F32),
                   jax.ShapeDtypeStruct((heads, s, V_DIM), F32)),
        compiler_params=_params("parallel", "arbitrary"),
    )(kp, v, qp, dcat, o, lse)


def _attn_bwd_post(dqp, dkp, dv, cos_t, sin_t):
    heads, s, _ = dqp.shape
    ts = _tile(s, 256, 8)

    def body(dq_ref, dk_ref, dv_ref, cos_ref, sin_ref, q_out, kv_out, kr_out):
        lane = lax.broadcasted_iota(jnp.int32, (ts, LANES), 1)
        cos_v = cos_ref[...]
        sin_v = sin_ref[...]

        def rope_t(dy):
            return dy * cos_v + _swap_halves(dy * sin_v, lane)

        kr_sum = jnp.zeros((ts, LANES), F32)
        for h in range(heads):
            q_out[:, h * LANES:(h + 1) * LANES] = dq_ref[h, :, :LANES].astype(BF16)
            kv_out[:, h * LANES:(h + 1) * LANES] = dk_ref[h, :, :LANES].astype(BF16)
            kv_out[:, (heads + h) * LANES:(heads + h + 1) * LANES] = dv_ref[h].astype(BF16)
            kr_sum = kr_sum + dk_ref[h, :, LANES:]
        for pair in range(heads // 2):
            r = jnp.where(lane < ROPE_DIM, dq_ref[2 * pair, :, LANES:], dq_ref[2 * pair + 1, :, LANES:])
            q_out[:, (heads + pair) * LANES:(heads + pair + 1) * LANES] = rope_t(r).astype(BF16)
        kr = rope_t(kr_sum)
        kr_out[...] = (kr + pltpu.roll(kr, ROPE_DIM, 1)).astype(BF16)

    wq = heads * (NOPE_DIM + ROPE_DIM)
    wkv = heads * (NOPE_DIM + V_DIM)
    tab = pl.BlockSpec((ts, LANES), lambda i: (i, 0))
    return _pcall(
        body, name="attn_bwd_post", grid=(s // ts,),
        in_specs=[pl.BlockSpec((heads, ts, 2 * LANES), lambda i: (0, i, 0)),
                  pl.BlockSpec((heads, ts, 2 * LANES), lambda i: (0, i, 0)),
                  pl.BlockSpec((heads, ts, LANES), lambda i: (0, i, 0)), tab, tab],
        out_specs=(pl.BlockSpec((ts, wq), lambda i: (i, 0)), pl.BlockSpec((ts, wkv), lambda i: (i, 0)), tab),
        out_shape=(jax.ShapeDtypeStruct((s, wq), BF16), jax.ShapeDtypeStruct((s, wkv), BF16),
                   jax.ShapeDtypeStruct((s, LANES), BF16)),
        compiler_params=_params("parallel"),
    )(dqp, dkp, dv, cos_t, sin_t)


CONV_HALO = 16


def _conv_gate(xe, cw_ref, cb_ref):
    x1 = pltpu.roll(xe, 1, 0)
    x2 = pltpu.roll(xe, 2, 0)
    return cw_ref[2:3, :] * xe + cw_ref[1:2, :] * x1 + cw_ref[0:1, :] * x2 + cb_ref[...], x1, x2


def _ffn_act_fwd(gu, conv_w, conv_b, *, d_ff, tc=256):
    s = gu.shape[0]
    tc = _tile(d_ff, tc)
    nf = d_ff // tc
    tr = _tile(s, ROW_CHUNK, CONV_HALO)
    n_chunks = s // tr

    def body(g_ref, u_ref, cw_ref, cb_ref, a_ref):
        r = pl.program_id(1)
        xe = _rows_with_halo(g_ref, r, tr, n_chunks, CONV_HALO, 0).astype(F32)
        gc = _conv_gate(xe, cw_ref, cb_ref)[0][CONV_HALO:]
        a_ref[...] = (gc * jax.nn.sigmoid(gc) * u_ref[...].astype(F32)).astype(BF16)

    return _pcall(
        body, name="ffn_act_fwd", grid=(nf, n_chunks),
        in_specs=[pl.BlockSpec((s, tc), lambda j, r: (0, j)), pl.BlockSpec((tr, tc), lambda j, r: (r, nf + j)),
                  pl.BlockSpec((CONV_TAPS, tc), lambda j, r: (0, j)), pl.BlockSpec((1, tc), lambda j, r: (0, j))],
        out_specs=pl.BlockSpec((tr, tc), lambda j, r: (r, j)),
        out_shape=jax.ShapeDtypeStruct((s, d_ff), BF16),
        compiler_params=_params("parallel", "arbitrary"),
    )(gu, gu, conv_w, conv_b.reshape(1, d_ff))


def _ffn_act_bwd(da, gu, conv_w, conv_b, *, d_ff, tc=256):
    s = gu.shape[0]
    tc = _tile(d_ff, tc)
    nf = d_ff // tc
    tr = _tile(s, ROW_CHUNK, CONV_HALO)
    n_chunks = s // tr
    rows = tr + 2 * CONV_HALO
    main = slice(CONV_HALO, CONV_HALO + tr)

    def body(da_ref, g_ref, u_ref, cw_ref, cb_ref, dg_ref, du_ref, dcw_ref, dcb_ref):
        r = pl.program_id(1)
        xe = _rows_with_halo(g_ref, r, tr, n_chunks, CONV_HALO, CONV_HALO).astype(F32)
        dae = _rows_with_halo(da_ref, r, tr, n_chunks, CONV_HALO, CONV_HALO).astype(F32)
        ue = _rows_with_halo(u_ref, r, tr, n_chunks, CONV_HALO, CONV_HALO).astype(F32)
        gc, x1, x2 = _conv_gate(xe, cw_ref, cb_ref)
        sg = jax.nn.sigmoid(gc)
        du_ref[...] = (dae * gc * sg)[main].astype(BF16)
        dgc = dae * ue * sg * (1.0 + gc * (1.0 - sg))
        dg = (cw_ref[2:3, :] * dgc + cw_ref[1:2, :] * pltpu.roll(dgc, rows - 1, 0)
              + cw_ref[0:1, :] * pltpu.roll(dgc, rows - 2, 0))
        dg_ref[...] = dg[main].astype(BF16)
        dgc_m = dgc[main]
        dcb = jnp.sum(dgc_m, axis=0, keepdims=True)
        dcw = jnp.concatenate([jnp.sum(dgc_m * x2[main], axis=0, keepdims=True),
                               jnp.sum(dgc_m * x1[main], axis=0, keepdims=True),
                               jnp.sum(dgc_m * xe[main], axis=0, keepdims=True)], axis=0)

        @pl.when(r == 0)
        def _():
            dcb_ref[...] = dcb
            dcw_ref[...] = dcw

        @pl.when(r > 0)
        def _():
            dcb_ref[...] += dcb
            dcw_ref[...] += dcw

    col = pl.BlockSpec((s, tc), lambda j, r: (0, j))
    out = pl.BlockSpec((tr, tc), lambda j, r: (r, j))
    return _pcall(
        body, name="ffn_act_bwd", grid=(nf, n_chunks),
        in_specs=[col, col, pl.BlockSpec((s, tc), lambda j, r: (0, nf + j)),
                  pl.BlockSpec((CONV_TAPS, tc), lambda j, r: (0, j)), pl.BlockSpec((1, tc), lambda j, r: (0, j))],
        out_specs=(out, out, pl.BlockSpec((CONV_TAPS, tc), lambda j, r: (0, j)),
                   pl.BlockSpec((1, tc), lambda j, r: (0, j))),
        out_shape=(jax.ShapeDtypeStruct((s, d_ff), BF16), jax.ShapeDtypeStruct((s, d_ff), BF16),
                   jax.ShapeDtypeStruct((CONV_TAPS, d_ff), F32), jax.ShapeDtypeStruct((1, d_ff), F32)),
        compiler_params=_params("parallel", "arbitrary"),
    )(da, gu, gu, conv_w, conv_b.reshape(1, d_ff))


def _ple_embed(p_ref, w_ref):
    return jnp.dot(p_ref[...].astype(BF16), w_ref[...], preferred_element_type=F32)


def _ple_fwd(h, z, p_i, w_ple):
    s, d = h.shape
    ts = _tile(s, 512, 8)

    def body(h_ref, z_ref, p_ref, w_ref, o_ref):
        o_ref[...] = h_ref[...] + _ple_embed(p_ref, w_ref) * jax.nn.sigmoid(z_ref[...])

    row = pl.BlockSpec((ts, d), lambda i: (i, 0))
    return _pcall(body, name="ple_fwd", grid=(s // ts,),
                  in_specs=[row, row, pl.BlockSpec((ts, p_i.shape[1]), lambda i: (i, 0)),
                            pl.BlockSpec(w_ple.shape, lambda i: (0, 0))],
                  out_specs=row, out_shape=jax.ShapeDtypeStruct((s, d), F32),
                  compiler_params=_params("parallel"))(h, z, p_i, w_ple)


def _ple_bwd(dh, z, p_i, w_ple, dep=None):
    s, d = dh.shape
    ts = _tile(s, 512, 8)

    def body(dh_ref, z_ref, p_ref, w_ref, *rest):
        de_ref, dz_ref = rest[-2:]
        gt = jax.nn.sigmoid(z_ref[...])
        dhv = dh_ref[...]
        de_ref[...] = (dhv * gt).astype(BF16)
        dz_ref[...] = (dhv * _ple_embed(p_ref, w_ref) * gt * (1.0 - gt)).astype(BF16)

    row = pl.BlockSpec((ts, d), lambda i: (i, 0))
    deps = [] if dep is None else [dep]
    return _pcall(body, name="ple_bwd", grid=(s // ts,),
                  in_specs=[row, row, pl.BlockSpec((ts, p_i.shape[1]), lambda i: (i, 0)),
                            pl.BlockSpec(w_ple.shape, lambda i: (0, 0))] + [HBM_SPEC] * len(deps),
                  out_specs=(row, row),
                  out_shape=(jax.ShapeDtypeStruct((s, d), BF16), jax.ShapeDtypeStruct((s, d), BF16)),
                  compiler_params=_params("parallel"))(dh, z, p_i, w_ple, *deps)


def _loss_head(h, g, target):
    s, d = h.shape
    ts = _tile(s, 512, 8)

    def body(h_ref, g_ref, t_ref, loss_ref, dh_ref, dg_ref):
        i = pl.program_id(0)
        x = h_ref[...]
        gv = g_ref[...]
        r = lax.rsqrt(jnp.mean(x * x, axis=-1, keepdims=True) + NORM_EPS)
        nh = x * r
        err = nh * gv - t_ref[...]
        part_loss = 0.5 * jnp.sum(jnp.mean(err * err, axis=-1, keepdims=True), axis=0, keepdims=True)
        dy = err * (1.0 / d)
        gd = dy * gv
        dh_ref[...] = (gd - nh * jnp.mean(gd * nh, axis=-1, keepdims=True)) * r
        part_g = jnp.sum(dy * nh, axis=0, keepdims=True)
        part_l = jnp.broadcast_to(part_loss, (1, LANES))

        @pl.when(i == 0)
        def _():
            dg_ref[...] = part_g
            loss_ref[...] = part_l

        @pl.when(i > 0)
        def _():
            dg_ref[...] += part_g
            loss_ref[...] += part_l

    row = pl.BlockSpec((ts, d), lambda i: (i, 0))
    vec = pl.BlockSpec((1, d), lambda i: (0, 0))
    return _pcall(
        body, name="loss_head", grid=(s // ts,), in_specs=[row, vec, row],
        out_specs=(pl.BlockSpec((1, LANES), lambda i: (0, 0)), row, vec),
        out_shape=(jax.ShapeDtypeStruct((1, LANES), F32), jax.ShapeDtypeStruct((s, d), F32),
                   jax.ShapeDtypeStruct((1, d), F32)),
        compiler_params=_params("arbitrary"),
    )(h, g.reshape(1, d), target)


HBM_SPEC = pl.BlockSpec(memory_space=pl.ANY)


def _flat_index(px, py, pc):
    return 4 * px + 2 * py + pc


def _all_gather(shards, *, name):
    n = len(shards)

    def body(*refs):
        ins, outs = refs[:n], refs[n:2 * n]
        send_sems, recv_sems, local_sems = refs[2 * n:]
        x, y, c = lax.axis_index("x"), lax.axis_index("y"), lax.axis_index("c")
        me, sibling = (x, y, c), (x, y, 1 - c)
        chips = [(1 - x, y), (x, 1 - y), (1 - x, 1 - y)]

        def copy(a, k, block, to, src=None):
            slot = outs[a].at[_flat_index(*block)]
            return pltpu.make_async_remote_copy(
                src_ref=slot if src is None else src, dst_ref=slot,
                send_sem=send_sems.at[a, k], recv_sem=recv_sems.at[a, k],
                device_id=to, device_id_type=pl.DeviceIdType.MESH)

        mine, first, passed = [], [], []
        for a in range(n):
            cp = pltpu.make_async_copy(ins[a], outs[a].at[_flat_index(*me)], local_sems.at[a])
            cp.start()
            mine.append(cp)
            first.append(copy(a, 0, me, sibling, src=ins[a]))
            first += [copy(a, 1 + j, me, (*chip, c), src=ins[a]) for j, chip in enumerate(chips)]
        for cp in first:
            cp.start()
        for j, chip in enumerate(chips):
            for a in range(n):
                copy(a, 1 + j, (*chip, c), me).wait_recv()
                fwd = copy(a, 4 + j, (*chip, c), sibling)
                fwd.start()
                passed.append(fwd)
        for a in range(n):
            copy(a, 0, sibling, me).wait_recv()
            for j, chip in enumerate(chips):
                copy(a, 4 + j, (*chip, 1 - c), me).wait_recv()
        for cp in first + passed:
            cp.wait_send()
        for cp in mine:
            cp.wait()

    return _pcall(
        body, name=name,
        in_specs=[HBM_SPEC] * n, out_specs=[HBM_SPEC] * n,
        out_shape=[jax.ShapeDtypeStruct((N_DEV,) + a.shape, a.dtype) for a in shards],
        scratch_shapes=[pltpu.SemaphoreType.DMA((n, 7)), pltpu.SemaphoreType.DMA((n, 7)),
                        pltpu.SemaphoreType.DMA((n,))],
    )(*shards)


HBM_ONLY = pl.BlockSpec(memory_space=pltpu.HBM)
SEM_SPEC = pl.BlockSpec(memory_space=pltpu.SEMAPHORE)
N_PEERS = N_DEV - 1
PEER_FLIPS = ((0, 0, 1), (1, 0, 0), (0, 1, 0), (1, 1, 0), (1, 0, 1), (0, 1, 1), (1, 1, 1))


def _exchange_refs(gather, src_refs, land_refs, send_sems, recv_sems):
    x, y, c = lax.axis_index("x"), lax.axis_index("y"), lax.axis_index("c")
    me = _flat_index(x, y, c)
    peers = [(x ^ fx, y ^ fy, c ^ fc) for fx, fy, fc in PEER_FLIPS]

    def out_copy(a, k):
        src = src_refs[a] if gather else src_refs[a].at[_flat_index(*peers[k])]
        return pltpu.make_async_remote_copy(
            src_ref=src, dst_ref=land_refs[a].at[me], send_sem=send_sems.at[a * N_PEERS + k],
            recv_sem=recv_sems.at[a * N_PEERS + k], device_id=peers[k], device_id_type=pl.DeviceIdType.MESH)

    def in_copy(a, k):
        src = src_refs[a] if gather else src_refs[a].at[me]
        return pltpu.make_async_remote_copy(
            src_ref=src, dst_ref=land_refs[a].at[_flat_index(*peers[k])], send_sem=send_sems.at[a * N_PEERS + k],
            recv_sem=recv_sems.at[a * N_PEERS + k], device_id=peers[k], device_id_type=pl.DeviceIdType.MESH)

    return out_copy, in_copy


def _exchange_start(srcs, *, gather, name, dep):
    n = len(srcs)
    lands = [lax.empty((N_DEV,) + a.shape if gather else a.shape, a.dtype) for a in srcs]

    def body(*refs):
        src_refs, land_refs = refs[:n], refs[n:2 * n]
        send_sems, recv_sems = refs[2 * n + 1], refs[2 * n + 2]
        token = refs[-1]
        out_copy, _ = _exchange_refs(gather, src_refs, land_refs, send_sems, recv_sems)
        for k in range(N_PEERS):
            for a in range(n):
                out_copy(a, k).start()
        token[...] = jnp.zeros_like(token)

    hbm = lambda a: pltpu.with_memory_space_constraint(a, pltpu.HBM)
    return _pcall(
        body, name=name,
        out_shape=(pltpu.SemaphoreType.DMA((n * N_PEERS,)), pltpu.SemaphoreType.DMA((n * N_PEERS,)),
                   *[pltpu.HBM(a.shape, a.dtype) for a in srcs], *[pltpu.HBM(a.shape, a.dtype) for a in lands],
                   jax.ShapeDtypeStruct((8, LANES), F32)),
        in_specs=[HBM_ONLY] * (2 * n) + [HBM_SPEC],
        out_specs=(SEM_SPEC, SEM_SPEC, *[HBM_ONLY] * (2 * n), pl.BlockSpec(memory_space=pltpu.VMEM)),
        input_output_aliases={i: 2 + i for i in range(2 * n)},
        compiler_params=pltpu.CompilerParams(has_side_effects=pltpu.SideEffectType.DATAFLOW_SIDE_EFFECTING),
    )(*[hbm(a) for a in srcs], *[hbm(a) for a in lands], dep)


def _exchange_wait(started, after, *, gather, name):
    send_sems, recv_sems = started[0], started[1]
    n = (len(started) - 3) // 2
    srcs, lands = started[2:2 + n], started[2 + n:2 + 2 * n]

    def body(*refs):
        src_refs, land_refs = refs[:n], refs[n:2 * n]
        s_sems, r_sems = refs[2 * n], refs[2 * n + 1]
        out_copy, in_copy = _exchange_refs(gather, src_refs, land_refs, s_sems, r_sems)
        for k in range(N_PEERS):
            for a in range(n):
                out_copy(a, k).wait_send()
                in_copy(a, k).wait_recv()

    res = _pcall(
        body, name=name,
        out_shape=tuple(pltpu.HBM(a.shape, a.dtype) for a in (*srcs, *lands)),
        in_specs=[HBM_ONLY] * (2 * n) + [SEM_SPEC, SEM_SPEC, HBM_SPEC],
        out_specs=tuple([HBM_ONLY] * (2 * n)),
        input_output_aliases={i: i for i in range(2 * n)},
        compiler_params=pltpu.CompilerParams(has_side_effects=pltpu.SideEffectType.DATAFLOW_SIDE_EFFECTING),
    )(*srcs, *lands, send_sems, recv_sems, after)
    return _with_own_slot(res[:n], res[n:], gather)


def _with_own_slot(srcs, lands, gather):
    me = _flat_index(lax.axis_index("x"), lax.axis_index("y"), lax.axis_index("c"))
    full = []
    for src, land in zip(srcs, lands):
        own = src if gather else lax.dynamic_index_in_dim(src, me, 0, keepdims=False)
        full.append(lax.dynamic_update_slice(land, own[None], (me,) + (0,) * own.ndim))
    return full


def _adam_math(g, w, m, v):
    m = ADAM_B1 * m + (1.0 - ADAM_B1) * g
    v = ADAM_B2 * v + (1.0 - ADAM_B2) * jnp.square(g)
    m_hat = m / (1.0 - ADAM_B1 ** ADAM_STEP)
    v_hat = v / (1.0 - ADAM_B2 ** ADAM_STEP)
    delta = -ADAM_LR * (m_hat / (jnp.sqrt(v_hat) + ADAM_EPS) + ADAM_WD * w)
    return delta, m, v


def _adamw(contribs, w, m, v, *, name):
    layers = len(contribs)
    _, r, c = contribs[0].shape
    tr = _tile(r, max(8, (256 * 1024 // c) // 8 * 8), 8)

    def body(*refs):
        g_refs = refs[:layers]
        w_ref, m_ref, v_ref, go_ref, d_ref, mo_ref, vo_ref = refs[layers:]
        layer = pl.program_id(0)
        for l2 in range(layers):

            @pl.when(layer == l2)
            def _(g_ref=g_refs[l2]):
                g = g_ref[0].astype(F32)
                for k in range(1, N_DEV):
                    g = g + g_ref[k].astype(F32)
                delta, m_new, v_new = _adam_math(g, w_ref[...], m_ref[...], v_ref[...])
                go_ref[...] = g
                d_ref[...] = delta
                mo_ref[...] = m_new
                vo_ref[...] = v_new

    g_specs = [pl.BlockSpec((N_DEV, tr, c), lambda l, i, l2=l2: (0, jnp.where(l == l2, i, 0), 0))
               for l2 in range(layers)]
    blk = pl.BlockSpec((None, tr, c), lambda l, i: (l, i, 0))
    out = jax.ShapeDtypeStruct((layers, r, c), F32)
    return _pcall(
        body, name=name, grid=(layers, r // tr), in_specs=g_specs + [blk, blk, blk],
        out_specs=(blk, blk, blk, blk), out_shape=(out, out, out, out),
        compiler_params=_params("arbitrary", "arbitrary"),
    )(*contribs, w, m, v)


def _heads_split(w, heads, first, second):
    k = w.shape[0]
    w3 = w.reshape(k, heads, first + second)
    return jnp.concatenate([w3[:, :, :first].reshape(k, heads * first),
                            w3[:, :, first:].reshape(k, heads * second)], axis=1)


def _heads_join(w, heads, first, second):
    k = w.shape[0]
    a = w[:, :heads * first].reshape(k, heads, first)
    b = w[:, heads * first:].reshape(k, heads, second)
    return jnp.concatenate([a, b], axis=2).reshape(k, heads * (first + second))


def _full_from_gathered(kind, g):
    if kind == "col":
        return jnp.transpose(g, (1, 0, 2)).reshape(g.shape[1], N_DEV * g.shape[2])
    if kind == "row":
        return g.reshape(N_DEV * g.shape[1], g.shape[2])
    return jnp.transpose(g, (1, 0, 2, 3)).reshape(g.shape[1], N_DEV * g.shape[2], g.shape[3])


def _blocks_from_full(kind, f):
    if kind == "col":
        k, n = f.shape
        return jnp.transpose(f.reshape(k, N_DEV, n // N_DEV), (1, 0, 2))
    if kind == "row":
        k, n = f.shape
        return f.reshape(N_DEV, k // N_DEV, n)
    g, c_in, c = f.shape
    return jnp.transpose(f.reshape(g, N_DEV, c_in // N_DEV, c), (1, 0, 2, 3))


def _rope_tables(positions):
    inv_freq = 1.0 / (ROPE_THETA ** (jnp.arange(0, ROPE_DIM, 2, dtype=F32) / ROPE_DIM))
    ang = positions.astype(F32)[:, None] * inv_freq
    cos, sin = jnp.cos(ang), jnp.sin(ang)
    return jnp.concatenate([cos, cos, cos, cos], axis=-1), jnp.concatenate([-sin, sin, -sin, sin], axis=-1)


def _layer_fwd(h0, p_i, fetch, rep, tabs, dims, dep=None):
    heads, d_pool, q_lora, d_ff = dims["heads"], dims["d_pool"], dims["q_lora"], dims["d_ff"]
    c = d_pool // POOL_GROUPS
    cos_t, sin_t = tabs
    scale = 1.0 / math.sqrt(NOPE_DIM + ROPE_DIM)
    n1 = _rms_fwd(h0, rep["norm_mix_g"], name="rms_mix_fwd", dep=dep)
    w = dict(fetch("mix", n1))
    u = _mm(n1, w["w_in"], name="mm_in_fwd")
    y_pool, diff = _pool_fwd(u, w["pool_w"], rep["pool_scale"], c=c)
    nq = _rms_fwd(u, rep["q_norm_g"], name="rms_q_fwd", col_block=d_pool // q_lora)
    nkv = _rms_fwd(u, rep["kv_norm_g"], name="rms_kv_fwd", col_block=d_pool // q_lora + 1)
    q = _mm(nq, w["w_uq"], name="mm_uq_fwd")
    kv = _mm(nkv, w["w_ukv"], name="mm_ukv_fwd")
    kr = u[:, d_pool + 2 * q_lora:]
    kr2 = jnp.concatenate([kr, kr], axis=-1)
    qp, kp, v = _qkv_prep(q, kv, kr2, cos_t, sin_t, heads=heads)
    o, o_mm, lse = _flash_fwd(qp, kp, v, scale=scale)
    cat = jnp.concatenate([y_pool, o_mm], axis=1)
    h1 = _mm(cat, w["w_out"], name="mm_out_fwd", add=h0)
    n2 = _rms_fwd(h1, rep["norm_ffn_g"], name="rms_ffn_fwd")
    w.update(fetch("up", n2))
    gu = _mm(n2, w["w_up"], name="mm_up_fwd", out_dtype=BF16, b_blocks=(0, N_DEV))
    a = _ffn_act_fwd(gu, w["conv_w"], rep["conv_b"], d_ff=d_ff)
    w.update(fetch("down", a))
    h2 = _mm(a, w["w_down"], name="mm_down_fwd", add=h1)
    n3 = _rms_fwd(h2, rep["norm_ple_g"], name="rms_ple_fwd")
    z = _mm(n3, w["w_ple_gate"], name="mm_pgate_fwd")
    h3 = _ple_fwd(h2, z, p_i, w["w_ple"])
    saved = dict(h0=h0, n1=n1, u=u, cat=cat, diff=diff, nq=nq, nkv=nkv, qp=qp, kp=kp, v=v, o=o, lse=lse,
                 h1=h1, n2=n2, gu=gu, a=a, h2=h2, n3=n3, z=z)
    return h3, saved, w


def _layer_bwd(dh3, p_i, w, rep, tabs, dims, sv, dep=None, hooks=None):
    hooks = hooks or {}
    heads, d_pool, q_lora, d_ff = dims["heads"], dims["d_pool"], dims["q_lora"], dims["d_ff"]
    c = d_pool // POOL_GROUPS
    cos_t, sin_t = tabs
    scale = 1.0 / math.sqrt(NOPE_DIM + ROPE_DIM)
    gr = {}
    de, dz = _ple_bwd(dh3, sv["z"], p_i, w["w_ple"], dep)
    gr["w_ple"] = _mm(p_i, de, name="mm_ple_dw", ta=True, out_dtype=BF16)
    gr["w_ple_gate"] = _mm(sv["n3"], dz, name="mm_pgate_dw", ta=True, out_dtype=BF16)
    dn3 = _mm(dz, w["w_ple_gate"], name="mm_pgate_dx", tb=True)
    dh2, dh2_mm, gr["norm_ple_g"] = _rms_bwd(dn3, sv["h2"], rep["norm_ple_g"], name="rms_ple_bwd", res=dh3,
                                             matmul_copy=True)
    gr["w_down"] = _mm(sv["a"], dh2_mm, name="mm_down_dw", ta=True, out_dtype=BF16)
    dep_down = hooks["down"](dh2, gr) if "down" in hooks else None
    da = _mm(dh2_mm, w["w_down"], name="mm_down_dx", tb=True, out_dtype=BF16, dep=dep_down)
    dgate, dup, gr["conv_w"], gr["conv_b"] = _ffn_act_bwd(da, sv["gu"], w["conv_w"], rep["conv_b"], d_ff=d_ff)
    half, per = N_DEV // 2, w["w_up"].shape[2]
    dw_gate = _mm(sv["n2"], dgate, name="mm_up_gate_dw", ta=True, out_dtype=BF16, out_blocks=(N_DEV, 0, per))
    gr["w_up"] = _mm(sv["n2"], dup, name="mm_up_up_dw", ta=True, out_dtype=BF16, out_blocks=(N_DEV, half, per),
                     out_init=dw_gate)
    dep_up = hooks["up"](gr["w_up"], gr) if "up" in hooks else None
    dn2 = _mm(dgate, w["w_up"], name="mm_up_gate_dx", tb=True, b_blocks=(0, half), dep=dep_up)
    dn2 = _mm(dup, w["w_up"], name="mm_up_up_dx", tb=True, b_blocks=(half, half), add=dn2)
    dh1, dh1_mm, gr["norm_ffn_g"] = _rms_bwd(dn2, sv["h1"], rep["norm_ffn_g"], name="rms_ffn_bwd", res=dh2,
                                             matmul_copy=True)
    gr["w_out"] = _mm(sv["cat"], dh1_mm, name="mm_out_dw", ta=True, out_dtype=BF16)
    dcat = _mm(dh1_mm, w["w_out"], name="mm_out_dx", tb=True)
    do_col0 = d_pool // V_DIM
    dqp, dkp, dv = _flash_bwd(sv["qp"], sv["kp"], sv["v"], dcat, sv["o"], sv["lse"], scale=scale, do_col0=do_col0)
    dq, dkv, dkr2 = _attn_bwd_post(dqp, dkp, dv, cos_t, sin_t)
    gr["w_uq"] = _mm(sv["nq"], dq, name="mm_uq_dw", ta=True, out_dtype=BF16)
    gr["w_ukv"] = _mm(sv["nkv"], dkv, name="mm_ukv_dw", ta=True, out_dtype=BF16)
    dnq = _mm(dq, w["w_uq"], name="mm_uq_dx", tb=True)
    dnkv = _mm(dkv, w["w_ukv"], name="mm_ukv_dx", tb=True)
    dcq, gr["q_norm_g"] = _rms_bwd(dnq, sv["u"], rep["q_norm_g"], name="rms_q_bwd",
                                   col_block=d_pool // q_lora, out_dtype=BF16)
    dckv, gr["kv_norm_g"] = _rms_bwd(dnkv, sv["u"], rep["kv_norm_g"], name="rms_kv_bwd",
                                     col_block=d_pool // q_lora + 1, out_dtype=BF16)
    du_pool, gr["pool_w"], gr["pool_scale"] = _pool_bwd(dcat, sv["diff"], w["pool_w"], rep["pool_scale"], c=c)
    du = jnp.concatenate([du_pool, dcq, dckv, dkr2[:, :ROPE_DIM]], axis=-1)
    gr["w_in"] = _mm(sv["n1"], du, name="mm_in_dw", ta=True, out_dtype=BF16)
    dn1 = _mm(du, w["w_in"], name="mm_in_dx", tb=True)
    dh0, gr["norm_mix_g"] = _rms_bwd(dn1, sv["h0"], rep["norm_mix_g"], name="rms_mix_bwd", res=dh1)
    return dh0, gr


def _as2d(a):
    return a.reshape(a.shape[0], -1, a.shape[-1])


def kernel(x, p, positions, norm_mix_g, w_in, pool_w, pool_scale, q_norm_g, w_uq, kv_norm_g, w_ukv, w_out, norm_ffn_g, w_up, conv_w, conv_b, w_down, norm_ple_g, w_ple, w_ple_gate, final_norm_g, loss_target, m_norm_mix_g, m_w_in, m_pool_w, m_pool_scale, m_q_norm_g, m_w_uq, m_kv_norm_g, m_w_ukv, m_w_out, m_norm_ffn_g, m_w_up, m_conv_w, m_conv_b, m_w_down, m_norm_ple_g, m_w_ple, m_w_ple_gate, m_final_norm_g, v_norm_mix_g, v_w_in, v_pool_w, v_pool_scale, v_q_norm_g, v_w_uq, v_kv_norm_g, v_w_ukv, v_w_out, v_norm_ffn_g, v_w_up, v_conv_w, v_conv_b, v_w_down, v_norm_ple_g, v_w_ple, v_w_ple_gate, v_final_norm_g):
    weights = dict(norm_mix_g=norm_mix_g, w_in=w_in, pool_w=pool_w, pool_scale=pool_scale, q_norm_g=q_norm_g,
                   w_uq=w_uq, kv_norm_g=kv_norm_g, w_ukv=w_ukv, w_out=w_out, norm_ffn_g=norm_ffn_g, w_up=w_up,
                   conv_w=conv_w, conv_b=conv_b, w_down=w_down, norm_ple_g=norm_ple_g, w_ple=w_ple,
                   w_ple_gate=w_ple_gate, final_norm_g=final_norm_g)
    m_in = dict(norm_mix_g=m_norm_mix_g, w_in=m_w_in, pool_w=m_pool_w, pool_scale=m_pool_scale, q_norm_g=m_q_norm_g,
                w_uq=m_w_uq, kv_norm_g=m_kv_norm_g, w_ukv=m_w_ukv, w_out=m_w_out, norm_ffn_g=m_norm_ffn_g,
                w_up=m_w_up, conv_w=m_conv_w, conv_b=m_conv_b, w_down=m_w_down, norm_ple_g=m_norm_ple_g,
                w_ple=m_w_ple, w_ple_gate=m_w_ple_gate, final_norm_g=m_final_norm_g)
    v_in = dict(norm_mix_g=v_norm_mix_g, w_in=v_w_in, pool_w=v_pool_w, pool_scale=v_pool_scale, q_norm_g=v_q_norm_g,
                w_uq=v_w_uq, kv_norm_g=v_kv_norm_g, w_ukv=v_w_ukv, w_out=v_w_out, norm_ffn_g=v_norm_ffn_g,
                w_up=v_w_up, conv_w=v_conv_w, conv_b=v_conv_b, w_down=v_w_down, norm_ple_g=v_norm_ple_g,
                w_ple=v_w_ple, w_ple_gate=v_w_ple_gate, final_norm_g=v_final_norm_g)

    depth = w_in.shape[0]
    s, d_model = x.shape[1], x.shape[2]
    d_pool = pool_scale.shape[-1]
    q_lora = q_norm_g.shape[-1]
    d_ff = conv_b.shape[-1]
    heads = (w_uq.shape[-1] * N_DEV) // (NOPE_DIM + ROPE_DIM)
    dims = dict(heads=heads, d_pool=d_pool, q_lora=q_lora, d_ff=d_ff)

    groups = {"mix": ("w_in", "pool_w", "w_uq", "w_ukv", "w_out"), "up": ("w_up", "conv_w"),
              "down": ("w_down", "w_ple", "w_ple_gate")}

    def group_shards(i, group):
        return [weights[n][i] if n == "conv_w" else weights[n][i].astype(BF16) for n in groups[group]]

    def start_weights(i, group, dep):
        return _exchange_start(group_shards(i, group), gather=True, name=f"weights_{group}_start_{i}", dep=dep)

    def full_group(group, gathered_g):
        w = {n: g if n == "w_up" else _full_from_gathered(SHARD_KIND[n], g) for n, g in zip(groups[group], gathered_g)}
        if group == "mix":
            w["w_uq"] = _heads_split(w["w_uq"], heads, NOPE_DIM, ROPE_DIM)
            w["w_ukv"] = _heads_split(w["w_ukv"], heads, NOPE_DIM, V_DIM)
        return w

    tabs = _rope_tables(positions[0])

    arrived = {(0, "mix"): _all_gather(group_shards(0, "mix"), name="weights_mix_gather_0")}
    travelling = {}
    token = arrived[(0, "mix")][0]
    for group in ("up", "down"):
        travelling[(0, group)] = start_weights(0, group, token)
        token = travelling[(0, group)][-1]
    layer_w = []
    h = x[0]
    saved = []
    for i in range(depth):
        if i + 1 < depth:
            for group in ("mix", "up", "down"):
                travelling[(i + 1, group)] = start_weights(i + 1, group, token if i == 0 and group == "mix" else
                                                           (h if group == "mix" else token))
                token = travelling[(i + 1, group)][-1]

        def fetch(group, after, i=i):
            if (i, group) not in arrived:
                arrived[(i, group)] = _exchange_wait(travelling[(i, group)], after, gather=True,
                                                     name=f"weights_{group}_wait_{i}")
            return full_group(group, arrived[(i, group)])

        rep = {n: weights[n][i] for n in REPLICATED}
        h, sv, w = _layer_fwd(h, p[i, 0], fetch, rep, tabs, dims, dep=token if i + 1 < depth or i == 0 else None)
        layer_w.append((w, rep))
        saved.append(sv)
    loss_row, dh, g_final = _loss_head(h, final_norm_g, loss_target[0])
    loss = lax.psum(loss_row[0, 0], MESH_AXES)

    def start_grads(group, gr, dep, i):
        names = groups[group]
        blocks = [gr[n] if n == "w_up" else _blocks_from_full(SHARD_KIND[n], gr[n]).astype(BF16) for n in names]
        return _exchange_start(blocks, gather=False, name=f"grads_{group}_start_{i}", dep=dep)

    def end_grads(group, started, after, i):
        got = _exchange_wait(started, after, gather=False, name=f"grads_{group}_wait_{i}")
        received[i].update(zip(groups[group], got))

    layer_grads = [None] * depth
    received = [dict() for _ in range(depth)]
    pending = None
    for i in reversed(range(depth)):
        w, rep = layer_w[i]
        state = {}

        def on_down(dh2, gr, i=i, state=state):
            state["down"] = start_grads("down", gr, dh2, i)
            return state["down"][-1]

        def on_up(dw_up, gr, i=i, pending=pending, state=state):
            order = gr["conv_w"]
            if pending is not None:
                for group in ("down", "up", "mix"):
                    end_grads(group, pending[group], order, i + 1)
            state["up"] = start_grads("up", gr, order, i)
            return state["up"][-1]

        dh, gr = _layer_bwd(dh, p[i, 0], w, rep, tabs, dims, saved[i],
                            dep=loss.reshape(1, 1) if pending is None else pending["mix"][-1],
                            hooks={"down": on_down, "up": on_up})
        gr["w_uq"] = _heads_join(gr["w_uq"], heads, NOPE_DIM, ROPE_DIM)
        gr["w_ukv"] = _heads_join(gr["w_ukv"], heads, NOPE_DIM, V_DIM)
        layer_grads[i] = gr
        state["mix"] = start_grads("mix", gr, dh, i)
        pending = state
    grad_x = dh[None]

    out = {}

    def update(n):
        shape = weights[n].shape
        recs = [received[i][n].reshape((N_DEV, -1, shape[-1])) for i in range(depth)]
        res = _adamw(recs, _as2d(weights[n]), _as2d(m_in[n]), _as2d(v_in[n]), name="adamw_" + n)
        out[n] = tuple(r.reshape(shape) for r in res)

    end_grads("down", pending["down"], pending["mix"][-1], 0)
    end_grads("up", pending["up"], pending["mix"][-1], 0)
    for n in groups["down"] + groups["up"]:
        update(n)
    end_grads("mix", pending["mix"], out["w_up"][0], 0)
    for n in groups["mix"]:
        update(n)

    small_names = REPLICATED + ("final_norm_g",)

    def pack(get):
        rows = [jnp.stack([get(n, i).reshape(-1) for i in range(depth)]).reshape(-1) for n in REPLICATED]
        rows.append(get("final_norm_g", None).reshape(-1))
        return jnp.concatenate(rows).reshape(1, -1, LANES)

    g_small = pack(lambda n, i: g_final if i is None else layer_grads[i][n])
    w_small = pack(lambda n, i: weights[n] if i is None else weights[n][i])
    m_small = pack(lambda n, i: m_in[n] if i is None else m_in[n][i])
    v_small = pack(lambda n, i: v_in[n] if i is None else v_in[n][i])
    (g_all,) = _all_gather([g_small], name="small_grads_all_gather")
    res_small = _adamw([g_all[:, 0]], w_small, m_small, v_small, name="adamw_small")

    def unpack(flat3):
        flat = flat3.reshape(-1)
        res, off = {}, 0
        for n in REPLICATED:
            width = weights[n].shape[-1]
            res[n] = flat[off:off + depth * width].reshape(depth, width)
            off += depth * width
        res["final_norm_g"] = flat[off:off + d_model]
        return res

    small = [unpack(r) for r in res_small]
    for n in small_names:
        out[n] = tuple(small[k][n] for k in range(4))

    outs = [loss, grad_x]
    for k in range(4):
        outs += [out[n][k] for n in WEIGHT_ORDER]
    return tuple(outs)
```
